```python
import math
import jax, jax.numpy as jnp
from jax import lax
import numpy as np

D_MODEL = 1024
BATCH = 2
SEQ = 16384
DEPTH = 2

N_MIXERS = 2
N_SSM_LAYERS = (DEPTH + 1) // 2
N_CONV_LAYERS = DEPTH // 2
SSM_WIDTH = D_MODEL
SSM_GROUP = 16
SSM_GROUPS = SSM_WIDTH // SSM_GROUP
SSM_STATE = 64
SSM_CHUNK = 128
DT_MIN = 0.001
DT_MAX = 0.1
CONV_WIDTH = D_MODEL
CONV_K = 3
N_GROUPS = 4
EXPERTS_PER_GROUP = 8
N_EXPERTS = N_GROUPS * EXPERTS_PER_GROUP
TOP_K = 2
D_EXPERT = D_MODEL // 2
RMS_EPS = 1e-6

kernel_name = 'hybrid_s5_shortconv_hier_moe'


def rmsnorm(x, g):
    x32 = x.astype(jnp.float32)
    y = x32 * lax.rsqrt(jnp.mean(x32 * x32, axis=-1, keepdims=True) + RMS_EPS)
    return (y * g.astype(jnp.float32)).astype(x.dtype)


def modulate(h, shift, scale):
    return h * (1 + scale[:, None, :]) + shift[:, None, :]


def _complex_affine_combine(e1, e2):
    a1r, a1i, b1r, b1i = e1
    a2r, a2i, b2r, b2i = e2
    return (a2r * a1r - a2i * a1i,
            a2r * a1i + a2i * a1r,
            a2r * b1r - a2i * b1i + b2r,
            a2r * b1i + a2i * b1r + b2i)


def s5_mixer(h, w_in, lam_re, lam_im, log_dt, b_re, b_im, c_re, c_im, d_skip, w_glu):
    f32 = jnp.float32
    bsz, seq, _ = h.shape
    u = (h @ w_in).astype(f32)
    lr, li = lam_re.astype(f32), lam_im.astype(f32)
    dt = jnp.exp(log_dt.astype(f32))[:, None]
    mag = jnp.exp(lr * dt)
    ab_re = mag * jnp.cos(li * dt)
    ab_im = mag * jnp.sin(li * dt)
    den = lr * lr + li * li
    cf_re = ((ab_re - 1) * lr + ab_im * li) / den
    cf_im = (ab_im * lr - (ab_re - 1) * li) / den
    br, bi = b_re.astype(f32), b_im.astype(f32)
    bb_re = cf_re[..., None] * br - cf_im[..., None] * bi
    bb_im = cf_re[..., None] * bi + cf_im[..., None] * br
    cr, ci = c_re.astype(f32), c_im.astype(f32)
    n_chunks = seq // SSM_CHUNK
    uc = u.reshape(bsz, n_chunks, SSM_CHUNK, SSM_GROUPS, SSM_GROUP).transpose(1, 0, 2, 3, 4)
    blk_shape = (bsz, SSM_CHUNK, SSM_GROUPS, SSM_STATE)
    a_re = jnp.broadcast_to(ab_re, blk_shape)
    a_im = jnp.broadcast_to(ab_im, blk_shape)

    def chunk_step(carry, u_blk):
        h_re, h_im = carry
        bu_re = jnp.einsum('btgk,gpk->btgp', u_blk, bb_re)
        bu_im = jnp.einsum('btgk,gpk->btgp', u_blk, bb_im)
        A_re, A_im, S_re, S_im = lax.associative_scan(
            _complex_affine_combine, (a_re, a_im, bu_re, bu_im), axis=1)
        st_re = S_re + A_re * h_re[:, None] - A_im * h_im[:, None]
        st_im = S_im + A_re * h_im[:, None] + A_im * h_re[:, None]
        y = (jnp.einsum('btgp,gkp->btgk', st_re, cr)
             - jnp.einsum('btgp,gkp->btgk', st_im, ci))
        return (st_re[:, -1], st_im[:, -1]), y

    h0 = jnp.zeros((bsz, SSM_GROUPS, SSM_STATE), f32)
    _, ys = lax.scan(chunk_step, (h0, h0), uc)
    y = ys.transpose(1, 0, 2, 3, 4).reshape(bsz, seq, SSM_WIDTH) + d_skip.astype(f32) * u
    y = jax.nn.gelu(y).astype(h.dtype)
    val, gate = jnp.split(y @ w_glu, 2, axis=-1)
    return val * jax.nn.sigmoid(gate)


def short_conv_mixer(h, w_in, conv_w, w_out):
    b_g, c_g, v = jnp.split(h @ w_in, 3, axis=-1)
    u = c_g * v
    z = lax.conv_general_dilated(
        u, conv_w[:, None, :].astype(u.dtype), window_strides=(1,),
        padding=[(CONV_K - 1, 0)], dimension_numbers=('NWC', 'WIO', 'NWC'),
        feature_group_count=CONV_WIDTH)
    return (b_g * z) @ w_out


def hier_moe(h, wg, bg, we, be, w1, w3, w2):
    f32 = jnp.float32
    bsz, seq, d = h.shape
    T = bsz * seq
    hf = h.reshape(T, d)
    g_prob = jax.nn.softmax((hf @ wg).astype(f32) + bg.astype(f32), axis=-1)
    gp, gi = lax.top_k(g_prob, 1)
    e_logits = ((hf @ we).astype(f32) + be.astype(f32)).reshape(T, N_GROUPS, EXPERTS_PER_GROUP)
    e_sel = jnp.take_along_axis(e_logits, gi[:, :, None], axis=1)[:, 0]
    ev, ei = lax.top_k(e_sel, TOP_K)
    wts = gp * jax.nn.softmax(ev, axis=-1)
    eid = gi * EXPERTS_PER_GROUP + ei
    flat_e = eid.reshape(-1)
    order = jnp.argsort(flat_e)
    tok = order // TOP_K
    sizes = jnp.bincount(flat_e, length=N_EXPERTS).astype(jnp.int32)
    xs = hf[tok]
    a = lax.ragged_dot(xs, w1, sizes)
    b = lax.ragged_dot(xs, w3, sizes)
    o = lax.ragged_dot(jax.nn.silu(a) * b, w2, sizes)
    o = o * wts.reshape(-1)[order][:, None].astype(o.dtype)
    y = jnp.zeros_like(hf).at[tok].add(o)
    return y.reshape(bsz, seq, d)


def setup_inputs(seed: int = 0) -> dict:
    f32 = jnp.float32
    key = jax.random.key(seed)
    ks = jax.random.split(key, 32)

    def nrm(k, shape, scale):
        return jax.random.normal(k, shape, f32) * scale

    n_idx = jnp.arange(SSM_STATE, dtype=f32)
    ssm_shape = (N_SSM_LAYERS, SSM_GROUPS, SSM_STATE)
    return {
        'x': nrm(ks[0], (BATCH, SEQ, D_MODEL), 1.0),
        'c': nrm(ks[1], (BATCH, D_MODEL), 1.0),
        'ada_w': nrm(ks[2], (DEPTH, D_MODEL, 6 * D_MODEL), 0.1 * D_MODEL ** -0.5),
        'ada_b': nrm(ks[3], (DEPTH, 6 * D_MODEL), 0.02),
        'norm1_g': 1.0 + nrm(ks[4], (DEPTH, D_MODEL), 0.02),
        'norm2_g': 1.0 + nrm(ks[5], (DEPTH, D_MODEL), 0.02),
        'ssm_w_in': nrm(ks[6], (N_SSM_LAYERS, D_MODEL, SSM_WIDTH), D_MODEL ** -0.5),
        'ssm_lam_re': -0.5 + nrm(ks[7], ssm_shape, 0.01),
        'ssm_lam_im': math.pi * n_idx + nrm(ks[8], ssm_shape, 0.01),
        'ssm_log_dt': jax.random.uniform(ks[9], (N_SSM_LAYERS, SSM_GROUPS), f32,
                                         math.log(DT_MIN), math.log(DT_MAX)),
        'ssm_b_re': nrm(ks[10], (N_SSM_LAYERS, SSM_GROUPS, SSM_STATE, SSM_GROUP), (2 * SSM_GROUP) ** -0.5),
        'ssm_b_im': nrm(ks[11], (N_SSM_LAYERS, SSM_GROUPS, SSM_STATE, SSM_GROUP), (2 * SSM_GROUP) ** -0.5),
        'ssm_c_re': nrm(ks[12], (N_SSM_LAYERS, SSM_GROUPS, SSM_GROUP, SSM_STATE), SSM_STATE ** -0.5),
        'ssm_c_im': nrm(ks[13], (N_SSM_LAYERS, SSM_GROUPS, SSM_GROUP, SSM_STATE), SSM_STATE ** -0.5),
        'ssm_d': nrm(ks[14], (N_SSM_LAYERS, SSM_WIDTH), 1.0),
        'ssm_w_glu': nrm(ks[15], (N_SSM_LAYERS, SSM_WIDTH, 2 * D_MODEL), SSM_WIDTH ** -0.5),
        'conv_w_in': nrm(ks[16], (N_CONV_LAYERS, D_MODEL, 3 * CONV_WIDTH), D_MODEL ** -0.5),
        'conv_w': nrm(ks[17], (N_CONV_LAYERS, CONV_K, CONV_WIDTH), CONV_K ** -0.5),
        'conv_w_out': nrm(ks[18], (N_CONV_LAYERS, CONV_WIDTH, D_MODEL), CONV_WIDTH ** -0.5),
        'moe_wg': nrm(ks[19], (DEPTH, D_MODEL, N_GROUPS), D_MODEL ** -0.5),
        'moe_bg': nrm(ks[20], (DEPTH, N_GROUPS), 0.01),
        'moe_we': nrm(ks[21], (DEPTH, D_MODEL, N_EXPERTS), D_MODEL ** -0.5),
        'moe_be': nrm(ks[22], (DEPTH, N_EXPERTS), 0.01),
        'moe_w1': nrm(ks[23], (DEPTH, N_EXPERTS, D_MODEL, D_EXPERT), D_MODEL ** -0.5),
        'moe_w3': nrm(ks[24], (DEPTH, N_EXPERTS, D_MODEL, D_EXPERT), D_MODEL ** -0.5),
        'moe_w2': nrm(ks[25], (DEPTH, N_EXPERTS, D_EXPERT, D_MODEL), D_EXPERT ** -0.5),
        'final_g': 1.0 + nrm(ks[26], (D_MODEL,), 0.02),
    }


def reference(x, c, ada_w, ada_b, norm1_g, norm2_g,
              ssm_w_in, ssm_lam_re, ssm_lam_im, ssm_log_dt, ssm_b_re, ssm_b_im,
              ssm_c_re, ssm_c_im, ssm_d, ssm_w_glu,
              conv_w_in, conv_w, conv_w_out,
              moe_wg, moe_bg, moe_we, moe_be, moe_w1, moe_w3, moe_w2, final_g):
    cond = jax.nn.silu(c)
    for i in range(DEPTH):
        mod = cond @ ada_w[i] + ada_b[i]
        sh1, sc1, g1, sh2, sc2, g2 = jnp.split(mod, 6, axis=-1)
        hn = modulate(rmsnorm(x, norm1_g[i]), sh1, sc1)
        j = i // N_MIXERS
        if i % N_MIXERS == 0:
            out = s5_mixer(hn, ssm_w_in[j], ssm_lam_re[j], ssm_lam_im[j], ssm_log_dt[j],
                           ssm_b_re[j], ssm_b_im[j], ssm_c_re[j], ssm_c_im[j],
                           ssm_d[j], ssm_w_glu[j])
        else:
            out = short_conv_mixer(hn, conv_w_in[j], conv_w[j], conv_w_out[j])
        x = x + (1 + g1)[:, None, :] * out
        hn = modulate(rmsnorm(x, norm2_g[i]), sh2, sc2)
        x = x + (1 + g2)[:, None, :] * hier_moe(hn, moe_wg[i], moe_bg[i], moe_we[i], moe_be[i],
                                                moe_w1[i], moe_w3[i], moe_w2[i])
    return rmsnorm(x, final_g)
```

```python
import functools
import math

import jax
import jax.numpy as jnp
from jax import lax
from jax.experimental import pallas as pl
from jax.experimental.pallas import tpu as pltpu

F32 = jnp.float32
BF16 = jnp.bfloat16
HIGHEST = lax.Precision.HIGHEST

RMS_EPS = 1e-6
SSM_GROUP = 16
SSM_CHUNK = 128
N_GROUPS = 4
EXPERTS_PER_GROUP = 8
N_EXPERTS = N_GROUPS * EXPERTS_PER_GROUP
TOP_K = 2
LANES = 128
TOKEN_TILE = 512
ROUTE_TILE = 256
EXPERT_TILE = 256
VMEM_LIMIT = 56 * 1024 * 1024
NEG_INF = -1e30


def _cparams(*sem):
    return pltpu.CompilerParams(dimension_semantics=sem, vmem_limit_bytes=VMEM_LIMIT)


def _adaln_kernel(c_ref, w_ref, b_ref, o_ref):
    c = c_ref[...]
    cond = c * jax.nn.sigmoid(c)
    o_ref[0] = jnp.dot(cond, w_ref[0], precision=HIGHEST, preferred_element_type=F32) + b_ref[0]


def _adaln(c8, ada_w, ada_b):
    depth, d, n = ada_w.shape
    tn = 1536
    return pl.pallas_call(
        _adaln_kernel,
        grid=(depth, n // tn),
        in_specs=[pl.BlockSpec((8, d), lambda i, j: (0, 0)),
                  pl.BlockSpec((1, d, tn), lambda i, j: (i, 0, j)),
                  pl.BlockSpec((1, 1, tn), lambda i, j: (i, 0, j))],
        out_specs=pl.BlockSpec((1, 8, tn), lambda i, j: (i, 0, j)),
        out_shape=jax.ShapeDtypeStruct((depth, 8, n), F32),
        compiler_params=_cparams("parallel", "parallel"),
        name="adaln",
    )(c8, ada_w, ada_b.reshape(depth, 1, n))


def _norm_mod(x, g, shift, scale):
    y = x * lax.rsqrt(jnp.mean(x * x, axis=-1, keepdims=True) + RMS_EPS)
    return (y * g) * (1.0 + scale) + shift


def _route(hn, wr_ref, br_ref, eid_ref, wts_ref):
    logits = jnp.dot(hn, wr_ref[...], precision=HIGHEST, preferred_element_type=F32) + br_ref[...]
    lane = lax.broadcasted_iota(jnp.int32, logits.shape, 1)
    is_grp = (lane >= N_EXPERTS) & (lane < N_EXPERTS + N_GROUPS)
    lg = jnp.where(is_grp, logits, NEG_INF)
    gmax = jnp.max(lg, axis=-1, keepdims=True)
    gsum = jnp.sum(jnp.where(is_grp, jnp.exp(lg - gmax), 0.0), axis=-1, keepdims=True)
    gp = 1.0 / gsum
    gi = jnp.min(jnp.where(lg == gmax, lane, 2 * LANES), axis=-1, keepdims=True) - N_EXPERTS
    in_grp = (lane < N_EXPERTS) & ((lane // EXPERTS_PER_GROUP) == gi)
    le = jnp.where(in_grp, logits, NEG_INF)
    v1 = jnp.max(le, axis=-1, keepdims=True)
    i1 = jnp.min(jnp.where(le == v1, lane, 2 * LANES), axis=-1, keepdims=True)
    le2 = jnp.where(lane == i1, NEG_INF, le)
    v2 = jnp.max(le2, axis=-1, keepdims=True)
    i2 = jnp.min(jnp.where(le2 == v2, lane, 2 * LANES), axis=-1, keepdims=True)
    e2 = jnp.exp(v2 - v1)
    den = 1.0 + e2
    col = lax.broadcasted_iota(jnp.int32, (hn.shape[0], TOP_K), 1)
    eid_ref[...] = jnp.where(col == 0, i1, i2)
    wts_ref[...] = jnp.where(col == 0, gp / den, gp * e2 / den)


def _residual_and_route(x, out, mod, g2_ref, wr_ref, br_ref, x_out_ref, hn_ref, eid_ref, wts_ref):
    x1 = x + (1.0 + mod[2:3]) * out
    x_out_ref[0] = x1
    hn = _norm_mod(x1, g2_ref[...], mod[3:4], mod[4:5])
    hn_ref[...] = hn
    _route(hn, wr_ref, br_ref, eid_ref, wts_ref)


def _s5_in_kernel(x_ref, mod_ref, g_ref, w_ref, u_ref):
    mod = mod_ref[0]
    hn = _norm_mod(x_ref[0], g_ref[...], mod[0:1], mod[1:2])
    u_ref[0] = jnp.dot(hn.astype(BF16), w_ref[...], preferred_element_type=F32)


def _s5_in(x, mod, g, w_bf):
    b, l, d = x.shape
    h = w_bf.shape[1]
    return pl.pallas_call(
        _s5_in_kernel,
        grid=(b, l // TOKEN_TILE),
        in_specs=[pl.BlockSpec((1, TOKEN_TILE, d), lambda i, j: (i, j, 0)),
                  pl.BlockSpec((1, 8, d), lambda i, j: (i, 0, 0)),
                  pl.BlockSpec((1, d), lambda i, j: (0, 0)),
                  pl.BlockSpec((d, h), lambda i, j: (0, 0))],
        out_specs=pl.BlockSpec((1, TOKEN_TILE, h), lambda i, j: (i, j, 0)),
        out_shape=jax.ShapeDtypeStruct((b, l, h), F32),
        compiler_params=_cparams("parallel", "parallel"),
        name="s5_in",
    )(x, mod, g, w_bf)


def _s5_scan_kernel(x_ref, m_ref, f_ref, e_ref, a_ref, d_ref, y_ref, s_ref, sw_ref, sp_ref, *, n_chunks, bsz):
    x = x_ref[0]
    xb = x.astype(BF16)
    s_loc = jnp.dot(xb, f_ref[0], preferred_element_type=F32)
    s_ref[...] = s_loc
    half = s_loc.shape[1] // 2
    sw_ref[...] = jnp.concatenate([s_loc[:, half:], s_loc[:, :half]], axis=1)
    a1 = a_ref[0, 0:1, :]
    a2 = a_ref[0, 1:2, :]
    s = jnp.zeros((bsz, s_loc.shape[1]), F32)
    sw = jnp.zeros((bsz, s_loc.shape[1]), F32)
    for c in range(n_chunks):
        rows = pl.ds(c * bsz, bsz)
        sp_ref[rows, :] = s
        s_new = a1 * s + a2 * sw + s_ref[rows, :]
        sw = a1 * sw - a2 * s + sw_ref[rows, :]
        s = s_new
    y = jnp.dot(xb, m_ref[0], preferred_element_type=F32)
    y = y + jnp.dot(sp_ref[...].astype(BF16), e_ref[0], preferred_element_type=F32)
    y_ref[0] = y + d_ref[0] * x


def _s5_scan(xg, m_bf, f_bf, e_bf, a_pack, d_row, bsz):
    g, rows, k = xg.shape
    p2 = f_bf.shape[2]
    return pl.pallas_call(
        functools.partial(_s5_scan_kernel, n_chunks=rows // bsz, bsz=bsz),
        grid=(g,),
        in_specs=[pl.BlockSpec((1, rows, k), lambda i: (i, 0, 0)),
                  pl.BlockSpec((1, k, k), lambda i: (i, 0, 0)),
                  pl.BlockSpec((1, k, p2), lambda i: (i, 0, 0)),
                  pl.BlockSpec((1, p2, k), lambda i: (i, 0, 0)),
                  pl.BlockSpec((1, 8, p2), lambda i: (i, 0, 0)),
                  pl.BlockSpec((1, 1, k), lambda i: (i, 0, 0))],
        out_specs=pl.BlockSpec((1, rows, k), lambda i: (i, 0, 0)),
        out_shape=jax.ShapeDtypeStruct((g, rows, k), F32),
        scratch_shapes=[pltpu.VMEM((rows, p2), F32), pltpu.VMEM((rows, p2), F32), pltpu.VMEM((rows, p2), F32)],
        compiler_params=_cparams("parallel"),
        name="s5_scan",
    )(xg, m_bf, f_bf, e_bf, a_pack, d_row)


def _s5_tables(lam_re, lam_im, log_dt, b_re, b_im, c_re, c_im, d_skip):
    g, p = lam_re.shape
    k = SSM_GROUP
    tc = SSM_CHUNK
    dt = jnp.exp(log_dt)[:, None]
    mag = jnp.exp(lam_re * dt)
    ab_re = mag * jnp.cos(lam_im * dt)
    ab_im = mag * jnp.sin(lam_im * dt)
    den = lam_re * lam_re + lam_im * lam_im
    cf_re = ((ab_re - 1) * lam_re + ab_im * lam_im) / den
    cf_im = (ab_im * lam_re - (ab_re - 1) * lam_im) / den
    bb_re = cf_re[..., None] * b_re - cf_im[..., None] * b_im
    bb_im = cf_re[..., None] * b_im + cf_im[..., None] * b_re
    lags = jnp.arange(tc + 1, dtype=F32)[:, None, None]
    pmag = jnp.exp(lags * (lam_re * dt)[None])
    pang = lags * (lam_im * dt)[None]
    p_re = pmag * jnp.cos(pang)
    p_im = pmag * jnp.sin(pang)
    cb_re = jnp.einsum('gjp,gpi->gpij', c_re, bb_re) - jnp.einsum('gjp,gpi->gpij', c_im, bb_im)
    cb_im = jnp.einsum('gjp,gpi->gpij', c_re, bb_im) + jnp.einsum('gjp,gpi->gpij', c_im, bb_re)
    kern = (jnp.einsum('lgp,gpij->gijl', p_re[:tc], cb_re, precision=HIGHEST)
            - jnp.einsum('lgp,gpij->gijl', p_im[:tc], cb_im, precision=HIGHEST))
    s_idx = jnp.arange(tc)[:, None]
    t_idx = jnp.arange(tc)[None, :]
    lag = jnp.clip(t_idx - s_idx, 0, tc - 1)
    m5 = jnp.where((t_idx >= s_idx)[None, None, None], kern[:, :, :, lag], 0.0)
    m_mat = m5.transpose(0, 1, 3, 2, 4).reshape(g, k * tc, k * tc).astype(BF16)
    pr = p_re[:tc][::-1]
    pi = p_im[:tc][::-1]
    f_re = jnp.einsum('sgp,gpi->gisp', pr, bb_re) - jnp.einsum('sgp,gpi->gisp', pi, bb_im)
    f_im = jnp.einsum('sgp,gpi->gisp', pr, bb_im) + jnp.einsum('sgp,gpi->gisp', pi, bb_re)
    f_mat = jnp.concatenate([f_re, f_im], axis=-1).reshape(g, k * tc, 2 * p).astype(BF16)
    qr = p_re[1:]
    qi = p_im[1:]
    e_re = jnp.einsum('gjp,tgp->gpjt', c_re, qr) - jnp.einsum('gjp,tgp->gpjt', c_im, qi)
    e_im = -(jnp.einsum('gjp,tgp->gpjt', c_re, qi) + jnp.einsum('gjp,tgp->gpjt', c_im, qr))
    e_mat = jnp.concatenate([e_re, e_im], axis=1).reshape(g, 2 * p, k * tc).astype(BF16)
    ar = p_re[tc]
    ai = p_im[tc]
    a1 = jnp.concatenate([ar, ar], axis=-1)
    a2 = jnp.concatenate([-ai, ai], axis=-1)
    a_pack = jnp.concatenate([a1[:, None], a2[:, None], jnp.zeros((g, 6, 2 * p), F32)], axis=1)
    d_row = jnp.repeat(d_skip.reshape(g, k), tc, axis=1).reshape(g, 1, k * tc)
    return m_mat, f_mat, e_mat, a_pack, d_row


def _gelu_tanh(x):
    return 0.5 * x * (1.0 + jnp.tanh(math.sqrt(2.0 / math.pi) * (x + 0.044715 * (x * x * x))))


def _s5_out_kernel(y_ref, x_ref, mod_ref, w_ref, g2_ref, wr_ref, br_ref,
                   x_out_ref, hn_ref, eid_ref, wts_ref):
    mod = mod_ref[0]
    yg = _gelu_tanh(y_ref[0]).astype(BF16)
    o = jnp.dot(yg, w_ref[...], preferred_element_type=F32)
    d = o.shape[1] // 2
    out = o[:, :d] * jax.nn.sigmoid(o[:, d:])
    _residual_and_route(x_ref[0], out, mod, g2_ref, wr_ref, br_ref, x_out_ref, hn_ref, eid_ref, wts_ref)


def _mixer_out_specs(b, l, d):
    nt = l // TOKEN_TILE
    specs = [pl.BlockSpec((1, TOKEN_TILE, d), lambda i, j: (i, j, 0)),
             pl.BlockSpec((TOKEN_TILE, d), lambda i, j: (i * nt + j, 0)),
             pl.BlockSpec((TOKEN_TILE, TOP_K), lambda i, j: (i * nt + j, 0)),
             pl.BlockSpec((TOKEN_TILE, TOP_K), lambda i, j: (i * nt + j, 0))]
    shapes = [jax.ShapeDtypeStruct((b, l, d), F32),
              jax.ShapeDtypeStruct((b * l, d), F32),
              jax.ShapeDtypeStruct((b * l, TOP_K), jnp.int32),
              jax.ShapeDtypeStruct((b * l, TOP_K), F32)]
    return specs, shapes


def _s5_out(y, x, mod, w_bf, g2, wr, br):
    b, l, d = x.shape
    specs, shapes = _mixer_out_specs(b, l, d)
    return pl.pallas_call(
        _s5_out_kernel,
        grid=(b, l // TOKEN_TILE),
        in_specs=[pl.BlockSpec((1, TOKEN_TILE, d), lambda i, j: (i, j, 0)),
                  pl.BlockSpec((1, TOKEN_TILE, d), lambda i, j: (i, j, 0)),
                  pl.BlockSpec((1, 8, d), lambda i, j: (i, 0, 0)),
                  pl.BlockSpec(w_bf.shape, lambda i, j: (0, 0)),
                  pl.BlockSpec((1, d), lambda i, j: (0, 0)),
                  pl.BlockSpec((d, LANES), lambda i, j: (0, 0)),
                  pl.BlockSpec((1, LANES), lambda i, j: (0, 0))],
        out_specs=specs,
        out_shape=shapes,
        compiler_params=_cparams("parallel", "parallel"),
        name="s5_out",
    )(y, x, mod, w_bf, g2, wr, br)


def _conv_mixer_kernel(x_ref, mod_ref, g1_ref, win_ref, cw_ref, wout_ref, g2_ref, wr_ref, br_ref,
                       x_out_ref, hn_ref, eid_ref, wts_ref, carry_ref):
    @pl.when(pl.program_id(1) == 0)
    def _():
        carry_ref[...] = jnp.zeros_like(carry_ref)

    mod = mod_ref[0]
    x = x_ref[0]
    hn = _norm_mod(x, g1_ref[...], mod[0:1], mod[1:2])
    p = jnp.dot(hn.astype(BF16), win_ref[...], preferred_element_type=F32)
    d = p.shape[1] // 3
    b_g = p[:, :d]
    u = p[:, d:2 * d] * p[:, 2 * d:]
    ext = jnp.concatenate([carry_ref[...], u], axis=0)
    t = u.shape[0]
    cw = cw_ref[...]
    z = cw[0:1] * ext[6:6 + t] + cw[1:2] * ext[7:7 + t] + cw[2:3] * u
    carry_ref[...] = u[t - 8:]
    out = jnp.dot((b_g * z).astype(BF16), wout_ref[...], preferred_element_type=F32)
    _residual_and_route(x, out, mod, g2_ref, wr_ref, br_ref, x_out_ref, hn_ref, eid_ref, wts_ref)


def _conv_mixer(x, mod, g1, win_bf, cw8, wout_bf, g2, wr, br):
    b, l, d = x.shape
    specs, shapes = _mixer_out_specs(b, l, d)
    return pl.pallas_call(
        _conv_mixer_kernel,
        grid=(b, l // TOKEN_TILE),
        in_specs=[pl.BlockSpec((1, TOKEN_TILE, d), lambda i, j: (i, j, 0)),
                  pl.BlockSpec((1, 8, d), lambda i, j: (i, 0, 0)),
                  pl.BlockSpec((1, d), lambda i, j: (0, 0)),
                  pl.BlockSpec(win_bf.shape, lambda i, j: (0, 0)),
                  pl.BlockSpec((8, d), lambda i, j: (0, 0)),
                  pl.BlockSpec(wout_bf.shape, lambda i, j: (0, 0)),
                  pl.BlockSpec((1, d), lambda i, j: (0, 0)),
                  pl.BlockSpec((d, LANES), lambda i, j: (0, 0)),
                  pl.BlockSpec((1, LANES), lambda i, j: (0, 0))],
        out_specs=specs,
        out_shape=shapes,
        scratch_shapes=[pltpu.VMEM((8, d), F32)],
        compiler_params=_cparams("parallel", "arbitrary"),
        name="conv_mixer",
    )(x, mod, g1, win_bf, cw8, wout_bf, g2, wr, br)


def _moe_rank_kernel(eid_ref, rank_ref, cnt_ref, run_ref):
    @pl.when(pl.program_id(0) == 0)
    def _():
        run_ref[...] = jnp.zeros_like(run_ref)

    eid = eid_ref[...]
    t = eid.shape[0]
    lane = lax.broadcasted_iota(jnp.int32, (t, LANES), 1)
    hit0 = lane == eid[:, 0:1]
    hit1 = lane == eid[:, 1:2]
    onehot = (hit0 | hit1).astype(BF16)
    r = lax.broadcasted_iota(jnp.int32, (t, t), 0)
    c = lax.broadcasted_iota(jnp.int32, (t, t), 1)
    tri = (c < r).astype(BF16)
    before = jnp.dot(tri, onehot, preferred_element_type=F32) + run_ref[0:1, :]
    col = lax.broadcasted_iota(jnp.int32, (t, TOP_K), 1)
    r0 = jnp.sum(jnp.where(hit0, before, 0.0), axis=-1, keepdims=True)
    r1 = jnp.sum(jnp.where(hit1, before, 0.0), axis=-1, keepdims=True)
    rank_ref[...] = jnp.where(col == 0, r0, r1).astype(jnp.int32)
    run_ref[0:1, :] = run_ref[0:1, :] + jnp.sum(onehot.astype(F32), axis=0, keepdims=True)
    cnt_ref[...] = run_ref[...]


def _moe_rank(eid):
    t = eid.shape[0]
    return pl.pallas_call(
        _moe_rank_kernel,
        grid=(t // ROUTE_TILE,),
        in_specs=[pl.BlockSpec((ROUTE_TILE, TOP_K), lambda i: (i, 0))],
        out_specs=[pl.BlockSpec((ROUTE_TILE, TOP_K), lambda i: (i, 0)),
                   pl.BlockSpec((8, LANES), lambda i: (0, 0))],
        out_shape=[jax.ShapeDtypeStruct((t, TOP_K), jnp.int32),
                   jax.ShapeDtypeStruct((8, LANES), F32)],
        scratch_shapes=[pltpu.VMEM((8, LANES), F32)],
        compiler_params=_cparams("arbitrary"),
        name="moe_rank",
    )(eid)


def _row_copy(src, s_row, dst, d_row, sem):
    return pltpu.make_async_copy(src.at[pl.ds(s_row, 1), :], dst.at[pl.ds(d_row, 1), :], sem)


def _moe_dispatch_kernel(pos_ref, hn_ref, xs_in_ref, xs_ref, sem):
    del xs_in_ref
    t = hn_ref.shape[0]

    def issue(r, carry):
        for k in range(TOP_K):
            _row_copy(hn_ref, r, xs_ref, pos_ref[TOP_K * r + k], sem).start()
        return carry

    lax.fori_loop(0, t, issue, 0)

    def drain(r, carry):
        for k in range(TOP_K):
            _row_copy(hn_ref, r, xs_ref, pos_ref[TOP_K * r + k], sem).wait()
        return carry

    lax.fori_loop(0, t, drain, 0)


def _moe_dispatch(pos_flat, hn, xs_init):
    t, d = hn.shape
    return pl.pallas_call(
        _moe_dispatch_kernel,
        grid=(t // ROUTE_TILE,),
        in_specs=[pl.BlockSpec((TOP_K * ROUTE_TILE,), lambda i: (i,), memory_space=pltpu.SMEM),
                  pl.BlockSpec((ROUTE_TILE, d), lambda i: (i, 0)),
                  pl.BlockSpec(memory_space=pl.ANY)],
        out_specs=pl.BlockSpec(memory_space=pl.ANY),
        out_shape=jax.ShapeDtypeStruct(xs_init.shape, xs_init.dtype),
        scratch_shapes=[pltpu.SemaphoreType.DMA(())],
        input_output_aliases={2: 0},
        compiler_params=_cparams("arbitrary"),
        name="moe_dispatch",
    )(pos_flat, hn, xs_init)


def _moe_experts_kernel(te_ref, tv_ref, xs_ref, w1_ref, w3_ref, w2_ref, o_ref):
    i = pl.program_id(0)

    @pl.when(tv_ref[i] != 0)
    def _():
        xb = xs_ref[...].astype(BF16)
        a = jnp.dot(xb, w1_ref[0], preferred_element_type=F32)
        b = jnp.dot(xb, w3_ref[0], preferred_element_type=F32)
        h = (a * jax.nn.sigmoid(a) * b).astype(BF16)
        o_ref[...] = jnp.dot(h, w2_ref[0], preferred_element_type=F32)

    @pl.when(tv_ref[i] == 0)
    def _():
        o_ref[...] = jnp.zeros_like(o_ref)


def _moe_experts(tile_expert, tile_valid, xs, w1_bf, w3_bf, w2_bf):
    r, d = xs.shape
    de = w1_bf.shape[2]
    grid_spec = pltpu.PrefetchScalarGridSpec(
        num_scalar_prefetch=2,
        grid=(r // EXPERT_TILE,),
        in_specs=[pl.BlockSpec((EXPERT_TILE, d), lambda i, te, tv: (i, 0)),
                  pl.BlockSpec((1, d, de), lambda i, te, tv: (te[i], 0, 0)),
                  pl.BlockSpec((1, d, de), lambda i, te, tv: (te[i], 0, 0)),
                  pl.BlockSpec((1, de, d), lambda i, te, tv: (te[i], 0, 0))],
        out_specs=pl.BlockSpec((EXPERT_TILE, d), lambda i, te, tv: (i, 0)),
    )
    return pl.pallas_call(
        _moe_experts_kernel,
        grid_spec=grid_spec,
        out_shape=jax.ShapeDtypeStruct((r, d), F32),
        compiler_params=_cparams("arbitrary"),
        name="moe_experts",
    )(tile_expert, tile_valid, xs, w1_bf, w3_bf, w2_bf)


def _moe_combine_kernel(pos_ref, x_ref, wts_ref, mod_ref, fg_ref, o_hbm_ref, out_ref, buf_ref, sem, *, final_norm):
    t = x_ref.shape[1]

    def issue(r, carry):
        for k in range(TOP_K):
            _row_copy(o_hbm_ref, pos_ref[TOP_K * r + k], buf_ref.at[k], r, sem).start()
        return carry

    lax.fori_loop(0, t, issue, 0)

    def drain(r, carry):
        for k in range(TOP_K):
            _row_copy(o_hbm_ref, pos_ref[TOP_K * r + k], buf_ref.at[k], r, sem).wait()
        return carry

    lax.fori_loop(0, t, drain, 0)
    wts = wts_ref[...]
    y = wts[:, 0:1] * buf_ref[0] + wts[:, 1:2] * buf_ref[1]
    x2 = x_ref[0] + (1.0 + mod_ref[0, 5:6]) * y
    if final_norm:
        x2 = (x2 * lax.rsqrt(jnp.mean(x2 * x2, axis=-1, keepdims=True) + RMS_EPS)) * fg_ref[...]
    out_ref[0] = x2


def _moe_combine(pos_flat, x, wts, mod, final_g, o_sorted, final_norm):
    b, l, d = x.shape
    nt = l // ROUTE_TILE
    return pl.pallas_call(
        functools.partial(_moe_combine_kernel, final_norm=final_norm),
        grid=(b, nt),
        in_specs=[pl.BlockSpec((TOP_K * ROUTE_TILE,), lambda i, j: (i * nt + j,), memory_space=pltpu.SMEM),
                  pl.BlockSpec((1, ROUTE_TILE, d), lambda i, j: (i, j, 0)),
                  pl.BlockSpec((ROUTE_TILE, TOP_K), lambda i, j: (i * nt + j, 0)),
                  pl.BlockSpec((1, 8, d), lambda i, j: (i, 0, 0)),
                  pl.BlockSpec((1, d), lambda i, j: (0, 0)),
                  pl.BlockSpec(memory_space=pl.ANY)],
        out_specs=pl.BlockSpec((1, ROUTE_TILE, d), lambda i, j: (i, j, 0)),
        out_shape=jax.ShapeDtypeStruct((b, l, d), F32),
        scratch_shapes=[pltpu.VMEM((TOP_K, ROUTE_TILE, d), F32), pltpu.SemaphoreType.DMA(())],
        compiler_params=_cparams("arbitrary", "arbitrary"),
        name="moe_combine",
    )(pos_flat, x, wts, mod, final_g, o_sorted)


def _moe(x, hn, eid, wts, mod, w1_bf, w3_bf, w2_bf, final_g, final_norm):
    t, d = hn.shape
    rank, cnt = _moe_rank(eid)
    counts = cnt[0, :N_EXPERTS].astype(jnp.int32)
    n_tiles_e = (counts + EXPERT_TILE - 1) // EXPERT_TILE
    tile_end = jnp.cumsum(n_tiles_e)
    row_off = (tile_end - n_tiles_e) * EXPERT_TILE
    pos = (row_off[eid] + rank).reshape(-1)
    max_tiles = (t * TOP_K) // EXPERT_TILE + N_EXPERTS
    tile_ids = jnp.arange(max_tiles, dtype=jnp.int32)
    tile_expert = jnp.minimum(jnp.searchsorted(tile_end, tile_ids, side='right'), N_EXPERTS - 1).astype(jnp.int32)
    tile_valid = (tile_ids < tile_end[-1]).astype(jnp.int32)
    xs = _moe_dispatch(pos, hn, jnp.zeros((max_tiles * EXPERT_TILE, d), F32))
    o_sorted = _moe_experts(tile_expert, tile_valid, xs, w1_bf, w3_bf, w2_bf)
    return _moe_combine(pos, x, wts, mod, final_g, o_sorted, final_norm)


def _router_pack(wg, bg, we, be):
    d = wg.shape[0]
    wr = jnp.zeros((d, LANES), F32).at[:, :N_EXPERTS].set(we).at[:, N_EXPERTS:N_EXPERTS + N_GROUPS].set(wg)
    br = jnp.zeros((1, LANES), F32).at[0, :N_EXPERTS].set(be).at[0, N_EXPERTS:N_EXPERTS + N_GROUPS].set(bg)
    return wr, br


def kernel(x, c, ada_w, ada_b, norm1_g, norm2_g, ssm_w_in, ssm_lam_re, ssm_lam_im, ssm_log_dt, ssm_b_re, ssm_b_im, ssm_c_re, ssm_c_im, ssm_d, ssm_w_glu, conv_w_in, conv_w, conv_w_out, moe_wg, moe_bg, moe_we, moe_be, moe_w1, moe_w3, moe_w2, final_g):
    b, l, d = x.shape
    depth = ada_w.shape[0]
    c8 = jnp.zeros((8, d), F32).at[:b].set(c)
    mod_all = _adaln(c8, ada_w, ada_b)[:, :b].reshape(depth, b, 6, d)
    mod_all = jnp.concatenate([mod_all, jnp.zeros((depth, b, 2, d), F32)], axis=2)
    fg = final_g.reshape(1, d)

    mod = mod_all[0]
    u = _s5_in(x, mod, norm1_g[0:1], ssm_w_in[0].astype(BF16))
    m_mat, f_mat, e_mat, a_pack, d_row = _s5_tables(
        ssm_lam_re[0], ssm_lam_im[0], ssm_log_dt[0], ssm_b_re[0], ssm_b_im[0],
        ssm_c_re[0], ssm_c_im[0], ssm_d[0])
    g = d // SSM_GROUP
    nc = l // SSM_CHUNK
    xg = (u.reshape(b, nc, SSM_CHUNK, g, SSM_GROUP).transpose(3, 1, 0, 4, 2)
          .reshape(g, nc * b, SSM_GROUP * SSM_CHUNK))
    yg = _s5_scan(xg, m_mat, f_mat, e_mat, a_pack, d_row, b)
    y = (yg.reshape(g, nc, b, SSM_GROUP, SSM_CHUNK).transpose(2, 1, 4, 0, 3).reshape(b, l, d))
    wr, br = _router_pack(moe_wg[0], moe_bg[0], moe_we[0], moe_be[0])
    x1, hn, eid, wts = _s5_out(y, x, mod, ssm_w_glu[0].astype(BF16), norm2_g[0:1], wr, br)
    x2 = _moe(x1, hn, eid, wts, mod, moe_w1[0].astype(BF16), moe_w3[0].astype(BF16),
              moe_w2[0].astype(BF16), fg, False)

    mod = mod_all[1]
    cw8 = jnp.zeros((8, d), F32).at[:conv_w.shape[1]].set(conv_w[0])
    wr, br = _router_pack(moe_wg[1], moe_bg[1], moe_we[1], moe_be[1])
    x3, hn, eid, wts = _conv_mixer(x2, mod, norm1_g[1:2], conv_w_in[0].astype(BF16), cw8,
                                   conv_w_out[0].astype(BF16), norm2_g[1:2], wr, br)
    return _moe(x3, hn, eid, wts, mod, moe_w1[1].astype(BF16), moe_w3[1].astype(BF16),
                moe_w2[1].astype(BF16), fg, True)
```

```python
import functools
import math

import jax
import jax.numpy as jnp
from jax import lax
from jax.experimental import pallas as pl
from jax.experimental.pallas import tpu as pltpu

F32 = jnp.float32
BF16 = jnp.bfloat16
HIGHEST = lax.Precision.HIGHEST

RMS_EPS = 1e-6
SSM_GROUP = 16
SSM_CHUNK = 128
N_GROUPS = 4
EXPERTS_PER_GROUP = 8
N_EXPERTS = N_GROUPS * EXPERTS_PER_GROUP
TOP_K = 2
LANES = 128
TOKEN_TILE = 512
S5_TILE = 1024
ROUTE_TILE = 256
EXPERT_TILE = 256
VMEM_LIMIT = 56 * 1024 * 1024
NEG_INF = -1e30


def _cparams(*sem):
    return pltpu.CompilerParams(dimension_semantics=sem, vmem_limit_bytes=VMEM_LIMIT)


def _adaln_kernel(c_ref, w_ref, b_ref, o_ref):
    c = c_ref[...]
    cond = c * jax.nn.sigmoid(c)
    o_ref[0] = jnp.dot(cond, w_ref[0], precision=HIGHEST, preferred_element_type=F32) + b_ref[0]


def _adaln(c8, ada_w, ada_b):
    depth, d, n = ada_w.shape
    tn = 1536
    return pl.pallas_call(
        _adaln_kernel,
        grid=(depth, n // tn),
        in_specs=[pl.BlockSpec((8, d), lambda i, j: (0, 0)),
                  pl.BlockSpec((1, d, tn), lambda i, j: (i, 0, j)),
                  pl.BlockSpec((1, 1, tn), lambda i, j: (i, 0, j))],
        out_specs=pl.BlockSpec((1, 8, tn), lambda i, j: (i, 0, j)),
        out_shape=jax.ShapeDtypeStruct((depth, 8, n), F32),
        compiler_params=_cparams("parallel", "parallel"),
        name="adaln",
    )(c8, ada_w, ada_b.reshape(depth, 1, n))


def _norm_mod(x, g, shift, scale):
    y = x * lax.rsqrt(jnp.mean(x * x, axis=-1, keepdims=True) + RMS_EPS)
    return (y * g) * (1.0 + scale) + shift


def _route(hn, wr_ref, br_ref, eid_ref, wts_ref):
    logits = jnp.dot(hn, wr_ref[...], precision=HIGHEST, preferred_element_type=F32) + br_ref[...]
    lane = lax.broadcasted_iota(jnp.int32, logits.shape, 1)
    is_grp = (lane >= N_EXPERTS) & (lane < N_EXPERTS + N_GROUPS)
    lg = jnp.where(is_grp, logits, NEG_INF)
    gmax = jnp.max(lg, axis=-1, keepdims=True)
    gsum = jnp.sum(jnp.where(is_grp, jnp.exp(lg - gmax), 0.0), axis=-1, keepdims=True)
    gp = 1.0 / gsum
    gi = jnp.min(jnp.where(lg == gmax, lane, 2 * LANES), axis=-1, keepdims=True) - N_EXPERTS
    in_grp = (lane < N_EXPERTS) & ((lane // EXPERTS_PER_GROUP) == gi)
    le = jnp.where(in_grp, logits, NEG_INF)
    v1 = jnp.max(le, axis=-1, keepdims=True)
    i1 = jnp.min(jnp.where(le == v1, lane, 2 * LANES), axis=-1, keepdims=True)
    le2 = jnp.where(lane == i1, NEG_INF, le)
    v2 = jnp.max(le2, axis=-1, keepdims=True)
    i2 = jnp.min(jnp.where(le2 == v2, lane, 2 * LANES), axis=-1, keepdims=True)
    e2 = jnp.exp(v2 - v1)
    den = 1.0 + e2
    col = lax.broadcasted_iota(jnp.int32, (hn.shape[0], TOP_K), 1)
    eid_ref[...] = jnp.where(col == 0, i1, i2)
    wts_ref[...] = jnp.where(col == 0, gp / den, gp * e2 / den)


def _residual_and_route(x, out, mod, g2_ref, wr_ref, br_ref, x_out_ref, hn_ref, eid_ref, wts_ref):
    x1 = x + (1.0 + mod[2:3]) * out
    x_out_ref[0] = x1
    hn = _norm_mod(x1, g2_ref[...], mod[3:4], mod[4:5])
    hn_ref[...] = hn
    _route(hn, wr_ref, br_ref, eid_ref, wts_ref)


def _s5_in_kernel(x_ref, mod_ref, g_ref, wt_ref, u_ref):
    mod = mod_ref[0]
    hn = _norm_mod(x_ref[0], g_ref[...], mod[0:1], mod[1:2])
    ut = lax.dot_general(wt_ref[...], hn.astype(BF16), (((1,), (1,)), ((), ())), preferred_element_type=F32)
    u_ref[0] = ut.reshape(ut.shape[0], S5_TILE // SSM_CHUNK, SSM_CHUNK)


def _s5_in(x, mod, g, wt_bf):
    b, l, d = x.shape
    h = wt_bf.shape[0]
    cpt = S5_TILE // SSM_CHUNK
    return pl.pallas_call(
        _s5_in_kernel,
        grid=(b, l // S5_TILE),
        in_specs=[pl.BlockSpec((1, S5_TILE, d), lambda i, j: (i, j, 0)),
                  pl.BlockSpec((1, 8, d), lambda i, j: (i, 0, 0)),
                  pl.BlockSpec((1, d), lambda i, j: (0, 0)),
                  pl.BlockSpec((h, d), lambda i, j: (0, 0))],
        out_specs=pl.BlockSpec((1, h, cpt, SSM_CHUNK), lambda i, j: (i, 0, j, 0)),
        out_shape=jax.ShapeDtypeStruct((b, h, l // SSM_CHUNK, SSM_CHUNK), F32),
        compiler_params=_cparams("parallel", "parallel"),
        name="s5_in",
    )(x, mod, g, wt_bf)


def _s5_scan_kernel(u_ref, k_ref, f_ref, e_ref, a_ref, d_ref, y_ref, m_ref, s_ref, sw_ref, sp_ref):
    bsz, grp, n_chunks, tc = u_ref.shape
    srow = lax.broadcasted_iota(jnp.int32, (tc, tc), 0)
    tcol = lax.broadcasted_iota(jnp.int32, (tc, tc), 1)
    causal = tcol >= srow

    def build(i, carry):
        for j in range(grp):
            row = k_ref[0, pl.ds(i * grp + j, 1), :]
            blk = pltpu.roll(jnp.broadcast_to(row, (tc, tc)), 0, 1, stride=1, stride_axis=0)
            m_ref[pl.ds(pl.multiple_of(i * tc, tc), tc), j * tc:(j + 1) * tc] = (
                jnp.where(causal, blk, 0.0).astype(BF16))
        return carry

    lax.fori_loop(0, grp, build, 0)

    x = jnp.concatenate(
        [jnp.concatenate([u_ref[b, k] for k in range(grp)], axis=1) for b in range(bsz)], axis=0)
    xb = x.astype(BF16)
    s_loc = jnp.dot(xb, f_ref[0], preferred_element_type=F32)
    s_ref[...] = s_loc
    half = s_loc.shape[1] // 2
    sw_ref[...] = jnp.concatenate([s_loc[:, half:], s_loc[:, :half]], axis=1)
    a1 = a_ref[0, 0:1, :]
    a2 = a_ref[0, 1:2, :]
    s = [jnp.zeros((1, s_loc.shape[1]), F32) for _ in range(bsz)]
    sw = [jnp.zeros((1, s_loc.shape[1]), F32) for _ in range(bsz)]
    for c in range(n_chunks):
        for b in range(bsz):
            r = b * n_chunks + c
            sp_ref[r:r + 1, :] = s[b]
            s_new = a1 * s[b] + a2 * sw[b] + s_ref[r:r + 1, :]
            sw[b] = a1 * sw[b] - a2 * s[b] + sw_ref[r:r + 1, :]
            s[b] = s_new
    y = jnp.dot(xb, m_ref[...], preferred_element_type=F32)
    y = y + jnp.dot(sp_ref[...].astype(BF16), e_ref[0], preferred_element_type=F32)
    y = y + d_ref[0] * x
    for b in range(bsz):
        for j in range(grp):
            y_ref[b, j] = y[b * n_chunks:(b + 1) * n_chunks, j * tc:(j + 1) * tc]


def _s5_scan(u4, kern, f_bf, e_bf, a_pack, d_row):
    b, h, nc, tc = u4.shape
    g = kern.shape[0]
    grp = h // g
    k = grp * tc
    p2 = f_bf.shape[2]
    rows = b * nc
    return pl.pallas_call(
        _s5_scan_kernel,
        grid=(g,),
        in_specs=[pl.BlockSpec((b, grp, nc, tc), lambda i: (0, i, 0, 0)),
                  pl.BlockSpec((1, grp * grp, tc), lambda i: (i, 0, 0)),
                  pl.BlockSpec((1, k, p2), lambda i: (i, 0, 0)),
                  pl.BlockSpec((1, p2, k), lambda i: (i, 0, 0)),
                  pl.BlockSpec((1, 8, p2), lambda i: (i, 0, 0)),
                  pl.BlockSpec((1, 1, k), lambda i: (i, 0, 0))],
        out_specs=pl.BlockSpec((b, grp, nc, tc), lambda i: (0, i, 0, 0)),
        out_shape=jax.ShapeDtypeStruct(u4.shape, F32),
        scratch_shapes=[pltpu.VMEM((k, k), BF16), pltpu.VMEM((rows, p2), F32),
                        pltpu.VMEM((rows, p2), F32), pltpu.VMEM((rows, p2), F32)],
        compiler_params=_cparams("parallel"),
        name="s5_scan",
    )(u4, kern, f_bf, e_bf, a_pack, d_row)


def _s5_tables(lam_re, lam_im, log_dt, b_re, b_im, c_re, c_im, d_skip):
    g, p = lam_re.shape
    k = SSM_GROUP
    tc = SSM_CHUNK
    dt = jnp.exp(log_dt)[:, None]
    mag = jnp.exp(lam_re * dt)
    ab_re = mag * jnp.cos(lam_im * dt)
    ab_im = mag * jnp.sin(lam_im * dt)
    den = lam_re * lam_re + lam_im * lam_im
    cf_re = ((ab_re - 1) * lam_re + ab_im * lam_im) / den
    cf_im = (ab_im * lam_re - (ab_re - 1) * lam_im) / den
    bb_re = cf_re[..., None] * b_re - cf_im[..., None] * b_im
    bb_im = cf_re[..., None] * b_im + cf_im[..., None] * b_re
    lags = jnp.arange(tc + 1, dtype=F32)[:, None, None]
    pmag = jnp.exp(lags * (lam_re * dt)[None])
    pang = lags * (lam_im * dt)[None]
    p_re = pmag * jnp.cos(pang)
    p_im = pmag * jnp.sin(pang)
    cb_re = jnp.einsum('gjp,gpi->gpij', c_re, bb_re) - jnp.einsum('gjp,gpi->gpij', c_im, bb_im)
    cb_im = jnp.einsum('gjp,gpi->gpij', c_re, bb_im) + jnp.einsum('gjp,gpi->gpij', c_im, bb_re)
    kern = (jnp.einsum('lgp,gpij->gijl', p_re[:tc], cb_re, precision=HIGHEST)
            - jnp.einsum('lgp,gpij->gijl', p_im[:tc], cb_im, precision=HIGHEST))
    kern = kern.reshape(g, k * k, tc)
    pr = p_re[:tc][::-1]
    pi = p_im[:tc][::-1]
    f_re = jnp.einsum('sgp,gpi->gisp', pr, bb_re) - jnp.einsum('sgp,gpi->gisp', pi, bb_im)
    f_im = jnp.einsum('sgp,gpi->gisp', pr, bb_im) + jnp.einsum('sgp,gpi->gisp', pi, bb_re)
    f_mat = jnp.concatenate([f_re, f_im], axis=-1).reshape(g, k * tc, 2 * p).astype(BF16)
    qr = p_re[1:]
    qi = p_im[1:]
    e_re = jnp.einsum('gjp,tgp->gpjt', c_re, qr) - jnp.einsum('gjp,tgp->gpjt', c_im, qi)
    e_im = -(jnp.einsum('gjp,tgp->gpjt', c_re, qi) + jnp.einsum('gjp,tgp->gpjt', c_im, qr))
    e_mat = jnp.concatenate([e_re, e_im], axis=1).reshape(g, 2 * p, k * tc).astype(BF16)
    ar = p_re[tc]
    ai = p_im[tc]
    a1 = jnp.concatenate([ar, ar], axis=-1)
    a2 = jnp.concatenate([-ai, ai], axis=-1)
    a_pack = jnp.concatenate([a1[:, None], a2[:, None], jnp.zeros((g, 6, 2 * p), F32)], axis=1)
    d_row = jnp.repeat(d_skip.reshape(g, k), tc, axis=1).reshape(g, 1, k * tc)
    return kern, f_mat, e_mat, a_pack, d_row


def _gelu_tanh(x):
    return 0.5 * x * (1.0 + jnp.tanh(math.sqrt(2.0 / math.pi) * (x + 0.044715 * (x * x * x))))


def _s5_out_kernel(y_ref, x_ref, mod_ref, w_ref, g2_ref, wr_ref, br_ref,
                   x_out_ref, hn_ref, eid_ref, wts_ref):
    mod = mod_ref[0]
    h = y_ref.shape[1]
    yt = _gelu_tanh(y_ref[0].reshape(h, S5_TILE))
    d = w_ref.shape[1] // 2
    for part in range(S5_TILE // TOKEN_TILE):
        rows = slice(part * TOKEN_TILE, (part + 1) * TOKEN_TILE)
        o = lax.dot_general(yt[:, rows].astype(BF16), w_ref[...], (((0,), (0,)), ((), ())),
                            preferred_element_type=F32)
        out = o[:, :d] * jax.nn.sigmoid(o[:, d:])
        x1 = x_ref[0, rows, :] + (1.0 + mod[2:3]) * out
        x_out_ref[0, rows, :] = x1
        hn = _norm_mod(x1, g2_ref[...], mod[3:4], mod[4:5])
        hn_ref[rows, :] = hn
        _route(hn, wr_ref, br_ref, eid_ref.at[rows, :], wts_ref.at[rows, :])


def _mixer_out_specs(b, l, d, tile):
    nt = l // tile
    specs = [pl.BlockSpec((1, tile, d), lambda i, j: (i, j, 0)),
             pl.BlockSpec((tile, d), lambda i, j: (i * nt + j, 0)),
             pl.BlockSpec((tile, TOP_K), lambda i, j: (i * nt + j, 0)),
             pl.BlockSpec((tile, TOP_K), lambda i, j: (i * nt + j, 0))]
    shapes = [jax.ShapeDtypeStruct((b, l, d), F32),
              jax.ShapeDtypeStruct((b * l, d), F32),
              jax.ShapeDtypeStruct((b * l, TOP_K), jnp.int32),
              jax.ShapeDtypeStruct((b * l, TOP_K), F32)]
    return specs, shapes


def _s5_out(y4, x, mod, w_bf, g2, wr, br):
    b, l, d = x.shape
    h = y4.shape[1]
    specs, shapes = _mixer_out_specs(b, l, d, S5_TILE)
    return pl.pallas_call(
        _s5_out_kernel,
        grid=(b, l // S5_TILE),
        in_specs=[pl.BlockSpec((1, h, S5_TILE // SSM_CHUNK, SSM_CHUNK), lambda i, j: (i, 0, j, 0)),
                  pl.BlockSpec((1, S5_TILE, d), lambda i, j: (i, j, 0)),
                  pl.BlockSpec((1, 8, d), lambda i, j: (i, 0, 0)),
                  pl.BlockSpec(w_bf.shape, lambda i, j: (0, 0), pipeline_mode=pl.Buffered(1)),
                  pl.BlockSpec((1, d), lambda i, j: (0, 0)),
                  pl.BlockSpec((d, LANES), lambda i, j: (0, 0)),
                  pl.BlockSpec((1, LANES), lambda i, j: (0, 0))],
        out_specs=specs,
        out_shape=shapes,
        compiler_params=_cparams("parallel", "parallel"),
        name="s5_out",
    )(y4, x, mod, w_bf, g2, wr, br)


def _conv_mixer_kernel(x_ref, mod_ref, g1_ref, win_ref, cw_ref, wout_ref, g2_ref, wr_ref, br_ref,
                       x_out_ref, hn_ref, eid_ref, wts_ref, carry_ref):
    @pl.when(pl.program_id(1) == 0)
    def _():
        carry_ref[...] = jnp.zeros_like(carry_ref)

    mod = mod_ref[0]
    x = x_ref[0]
    hn = _norm_mod(x, g1_ref[...], mod[0:1], mod[1:2])
    p = jnp.dot(hn.astype(BF16), win_ref[...], preferred_element_type=F32)
    d = p.shape[1] // 3
    b_g = p[:, :d]
    u = p[:, d:2 * d] * p[:, 2 * d:]
    ext = jnp.concatenate([carry_ref[...], u], axis=0)
    t = u.shape[0]
    cw = cw_ref[...]
    z = cw[0:1] * ext[6:6 + t] + cw[1:2] * ext[7:7 + t] + cw[2:3] * u
    carry_ref[...] = u[t - 8:]
    out = jnp.dot((b_g * z).astype(BF16), wout_ref[...], preferred_element_type=F32)
    _residual_and_route(x, out, mod, g2_ref, wr_ref, br_ref, x_out_ref, hn_ref, eid_ref, wts_ref)


def _conv_mixer(x, mod, g1, win_bf, cw8, wout_bf, g2, wr, br):
    b, l, d = x.shape
    specs, shapes = _mixer_out_specs(b, l, d, TOKEN_TILE)
    return pl.pallas_call(
        _conv_mixer_kernel,
        grid=(b, l // TOKEN_TILE),
        in_specs=[pl.BlockSpec((1, TOKEN_TILE, d), lambda i, j: (i, j, 0)),
                  pl.BlockSpec((1, 8, d), lambda i, j: (i, 0, 0)),
                  pl.BlockSpec((1, d), lambda i, j: (0, 0)),
                  pl.BlockSpec(win_bf.shape, lambda i, j: (0, 0)),
                  pl.BlockSpec((8, d), lambda i, j: (0, 0)),
                  pl.BlockSpec(wout_bf.shape, lambda i, j: (0, 0)),
                  pl.BlockSpec((1, d), lambda i, j: (0, 0)),
                  pl.BlockSpec((d, LANES), lambda i, j: (0, 0)),
                  pl.BlockSpec((1, LANES), lambda i, j: (0, 0))],
        out_specs=specs,
        out_shape=shapes,
        scratch_shapes=[pltpu.VMEM((8, d), F32)],
        compiler_params=_cparams("parallel", "arbitrary"),
        name="conv_mixer",
    )(x, mod, g1, win_bf, cw8, wout_bf, g2, wr, br)


def _moe_rank_kernel(eid_ref, rank_ref, cnt_ref, run_ref):
    @pl.when(pl.program_id(0) == 0)
    def _():
        run_ref[...] = jnp.zeros_like(run_ref)

    eid = eid_ref[...]
    t = eid.shape[0]
    lane = lax.broadcasted_iota(jnp.int32, (t, LANES), 1)
    hit0 = lane == eid[:, 0:1]
    hit1 = lane == eid[:, 1:2]
    onehot = (hit0 | hit1).astype(BF16)
    r = lax.broadcasted_iota(jnp.int32, (t, t), 0)
    c = lax.broadcasted_iota(jnp.int32, (t, t), 1)
    tri = (c < r).astype(BF16)
    before = jnp.dot(tri, onehot, preferred_element_type=F32) + run_ref[0:1, :]
    col = lax.broadcasted_iota(jnp.int32, (t, TOP_K), 1)
    r0 = jnp.sum(jnp.where(hit0, before, 0.0), axis=-1, keepdims=True)
    r1 = jnp.sum(jnp.where(hit1, before, 0.0), axis=-1, keepdims=True)
    rank_ref[...] = jnp.where(col == 0, r0, r1).astype(jnp.int32)
    run_ref[0:1, :] = run_ref[0:1, :] + jnp.sum(onehot.astype(F32), axis=0, keepdims=True)
    cnt_ref[...] = run_ref[...]


def _moe_rank(eid):
    t = eid.shape[0]
    return pl.pallas_call(
        _moe_rank_kernel,
        grid=(t // ROUTE_TILE,),
        in_specs=[pl.BlockSpec((ROUTE_TILE, TOP_K), lambda i: (i, 0))],
        out_specs=[pl.BlockSpec((ROUTE_TILE, TOP_K), lambda i: (i, 0)),
                   pl.BlockSpec((8, LANES), lambda i: (0, 0))],
        out_shape=[jax.ShapeDtypeStruct((t, TOP_K), jnp.int32),
                   jax.ShapeDtypeStruct((8, LANES), F32)],
        scratch_shapes=[pltpu.VMEM((8, LANES), F32)],
        compiler_params=_cparams("arbitrary"),
        name="moe_rank",
    )(eid)


def _row_copy(src, s_row, dst, d_row, sem):
    return pltpu.make_async_copy(src.at[pl.ds(s_row, 1), :], dst.at[pl.ds(d_row, 1), :], sem)


def _moe_dispatch_kernel(pos_ref, hn_ref, xs_in_ref, xs_ref, sem):
    del xs_in_ref
    t = hn_ref.shape[0]

    def issue(r, carry):
        for k in range(TOP_K):
            _row_copy(hn_ref, r, xs_ref, pos_ref[TOP_K * r + k], sem).start()
        return carry

    lax.fori_loop(0, t, issue, 0)

    def drain(r, carry):
        for k in range(TOP_K):
            _row_copy(hn_ref, r, xs_ref, pos_ref[TOP_K * r + k], sem).wait()
        return carry

    lax.fori_loop(0, t, drain, 0)


def _moe_dispatch(pos_flat, hn, xs_init):
    t, d = hn.shape
    return pl.pallas_call(
        _moe_dispatch_kernel,
        grid=(t // ROUTE_TILE,),
        in_specs=[pl.BlockSpec((TOP_K * ROUTE_TILE,), lambda i: (i,), memory_space=pltpu.SMEM),
                  pl.BlockSpec((ROUTE_TILE, d), lambda i: (i, 0)),
                  pl.BlockSpec(memory_space=pl.ANY)],
        out_specs=pl.BlockSpec(memory_space=pl.ANY),
        out_shape=jax.ShapeDtypeStruct(xs_init.shape, xs_init.dtype),
        scratch_shapes=[pltpu.SemaphoreType.DMA(())],
        input_output_aliases={2: 0},
        compiler_params=_cparams("arbitrary"),
        name="moe_dispatch",
    )(pos_flat, hn, xs_init)


def _moe_experts_kernel(te_ref, tv_ref, tf_ref, xs_ref, w1_ref, w3_ref, w2_ref, o_ref, w1b_ref, w3b_ref, w2b_ref):
    i = pl.program_id(0)

    @pl.when(tf_ref[i] != 0)
    def _():
        w1b_ref[...] = w1_ref[0, 0].astype(BF16)
        w3b_ref[...] = w3_ref[0, 0].astype(BF16)
        w2b_ref[...] = w2_ref[0, 0].astype(BF16)

    @pl.when(tv_ref[i] != 0)
    def _():
        xb = xs_ref[...].astype(BF16)
        a = jnp.dot(xb, w1b_ref[...], preferred_element_type=F32)
        b = jnp.dot(xb, w3b_ref[...], preferred_element_type=F32)
        h = (a * jax.nn.sigmoid(a) * b).astype(BF16)
        o_ref[...] = jnp.dot(h, w2b_ref[...], preferred_element_type=F32)

    @pl.when(tv_ref[i] == 0)
    def _():
        o_ref[...] = jnp.zeros_like(o_ref)


def _moe_experts(tile_expert, tile_valid, tile_first, xs, w1, w3, w2, layer):
    r, d = xs.shape
    de = w1.shape[3]
    grid_spec = pltpu.PrefetchScalarGridSpec(
        num_scalar_prefetch=3,
        grid=(r // EXPERT_TILE,),
        in_specs=[pl.BlockSpec((EXPERT_TILE, d), lambda i, te, tv, tf: (i, 0)),
                  pl.BlockSpec((1, 1, d, de), lambda i, te, tv, tf: (layer, te[i], 0, 0)),
                  pl.BlockSpec((1, 1, d, de), lambda i, te, tv, tf: (layer, te[i], 0, 0)),
                  pl.BlockSpec((1, 1, de, d), lambda i, te, tv, tf: (layer, te[i], 0, 0))],
        out_specs=pl.BlockSpec((EXPERT_TILE, d), lambda i, te, tv, tf: (i, 0)),
        scratch_shapes=[pltpu.VMEM((d, de), BF16), pltpu.VMEM((d, de), BF16), pltpu.VMEM((de, d), BF16)],
    )
    return pl.pallas_call(
        _moe_experts_kernel,
        grid_spec=grid_spec,
        out_shape=jax.ShapeDtypeStruct((r, d), F32),
        compiler_params=_cparams("arbitrary"),
        name="moe_experts",
    )(tile_expert, tile_valid, tile_first, xs, w1, w3, w2)


def _moe_combine_kernel(pos_ref, x_ref, wts_ref, mod_ref, fg_ref, o_hbm_ref, out_ref, buf_ref, sem, *, final_norm):
    t = x_ref.shape[1]

    def issue(r, carry):
        for k in range(TOP_K):
            _row_copy(o_hbm_ref, pos_ref[TOP_K * r + k], buf_ref.at[k], r, sem).start()
        return carry

    lax.fori_loop(0, t, issue, 0)

    def drain(r, carry):
        for k in range(TOP_K):
            _row_copy(o_hbm_ref, pos_ref[TOP_K * r + k], buf_ref.at[k], r, sem).wait()
        return carry

    lax.fori_loop(0, t, drain, 0)
    wts = wts_ref[...]
    y = wts[:, 0:1] * buf_ref[0] + wts[:, 1:2] * buf_ref[1]
    x2 = x_ref[0] + (1.0 + mod_ref[0, 5:6]) * y
    if final_norm:
        x2 = (x2 * lax.rsqrt(jnp.mean(x2 * x2, axis=-1, keepdims=True) + RMS_EPS)) * fg_ref[...]
    out_ref[0] = x2


def _moe_combine(pos_flat, x, wts, mod, final_g, o_sorted, final_norm):
    b, l, d = x.shape
    nt = l // ROUTE_TILE
    return pl.pallas_call(
        functools.partial(_moe_combine_kernel, final_norm=final_norm),
        grid=(b, nt),
        in_specs=[pl.BlockSpec((TOP_K * ROUTE_TILE,), lambda i, j: (i * nt + j,), memory_space=pltpu.SMEM),
                  pl.BlockSpec((1, ROUTE_TILE, d), lambda i, j: (i, j, 0)),
                  pl.BlockSpec((ROUTE_TILE, TOP_K), lambda i, j: (i * nt + j, 0)),
                  pl.BlockSpec((1, 8, d), lambda i, j: (i, 0, 0)),
                  pl.BlockSpec((1, d), lambda i, j: (0, 0)),
                  pl.BlockSpec(memory_space=pl.ANY)],
        out_specs=pl.BlockSpec((1, ROUTE_TILE, d), lambda i, j: (i, j, 0)),
        out_shape=jax.ShapeDtypeStruct((b, l, d), F32),
        scratch_shapes=[pltpu.VMEM((TOP_K, ROUTE_TILE, d), F32), pltpu.SemaphoreType.DMA(())],
        compiler_params=_cparams("arbitrary", "arbitrary"),
        name="moe_combine",
    )(pos_flat, x, wts, mod, final_g, o_sorted)


def _moe(x, hn, eid, wts, mod, w1, w3, w2, layer, final_g, final_norm):
    t, d = hn.shape
    rank, cnt = _moe_rank(eid)
    counts = cnt[0, :N_EXPERTS].astype(jnp.int32)
    n_tiles_e = (counts + EXPERT_TILE - 1) // EXPERT_TILE
    tile_end = jnp.cumsum(n_tiles_e)
    row_off = (tile_end - n_tiles_e) * EXPERT_TILE
    pos = (row_off[eid] + rank).reshape(-1)
    max_tiles = (t * TOP_K) // EXPERT_TILE + N_EXPERTS
    tile_ids = jnp.arange(max_tiles, dtype=jnp.int32)
    tile_expert = jnp.minimum(jnp.sum((tile_ids[:, None] >= tile_end[None, :]).astype(jnp.int32), axis=1),
                              N_EXPERTS - 1)
    tile_valid = (tile_ids < tile_end[-1]).astype(jnp.int32)
    tile_first = jnp.concatenate([jnp.ones((1,), jnp.int32),
                                  (tile_expert[1:] != tile_expert[:-1]).astype(jnp.int32)])
    xs = _moe_dispatch(pos, hn, jnp.zeros((max_tiles * EXPERT_TILE, d), F32))
    o_sorted = _moe_experts(tile_expert, tile_valid, tile_first, xs, w1, w3, w2, layer)
    return _moe_combine(pos, x, wts, mod, final_g, o_sorted, final_norm)


def _router_pack(wg, bg, we, be):
    d = wg.shape[0]
    wr = jnp.zeros((d, LANES), F32).at[:, :N_EXPERTS].set(we).at[:, N_EXPERTS:N_EXPERTS + N_GROUPS].set(wg)
    br = jnp.zeros((1, LANES), F32).at[0, :N_EXPERTS].set(be).at[0, N_EXPERTS:N_EXPERTS + N_GROUPS].set(bg)
    return wr, br


def kernel(x, c, ada_w, ada_b, norm1_g, norm2_g, ssm_w_in, ssm_lam_re, ssm_lam_im, ssm_log_dt, ssm_b_re, ssm_b_im, ssm_c_re, ssm_c_im, ssm_d, ssm_w_glu, conv_w_in, conv_w, conv_w_out, moe_wg, moe_bg, moe_we, moe_be, moe_w1, moe_w3, moe_w2, final_g):
    b, l, d = x.shape
    depth = ada_w.shape[0]
    c8 = jnp.zeros((8, d), F32).at[:b].set(c)
    mod_all = _adaln(c8, ada_w, ada_b)[:, :b].reshape(depth, b, 6, d)
    mod_all = jnp.concatenate([mod_all, jnp.zeros((depth, b, 2, d), F32)], axis=2)
    fg = final_g.reshape(1, d)

    mod = mod_all[0]
    u4 = _s5_in(x, mod, norm1_g[0:1], ssm_w_in[0].T.astype(BF16))
    kern, f_mat, e_mat, a_pack, d_row = _s5_tables(
        ssm_lam_re[0], ssm_lam_im[0], ssm_log_dt[0], ssm_b_re[0], ssm_b_im[0],
        ssm_c_re[0], ssm_c_im[0], ssm_d[0])
    y4 = _s5_scan(u4, kern, f_mat, e_mat, a_pack, d_row)
    wr, br = _router_pack(moe_wg[0], moe_bg[0], moe_we[0], moe_be[0])
    x1, hn, eid, wts = _s5_out(y4, x, mod, ssm_w_glu[0].astype(BF16), norm2_g[0:1], wr, br)
    x2 = _moe(x1, hn, eid, wts, mod, moe_w1, moe_w3, moe_w2, 0, fg, False)

    mod = mod_all[1]
    cw8 = jnp.zeros((8, d), F32).at[:conv_w.shape[1]].set(conv_w[0])
    wr, br = _router_pack(moe_wg[1], moe_bg[1], moe_we[1], moe_be[1])
    x3, hn, eid, wts = _conv_mixer(x2, mod, norm1_g[1:2], conv_w_in[0].astype(BF16), cw8,
                                   conv_w_out[0].astype(BF16), norm2_g[1:2], wr, br)
    return _moe(x3, hn, eid, wts, mod, moe_w1, moe_w3, moe_w2, 1, fg, True)
```

```python
import functools
import math

import jax
import jax.numpy as jnp
from jax import lax
from jax.experimental import pallas as pl
from jax.experimental.pallas import tpu as pltpu

F32 = jnp.float32
BF16 = jnp.bfloat16
HIGHEST = lax.Precision.HIGHEST

RMS_EPS = 1e-6
SSM_GROUP = 16
SSM_CHUNK = 128
N_GROUPS = 4
EXPERTS_PER_GROUP = 8
N_EXPERTS = N_GROUPS * EXPERTS_PER_GROUP
TOP_K = 2
LANES = 128
TOKEN_TILE = 512
S5_TILE = 1024
MOE_BLOCK = TOKEN_TILE
CHUNK_ROWS = 8
LOCAL_ROWS = -(-(TOP_K * MOE_BLOCK + N_EXPERTS * (CHUNK_ROWS - 1)) // LANES) * LANES
EXPERT_TILE = 256
VMEM_LIMIT = 56 * 1024 * 1024
NEG_INF = -1e30


def _cparams(*sem):
    return pltpu.CompilerParams(dimension_semantics=sem, vmem_limit_bytes=VMEM_LIMIT)


def _adaln_kernel(c_ref, w_ref, b_ref, o_ref):
    c = c_ref[...]
    cond = c * jax.nn.sigmoid(c)
    o_ref[0] = jnp.dot(cond, w_ref[0], precision=HIGHEST, preferred_element_type=F32) + b_ref[0]


def _adaln(c8, ada_w, ada_b):
    depth, d, n = ada_w.shape
    tn = 1536
    return pl.pallas_call(
        _adaln_kernel,
        grid=(depth, n // tn),
        in_specs=[pl.BlockSpec((8, d), lambda i, j: (0, 0)),
                  pl.BlockSpec((1, d, tn), lambda i, j: (i, 0, j)),
                  pl.BlockSpec((1, 1, tn), lambda i, j: (i, 0, j))],
        out_specs=pl.BlockSpec((1, 8, tn), lambda i, j: (i, 0, j)),
        out_shape=jax.ShapeDtypeStruct((depth, 8, n), F32),
        compiler_params=_cparams("parallel", "parallel"),
        name="adaln",
    )(c8, ada_w, ada_b.reshape(depth, 1, n))


def _norm_mod(x, g, shift, scale):
    y = x * lax.rsqrt(jnp.mean(x * x, axis=-1, keepdims=True) + RMS_EPS)
    return (y * g) * (1.0 + scale) + shift


def _route(hn, hn_hi, wrh_ref, wrl_ref, br_ref, eid_ref, wts_ref, cnt_ref):
    hn_lo = (hn - hn_hi.astype(F32)).astype(BF16)
    wrh = wrh_ref[...]
    logits = (jnp.dot(hn_hi, wrh, preferred_element_type=F32)
              + jnp.dot(hn_lo, wrh, preferred_element_type=F32)
              + jnp.dot(hn_hi, wrl_ref[...], preferred_element_type=F32)) + br_ref[...]
    lane = lax.broadcasted_iota(jnp.int32, logits.shape, 1)
    is_grp = (lane >= N_EXPERTS) & (lane < N_EXPERTS + N_GROUPS)
    lg = jnp.where(is_grp, logits, NEG_INF)
    gmax = jnp.max(lg, axis=-1, keepdims=True)
    gsum = jnp.sum(jnp.where(is_grp, jnp.exp(lg - gmax), 0.0), axis=-1, keepdims=True)
    gp = 1.0 / gsum
    gi = jnp.min(jnp.where(lg == gmax, lane, 2 * LANES), axis=-1, keepdims=True) - N_EXPERTS
    in_grp = (lane < N_EXPERTS) & ((lane // EXPERTS_PER_GROUP) == gi)
    le = jnp.where(in_grp, logits, NEG_INF)
    v1 = jnp.max(le, axis=-1, keepdims=True)
    i1 = jnp.min(jnp.where(le == v1, lane, 2 * LANES), axis=-1, keepdims=True)
    le2 = jnp.where(lane == i1, NEG_INF, le)
    v2 = jnp.max(le2, axis=-1, keepdims=True)
    i2 = jnp.min(jnp.where(le2 == v2, lane, 2 * LANES), axis=-1, keepdims=True)
    e2 = jnp.exp(v2 - v1)
    den = 1.0 + e2
    col = lax.broadcasted_iota(jnp.int32, (hn.shape[0], TOP_K), 1)
    eid_ref[...] = jnp.where(col == 0, i1, i2)
    wts_ref[...] = jnp.where(col == 0, gp / den, gp * e2 / den)
    chosen = ((lane == i1) | (lane == i2)).astype(F32)
    cnt_ref[...] = jnp.broadcast_to(jnp.sum(chosen, axis=0, keepdims=True), cnt_ref.shape)


def _residual_and_route(x, out, mod, g2_ref, wrh_ref, wrl_ref, br_ref, x_out_ref, hn_ref, eid_ref, wts_ref, cnt_ref):
    x1 = x + (1.0 + mod[2:3]) * out
    x_out_ref[...] = x1
    hn = _norm_mod(x1, g2_ref[...], mod[3:4], mod[4:5])
    hn_hi = hn.astype(BF16)
    hn_ref[...] = hn_hi
    _route(hn, hn_hi, wrh_ref, wrl_ref, br_ref, eid_ref, wts_ref, cnt_ref)


def _s5_in_kernel(x_ref, mod_ref, g_ref, wt_ref, u_ref):
    mod = mod_ref[0]
    hn = _norm_mod(x_ref[0], g_ref[...], mod[0:1], mod[1:2])
    ut = lax.dot_general(wt_ref[...], hn.astype(BF16), (((1,), (1,)), ((), ())), preferred_element_type=F32)
    u_ref[0] = ut.reshape(ut.shape[0], S5_TILE // SSM_CHUNK, SSM_CHUNK)


def _s5_in(x, mod, g, wt_bf):
    b, l, d = x.shape
    h = wt_bf.shape[0]
    cpt = S5_TILE // SSM_CHUNK
    return pl.pallas_call(
        _s5_in_kernel,
        grid=(b, l // S5_TILE),
        in_specs=[pl.BlockSpec((1, S5_TILE, d), lambda i, j: (i, j, 0)),
                  pl.BlockSpec((1, 8, d), lambda i, j: (i, 0, 0)),
                  pl.BlockSpec((1, d), lambda i, j: (0, 0)),
                  pl.BlockSpec((h, d), lambda i, j: (0, 0))],
        out_specs=pl.BlockSpec((1, h, cpt, SSM_CHUNK), lambda i, j: (i, 0, j, 0)),
        out_shape=jax.ShapeDtypeStruct((b, h, l // SSM_CHUNK, SSM_CHUNK), F32),
        compiler_params=_cparams("parallel", "parallel"),
        name="s5_in",
    )(x, mod, g, wt_bf)


def _s5_scan_kernel(u_ref, k_ref, f_ref, e_ref, a_ref, d_ref, y_ref, m_ref, s_ref, sw_ref, sp_ref):
    bsz, grp, n_chunks, tc = u_ref.shape
    srow = lax.broadcasted_iota(jnp.int32, (tc, tc), 0)
    tcol = lax.broadcasted_iota(jnp.int32, (tc, tc), 1)
    causal = tcol >= srow

    def build(i, carry):
        for j in range(grp):
            row = k_ref[0, pl.ds(i * grp + j, 1), :]
            blk = pltpu.roll(jnp.broadcast_to(row, (tc, tc)), 0, 1, stride=1, stride_axis=0)
            m_ref[pl.ds(pl.multiple_of(i * tc, tc), tc), j * tc:(j + 1) * tc] = (
                jnp.where(causal, blk, 0.0).astype(BF16))
        return carry

    lax.fori_loop(0, grp, build, 0)

    x = jnp.concatenate(
        [jnp.concatenate([u_ref[b, k] for k in range(grp)], axis=1) for b in range(bsz)], axis=0)
    xb = x.astype(BF16)
    s_loc = jnp.dot(xb, f_ref[0], preferred_element_type=F32)
    s_ref[...] = s_loc
    half = s_loc.shape[1] // 2
    sw_ref[...] = jnp.concatenate([s_loc[:, half:], s_loc[:, :half]], axis=1)
    a1 = a_ref[0, 0:1, :]
    a2 = a_ref[0, 1:2, :]
    s = [jnp.zeros((1, s_loc.shape[1]), F32) for _ in range(bsz)]
    sw = [jnp.zeros((1, s_loc.shape[1]), F32) for _ in range(bsz)]
    for c in range(n_chunks):
        for b in range(bsz):
            r = b * n_chunks + c
            sp_ref[r:r + 1, :] = s[b]
            s_new = a1 * s[b] + a2 * sw[b] + s_ref[r:r + 1, :]
            sw[b] = a1 * sw[b] - a2 * s[b] + sw_ref[r:r + 1, :]
            s[b] = s_new
    y = jnp.dot(xb, m_ref[...], preferred_element_type=F32)
    y = y + jnp.dot(sp_ref[...].astype(BF16), e_ref[0], preferred_element_type=F32)
    y = y + d_ref[0] * x
    for b in range(bsz):
        for j in range(grp):
            y_ref[b, j] = y[b * n_chunks:(b + 1) * n_chunks, j * tc:(j + 1) * tc]


def _s5_scan(u4, kern, f_bf, e_bf, a_pack, d_row):
    b, h, nc, tc = u4.shape
    g = kern.shape[0]
    grp = h // g
    k = grp * tc
    p2 = f_bf.shape[2]
    rows = b * nc
    return pl.pallas_call(
        _s5_scan_kernel,
        grid=(g,),
        in_specs=[pl.BlockSpec((b, grp, nc, tc), lambda i: (0, i, 0, 0)),
                  pl.BlockSpec((1, grp * grp, tc), lambda i: (i, 0, 0)),
                  pl.BlockSpec((1, k, p2), lambda i: (i, 0, 0)),
                  pl.BlockSpec((1, p2, k), lambda i: (i, 0, 0)),
                  pl.BlockSpec((1, 8, p2), lambda i: (i, 0, 0)),
                  pl.BlockSpec((1, 1, k), lambda i: (i, 0, 0))],
        out_specs=pl.BlockSpec((b, grp, nc, tc), lambda i: (0, i, 0, 0)),
        out_shape=jax.ShapeDtypeStruct(u4.shape, F32),
        scratch_shapes=[pltpu.VMEM((k, k), BF16), pltpu.VMEM((rows, p2), F32),
                        pltpu.VMEM((rows, p2), F32), pltpu.VMEM((rows, p2), F32)],
        compiler_params=_cparams("parallel"),
        name="s5_scan",
    )(u4, kern, f_bf, e_bf, a_pack, d_row)


def _s5_tables(lam_re, lam_im, log_dt, b_re, b_im, c_re, c_im, d_skip):
    g, p = lam_re.shape
    k = SSM_GROUP
    tc = SSM_CHUNK
    dt = jnp.exp(log_dt)[:, None]
    mag = jnp.exp(lam_re * dt)
    ab_re = mag * jnp.cos(lam_im * dt)
    ab_im = mag * jnp.sin(lam_im * dt)
    den = lam_re * lam_re + lam_im * lam_im
    cf_re = ((ab_re - 1) * lam_re + ab_im * lam_im) / den
    cf_im = (ab_im * lam_re - (ab_re - 1) * lam_im) / den
    bb_re = cf_re[..., None] * b_re - cf_im[..., None] * b_im
    bb_im = cf_re[..., None] * b_im + cf_im[..., None] * b_re
    lags = jnp.arange(tc + 1, dtype=F32)[:, None, None]
    pmag = jnp.exp(lags * (lam_re * dt)[None])
    pang = lags * (lam_im * dt)[None]
    p_re = pmag * jnp.cos(pang)
    p_im = pmag * jnp.sin(pang)
    cb_re = jnp.einsum('gjp,gpi->gpij', c_re, bb_re) - jnp.einsum('gjp,gpi->gpij', c_im, bb_im)
    cb_im = jnp.einsum('gjp,gpi->gpij', c_re, bb_im) + jnp.einsum('gjp,gpi->gpij', c_im, bb_re)
    kern = (jnp.einsum('lgp,gpij->gijl', p_re[:tc], cb_re, precision=HIGHEST)
            - jnp.einsum('lgp,gpij->gijl', p_im[:tc], cb_im, precision=HIGHEST))
    kern = kern.reshape(g, k * k, tc)
    pr = p_re[:tc][::-1]
    pi = p_im[:tc][::-1]
    f_re = jnp.einsum('sgp,gpi->gisp', pr, bb_re) - jnp.einsum('sgp,gpi->gisp', pi, bb_im)
    f_im = jnp.einsum('sgp,gpi->gisp', pr, bb_im) + jnp.einsum('sgp,gpi->gisp', pi, bb_re)
    f_mat = jnp.concatenate([f_re, f_im], axis=-1).reshape(g, k * tc, 2 * p).astype(BF16)
    qr = p_re[1:]
    qi = p_im[1:]
    e_re = jnp.einsum('gjp,tgp->gpjt', c_re, qr) - jnp.einsum('gjp,tgp->gpjt', c_im, qi)
    e_im = -(jnp.einsum('gjp,tgp->gpjt', c_re, qi) + jnp.einsum('gjp,tgp->gpjt', c_im, qr))
    e_mat = jnp.concatenate([e_re, e_im], axis=1).reshape(g, 2 * p, k * tc).astype(BF16)
    ar = p_re[tc]
    ai = p_im[tc]
    a1 = jnp.concatenate([ar, ar], axis=-1)
    a2 = jnp.concatenate([-ai, ai], axis=-1)
    a_pack = jnp.concatenate([a1[:, None], a2[:, None], jnp.zeros((g, 6, 2 * p), F32)], axis=1)
    d_row = jnp.repeat(d_skip.reshape(g, k), tc, axis=1).reshape(g, 1, k * tc)
    return kern, f_mat, e_mat, a_pack, d_row


def _gelu_tanh(x):
    return 0.5 * x * (1.0 + jnp.tanh(math.sqrt(2.0 / math.pi) * (x + 0.044715 * (x * x * x))))


def _s5_out_kernel(y_ref, x_ref, mod_ref, w_ref, g2_ref, wrh_ref, wrl_ref, br_ref,
                   x_out_ref, hn_ref, eid_ref, wts_ref, cnt_ref):
    mod = mod_ref[0]
    h = y_ref.shape[1]
    yt = _gelu_tanh(y_ref[0].reshape(h, S5_TILE))
    d = w_ref.shape[1] // 2
    for part in range(S5_TILE // MOE_BLOCK):
        rows = slice(part * MOE_BLOCK, (part + 1) * MOE_BLOCK)
        o = lax.dot_general(yt[:, rows].astype(BF16), w_ref[...], (((0,), (0,)), ((), ())),
                            preferred_element_type=F32)
        out = o[:, :d] * jax.nn.sigmoid(o[:, d:])
        _residual_and_route(x_ref[0, rows, :], out, mod, g2_ref, wrh_ref, wrl_ref, br_ref,
                            x_out_ref.at[0, rows, :], hn_ref.at[rows, :], eid_ref.at[rows, :],
                            wts_ref.at[rows, :], cnt_ref.at[part])


def _mixer_out_specs(b, l, d, tile):
    nt = l // tile
    nblk = tile // MOE_BLOCK
    specs = [pl.BlockSpec((1, tile, d), lambda i, j: (i, j, 0)),
             pl.BlockSpec((tile, d), lambda i, j: (i * nt + j, 0)),
             pl.BlockSpec((tile, TOP_K), lambda i, j: (i * nt + j, 0)),
             pl.BlockSpec((tile, TOP_K), lambda i, j: (i * nt + j, 0)),
             pl.BlockSpec((nblk, 8, LANES), lambda i, j: (i * nt + j, 0, 0))]
    shapes = [jax.ShapeDtypeStruct((b, l, d), F32),
              jax.ShapeDtypeStruct((b * l, d), BF16),
              jax.ShapeDtypeStruct((b * l, TOP_K), jnp.int32),
              jax.ShapeDtypeStruct((b * l, TOP_K), F32),
              jax.ShapeDtypeStruct((b * l // MOE_BLOCK, 8, LANES), F32)]
    return specs, shapes


def _s5_out(y4, x, mod, w_bf, g2, wrh, wrl, br):
    b, l, d = x.shape
    h = y4.shape[1]
    specs, shapes = _mixer_out_specs(b, l, d, S5_TILE)
    return pl.pallas_call(
        _s5_out_kernel,
        grid=(b, l // S5_TILE),
        in_specs=[pl.BlockSpec((1, h, S5_TILE // SSM_CHUNK, SSM_CHUNK), lambda i, j: (i, 0, j, 0)),
                  pl.BlockSpec((1, S5_TILE, d), lambda i, j: (i, j, 0)),
                  pl.BlockSpec((1, 8, d), lambda i, j: (i, 0, 0)),
                  pl.BlockSpec(w_bf.shape, lambda i, j: (0, 0), pipeline_mode=pl.Buffered(1)),
                  pl.BlockSpec((1, d), lambda i, j: (0, 0)),
                  pl.BlockSpec((d, LANES), lambda i, j: (0, 0)),
                  pl.BlockSpec((d, LANES), lambda i, j: (0, 0)),
                  pl.BlockSpec((1, LANES), lambda i, j: (0, 0))],
        out_specs=specs,
        out_shape=shapes,
        compiler_params=_cparams("parallel", "parallel"),
        name="s5_out",
    )(y4, x, mod, w_bf, g2, wrh, wrl, br)


def _conv_mixer_kernel(x_ref, mod_ref, g1_ref, win_ref, cw_ref, wout_ref, g2_ref, wrh_ref, wrl_ref, br_ref,
                       x_out_ref, hn_ref, eid_ref, wts_ref, cnt_ref, carry_ref):
    @pl.when(pl.program_id(1) == 0)
    def _():
        carry_ref[...] = jnp.zeros_like(carry_ref)

    mod = mod_ref[0]
    x = x_ref[0]
    hn = _norm_mod(x, g1_ref[...], mod[0:1], mod[1:2])
    p = jnp.dot(hn.astype(BF16), win_ref[...], preferred_element_type=F32)
    d = p.shape[1] // 3
    b_g = p[:, :d]
    u = p[:, d:2 * d] * p[:, 2 * d:]
    ext = jnp.concatenate([carry_ref[...], u], axis=0)
    t = u.shape[0]
    cw = cw_ref[...]
    z = cw[0:1] * ext[6:6 + t] + cw[1:2] * ext[7:7 + t] + cw[2:3] * u
    carry_ref[...] = u[t - 8:]
    out = jnp.dot((b_g * z).astype(BF16), wout_ref[...], preferred_element_type=F32)
    _residual_and_route(x, out, mod, g2_ref, wrh_ref, wrl_ref, br_ref, x_out_ref.at[0], hn_ref, eid_ref,
                        wts_ref, cnt_ref.at[0])


def _conv_mixer(x, mod, g1, win_bf, cw8, wout_bf, g2, wrh, wrl, br):
    b, l, d = x.shape
    specs, shapes = _mixer_out_specs(b, l, d, TOKEN_TILE)
    return pl.pallas_call(
        _conv_mixer_kernel,
        grid=(b, l // TOKEN_TILE),
        in_specs=[pl.BlockSpec((1, TOKEN_TILE, d), lambda i, j: (i, j, 0)),
                  pl.BlockSpec((1, 8, d), lambda i, j: (i, 0, 0)),
                  pl.BlockSpec((1, d), lambda i, j: (0, 0)),
                  pl.BlockSpec(win_bf.shape, lambda i, j: (0, 0)),
                  pl.BlockSpec((8, d), lambda i, j: (0, 0)),
                  pl.BlockSpec(wout_bf.shape, lambda i, j: (0, 0)),
                  pl.BlockSpec((1, d), lambda i, j: (0, 0)),
                  pl.BlockSpec((d, LANES), lambda i, j: (0, 0)),
                  pl.BlockSpec((d, LANES), lambda i, j: (0, 0)),
                  pl.BlockSpec((1, LANES), lambda i, j: (0, 0))],
        out_specs=specs,
        out_shape=shapes,
        scratch_shapes=[pltpu.VMEM((8, d), F32)],
        compiler_params=_cparams("parallel", "arbitrary"),
        name="conv_mixer",
    )(x, mod, g1, win_bf, cw8, wout_bf, g2, wrh, wrl, br)


_TN = (((0,), (0,)), ((), ()))


def _chunk_copy(src, s_row, dst, d_row, sem):
    return pltpu.make_async_copy(src.at[pl.ds(s_row, CHUNK_ROWS), :], dst.at[pl.ds(d_row, CHUNK_ROWS), :], sem)


def _block_chunk_copies(blk, nch_ref, loff_ref, goff_ref, make_copy):
    def per_expert(e, carry):
        idx = blk * N_EXPERTS + e
        lo = loff_ref[idx]
        go = goff_ref[idx]

        def per_chunk(j, c):
            make_copy(pl.multiple_of(lo + j * CHUNK_ROWS, CHUNK_ROWS),
                      pl.multiple_of(go + j * CHUNK_ROWS, CHUNK_ROWS)).start()
            return c

        lax.fori_loop(0, nch_ref[idx], per_chunk, 0)
        return carry

    lax.fori_loop(0, N_EXPERTS, per_expert, 0)


def _one_hot_slots(lp, width):
    slot = lax.broadcasted_iota(jnp.int32, (lp.shape[0], width), 1)
    return (slot == lp).astype(BF16)


def _split3_lanes(w):
    hi = w.astype(BF16).astype(F32)
    mid = (w - hi).astype(BF16).astype(F32)
    lo = (w - hi - mid).astype(BF16).astype(F32)
    lane = lax.broadcasted_iota(jnp.int32, (w.shape[0], LANES), 1)
    return jnp.where(lane == 0, hi, jnp.where(lane == 1, mid, jnp.where(lane == 2, lo, 0.0))).astype(BF16)


def _moe_dispatch_kernel(nch_ref, loff_ref, goff_ref, tot_ref, padn_ref, pads_ref, nv_ref,
                         eid_ref, wts_ref, hn_ref, lofff_ref, xs_ref, lpos_ref, loc_ref, zero_ref, sem):
    blk = pl.program_id(0)
    d = hn_ref.shape[1]
    eid = eid_ref[...]
    t = eid.shape[0]
    lane = lax.broadcasted_iota(jnp.int32, (t, LANES), 1)
    hit0 = lane == eid[:, 0:1]
    hit1 = lane == eid[:, 1:2]
    onehot = (hit0 | hit1).astype(BF16)
    r = lax.broadcasted_iota(jnp.int32, (t, t), 0)
    c = lax.broadcasted_iota(jnp.int32, (t, t), 1)
    tri = (c < r).astype(BF16)
    before = jnp.dot(tri, onehot, preferred_element_type=F32) + lofff_ref[0]
    lp0 = jnp.sum(jnp.where(hit0, before, 0.0), axis=-1, keepdims=True).astype(jnp.int32)
    lp1 = jnp.sum(jnp.where(hit1, before, 0.0), axis=-1, keepdims=True).astype(jnp.int32)
    col = lax.broadcasted_iota(jnp.int32, (t, TOP_K), 1)
    lpos_ref[...] = jnp.where(col == 0, lp0, lp1)
    pt0 = _one_hot_slots(lp0, LOCAL_ROWS)
    pt1 = _one_hot_slots(lp1, LOCAL_ROWS)
    loc_ref[:, :d] = lax.dot_general(pt0 + pt1, hn_ref[...], _TN, preferred_element_type=F32)
    wts = wts_ref[...]
    loc_ref[:, d:] = (lax.dot_general(pt0, _split3_lanes(wts[:, 0:1]), _TN, preferred_element_type=F32)
                      + lax.dot_general(pt1, _split3_lanes(wts[:, 1:2]), _TN, preferred_element_type=F32))

    _block_chunk_copies(blk, nch_ref, loff_ref, goff_ref,
                        lambda lo, go: _chunk_copy(loc_ref, lo, xs_ref, go, sem))

    def drain(j, carry):
        _chunk_copy(loc_ref, 0, xs_ref, 0, sem).wait()
        return carry

    lax.fori_loop(0, tot_ref[blk], drain, 0)

    @pl.when(blk == pl.num_programs(0) - 1)
    def _():
        zero_ref[...] = jnp.zeros_like(zero_ref)

        def per_expert(e, carry):
            def per_chunk(j, c):
                _chunk_copy(zero_ref, 0, xs_ref, pl.multiple_of(pads_ref[e] + j * CHUNK_ROWS, CHUNK_ROWS), sem).start()
                return c

            lax.fori_loop(0, padn_ref[e], per_chunk, 0)
            return carry

        lax.fori_loop(0, N_EXPERTS, per_expert, 0)

        def per_expert_wait(e, carry):
            def per_chunk(j, c):
                _chunk_copy(zero_ref, 0, xs_ref, 0, sem).wait()
                return c

            lax.fori_loop(0, padn_ref[e], per_chunk, 0)
            return carry

        lax.fori_loop(0, N_EXPERTS, per_expert_wait, 0)

        def tile_copy(i):
            return pltpu.make_async_copy(zero_ref, xs_ref.at[pl.ds(pl.multiple_of(i * EXPERT_TILE, EXPERT_TILE),
                                                                   EXPERT_TILE), :], sem)

        n_tiles = xs_ref.shape[0] // EXPERT_TILE

        def start_tile(i, c):
            tile_copy(i).start()
            return c

        def wait_tile(i, c):
            tile_copy(i).wait()
            return c

        lax.fori_loop(nv_ref[0], n_tiles, start_tile, 0)
        lax.fori_loop(nv_ref[0], n_tiles, wait_tile, 0)


def _moe_dispatch(tables, eid, wts, hn, loff_f, n_rows):
    t, d = hn.shape
    width = d + LANES
    grid_spec = pltpu.PrefetchScalarGridSpec(
        num_scalar_prefetch=7,
        grid=(t // MOE_BLOCK,),
        in_specs=[pl.BlockSpec((MOE_BLOCK, TOP_K), lambda i, *_: (i, 0)),
                  pl.BlockSpec((MOE_BLOCK, TOP_K), lambda i, *_: (i, 0)),
                  pl.BlockSpec((MOE_BLOCK, d), lambda i, *_: (i, 0)),
                  pl.BlockSpec((1, 1, LANES), lambda i, *_: (i, 0, 0))],
        out_specs=[pl.BlockSpec(memory_space=pl.ANY),
                   pl.BlockSpec((MOE_BLOCK, TOP_K), lambda i, *_: (i, 0))],
        scratch_shapes=[pltpu.VMEM((LOCAL_ROWS, width), F32), pltpu.VMEM((EXPERT_TILE, width), F32),
                        pltpu.SemaphoreType.DMA(())],
    )
    return pl.pallas_call(
        _moe_dispatch_kernel,
        grid_spec=grid_spec,
        out_shape=[jax.ShapeDtypeStruct((n_rows, width), F32),
                   jax.ShapeDtypeStruct((t, TOP_K), jnp.int32)],
        compiler_params=_cparams("arbitrary"),
        name="moe_dispatch",
    )(*tables, eid, wts, hn, loff_f)


def _moe_experts_kernel(te_ref, tv_ref, tf_ref, ts_ref, xs_ref, w1_ref, w3_ref, w2_ref, o_ref,
                        w1b_ref, w3b_ref, w2b_ref):
    del te_ref, ts_ref
    i = pl.program_id(0)

    @pl.when(tf_ref[i] != 0)
    def _():
        w1b_ref[...] = w1_ref[0, 0].astype(BF16)
        w3b_ref[...] = w3_ref[0, 0].astype(BF16)
        w2b_ref[...] = w2_ref[0, 0].astype(BF16)

    @pl.when(tv_ref[i] != 0)
    def _():
        d = o_ref.shape[1]
        xb = xs_ref[:, :d].astype(BF16)
        wl = xs_ref[:, d:]
        w = wl[:, 0:1] + wl[:, 1:2] + wl[:, 2:3]
        a = jnp.dot(xb, w1b_ref[...], preferred_element_type=F32)
        b = jnp.dot(xb, w3b_ref[...], preferred_element_type=F32)
        h = (a * jax.nn.sigmoid(a) * b).astype(BF16)
        o_ref[...] = jnp.dot(h, w2b_ref[...], preferred_element_type=F32) * w

    @pl.when(tv_ref[i] == 0)
    def _():
        o_ref[...] = jnp.zeros_like(o_ref)


def _moe_experts(tile_expert, tile_valid, tile_first, tile_src, xs, w1, w3, w2, layer):
    r, width = xs.shape
    d = width - LANES
    de = w1.shape[3]
    grid_spec = pltpu.PrefetchScalarGridSpec(
        num_scalar_prefetch=4,
        grid=(r // EXPERT_TILE,),
        in_specs=[pl.BlockSpec((EXPERT_TILE, width), lambda i, te, tv, tf, ts: (ts[i], 0)),
                  pl.BlockSpec((1, 1, d, de), lambda i, te, tv, tf, ts: (layer, te[i], 0, 0)),
                  pl.BlockSpec((1, 1, d, de), lambda i, te, tv, tf, ts: (layer, te[i], 0, 0)),
                  pl.BlockSpec((1, 1, de, d), lambda i, te, tv, tf, ts: (layer, te[i], 0, 0))],
        out_specs=pl.BlockSpec((EXPERT_TILE, d), lambda i, te, tv, tf, ts: (i, 0)),
        scratch_shapes=[pltpu.VMEM((d, de), BF16), pltpu.VMEM((d, de), BF16), pltpu.VMEM((de, d), BF16)],
    )
    return pl.pallas_call(
        _moe_experts_kernel,
        grid_spec=grid_spec,
        out_shape=jax.ShapeDtypeStruct((r, d), F32),
        compiler_params=_cparams("arbitrary"),
        name="moe_experts",
    )(tile_expert, tile_valid, tile_first, tile_src, xs, w1, w3, w2)


def _moe_combine_kernel(nch_ref, loff_ref, goff_ref, tot_ref, lpos_ref, x_ref, mod_ref, fg_ref, o_hbm_ref,
                        out_ref, loc_ref, sem, *, final_norm):
    blk = pl.program_id(0) * pl.num_programs(1) + pl.program_id(1)
    t = x_ref.shape[1]
    loc_ref[TOP_K * t:, :] = jnp.zeros((LOCAL_ROWS - TOP_K * t, loc_ref.shape[1]), F32)
    _block_chunk_copies(blk, nch_ref, loff_ref, goff_ref,
                        lambda lo, go: _chunk_copy(o_hbm_ref, go, loc_ref, lo, sem))

    def drain(j, carry):
        _chunk_copy(o_hbm_ref, 0, loc_ref, 0, sem).wait()
        return carry

    lax.fori_loop(0, tot_ref[blk], drain, 0)
    lp = lpos_ref[...]
    pt = _one_hot_slots(lp[:, 0:1], LOCAL_ROWS) + _one_hot_slots(lp[:, 1:2], LOCAL_ROWS)
    y = jnp.dot(pt, loc_ref[...].astype(BF16), preferred_element_type=F32)
    x2 = x_ref[0] + (1.0 + mod_ref[0, 5:6]) * y
    if final_norm:
        x2 = (x2 * lax.rsqrt(jnp.mean(x2 * x2, axis=-1, keepdims=True) + RMS_EPS)) * fg_ref[...]
    out_ref[0] = x2


def _moe_combine(tables, lpos, x, mod, final_g, o_sorted, final_norm):
    b, l, d = x.shape
    nt = l // MOE_BLOCK
    grid_spec = pltpu.PrefetchScalarGridSpec(
        num_scalar_prefetch=4,
        grid=(b, nt),
        in_specs=[pl.BlockSpec((MOE_BLOCK, TOP_K), lambda i, j, *_: (i * nt + j, 0)),
                  pl.BlockSpec((1, MOE_BLOCK, d), lambda i, j, *_: (i, j, 0)),
                  pl.BlockSpec((1, 8, d), lambda i, j, *_: (i, 0, 0)),
                  pl.BlockSpec((1, d), lambda i, j, *_: (0, 0)),
                  pl.BlockSpec(memory_space=pl.ANY)],
        out_specs=pl.BlockSpec((1, MOE_BLOCK, d), lambda i, j, *_: (i, j, 0)),
        scratch_shapes=[pltpu.VMEM((LOCAL_ROWS, d), F32), pltpu.SemaphoreType.DMA(())],
    )
    return pl.pallas_call(
        functools.partial(_moe_combine_kernel, final_norm=final_norm),
        grid_spec=grid_spec,
        out_shape=jax.ShapeDtypeStruct((b, l, d), F32),
        compiler_params=_cparams("arbitrary", "arbitrary"),
        name="moe_combine",
    )(*tables[:4], lpos, x, mod, final_g, o_sorted)


def _moe(x, hn, eid, wts, cnt, mod, w1, w3, w2, layer, final_g, final_norm):
    t, d = hn.shape
    nblk = t // MOE_BLOCK
    i32 = jnp.int32
    n = cnt[:, 0, :N_EXPERTS].astype(i32)
    run = (n + CHUNK_ROWS - 1) // CHUNK_ROWS * CHUNK_ROWS
    loff = jnp.cumsum(run, axis=1) - run
    rows_e = jnp.sum(run, axis=0)
    tiles_e = (rows_e + EXPERT_TILE - 1) // EXPERT_TILE
    tile_end = jnp.cumsum(tiles_e)
    base = (tile_end - tiles_e) * EXPERT_TILE
    goff = base[None, :] + jnp.cumsum(run, axis=0) - run
    nch = run // CHUNK_ROWS
    tot = jnp.sum(nch, axis=1)
    padn = (tiles_e * EXPERT_TILE - rows_e) // CHUNK_ROWS
    pads = base + rows_e
    tables = (nch.reshape(-1), loff.reshape(-1), goff.reshape(-1), tot, padn, pads, tile_end[-1:])
    loff_f = jnp.zeros((nblk, 1, LANES), F32).at[:, 0, :N_EXPERTS].set(loff.astype(F32))

    max_rows = t * TOP_K + nblk * N_EXPERTS * (CHUNK_ROWS - 1) + N_EXPERTS * (EXPERT_TILE - 1)
    max_tiles = -(-max_rows // EXPERT_TILE)
    tile_ids = jnp.arange(max_tiles, dtype=i32)
    n_valid = tile_end[-1]
    tile_src = jnp.minimum(tile_ids, n_valid - 1)
    tile_expert = jnp.sum((tile_src[:, None] >= tile_end[None, :]).astype(i32), axis=1)
    tile_valid = (tile_ids < n_valid).astype(i32)
    tile_first = jnp.concatenate([jnp.ones((1,), i32), (tile_expert[1:] != tile_expert[:-1]).astype(i32)])

    xs, lpos = _moe_dispatch(tables, eid, wts, hn, loff_f, max_tiles * EXPERT_TILE)
    o_sorted = _moe_experts(tile_expert, tile_valid, tile_first, tile_src, xs, w1, w3, w2, layer)
    return _moe_combine(tables, lpos, x, mod, final_g, o_sorted, final_norm)


def _router_pack(wg, bg, we, be):
    d = wg.shape[0]
    wr = jnp.zeros((d, LANES), F32).at[:, :N_EXPERTS].set(we).at[:, N_EXPERTS:N_EXPERTS + N_GROUPS].set(wg)
    br = jnp.zeros((1, LANES), F32).at[0, :N_EXPERTS].set(be).at[0, N_EXPERTS:N_EXPERTS + N_GROUPS].set(bg)
    wr_hi = wr.astype(BF16)
    wr_lo = (wr - wr_hi.astype(F32)).astype(BF16)
    return wr_hi, wr_lo, br


def kernel(x, c, ada_w, ada_b, norm1_g, norm2_g, ssm_w_in, ssm_lam_re, ssm_lam_im, ssm_log_dt, ssm_b_re, ssm_b_im, ssm_c_re, ssm_c_im, ssm_d, ssm_w_glu, conv_w_in, conv_w, conv_w_out, moe_wg, moe_bg, moe_we, moe_be, moe_w1, moe_w3, moe_w2, final_g):
    b, l, d = x.shape
    depth = ada_w.shape[0]
    c8 = jnp.zeros((8, d), F32).at[:b].set(c)
    mod_all = _adaln(c8, ada_w, ada_b)[:, :b].reshape(depth, b, 6, d)
    mod_all = jnp.concatenate([mod_all, jnp.zeros((depth, b, 2, d), F32)], axis=2)
    fg = final_g.reshape(1, d)

    mod = mod_all[0]
    u4 = _s5_in(x, mod, norm1_g[0:1], ssm_w_in[0].T.astype(BF16))
    kern, f_mat, e_mat, a_pack, d_row = _s5_tables(
        ssm_lam_re[0], ssm_lam_im[0], ssm_log_dt[0], ssm_b_re[0], ssm_b_im[0],
        ssm_c_re[0], ssm_c_im[0], ssm_d[0])
    y4 = _s5_scan(u4, kern, f_mat, e_mat, a_pack, d_row)
    wrh, wrl, br = _router_pack(moe_wg[0], moe_bg[0], moe_we[0], moe_be[0])
    x1, hn, eid, wts, cnt = _s5_out(y4, x, mod, ssm_w_glu[0].astype(BF16), norm2_g[0:1], wrh, wrl, br)
    x2 = _moe(x1, hn, eid, wts, cnt, mod, moe_w1, moe_w3, moe_w2, 0, fg, False)

    mod = mod_all[1]
    cw8 = jnp.zeros((8, d), F32).at[:conv_w.shape[1]].set(conv_w[0])
    wrh, wrl, br = _router_pack(moe_wg[1], moe_bg[1], moe_we[1], moe_be[1])
    x3, hn, eid, wts, cnt = _conv_mixer(x2, mod, norm1_g[1:2], conv_w_in[0].astype(BF16), cw8,
                                        conv_w_out[0].astype(BF16), norm2_g[1:2], wrh, wrl, br)
    return _moe(x3, hn, eid, wts, cnt, mod, moe_w1, moe_w3, moe_w2, 1, fg, True)
```

```python
import functools
import math

import jax
import jax.numpy as jnp
from jax import lax
from jax.experimental import pallas as pl
from jax.experimental.pallas import tpu as pltpu

F32 = jnp.float32
BF16 = jnp.bfloat16
HIGHEST = lax.Precision.HIGHEST

RMS_EPS = 1e-6
SSM_GROUP = 16
SSM_CHUNK = 128
N_GROUPS = 4
EXPERTS_PER_GROUP = 8
N_EXPERTS = N_GROUPS * EXPERTS_PER_GROUP
TOP_K = 2
LANES = 128
TOKEN_TILE = 512
S5_TILE = 1024
MOE_BLOCK = TOKEN_TILE
CHUNK_ROWS = 8
LOCAL_ROWS = -(-(TOP_K * MOE_BLOCK + N_EXPERTS * (CHUNK_ROWS - 1)) // LANES) * LANES
EXPERT_TILE = 512
VMEM_LIMIT = 56 * 1024 * 1024
NEG_INF = -1e30


def _cparams(*sem):
    return pltpu.CompilerParams(dimension_semantics=sem, vmem_limit_bytes=VMEM_LIMIT)


def _adaln_kernel(c_ref, w_ref, b_ref, o_ref):
    c = c_ref[...]
    cond = c * jax.nn.sigmoid(c)
    o_ref[0] = jnp.dot(cond, w_ref[0], precision=HIGHEST, preferred_element_type=F32) + b_ref[0]


def _adaln(c8, ada_w, ada_b):
    depth, d, n = ada_w.shape
    tn = 1536
    return pl.pallas_call(
        _adaln_kernel,
        grid=(depth, n // tn),
        in_specs=[pl.BlockSpec((8, d), lambda i, j: (0, 0)),
                  pl.BlockSpec((1, d, tn), lambda i, j: (i, 0, j)),
                  pl.BlockSpec((1, 1, tn), lambda i, j: (i, 0, j))],
        out_specs=pl.BlockSpec((1, 8, tn), lambda i, j: (i, 0, j)),
        out_shape=jax.ShapeDtypeStruct((depth, 8, n), F32),
        compiler_params=_cparams("parallel", "parallel"),
        name="adaln",
    )(c8, ada_w, ada_b.reshape(depth, 1, n))


def _norm_mod(x, g, shift, scale):
    y = x * lax.rsqrt(jnp.mean(x * x, axis=-1, keepdims=True) + RMS_EPS)
    return (y * g) * (1.0 + scale) + shift


def _route(hn, hn_hi, wrh_ref, wrl_ref, br_ref, eid_ref, wts_ref, cnt_ref):
    hn_lo = (hn - hn_hi.astype(F32)).astype(BF16)
    wrh = wrh_ref[...]
    logits = (jnp.dot(hn_hi, wrh, preferred_element_type=F32)
              + jnp.dot(hn_lo, wrh, preferred_element_type=F32)
              + jnp.dot(hn_hi, wrl_ref[...], preferred_element_type=F32)) + br_ref[...]
    lane = lax.broadcasted_iota(jnp.int32, logits.shape, 1)
    is_grp = (lane >= N_EXPERTS) & (lane < N_EXPERTS + N_GROUPS)
    lg = jnp.where(is_grp, logits, NEG_INF)
    gmax = jnp.max(lg, axis=-1, keepdims=True)
    gsum = jnp.sum(jnp.where(is_grp, jnp.exp(lg - gmax), 0.0), axis=-1, keepdims=True)
    gp = 1.0 / gsum
    gi = jnp.min(jnp.where(lg == gmax, lane, 2 * LANES), axis=-1, keepdims=True) - N_EXPERTS
    in_grp = (lane < N_EXPERTS) & ((lane // EXPERTS_PER_GROUP) == gi)
    le = jnp.where(in_grp, logits, NEG_INF)
    v1 = jnp.max(le, axis=-1, keepdims=True)
    i1 = jnp.min(jnp.where(le == v1, lane, 2 * LANES), axis=-1, keepdims=True)
    le2 = jnp.where(lane == i1, NEG_INF, le)
    v2 = jnp.max(le2, axis=-1, keepdims=True)
    i2 = jnp.min(jnp.where(le2 == v2, lane, 2 * LANES), axis=-1, keepdims=True)
    e2 = jnp.exp(v2 - v1)
    den = 1.0 + e2
    col = lax.broadcasted_iota(jnp.int32, (hn.shape[0], TOP_K), 1)
    eid_ref[...] = jnp.where(col == 0, i1, i2)
    wts_ref[...] = jnp.where(col == 0, gp / den, gp * e2 / den)
    chosen = ((lane == i1) | (lane == i2)).astype(F32)
    cnt_ref[...] = jnp.broadcast_to(jnp.sum(chosen, axis=0, keepdims=True), cnt_ref.shape)


def _residual_and_route(x, out, mod, g2_ref, wrh_ref, wrl_ref, br_ref, x_out_ref, hn_ref, eid_ref, wts_ref, cnt_ref):
    x1 = x + (1.0 + mod[2:3]) * out
    x_out_ref[...] = x1
    hn = _norm_mod(x1, g2_ref[...], mod[3:4], mod[4:5])
    hn_hi = hn.astype(BF16)
    hn_ref[...] = hn_hi
    _route(hn, hn_hi, wrh_ref, wrl_ref, br_ref, eid_ref, wts_ref, cnt_ref)


def _s5_in_kernel(x_ref, mod_ref, g_ref, wt_ref, u_ref):
    mod = mod_ref[0]
    hn = _norm_mod(x_ref[0], g_ref[...], mod[0:1], mod[1:2])
    ut = lax.dot_general(wt_ref[...], hn.astype(BF16), (((1,), (1,)), ((), ())), preferred_element_type=F32)
    u_ref[0] = ut.reshape(ut.shape[0], S5_TILE // SSM_CHUNK, SSM_CHUNK)


def _s5_in(x, mod, g, wt_bf):
    b, l, d = x.shape
    h = wt_bf.shape[0]
    cpt = S5_TILE // SSM_CHUNK
    return pl.pallas_call(
        _s5_in_kernel,
        grid=(b, l // S5_TILE),
        in_specs=[pl.BlockSpec((1, S5_TILE, d), lambda i, j: (i, j, 0)),
                  pl.BlockSpec((1, 8, d), lambda i, j: (i, 0, 0)),
                  pl.BlockSpec((1, d), lambda i, j: (0, 0)),
                  pl.BlockSpec((h, d), lambda i, j: (0, 0))],
        out_specs=pl.BlockSpec((1, h, cpt, SSM_CHUNK), lambda i, j: (i, 0, j, 0)),
        out_shape=jax.ShapeDtypeStruct((b, h, l // SSM_CHUNK, SSM_CHUNK), F32),
        compiler_params=_cparams("parallel", "parallel"),
        name="s5_in",
    )(x, mod, g, wt_bf)


def _s5_scan_kernel(u_ref, k_ref, f_ref, e_ref, a_ref, d_ref, y_ref, m_ref, s_ref, sw_ref, sp_ref):
    bsz, grp, n_chunks, tc = u_ref.shape
    srow = lax.broadcasted_iota(jnp.int32, (tc, tc), 0)
    tcol = lax.broadcasted_iota(jnp.int32, (tc, tc), 1)
    causal = tcol >= srow

    def build(i, carry):
        for j in range(grp):
            row = k_ref[0, pl.ds(i * grp + j, 1), :]
            blk = pltpu.roll(jnp.broadcast_to(row, (tc, tc)), 0, 1, stride=1, stride_axis=0)
            m_ref[pl.ds(pl.multiple_of(i * tc, tc), tc), j * tc:(j + 1) * tc] = (
                jnp.where(causal, blk, 0.0).astype(BF16))
        return carry

    lax.fori_loop(0, grp, build, 0)

    x = jnp.concatenate(
        [jnp.concatenate([u_ref[b, k] for k in range(grp)], axis=1) for b in range(bsz)], axis=0)
    xb = x.astype(BF16)
    s_loc = jnp.dot(xb, f_ref[0], preferred_element_type=F32)
    s_ref[...] = s_loc
    half = s_loc.shape[1] // 2
    sw_ref[...] = jnp.concatenate([s_loc[:, half:], s_loc[:, :half]], axis=1)
    a1 = a_ref[0, 0:1, :]
    a2 = a_ref[0, 1:2, :]
    s = [jnp.zeros((1, s_loc.shape[1]), F32) for _ in range(bsz)]
    sw = [jnp.zeros((1, s_loc.shape[1]), F32) for _ in range(bsz)]
    for c in range(n_chunks):
        for b in range(bsz):
            r = b * n_chunks + c
            sp_ref[r:r + 1, :] = s[b]
            s_new = a1 * s[b] + a2 * sw[b] + s_ref[r:r + 1, :]
            sw[b] = a1 * sw[b] - a2 * s[b] + sw_ref[r:r + 1, :]
            s[b] = s_new
    y = jnp.dot(xb, m_ref[...], preferred_element_type=F32)
    y = y + jnp.dot(sp_ref[...].astype(BF16), e_ref[0], preferred_element_type=F32)
    y = y + d_ref[0] * x
    for b in range(bsz):
        for j in range(grp):
            y_ref[b, j] = y[b * n_chunks:(b + 1) * n_chunks, j * tc:(j + 1) * tc]


def _s5_scan(u4, kern, f_bf, e_bf, a_pack, d_row):
    b, h, nc, tc = u4.shape
    g = kern.shape[0]
    grp = h // g
    k = grp * tc
    p2 = f_bf.shape[2]
    rows = b * nc
    return pl.pallas_call(
        _s5_scan_kernel,
        grid=(g,),
        in_specs=[pl.BlockSpec((b, grp, nc, tc), lambda i: (0, i, 0, 0)),
                  pl.BlockSpec((1, grp * grp, tc), lambda i: (i, 0, 0)),
                  pl.BlockSpec((1, k, p2), lambda i: (i, 0, 0)),
                  pl.BlockSpec((1, p2, k), lambda i: (i, 0, 0)),
                  pl.BlockSpec((1, 8, p2), lambda i: (i, 0, 0)),
                  pl.BlockSpec((1, 1, k), lambda i: (i, 0, 0))],
        out_specs=pl.BlockSpec((b, grp, nc, tc), lambda i: (0, i, 0, 0)),
        out_shape=jax.ShapeDtypeStruct(u4.shape, F32),
        scratch_shapes=[pltpu.VMEM((k, k), BF16), pltpu.VMEM((rows, p2), F32),
                        pltpu.VMEM((rows, p2), F32), pltpu.VMEM((rows, p2), F32)],
        compiler_params=_cparams("parallel"),
        name="s5_scan",
    )(u4, kern, f_bf, e_bf, a_pack, d_row)


def _s5_tables(lam_re, lam_im, log_dt, b_re, b_im, c_re, c_im, d_skip):
    g, p = lam_re.shape
    k = SSM_GROUP
    tc = SSM_CHUNK
    dt = jnp.exp(log_dt)[:, None]
    mag = jnp.exp(lam_re * dt)
    ab_re = mag * jnp.cos(lam_im * dt)
    ab_im = mag * jnp.sin(lam_im * dt)
    den = lam_re * lam_re + lam_im * lam_im
    cf_re = ((ab_re - 1) * lam_re + ab_im * lam_im) / den
    cf_im = (ab_im * lam_re - (ab_re - 1) * lam_im) / den
    bb_re = cf_re[..., None] * b_re - cf_im[..., None] * b_im
    bb_im = cf_re[..., None] * b_im + cf_im[..., None] * b_re
    lags = jnp.arange(tc + 1, dtype=F32)[:, None, None]
    pmag = jnp.exp(lags * (lam_re * dt)[None])
    pang = lags * (lam_im * dt)[None]
    p_re = pmag * jnp.cos(pang)
    p_im = pmag * jnp.sin(pang)
    cb_re = jnp.einsum('gjp,gpi->gpij', c_re, bb_re) - jnp.einsum('gjp,gpi->gpij', c_im, bb_im)
    cb_im = jnp.einsum('gjp,gpi->gpij', c_re, bb_im) + jnp.einsum('gjp,gpi->gpij', c_im, bb_re)
    kern = (jnp.einsum('lgp,gpij->gijl', p_re[:tc], cb_re, precision=HIGHEST)
            - jnp.einsum('lgp,gpij->gijl', p_im[:tc], cb_im, precision=HIGHEST))
    kern = kern.reshape(g, k * k, tc)
    pr = p_re[:tc][::-1]
    pi = p_im[:tc][::-1]
    f_re = jnp.einsum('sgp,gpi->gisp', pr, bb_re) - jnp.einsum('sgp,gpi->gisp', pi, bb_im)
    f_im = jnp.einsum('sgp,gpi->gisp', pr, bb_im) + jnp.einsum('sgp,gpi->gisp', pi, bb_re)
    f_mat = jnp.concatenate([f_re, f_im], axis=-1).reshape(g, k * tc, 2 * p).astype(BF16)
    qr = p_re[1:]
    qi = p_im[1:]
    e_re = jnp.einsum('gjp,tgp->gpjt', c_re, qr) - jnp.einsum('gjp,tgp->gpjt', c_im, qi)
    e_im = -(jnp.einsum('gjp,tgp->gpjt', c_re, qi) + jnp.einsum('gjp,tgp->gpjt', c_im, qr))
    e_mat = jnp.concatenate([e_re, e_im], axis=1).reshape(g, 2 * p, k * tc).astype(BF16)
    ar = p_re[tc]
    ai = p_im[tc]
    a1 = jnp.concatenate([ar, ar], axis=-1)
    a2 = jnp.concatenate([-ai, ai], axis=-1)
    a_pack = jnp.concatenate([a1[:, None], a2[:, None], jnp.zeros((g, 6, 2 * p), F32)], axis=1)
    d_row = jnp.repeat(d_skip.reshape(g, k), tc, axis=1).reshape(g, 1, k * tc)
    return kern, f_mat, e_mat, a_pack, d_row


def _gelu_tanh(x):
    return 0.5 * x * (1.0 + jnp.tanh(math.sqrt(2.0 / math.pi) * (x + 0.044715 * (x * x * x))))


def _s5_out_kernel(y_ref, x_ref, mod_ref, w_ref, g2_ref, wrh_ref, wrl_ref, br_ref,
                   x_out_ref, hn_ref, eid_ref, wts_ref, cnt_ref):
    mod = mod_ref[0]
    h = y_ref.shape[1]
    yt = _gelu_tanh(y_ref[0].reshape(h, S5_TILE))
    d = w_ref.shape[1] // 2
    for part in range(S5_TILE // MOE_BLOCK):
        rows = slice(part * MOE_BLOCK, (part + 1) * MOE_BLOCK)
        o = lax.dot_general(yt[:, rows].astype(BF16), w_ref[...], (((0,), (0,)), ((), ())),
                            preferred_element_type=F32)
        out = o[:, :d] * jax.nn.sigmoid(o[:, d:])
        _residual_and_route(x_ref[0, rows, :], out, mod, g2_ref, wrh_ref, wrl_ref, br_ref,
                            x_out_ref.at[0, rows, :], hn_ref.at[rows, :], eid_ref.at[rows, :],
                            wts_ref.at[rows, :], cnt_ref.at[part])


def _mixer_out_specs(b, l, d, tile):
    nt = l // tile
    nblk = tile // MOE_BLOCK
    specs = [pl.BlockSpec((1, tile, d), lambda i, j: (i, j, 0)),
             pl.BlockSpec((tile, d), lambda i, j: (i * nt + j, 0)),
             pl.BlockSpec((tile, TOP_K), lambda i, j: (i * nt + j, 0)),
             pl.BlockSpec((tile, TOP_K), lambda i, j: (i * nt + j, 0)),
             pl.BlockSpec((nblk, 8, LANES), lambda i, j: (i * nt + j, 0, 0))]
    shapes = [jax.ShapeDtypeStruct((b, l, d), F32),
              jax.ShapeDtypeStruct((b * l, d), BF16),
              jax.ShapeDtypeStruct((b * l, TOP_K), jnp.int32),
              jax.ShapeDtypeStruct((b * l, TOP_K), F32),
              jax.ShapeDtypeStruct((b * l // MOE_BLOCK, 8, LANES), F32)]
    return specs, shapes


def _s5_out(y4, x, mod, w_bf, g2, wrh, wrl, br):
    b, l, d = x.shape
    h = y4.shape[1]
    specs, shapes = _mixer_out_specs(b, l, d, S5_TILE)
    return pl.pallas_call(
        _s5_out_kernel,
        grid=(b, l // S5_TILE),
        in_specs=[pl.BlockSpec((1, h, S5_TILE // SSM_CHUNK, SSM_CHUNK), lambda i, j: (i, 0, j, 0)),
                  pl.BlockSpec((1, S5_TILE, d), lambda i, j: (i, j, 0)),
                  pl.BlockSpec((1, 8, d), lambda i, j: (i, 0, 0)),
                  pl.BlockSpec(w_bf.shape, lambda i, j: (0, 0), pipeline_mode=pl.Buffered(1)),
                  pl.BlockSpec((1, d), lambda i, j: (0, 0)),
                  pl.BlockSpec((d, LANES), lambda i, j: (0, 0)),
                  pl.BlockSpec((d, LANES), lambda i, j: (0, 0)),
                  pl.BlockSpec((1, LANES), lambda i, j: (0, 0))],
        out_specs=specs,
        out_shape=shapes,
        compiler_params=_cparams("parallel", "parallel"),
        name="s5_out",
    )(y4, x, mod, w_bf, g2, wrh, wrl, br)


def _conv_mixer_kernel(x_ref, mod_ref, g1_ref, win_ref, cw_ref, wout_ref, g2_ref, wrh_ref, wrl_ref, br_ref,
                       x_out_ref, hn_ref, eid_ref, wts_ref, cnt_ref, carry_ref):
    @pl.when(pl.program_id(1) == 0)
    def _():
        carry_ref[...] = jnp.zeros_like(carry_ref)

    mod = mod_ref[0]
    x = x_ref[0]
    hn = _norm_mod(x, g1_ref[...], mod[0:1], mod[1:2])
    p = jnp.dot(hn.astype(BF16), win_ref[...], preferred_element_type=F32)
    d = p.shape[1] // 3
    b_g = p[:, :d]
    u = p[:, d:2 * d] * p[:, 2 * d:]
    ext = jnp.concatenate([carry_ref[...], u], axis=0)
    t = u.shape[0]
    cw = cw_ref[...]
    z = cw[0:1] * ext[6:6 + t] + cw[1:2] * ext[7:7 + t] + cw[2:3] * u
    carry_ref[...] = u[t - 8:]
    out = jnp.dot((b_g * z).astype(BF16), wout_ref[...], preferred_element_type=F32)
    _residual_and_route(x, out, mod, g2_ref, wrh_ref, wrl_ref, br_ref, x_out_ref.at[0], hn_ref, eid_ref,
                        wts_ref, cnt_ref.at[0])


def _conv_mixer(x, mod, g1, win_bf, cw8, wout_bf, g2, wrh, wrl, br):
    b, l, d = x.shape
    specs, shapes = _mixer_out_specs(b, l, d, TOKEN_TILE)
    return pl.pallas_call(
        _conv_mixer_kernel,
        grid=(b, l // TOKEN_TILE),
        in_specs=[pl.BlockSpec((1, TOKEN_TILE, d), lambda i, j: (i, j, 0)),
                  pl.BlockSpec((1, 8, d), lambda i, j: (i, 0, 0)),
                  pl.BlockSpec((1, d), lambda i, j: (0, 0)),
                  pl.BlockSpec(win_bf.shape, lambda i, j: (0, 0)),
                  pl.BlockSpec((8, d), lambda i, j: (0, 0)),
                  pl.BlockSpec(wout_bf.shape, lambda i, j: (0, 0)),
                  pl.BlockSpec((1, d), lambda i, j: (0, 0)),
                  pl.BlockSpec((d, LANES), lambda i, j: (0, 0)),
                  pl.BlockSpec((d, LANES), lambda i, j: (0, 0)),
                  pl.BlockSpec((1, LANES), lambda i, j: (0, 0))],
        out_specs=specs,
        out_shape=shapes,
        scratch_shapes=[pltpu.VMEM((8, d), F32)],
        compiler_params=_cparams("parallel", "arbitrary"),
        name="conv_mixer",
    )(x, mod, g1, win_bf, cw8, wout_bf, g2, wrh, wrl, br)


_TN = (((0,), (0,)), ((), ()))


def _chunk_copy(src, s_row, dst, d_row, sem):
    return pltpu.make_async_copy(src.at[pl.ds(s_row, CHUNK_ROWS), :], dst.at[pl.ds(d_row, CHUNK_ROWS), :], sem)


def _block_chunk_copies(blk, nch_ref, loff_ref, goff_ref, make_copy):
    def per_expert(e, carry):
        idx = blk * N_EXPERTS + e
        lo = loff_ref[idx]
        go = goff_ref[idx]

        def per_chunk(j, c):
            make_copy(pl.multiple_of(lo + j * CHUNK_ROWS, CHUNK_ROWS),
                      pl.multiple_of(go + j * CHUNK_ROWS, CHUNK_ROWS)).start()
            return c

        lax.fori_loop(0, nch_ref[idx], per_chunk, 0)
        return carry

    lax.fori_loop(0, N_EXPERTS, per_expert, 0)


def _one_hot_slots(lp, width):
    slot = lax.broadcasted_iota(jnp.int32, (lp.shape[0], width), 1)
    return (slot == lp).astype(BF16)


def _split3_lanes(w):
    hi = w.astype(BF16).astype(F32)
    mid = (w - hi).astype(BF16).astype(F32)
    lo = (w - hi - mid).astype(BF16).astype(F32)
    lane = lax.broadcasted_iota(jnp.int32, (w.shape[0], LANES), 1)
    return jnp.where(lane == 0, hi, jnp.where(lane == 1, mid, jnp.where(lane == 2, lo, 0.0))).astype(BF16)


def _moe_dispatch_kernel(nch_ref, loff_ref, goff_ref, tot_ref, padn_ref, pads_ref, nv_ref,
                         eid_ref, wts_ref, hn_ref, lofff_ref, xs_ref, lpos_ref, loc_ref, zero_ref, sems):
    blk = pl.program_id(0)
    d = hn_ref.shape[1]
    eid = eid_ref[...]
    t = eid.shape[0]
    lane = lax.broadcasted_iota(jnp.int32, (t, LANES), 1)
    hit0 = lane == eid[:, 0:1]
    hit1 = lane == eid[:, 1:2]
    onehot = (hit0 | hit1).astype(BF16)
    r = lax.broadcasted_iota(jnp.int32, (t, t), 0)
    c = lax.broadcasted_iota(jnp.int32, (t, t), 1)
    tri = (c < r).astype(BF16)
    before = jnp.dot(tri, onehot, preferred_element_type=F32) + lofff_ref[0]
    lp0 = jnp.sum(jnp.where(hit0, before, 0.0), axis=-1, keepdims=True).astype(jnp.int32)
    lp1 = jnp.sum(jnp.where(hit1, before, 0.0), axis=-1, keepdims=True).astype(jnp.int32)
    col = lax.broadcasted_iota(jnp.int32, (t, TOP_K), 1)
    lpos_ref[...] = jnp.where(col == 0, lp0, lp1)
    pt0 = _one_hot_slots(lp0, LOCAL_ROWS)
    pt1 = _one_hot_slots(lp1, LOCAL_ROWS)

    buf = blk % 2
    loc = loc_ref.at[buf]

    def drain(which, count):
        def body(j, carry):
            _chunk_copy(loc_ref.at[which], 0, xs_ref, 0, sems.at[which]).wait()
            return carry

        lax.fori_loop(0, count, body, 0)

    @pl.when(blk >= 2)
    def _():
        drain(buf, tot_ref[blk - 2])

    loc[:, :d] = lax.dot_general(pt0 + pt1, hn_ref[...], _TN, preferred_element_type=F32)
    wts = wts_ref[...]
    loc[:, d:] = (lax.dot_general(pt0, _split3_lanes(wts[:, 0:1]), _TN, preferred_element_type=F32)
                  + lax.dot_general(pt1, _split3_lanes(wts[:, 1:2]), _TN, preferred_element_type=F32))

    _block_chunk_copies(blk, nch_ref, loff_ref, goff_ref,
                        lambda lo, go: _chunk_copy(loc, lo, xs_ref, go, sems.at[buf]))

    @pl.when(blk == pl.num_programs(0) - 1)
    def _():
        drain(1 - buf, tot_ref[blk - 1])
        drain(buf, tot_ref[blk])
        sem = sems.at[0]
        zero_ref[...] = jnp.zeros_like(zero_ref)

        def per_expert(e, carry):
            def per_chunk(j, c):
                _chunk_copy(zero_ref, 0, xs_ref, pl.multiple_of(pads_ref[e] + j * CHUNK_ROWS, CHUNK_ROWS), sem).start()
                return c

            lax.fori_loop(0, padn_ref[e], per_chunk, 0)
            return carry

        lax.fori_loop(0, N_EXPERTS, per_expert, 0)

        def per_expert_wait(e, carry):
            def per_chunk(j, c):
                _chunk_copy(zero_ref, 0, xs_ref, 0, sem).wait()
                return c

            lax.fori_loop(0, padn_ref[e], per_chunk, 0)
            return carry

        lax.fori_loop(0, N_EXPERTS, per_expert_wait, 0)

        def tile_copy(i):
            return pltpu.make_async_copy(zero_ref, xs_ref.at[pl.ds(pl.multiple_of(i * EXPERT_TILE, EXPERT_TILE),
                                                                   EXPERT_TILE), :], sem)

        n_tiles = xs_ref.shape[0] // EXPERT_TILE

        def start_tile(i, c):
            tile_copy(i).start()
            return c

        def wait_tile(i, c):
            tile_copy(i).wait()
            return c

        lax.fori_loop(nv_ref[0], n_tiles, start_tile, 0)
        lax.fori_loop(nv_ref[0], n_tiles, wait_tile, 0)


def _moe_dispatch(tables, eid, wts, hn, loff_f, n_rows):
    t, d = hn.shape
    width = d + LANES
    grid_spec = pltpu.PrefetchScalarGridSpec(
        num_scalar_prefetch=7,
        grid=(t // MOE_BLOCK,),
        in_specs=[pl.BlockSpec((MOE_BLOCK, TOP_K), lambda i, *_: (i, 0)),
                  pl.BlockSpec((MOE_BLOCK, TOP_K), lambda i, *_: (i, 0)),
                  pl.BlockSpec((MOE_BLOCK, d), lambda i, *_: (i, 0)),
                  pl.BlockSpec((1, 1, LANES), lambda i, *_: (i, 0, 0))],
        out_specs=[pl.BlockSpec(memory_space=pl.ANY),
                   pl.BlockSpec((MOE_BLOCK, TOP_K), lambda i, *_: (i, 0))],
        scratch_shapes=[pltpu.VMEM((2, LOCAL_ROWS, width), F32), pltpu.VMEM((EXPERT_TILE, width), F32),
                        pltpu.SemaphoreType.DMA((2,))],
    )
    return pl.pallas_call(
        _moe_dispatch_kernel,
        grid_spec=grid_spec,
        out_shape=[jax.ShapeDtypeStruct((n_rows, width), F32),
                   jax.ShapeDtypeStruct((t, TOP_K), jnp.int32)],
        compiler_params=_cparams("arbitrary"),
        name="moe_dispatch",
    )(*tables, eid, wts, hn, loff_f)


def _moe_experts_kernel(te_ref, tv_ref, tf_ref, ts_ref, xs_ref, w1_ref, w3_ref, w2_ref, o_ref,
                        w1b_ref, w3b_ref, w2b_ref):
    del te_ref, ts_ref
    i = pl.program_id(0)

    @pl.when(tf_ref[i] != 0)
    def _():
        w1b_ref[...] = w1_ref[0, 0].astype(BF16)
        w3b_ref[...] = w3_ref[0, 0].astype(BF16)
        w2b_ref[...] = w2_ref[0, 0].astype(BF16)

    @pl.when(tv_ref[i] != 0)
    def _():
        d = o_ref.shape[1]
        xb = xs_ref[:, :d].astype(BF16)
        wl = xs_ref[:, d:]
        w = wl[:, 0:1] + wl[:, 1:2] + wl[:, 2:3]
        a = jnp.dot(xb, w1b_ref[...], preferred_element_type=F32)
        b = jnp.dot(xb, w3b_ref[...], preferred_element_type=F32)
        h = (a * jax.nn.sigmoid(a) * b).astype(BF16)
        o_ref[...] = jnp.dot(h, w2b_ref[...], preferred_element_type=F32) * w

    @pl.when(tv_ref[i] == 0)
    def _():
        o_ref[...] = jnp.zeros_like(o_ref)


def _moe_experts(tile_expert, tile_valid, tile_first, tile_src, xs, w1, w3, w2, layer):
    r, width = xs.shape
    d = width - LANES
    de = w1.shape[3]
    grid_spec = pltpu.PrefetchScalarGridSpec(
        num_scalar_prefetch=4,
        grid=(r // EXPERT_TILE,),
        in_specs=[pl.BlockSpec((EXPERT_TILE, width), lambda i, te, tv, tf, ts: (ts[i], 0)),
                  pl.BlockSpec((1, 1, d, de), lambda i, te, tv, tf, ts: (layer, te[i], 0, 0)),
                  pl.BlockSpec((1, 1, d, de), lambda i, te, tv, tf, ts: (layer, te[i], 0, 0)),
                  pl.BlockSpec((1, 1, de, d), lambda i, te, tv, tf, ts: (layer, te[i], 0, 0))],
        out_specs=pl.BlockSpec((EXPERT_TILE, d), lambda i, te, tv, tf, ts: (i, 0)),
        scratch_shapes=[pltpu.VMEM((d, de), BF16), pltpu.VMEM((d, de), BF16), pltpu.VMEM((de, d), BF16)],
    )
    return pl.pallas_call(
        _moe_experts_kernel,
        grid_spec=grid_spec,
        out_shape=jax.ShapeDtypeStruct((r, d), F32),
        compiler_params=_cparams("arbitrary"),
        name="moe_experts",
    )(tile_expert, tile_valid, tile_first, tile_src, xs, w1, w3, w2)


def _moe_combine_kernel(nch_ref, loff_ref, goff_ref, tot_ref, lpos_ref, x_ref, mod_ref, fg_ref, o_hbm_ref,
                        out_ref, loc_ref, sems, *, final_norm):
    blk = pl.program_id(0) * pl.num_programs(1) + pl.program_id(1)
    n_blk = pl.num_programs(0) * pl.num_programs(1)
    t = x_ref.shape[1]
    buf = blk % 2

    def fetch(b, which):
        loc = loc_ref.at[which]
        loc[TOP_K * t:, :] = jnp.zeros((LOCAL_ROWS - TOP_K * t, loc_ref.shape[2]), F32)
        _block_chunk_copies(b, nch_ref, loff_ref, goff_ref,
                            lambda lo, go: _chunk_copy(o_hbm_ref, go, loc, lo, sems.at[which]))

    @pl.when(blk == 0)
    def _():
        fetch(blk, buf)

    @pl.when(blk + 1 < n_blk)
    def _():
        fetch(blk + 1, 1 - buf)

    def drain(j, carry):
        _chunk_copy(o_hbm_ref, 0, loc_ref.at[buf], 0, sems.at[buf]).wait()
        return carry

    lax.fori_loop(0, tot_ref[blk], drain, 0)
    lp = lpos_ref[...]
    pt = _one_hot_slots(lp[:, 0:1], LOCAL_ROWS) + _one_hot_slots(lp[:, 1:2], LOCAL_ROWS)
    y = jnp.dot(pt, loc_ref[buf].astype(BF16), preferred_element_type=F32)
    x2 = x_ref[0] + (1.0 + mod_ref[0, 5:6]) * y
    if final_norm:
        x2 = (x2 * lax.rsqrt(jnp.mean(x2 * x2, axis=-1, keepdims=True) + RMS_EPS)) * fg_ref[...]
    out_ref[0] = x2


def _moe_combine(tables, lpos, x, mod, final_g, o_sorted, final_norm):
    b, l, d = x.shape
    nt = l // MOE_BLOCK
    grid_spec = pltpu.PrefetchScalarGridSpec(
        num_scalar_prefetch=4,
        grid=(b, nt),
        in_specs=[pl.BlockSpec((MOE_BLOCK, TOP_K), lambda i, j, *_: (i * nt + j, 0)),
                  pl.BlockSpec((1, MOE_BLOCK, d), lambda i, j, *_: (i, j, 0)),
                  pl.BlockSpec((1, 8, d), lambda i, j, *_: (i, 0, 0)),
                  pl.BlockSpec((1, d), lambda i, j, *_: (0, 0)),
                  pl.BlockSpec(memory_space=pl.ANY)],
        out_specs=pl.BlockSpec((1, MOE_BLOCK, d), lambda i, j, *_: (i, j, 0)),
        scratch_shapes=[pltpu.VMEM((2, LOCAL_ROWS, d), F32), pltpu.SemaphoreType.DMA((2,))],
    )
    return pl.pallas_call(
        functools.partial(_moe_combine_kernel, final_norm=final_norm),
        grid_spec=grid_spec,
        out_shape=jax.ShapeDtypeStruct((b, l, d), F32),
        compiler_params=_cparams("arbitrary", "arbitrary"),
        name="moe_combine",
    )(*tables[:4], lpos, x, mod, final_g, o_sorted)


def _moe(x, hn, eid, wts, cnt, mod, w1, w3, w2, layer, final_g, final_norm):
    t, d = hn.shape
    nblk = t // MOE_BLOCK
    i32 = jnp.int32
    n = cnt[:, 0, :N_EXPERTS].astype(i32)
    run = (n + CHUNK_ROWS - 1) // CHUNK_ROWS * CHUNK_ROWS
    loff = jnp.cumsum(run, axis=1) - run
    rows_e = jnp.sum(run, axis=0)
    tiles_e = (rows_e + EXPERT_TILE - 1) // EXPERT_TILE
    tile_end = jnp.cumsum(tiles_e)
    base = (tile_end - tiles_e) * EXPERT_TILE
    goff = base[None, :] + jnp.cumsum(run, axis=0) - run
    nch = run // CHUNK_ROWS
    tot = jnp.sum(nch, axis=1)
    padn = (tiles_e * EXPERT_TILE - rows_e) // CHUNK_ROWS
    pads = base + rows_e
    tables = (nch.reshape(-1), loff.reshape(-1), goff.reshape(-1), tot, padn, pads, tile_end[-1:])
    loff_f = jnp.zeros((nblk, 1, LANES), F32).at[:, 0, :N_EXPERTS].set(loff.astype(F32))

    max_rows = t * TOP_K + nblk * N_EXPERTS * (CHUNK_ROWS - 1) + N_EXPERTS * (EXPERT_TILE - 1)
    max_tiles = -(-max_rows // EXPERT_TILE)
    tile_ids = jnp.arange(max_tiles, dtype=i32)
    n_valid = tile_end[-1]
    tile_src = jnp.minimum(tile_ids, n_valid - 1)
    tile_expert = jnp.sum((tile_src[:, None] >= tile_end[None, :]).astype(i32), axis=1)
    tile_valid = (tile_ids < n_valid).astype(i32)
    tile_first = jnp.concatenate([jnp.ones((1,), i32), (tile_expert[1:] != tile_expert[:-1]).astype(i32)])

    xs, lpos = _moe_dispatch(tables, eid, wts, hn, loff_f, max_tiles * EXPERT_TILE)
    o_sorted = _moe_experts(tile_expert, tile_valid, tile_first, tile_src, xs, w1, w3, w2, layer)
    return _moe_combine(tables, lpos, x, mod, final_g, o_sorted, final_norm)


def _router_pack(wg, bg, we, be):
    d = wg.shape[0]
    wr = jnp.zeros((d, LANES), F32).at[:, :N_EXPERTS].set(we).at[:, N_EXPERTS:N_EXPERTS + N_GROUPS].set(wg)
    br = jnp.zeros((1, LANES), F32).at[0, :N_EXPERTS].set(be).at[0, N_EXPERTS:N_EXPERTS + N_GROUPS].set(bg)
    wr_hi = wr.astype(BF16)
    wr_lo = (wr - wr_hi.astype(F32)).astype(BF16)
    return wr_hi, wr_lo, br


def kernel(x, c, ada_w, ada_b, norm1_g, norm2_g, ssm_w_in, ssm_lam_re, ssm_lam_im, ssm_log_dt, ssm_b_re, ssm_b_im, ssm_c_re, ssm_c_im, ssm_d, ssm_w_glu, conv_w_in, conv_w, conv_w_out, moe_wg, moe_bg, moe_we, moe_be, moe_w1, moe_w3, moe_w2, final_g):
    b, l, d = x.shape
    depth = ada_w.shape[0]
    c8 = jnp.zeros((8, d), F32).at[:b].set(c)
    mod_all = _adaln(c8, ada_w, ada_b)[:, :b].reshape(depth, b, 6, d)
    mod_all = jnp.concatenate([mod_all, jnp.zeros((depth, b, 2, d), F32)], axis=2)
    fg = final_g.reshape(1, d)

    mod = mod_all[0]
    u4 = _s5_in(x, mod, norm1_g[0:1], ssm_w_in[0].T.astype(BF16))
    kern, f_mat, e_mat, a_pack, d_row = _s5_tables(
        ssm_lam_re[0], ssm_lam_im[0], ssm_log_dt[0], ssm_b_re[0], ssm_b_im[0],
        ssm_c_re[0], ssm_c_im[0], ssm_d[0])
    y4 = _s5_scan(u4, kern, f_mat, e_mat, a_pack, d_row)
    wrh, wrl, br = _router_pack(moe_wg[0], moe_bg[0], moe_we[0], moe_be[0])
    x1, hn, eid, wts, cnt = _s5_out(y4, x, mod, ssm_w_glu[0].astype(BF16), norm2_g[0:1], wrh, wrl, br)
    x2 = _moe(x1, hn, eid, wts, cnt, mod, moe_w1, moe_w3, moe_w2, 0, fg, False)

    mod = mod_all[1]
    cw8 = jnp.zeros((8, d), F32).at[:conv_w.shape[1]].set(conv_w[0])
    wrh, wrl, br = _router_pack(moe_wg[1], moe_bg[1], moe_we[1], moe_be[1])
    x3, hn, eid, wts, cnt = _conv_mixer(x2, mod, norm1_g[1:2], conv_w_in[0].astype(BF16), cw8,
                                        conv_w_out[0].astype(BF16), norm2_g[1:2], wrh, wrl, br)
    return _moe(x3, hn, eid, wts, cnt, mod, moe_w1, moe_w3, moe_w2, 1, fg, True)
```

```python
import functools
import math

import jax
import jax.numpy as jnp
from jax import lax
from jax.experimental import pallas as pl
from jax.experimental.pallas import tpu as pltpu

F32 = jnp.float32
BF16 = jnp.bfloat16
HIGHEST = lax.Precision.HIGHEST

RMS_EPS = 1e-6
SSM_GROUP = 16
SSM_CHUNK = 128
N_GROUPS = 4
EXPERTS_PER_GROUP = 8
N_EXPERTS = N_GROUPS * EXPERTS_PER_GROUP
TOP_K = 2
LANES = 128
TOKEN_TILE = 512
S5_TILE = 1024
MOE_BLOCK = TOKEN_TILE
CHUNK_ROWS = 8
LOCAL_ROWS = -(-(TOP_K * MOE_BLOCK + N_EXPERTS * (CHUNK_ROWS - 1)) // LANES) * LANES
EXPERT_TILE = 512
VMEM_LIMIT = 56 * 1024 * 1024
NEG_INF = -1e30


def _cparams(*sem):
    return pltpu.CompilerParams(dimension_semantics=sem, vmem_limit_bytes=VMEM_LIMIT)


def _adaln_kernel(c_ref, w_ref, b_ref, o_ref):
    c = c_ref[...]
    cond = c * jax.nn.sigmoid(c)
    o_ref[0] = jnp.dot(cond, w_ref[0], precision=HIGHEST, preferred_element_type=F32) + b_ref[0]


def _adaln(c8, ada_w, ada_b):
    depth, d, n = ada_w.shape
    tn = 1536
    return pl.pallas_call(
        _adaln_kernel,
        grid=(depth, n // tn),
        in_specs=[pl.BlockSpec((8, d), lambda i, j: (0, 0)),
                  pl.BlockSpec((1, d, tn), lambda i, j: (i, 0, j)),
                  pl.BlockSpec((1, 1, tn), lambda i, j: (i, 0, j))],
        out_specs=pl.BlockSpec((1, 8, tn), lambda i, j: (i, 0, j)),
        out_shape=jax.ShapeDtypeStruct((depth, 8, n), F32),
        compiler_params=_cparams("parallel", "parallel"),
        name="adaln",
    )(c8, ada_w, ada_b.reshape(depth, 1, n))


def _norm_mod(x, g, shift, scale):
    y = x * lax.rsqrt(jnp.mean(x * x, axis=-1, keepdims=True) + RMS_EPS)
    return (y * g) * (1.0 + scale) + shift


def _route(hn, hn_hi, wrh_ref, wrl_ref, br_ref, eid_ref, wts_ref, cnt_ref):
    hn_lo = (hn - hn_hi.astype(F32)).astype(BF16)
    wrh = wrh_ref[...]
    logits = (jnp.dot(hn_hi, wrh, preferred_element_type=F32)
              + jnp.dot(hn_lo, wrh, preferred_element_type=F32)
              + jnp.dot(hn_hi, wrl_ref[...], preferred_element_type=F32)) + br_ref[...]
    lane = lax.broadcasted_iota(jnp.int32, logits.shape, 1)
    is_grp = (lane >= N_EXPERTS) & (lane < N_EXPERTS + N_GROUPS)
    lg = jnp.where(is_grp, logits, NEG_INF)
    gmax = jnp.max(lg, axis=-1, keepdims=True)
    gsum = jnp.sum(jnp.where(is_grp, jnp.exp(lg - gmax), 0.0), axis=-1, keepdims=True)
    gp = 1.0 / gsum
    gi = jnp.min(jnp.where(lg == gmax, lane, 2 * LANES), axis=-1, keepdims=True) - N_EXPERTS
    in_grp = (lane < N_EXPERTS) & ((lane // EXPERTS_PER_GROUP) == gi)
    le = jnp.where(in_grp, logits, NEG_INF)
    v1 = jnp.max(le, axis=-1, keepdims=True)
    i1 = jnp.min(jnp.where(le == v1, lane, 2 * LANES), axis=-1, keepdims=True)
    le2 = jnp.where(lane == i1, NEG_INF, le)
    v2 = jnp.max(le2, axis=-1, keepdims=True)
    i2 = jnp.min(jnp.where(le2 == v2, lane, 2 * LANES), axis=-1, keepdims=True)
    e2 = jnp.exp(v2 - v1)
    den = 1.0 + e2
    col = lax.broadcasted_iota(jnp.int32, (hn.shape[0], TOP_K), 1)
    eid_ref[...] = jnp.where(col == 0, i1, i2)
    wts_ref[...] = jnp.where(col == 0, gp / den, gp * e2 / den)
    chosen = ((lane == i1) | (lane == i2)).astype(F32)
    cnt_ref[...] = jnp.broadcast_to(jnp.sum(chosen, axis=0, keepdims=True), cnt_ref.shape)


def _residual_and_route(x, out, mod, g2_ref, wrh_ref, wrl_ref, br_ref, x_out_ref, hn_ref, eid_ref, wts_ref, cnt_ref):
    x1 = x + (1.0 + mod[2:3]) * out
    x_out_ref[...] = x1
    hn = _norm_mod(x1, g2_ref[...], mod[3:4], mod[4:5])
    hn_hi = hn.astype(BF16)
    hn_ref[...] = hn_hi
    _route(hn, hn_hi, wrh_ref, wrl_ref, br_ref, eid_ref, wts_ref, cnt_ref)


def _s5_in_kernel(x_ref, mod_ref, g_ref, wt_ref, u_ref):
    mod = mod_ref[0]
    hn = _norm_mod(x_ref[0], g_ref[...], mod[0:1], mod[1:2])
    ut = lax.dot_general(wt_ref[...], hn.astype(BF16), (((1,), (1,)), ((), ())), preferred_element_type=F32)
    u_ref[0] = ut.reshape(ut.shape[0], S5_TILE // SSM_CHUNK, SSM_CHUNK)


def _s5_in(x, mod, g, wt_bf):
    b, l, d = x.shape
    h = wt_bf.shape[0]
    cpt = S5_TILE // SSM_CHUNK
    return pl.pallas_call(
        _s5_in_kernel,
        grid=(b, l // S5_TILE),
        in_specs=[pl.BlockSpec((1, S5_TILE, d), lambda i, j: (i, j, 0)),
                  pl.BlockSpec((1, 8, d), lambda i, j: (i, 0, 0)),
                  pl.BlockSpec((1, d), lambda i, j: (0, 0)),
                  pl.BlockSpec((h, d), lambda i, j: (0, 0))],
        out_specs=pl.BlockSpec((1, h, cpt, SSM_CHUNK), lambda i, j: (i, 0, j, 0)),
        out_shape=jax.ShapeDtypeStruct((b, h, l // SSM_CHUNK, SSM_CHUNK), F32),
        compiler_params=_cparams("parallel", "parallel"),
        name="s5_in",
    )(x, mod, g, wt_bf)


def _s5_scan_kernel(u_ref, k_ref, f_ref, e_ref, a_ref, d_ref, y_ref, m_ref, s_ref, sw_ref, sp_ref):
    bsz, grp, n_chunks, tc = u_ref.shape
    srow = lax.broadcasted_iota(jnp.int32, (tc, tc), 0)
    tcol = lax.broadcasted_iota(jnp.int32, (tc, tc), 1)
    causal = tcol >= srow

    def build(i, carry):
        for j in range(grp):
            row = k_ref[0, pl.ds(i * grp + j, 1), :]
            blk = pltpu.roll(jnp.broadcast_to(row, (tc, tc)), 0, 1, stride=1, stride_axis=0)
            m_ref[pl.ds(pl.multiple_of(i * tc, tc), tc), j * tc:(j + 1) * tc] = (
                jnp.where(causal, blk, 0.0).astype(BF16))
        return carry

    lax.fori_loop(0, grp, build, 0)

    x = jnp.concatenate(
        [jnp.concatenate([u_ref[b, k] for k in range(grp)], axis=1) for b in range(bsz)], axis=0)
    xb = x.astype(BF16)
    s_loc = jnp.dot(xb, f_ref[0], preferred_element_type=F32)
    s_ref[...] = s_loc
    half = s_loc.shape[1] // 2
    sw_ref[...] = jnp.concatenate([s_loc[:, half:], s_loc[:, :half]], axis=1)
    a1 = a_ref[0, 0:1, :]
    a2 = a_ref[0, 1:2, :]
    s = [jnp.zeros((1, s_loc.shape[1]), F32) for _ in range(bsz)]
    sw = [jnp.zeros((1, s_loc.shape[1]), F32) for _ in range(bsz)]
    for c in range(n_chunks):
        for b in range(bsz):
            r = b * n_chunks + c
            sp_ref[r:r + 1, :] = s[b]
            s_new = a1 * s[b] + a2 * sw[b] + s_ref[r:r + 1, :]
            sw[b] = a1 * sw[b] - a2 * s[b] + sw_ref[r:r + 1, :]
            s[b] = s_new
    y = jnp.dot(xb, m_ref[...], preferred_element_type=F32)
    y = y + jnp.dot(sp_ref[...].astype(BF16), e_ref[0], preferred_element_type=F32)
    y = y + d_ref[0] * x
    for b in range(bsz):
        for j in range(grp):
            y_ref[b, j] = y[b * n_chunks:(b + 1) * n_chunks, j * tc:(j + 1) * tc]


def _s5_scan(u4, kern, f_bf, e_bf, a_pack, d_row):
    b, h, nc, tc = u4.shape
    g = kern.shape[0]
    grp = h // g
    k = grp * tc
    p2 = f_bf.shape[2]
    rows = b * nc
    return pl.pallas_call(
        _s5_scan_kernel,
        grid=(g,),
        in_specs=[pl.BlockSpec((b, grp, nc, tc), lambda i: (0, i, 0, 0)),
                  pl.BlockSpec((1, grp * grp, tc), lambda i: (i, 0, 0)),
                  pl.BlockSpec((1, k, p2), lambda i: (i, 0, 0)),
                  pl.BlockSpec((1, p2, k), lambda i: (i, 0, 0)),
                  pl.BlockSpec((1, 8, p2), lambda i: (i, 0, 0)),
                  pl.BlockSpec((1, 1, k), lambda i: (i, 0, 0))],
        out_specs=pl.BlockSpec((b, grp, nc, tc), lambda i: (0, i, 0, 0)),
        out_shape=jax.ShapeDtypeStruct(u4.shape, F32),
        scratch_shapes=[pltpu.VMEM((k, k), BF16), pltpu.VMEM((rows, p2), F32),
                        pltpu.VMEM((rows, p2), F32), pltpu.VMEM((rows, p2), F32)],
        compiler_params=_cparams("parallel"),
        name="s5_scan",
    )(u4, kern, f_bf, e_bf, a_pack, d_row)


def _s5_tables(lam_re, lam_im, log_dt, b_re, b_im, c_re, c_im, d_skip):
    g, p = lam_re.shape
    k = SSM_GROUP
    tc = SSM_CHUNK
    dt = jnp.exp(log_dt)[:, None]
    mag = jnp.exp(lam_re * dt)
    ab_re = mag * jnp.cos(lam_im * dt)
    ab_im = mag * jnp.sin(lam_im * dt)
    den = lam_re * lam_re + lam_im * lam_im
    cf_re = ((ab_re - 1) * lam_re + ab_im * lam_im) / den
    cf_im = (ab_im * lam_re - (ab_re - 1) * lam_im) / den
    bb_re = cf_re[..., None] * b_re - cf_im[..., None] * b_im
    bb_im = cf_re[..., None] * b_im + cf_im[..., None] * b_re
    lags = jnp.arange(tc + 1, dtype=F32)[:, None, None]
    pmag = jnp.exp(lags * (lam_re * dt)[None])
    pang = lags * (lam_im * dt)[None]
    p_re = pmag * jnp.cos(pang)
    p_im = pmag * jnp.sin(pang)
    cb_re = jnp.einsum('gjp,gpi->gpij', c_re, bb_re) - jnp.einsum('gjp,gpi->gpij', c_im, bb_im)
    cb_im = jnp.einsum('gjp,gpi->gpij', c_re, bb_im) + jnp.einsum('gjp,gpi->gpij', c_im, bb_re)
    kern = (jnp.einsum('lgp,gpij->gijl', p_re[:tc], cb_re, precision=HIGHEST)
            - jnp.einsum('lgp,gpij->gijl', p_im[:tc], cb_im, precision=HIGHEST))
    kern = kern.reshape(g, k * k, tc)
    pr = p_re[:tc][::-1]
    pi = p_im[:tc][::-1]
    f_re = jnp.einsum('sgp,gpi->gisp', pr, bb_re) - jnp.einsum('sgp,gpi->gisp', pi, bb_im)
    f_im = jnp.einsum('sgp,gpi->gisp', pr, bb_im) + jnp.einsum('sgp,gpi->gisp', pi, bb_re)
    f_mat = jnp.concatenate([f_re, f_im], axis=-1).reshape(g, k * tc, 2 * p).astype(BF16)
    qr = p_re[1:]
    qi = p_im[1:]
    e_re = jnp.einsum('gjp,tgp->gpjt', c_re, qr) - jnp.einsum('gjp,tgp->gpjt', c_im, qi)
    e_im = -(jnp.einsum('gjp,tgp->gpjt', c_re, qi) + jnp.einsum('gjp,tgp->gpjt', c_im, qr))
    e_mat = jnp.concatenate([e_re, e_im], axis=1).reshape(g, 2 * p, k * tc).astype(BF16)
    ar = p_re[tc]
    ai = p_im[tc]
    a1 = jnp.concatenate([ar, ar], axis=-1)
    a2 = jnp.concatenate([-ai, ai], axis=-1)
    a_pack = jnp.concatenate([a1[:, None], a2[:, None], jnp.zeros((g, 6, 2 * p), F32)], axis=1)
    d_row = jnp.repeat(d_skip.reshape(g, k), tc, axis=1).reshape(g, 1, k * tc)
    return kern, f_mat, e_mat, a_pack, d_row


def _gelu_tanh(x):
    return 0.5 * x * (1.0 + jnp.tanh(math.sqrt(2.0 / math.pi) * (x + 0.044715 * (x * x * x))))


def _s5_out_kernel(y_ref, x_ref, mod_ref, w_ref, g2_ref, wrh_ref, wrl_ref, br_ref,
                   x_out_ref, hn_ref, eid_ref, wts_ref, cnt_ref):
    mod = mod_ref[0]
    h = y_ref.shape[1]
    yt = _gelu_tanh(y_ref[0].reshape(h, S5_TILE))
    d = w_ref.shape[1] // 2
    for part in range(S5_TILE // MOE_BLOCK):
        rows = slice(part * MOE_BLOCK, (part + 1) * MOE_BLOCK)
        o = lax.dot_general(yt[:, rows].astype(BF16), w_ref[...], (((0,), (0,)), ((), ())),
                            preferred_element_type=F32)
        out = o[:, :d] * jax.nn.sigmoid(o[:, d:])
        _residual_and_route(x_ref[0, rows, :], out, mod, g2_ref, wrh_ref, wrl_ref, br_ref,
                            x_out_ref.at[0, rows, :], hn_ref.at[rows, :], eid_ref.at[rows, :],
                            wts_ref.at[rows, :], cnt_ref.at[part])


def _mixer_out_specs(b, l, d, tile):
    nt = l // tile
    nblk = tile // MOE_BLOCK
    specs = [pl.BlockSpec((1, tile, d), lambda i, j: (i, j, 0)),
             pl.BlockSpec((tile, d), lambda i, j: (i * nt + j, 0)),
             pl.BlockSpec((tile, TOP_K), lambda i, j: (i * nt + j, 0)),
             pl.BlockSpec((tile, TOP_K), lambda i, j: (i * nt + j, 0)),
             pl.BlockSpec((nblk, 8, LANES), lambda i, j: (i * nt + j, 0, 0))]
    shapes = [jax.ShapeDtypeStruct((b, l, d), F32),
              jax.ShapeDtypeStruct((b * l, d), BF16),
              jax.ShapeDtypeStruct((b * l, TOP_K), jnp.int32),
              jax.ShapeDtypeStruct((b * l, TOP_K), F32),
              jax.ShapeDtypeStruct((b * l // MOE_BLOCK, 8, LANES), F32)]
    return specs, shapes


def _s5_out(y4, x, mod, w_bf, g2, wrh, wrl, br):
    b, l, d = x.shape
    h = y4.shape[1]
    specs, shapes = _mixer_out_specs(b, l, d, S5_TILE)
    return pl.pallas_call(
        _s5_out_kernel,
        grid=(b, l // S5_TILE),
        in_specs=[pl.BlockSpec((1, h, S5_TILE // SSM_CHUNK, SSM_CHUNK), lambda i, j: (i, 0, j, 0)),
                  pl.BlockSpec((1, S5_TILE, d), lambda i, j: (i, j, 0)),
                  pl.BlockSpec((1, 8, d), lambda i, j: (i, 0, 0)),
                  pl.BlockSpec(w_bf.shape, lambda i, j: (0, 0), pipeline_mode=pl.Buffered(1)),
                  pl.BlockSpec((1, d), lambda i, j: (0, 0)),
                  pl.BlockSpec((d, LANES), lambda i, j: (0, 0)),
                  pl.BlockSpec((d, LANES), lambda i, j: (0, 0)),
                  pl.BlockSpec((1, LANES), lambda i, j: (0, 0))],
        out_specs=specs,
        out_shape=shapes,
        compiler_params=_cparams("parallel", "parallel"),
        name="s5_out",
    )(y4, x, mod, w_bf, g2, wrh, wrl, br)


def _conv_mixer_kernel(x_ref, mod_ref, g1_ref, win_ref, cw_ref, wout_ref, g2_ref, wrh_ref, wrl_ref, br_ref,
                       x_out_ref, hn_ref, eid_ref, wts_ref, cnt_ref, carry_ref):
    @pl.when(pl.program_id(1) == 0)
    def _():
        carry_ref[...] = jnp.zeros_like(carry_ref)

    mod = mod_ref[0]
    x = x_ref[0]
    hn = _norm_mod(x, g1_ref[...], mod[0:1], mod[1:2])
    p = jnp.dot(hn.astype(BF16), win_ref[...], preferred_element_type=F32)
    d = p.shape[1] // 3
    b_g = p[:, :d]
    u = p[:, d:2 * d] * p[:, 2 * d:]
    ext = jnp.concatenate([carry_ref[...], u], axis=0)
    t = u.shape[0]
    cw = cw_ref[...]
    z = cw[0:1] * ext[6:6 + t] + cw[1:2] * ext[7:7 + t] + cw[2:3] * u
    carry_ref[...] = u[t - 8:]
    out = jnp.dot((b_g * z).astype(BF16), wout_ref[...], preferred_element_type=F32)
    _residual_and_route(x, out, mod, g2_ref, wrh_ref, wrl_ref, br_ref, x_out_ref.at[0], hn_ref, eid_ref,
                        wts_ref, cnt_ref.at[0])


def _conv_mixer(x, mod, g1, win_bf, cw8, wout_bf, g2, wrh, wrl, br):
    b, l, d = x.shape
    specs, shapes = _mixer_out_specs(b, l, d, TOKEN_TILE)
    return pl.pallas_call(
        _conv_mixer_kernel,
        grid=(b, l // TOKEN_TILE),
        in_specs=[pl.BlockSpec((1, TOKEN_TILE, d), lambda i, j: (i, j, 0)),
                  pl.BlockSpec((1, 8, d), lambda i, j: (i, 0, 0)),
                  pl.BlockSpec((1, d), lambda i, j: (0, 0)),
                  pl.BlockSpec(win_bf.shape, lambda i, j: (0, 0)),
                  pl.BlockSpec((8, d), lambda i, j: (0, 0)),
                  pl.BlockSpec(wout_bf.shape, lambda i, j: (0, 0)),
                  pl.BlockSpec((1, d), lambda i, j: (0, 0)),
                  pl.BlockSpec((d, LANES), lambda i, j: (0, 0)),
                  pl.BlockSpec((d, LANES), lambda i, j: (0, 0)),
                  pl.BlockSpec((1, LANES), lambda i, j: (0, 0))],
        out_specs=specs,
        out_shape=shapes,
        scratch_shapes=[pltpu.VMEM((8, d), F32)],
        compiler_params=_cparams("parallel", "arbitrary"),
        name="conv_mixer",
    )(x, mod, g1, win_bf, cw8, wout_bf, g2, wrh, wrl, br)


_TN = (((0,), (0,)), ((), ()))


MAX_CHUNKS = LOCAL_ROWS // CHUNK_ROWS


def _rows_copy(src, s_row, dst, d_row, rows, sem):
    return pltpu.make_async_copy(src.at[pl.ds(s_row, rows), :], dst.at[pl.ds(d_row, rows), :], sem)


def _start_block_chunks(blk, count, crow_ref, make_copy):
    def body(q, carry):
        make_copy(pl.multiple_of(q * CHUNK_ROWS, CHUNK_ROWS),
                  pl.multiple_of(crow_ref[blk * MAX_CHUNKS + q], CHUNK_ROWS)).start()
        return carry

    lax.fori_loop(0, count, body, 0)


def _wait_chunks(count, max_count, make_wait):
    for k in range(max_count.bit_length()):
        @pl.when(((count >> k) & 1) == 1)
        def _():
            make_wait(CHUNK_ROWS << k).wait()


def _slot_masks(lp0, lp1):
    slot = lax.broadcasted_iota(jnp.int32, (lp0.shape[0], LOCAL_ROWS), 1)
    return slot == lp0, slot == lp1


def _moe_dispatch_kernel(crow_ref, tot_ref, padn_ref, pads_ref, nv_ref,
                         eid_ref, wts_ref, hn_ref, lofff_ref, xs_ref, lpos_ref, loc_ref, zero_ref, sems):
    blk = pl.program_id(0)
    d = hn_ref.shape[1]
    eid = eid_ref[...]
    t = eid.shape[0]
    lane = lax.broadcasted_iota(jnp.int32, (t, LANES), 1)
    hit0 = lane == eid[:, 0:1]
    hit1 = lane == eid[:, 1:2]
    onehot = (hit0 | hit1).astype(BF16)
    r = lax.broadcasted_iota(jnp.int32, (t, t), 0)
    c = lax.broadcasted_iota(jnp.int32, (t, t), 1)
    tri = (c < r).astype(BF16)
    before = jnp.dot(tri, onehot, preferred_element_type=F32) + lofff_ref[0]
    lp0 = jnp.sum(jnp.where(hit0, before, 0.0), axis=-1, keepdims=True).astype(jnp.int32)
    lp1 = jnp.sum(jnp.where(hit1, before, 0.0), axis=-1, keepdims=True).astype(jnp.int32)
    col = lax.broadcasted_iota(jnp.int32, (t, TOP_K), 1)
    lpos_ref[...] = jnp.where(col == 0, lp0, lp1)
    m0, m1 = _slot_masks(lp0, lp1)
    wts = wts_ref[...]
    wrow = jnp.sum(jnp.where(m0, wts[:, 0:1], 0.0) + jnp.where(m1, wts[:, 1:2], 0.0), axis=0, keepdims=True)

    buf = blk % 2
    loc = loc_ref.at[buf]

    def drain(which, count):
        _wait_chunks(count, MAX_CHUNKS,
                     lambda rows: _rows_copy(loc_ref.at[which], 0, xs_ref, 0, rows, sems.at[which]))

    @pl.when(blk >= 2)
    def _():
        drain(buf, tot_ref[blk - 2])

    loc[:, :d] = lax.dot_general((m0 | m1).astype(BF16), hn_ref[...], _TN, preferred_element_type=F32)
    loc[:, d:] = jnp.broadcast_to(wrow, (LANES, LOCAL_ROWS)).T

    _start_block_chunks(blk, tot_ref[blk], crow_ref,
                        lambda lo, go: _rows_copy(loc, lo, xs_ref, go, CHUNK_ROWS, sems.at[buf]))

    @pl.when(blk == pl.num_programs(0) - 1)
    def _():
        drain(1 - buf, tot_ref[blk - 1])
        drain(buf, tot_ref[blk])
        sem = sems.at[0]
        zero_ref[...] = jnp.zeros_like(zero_ref)
        pad_bits = (EXPERT_TILE // CHUNK_ROWS - 1).bit_length()

        def pad_copies(e, carry, *, wait):
            padn = padn_ref[e]
            for k in range(pad_bits):
                @pl.when(((padn >> k) & 1) == 1)
                def _():
                    done = (padn & ((1 << k) - 1)) * CHUNK_ROWS
                    cp = _rows_copy(zero_ref, 0, xs_ref, pl.multiple_of(pads_ref[e] + done, CHUNK_ROWS),
                                    CHUNK_ROWS << k, sem)
                    if wait:
                        cp.wait()
                    else:
                        cp.start()
            return carry

        lax.fori_loop(0, N_EXPERTS, functools.partial(pad_copies, wait=False), 0)
        lax.fori_loop(0, N_EXPERTS, functools.partial(pad_copies, wait=True), 0)

        def tile_copy(i):
            return pltpu.make_async_copy(zero_ref, xs_ref.at[pl.ds(pl.multiple_of(i * EXPERT_TILE, EXPERT_TILE),
                                                                   EXPERT_TILE), :], sem)

        n_tiles = xs_ref.shape[0] // EXPERT_TILE

        def start_tile(i, c):
            tile_copy(i).start()
            return c

        def wait_tile(i, c):
            tile_copy(i).wait()
            return c

        lax.fori_loop(nv_ref[0], n_tiles, start_tile, 0)
        lax.fori_loop(nv_ref[0], n_tiles, wait_tile, 0)


def _moe_dispatch(tables, eid, wts, hn, loff_f, n_rows):
    t, d = hn.shape
    width = d + LANES
    grid_spec = pltpu.PrefetchScalarGridSpec(
        num_scalar_prefetch=5,
        grid=(t // MOE_BLOCK,),
        in_specs=[pl.BlockSpec((MOE_BLOCK, TOP_K), lambda i, *_: (i, 0)),
                  pl.BlockSpec((MOE_BLOCK, TOP_K), lambda i, *_: (i, 0)),
                  pl.BlockSpec((MOE_BLOCK, d), lambda i, *_: (i, 0)),
                  pl.BlockSpec((1, 1, LANES), lambda i, *_: (i, 0, 0))],
        out_specs=[pl.BlockSpec(memory_space=pl.ANY),
                   pl.BlockSpec((MOE_BLOCK, TOP_K), lambda i, *_: (i, 0))],
        scratch_shapes=[pltpu.VMEM((2, LOCAL_ROWS, width), F32), pltpu.VMEM((EXPERT_TILE, width), F32),
                        pltpu.SemaphoreType.DMA((2,))],
    )
    return pl.pallas_call(
        _moe_dispatch_kernel,
        grid_spec=grid_spec,
        out_shape=[jax.ShapeDtypeStruct((n_rows, width), F32),
                   jax.ShapeDtypeStruct((t, TOP_K), jnp.int32)],
        compiler_params=_cparams("arbitrary"),
        name="moe_dispatch",
    )(*tables, eid, wts, hn, loff_f)


def _moe_experts_kernel(te_ref, tv_ref, tf_ref, ts_ref, xs_ref, w1_ref, w3_ref, w2_ref, o_ref,
                        w1b_ref, w3b_ref, w2b_ref):
    del te_ref, ts_ref
    i = pl.program_id(0)

    @pl.when(tf_ref[i] != 0)
    def _():
        w1b_ref[...] = w1_ref[0, 0].astype(BF16)
        w3b_ref[...] = w3_ref[0, 0].astype(BF16)
        w2b_ref[...] = w2_ref[0, 0].astype(BF16)

    @pl.when(tv_ref[i] != 0)
    def _():
        d = o_ref.shape[1]
        xb = xs_ref[:, :d].astype(BF16)
        w = xs_ref[:, d:d + 1]
        a = jnp.dot(xb, w1b_ref[...], preferred_element_type=F32)
        b = jnp.dot(xb, w3b_ref[...], preferred_element_type=F32)
        h = (a * jax.nn.sigmoid(a) * b).astype(BF16)
        o_ref[...] = jnp.dot(h, w2b_ref[...], preferred_element_type=F32) * w

    @pl.when(tv_ref[i] == 0)
    def _():
        o_ref[...] = jnp.zeros_like(o_ref)


def _moe_experts(tile_expert, tile_valid, tile_first, tile_src, xs, w1, w3, w2, layer):
    r, width = xs.shape
    d = width - LANES
    de = w1.shape[3]
    grid_spec = pltpu.PrefetchScalarGridSpec(
        num_scalar_prefetch=4,
        grid=(r // EXPERT_TILE,),
        in_specs=[pl.BlockSpec((EXPERT_TILE, width), lambda i, te, tv, tf, ts: (ts[i], 0)),
                  pl.BlockSpec((1, 1, d, de), lambda i, te, tv, tf, ts: (layer, te[i], 0, 0)),
                  pl.BlockSpec((1, 1, d, de), lambda i, te, tv, tf, ts: (layer, te[i], 0, 0)),
                  pl.BlockSpec((1, 1, de, d), lambda i, te, tv, tf, ts: (layer, te[i], 0, 0))],
        out_specs=pl.BlockSpec((EXPERT_TILE, d), lambda i, te, tv, tf, ts: (i, 0)),
        scratch_shapes=[pltpu.VMEM((d, de), BF16), pltpu.VMEM((d, de), BF16), pltpu.VMEM((de, d), BF16)],
    )
    return pl.pallas_call(
        _moe_experts_kernel,
        grid_spec=grid_spec,
        out_shape=jax.ShapeDtypeStruct((r, d), F32),
        compiler_params=_cparams("arbitrary"),
        name="moe_experts",
    )(tile_expert, tile_valid, tile_first, tile_src, xs, w1, w3, w2)


def _moe_combine_kernel(crow_ref, tot_ref, lpos_ref, x_ref, mod_ref, fg_ref, o_hbm_ref,
                        out_ref, loc_ref, sems, *, final_norm):
    blk = pl.program_id(0) * pl.num_programs(1) + pl.program_id(1)
    n_blk = pl.num_programs(0) * pl.num_programs(1)
    t = x_ref.shape[1]
    buf = blk % 2

    def fetch(b, which):
        loc = loc_ref.at[which]
        loc[TOP_K * t:, :] = jnp.zeros((LOCAL_ROWS - TOP_K * t, loc_ref.shape[2]), F32)
        _start_block_chunks(b, tot_ref[b], crow_ref,
                            lambda lo, go: _rows_copy(o_hbm_ref, go, loc, lo, CHUNK_ROWS, sems.at[which]))

    @pl.when(blk == 0)
    def _():
        fetch(blk, buf)

    @pl.when(blk + 1 < n_blk)
    def _():
        fetch(blk + 1, 1 - buf)

    _wait_chunks(tot_ref[blk], MAX_CHUNKS,
                 lambda rows: _rows_copy(o_hbm_ref, 0, loc_ref.at[buf], 0, rows, sems.at[buf]))
    lp = lpos_ref[...]
    m0, m1 = _slot_masks(lp[:, 0:1], lp[:, 1:2])
    y = jnp.dot((m0 | m1).astype(BF16), loc_ref[buf].astype(BF16), preferred_element_type=F32)
    x2 = x_ref[0] + (1.0 + mod_ref[0, 5:6]) * y
    if final_norm:
        x2 = (x2 * lax.rsqrt(jnp.mean(x2 * x2, axis=-1, keepdims=True) + RMS_EPS)) * fg_ref[...]
    out_ref[0] = x2


def _moe_combine(tables, lpos, x, mod, final_g, o_sorted, final_norm):
    b, l, d = x.shape
    nt = l // MOE_BLOCK
    grid_spec = pltpu.PrefetchScalarGridSpec(
        num_scalar_prefetch=2,
        grid=(b, nt),
        in_specs=[pl.BlockSpec((MOE_BLOCK, TOP_K), lambda i, j, *_: (i * nt + j, 0)),
                  pl.BlockSpec((1, MOE_BLOCK, d), lambda i, j, *_: (i, j, 0)),
                  pl.BlockSpec((1, 8, d), lambda i, j, *_: (i, 0, 0)),
                  pl.BlockSpec((1, d), lambda i, j, *_: (0, 0)),
                  pl.BlockSpec(memory_space=pl.ANY)],
        out_specs=pl.BlockSpec((1, MOE_BLOCK, d), lambda i, j, *_: (i, j, 0)),
        scratch_shapes=[pltpu.VMEM((2, LOCAL_ROWS, d), F32), pltpu.SemaphoreType.DMA((2,))],
    )
    return pl.pallas_call(
        functools.partial(_moe_combine_kernel, final_norm=final_norm),
        grid_spec=grid_spec,
        out_shape=jax.ShapeDtypeStruct((b, l, d), F32),
        compiler_params=_cparams("arbitrary", "arbitrary"),
        name="moe_combine",
    )(*tables[:2], lpos, x, mod, final_g, o_sorted)


def _moe(x, hn, eid, wts, cnt, mod, w1, w3, w2, layer, final_g, final_norm):
    t, d = hn.shape
    nblk = t // MOE_BLOCK
    i32 = jnp.int32
    n = cnt[:, 0, :N_EXPERTS].astype(i32)
    run = (n + CHUNK_ROWS - 1) // CHUNK_ROWS * CHUNK_ROWS
    loff = jnp.cumsum(run, axis=1) - run
    rows_e = jnp.sum(run, axis=0)
    tiles_e = (rows_e + EXPERT_TILE - 1) // EXPERT_TILE
    tile_end = jnp.cumsum(tiles_e)
    base = (tile_end - tiles_e) * EXPERT_TILE
    goff = base[None, :] + jnp.cumsum(run, axis=0) - run
    nch = run // CHUNK_ROWS
    tot = jnp.sum(nch, axis=1)
    ch_end = jnp.cumsum(nch, axis=1)
    q = jnp.arange(MAX_CHUNKS, dtype=i32)
    e_of_q = jnp.minimum(jnp.sum((q[None, :, None] >= ch_end[:, None, :]).astype(i32), axis=-1), N_EXPERTS - 1)
    crow = (jnp.take_along_axis(goff, e_of_q, axis=1)
            + (q[None, :] - jnp.take_along_axis(ch_end - nch, e_of_q, axis=1)) * CHUNK_ROWS)
    padn = (tiles_e * EXPERT_TILE - rows_e) // CHUNK_ROWS
    pads = base + rows_e
    tables = (crow.reshape(-1), tot, padn, pads, tile_end[-1:])
    loff_f = jnp.zeros((nblk, 1, LANES), F32).at[:, 0, :N_EXPERTS].set(loff.astype(F32))

    max_rows = t * TOP_K + nblk * N_EXPERTS * (CHUNK_ROWS - 1) + N_EXPERTS * (EXPERT_TILE - 1)
    max_tiles = -(-max_rows // EXPERT_TILE)
    tile_ids = jnp.arange(max_tiles, dtype=i32)
    n_valid = tile_end[-1]
    tile_src = jnp.minimum(tile_ids, n_valid - 1)
    tile_expert = jnp.sum((tile_src[:, None] >= tile_end[None, :]).astype(i32), axis=1)
    tile_valid = (tile_ids < n_valid).astype(i32)
    tile_first = jnp.concatenate([jnp.ones((1,), i32), (tile_expert[1:] != tile_expert[:-1]).astype(i32)])

    xs, lpos = _moe_dispatch(tables, eid, wts, hn, loff_f, max_tiles * EXPERT_TILE)
    o_sorted = _moe_experts(tile_expert, tile_valid, tile_first, tile_src, xs, w1, w3, w2, layer)
    return _moe_combine(tables, lpos, x, mod, final_g, o_sorted, final_norm)


def _router_pack(wg, bg, we, be):
    d = wg.shape[0]
    wr = jnp.zeros((d, LANES), F32).at[:, :N_EXPERTS].set(we).at[:, N_EXPERTS:N_EXPERTS + N_GROUPS].set(wg)
    br = jnp.zeros((1, LANES), F32).at[0, :N_EXPERTS].set(be).at[0, N_EXPERTS:N_EXPERTS + N_GROUPS].set(bg)
    wr_hi = wr.astype(BF16)
    wr_lo = (wr - wr_hi.astype(F32)).astype(BF16)
    return wr_hi, wr_lo, br


def kernel(x, c, ada_w, ada_b, norm1_g, norm2_g, ssm_w_in, ssm_lam_re, ssm_lam_im, ssm_log_dt, ssm_b_re, ssm_b_im, ssm_c_re, ssm_c_im, ssm_d, ssm_w_glu, conv_w_in, conv_w, conv_w_out, moe_wg, moe_bg, moe_we, moe_be, moe_w1, moe_w3, moe_w2, final_g):
    b, l, d = x.shape
    depth = ada_w.shape[0]
    c8 = jnp.zeros((8, d), F32).at[:b].set(c)
    mod_all = _adaln(c8, ada_w, ada_b)[:, :b].reshape(depth, b, 6, d)
    mod_all = jnp.concatenate([mod_all, jnp.zeros((depth, b, 2, d), F32)], axis=2)
    fg = final_g.reshape(1, d)

    mod = mod_all[0]
    u4 = _s5_in(x, mod, norm1_g[0:1], ssm_w_in[0].T.astype(BF16))
    kern, f_mat, e_mat, a_pack, d_row = _s5_tables(
        ssm_lam_re[0], ssm_lam_im[0], ssm_log_dt[0], ssm_b_re[0], ssm_b_im[0],
        ssm_c_re[0], ssm_c_im[0], ssm_d[0])
    y4 = _s5_scan(u4, kern, f_mat, e_mat, a_pack, d_row)
    wrh, wrl, br = _router_pack(moe_wg[0], moe_bg[0], moe_we[0], moe_be[0])
    x1, hn, eid, wts, cnt = _s5_out(y4, x, mod, ssm_w_glu[0].astype(BF16), norm2_g[0:1], wrh, wrl, br)
    x2 = _moe(x1, hn, eid, wts, cnt, mod, moe_w1, moe_w3, moe_w2, 0, fg, False)

    mod = mod_all[1]
    cw8 = jnp.zeros((8, d), F32).at[:conv_w.shape[1]].set(conv_w[0])
    wrh, wrl, br = _router_pack(moe_wg[1], moe_bg[1], moe_we[1], moe_be[1])
    x3, hn, eid, wts, cnt = _conv_mixer(x2, mod, norm1_g[1:2], conv_w_in[0].astype(BF16), cw8,
                                        conv_w_out[0].astype(BF16), norm2_g[1:2], wrh, wrl, br)
    return _moe(x3, hn, eid, wts, cnt, mod, moe_w1, moe_w3, moe_w2, 1, fg, True)
```

```python
import functools
import math

import jax
import jax.numpy as jnp
from jax import lax
from jax.experimental import pallas as pl
from jax.experimental.pallas import tpu as pltpu

F32 = jnp.float32
BF16 = jnp.bfloat16
HIGHEST = lax.Precision.HIGHEST

RMS_EPS = 1e-6
SSM_GROUP = 16
SSM_CHUNK = 128
N_GROUPS = 4
EXPERTS_PER_GROUP = 8
N_EXPERTS = N_GROUPS * EXPERTS_PER_GROUP
TOP_K = 2
LANES = 128
TOKEN_TILE = 512
S5_TILE = 1024
MOE_BLOCK = TOKEN_TILE
CHUNK_ROWS = 8
LOCAL_ROWS = -(-(TOP_K * MOE_BLOCK + N_EXPERTS * (CHUNK_ROWS - 1)) // LANES) * LANES
EXPERT_TILE = 512
VMEM_LIMIT = 56 * 1024 * 1024
NEG_INF = -1e30


def _cparams(*sem):
    return pltpu.CompilerParams(dimension_semantics=sem, vmem_limit_bytes=VMEM_LIMIT)


def _adaln_kernel(c_ref, w_ref, b_ref, o_ref):
    c = c_ref[...]
    cond = c * jax.nn.sigmoid(c)
    o_ref[0] = jnp.dot(cond, w_ref[0], precision=HIGHEST, preferred_element_type=F32) + b_ref[0]


def _adaln(c8, ada_w, ada_b):
    depth, d, n = ada_w.shape
    tn = 1536
    return pl.pallas_call(
        _adaln_kernel,
        grid=(depth, n // tn),
        in_specs=[pl.BlockSpec((8, d), lambda i, j: (0, 0)),
                  pl.BlockSpec((1, d, tn), lambda i, j: (i, 0, j)),
                  pl.BlockSpec((1, 1, tn), lambda i, j: (i, 0, j))],
        out_specs=pl.BlockSpec((1, 8, tn), lambda i, j: (i, 0, j)),
        out_shape=jax.ShapeDtypeStruct((depth, 8, n), F32),
        compiler_params=_cparams("parallel", "parallel"),
        name="adaln",
    )(c8, ada_w, ada_b.reshape(depth, 1, n))


def _norm_mod(x, g, shift, scale):
    y = x * lax.rsqrt(jnp.mean(x * x, axis=-1, keepdims=True) + RMS_EPS)
    return (y * g) * (1.0 + scale) + shift


def _route(hn, hn_hi, wrh_ref, wrl_ref, br_ref, eid_ref, wts_ref, cnt_ref):
    hn_lo = (hn - hn_hi.astype(F32)).astype(BF16)
    wrh = wrh_ref[...]
    logits = (jnp.dot(hn_hi, wrh, preferred_element_type=F32)
              + jnp.dot(hn_lo, wrh, preferred_element_type=F32)
              + jnp.dot(hn_hi, wrl_ref[...], preferred_element_type=F32)) + br_ref[...]
    lane = lax.broadcasted_iota(jnp.int32, logits.shape, 1)
    is_grp = (lane >= N_EXPERTS) & (lane < N_EXPERTS + N_GROUPS)
    lg = jnp.where(is_grp, logits, NEG_INF)
    gmax = jnp.max(lg, axis=-1, keepdims=True)
    gsum = jnp.sum(jnp.where(is_grp, jnp.exp(lg - gmax), 0.0), axis=-1, keepdims=True)
    gp = 1.0 / gsum
    gi = jnp.min(jnp.where(lg == gmax, lane, 2 * LANES), axis=-1, keepdims=True) - N_EXPERTS
    in_grp = (lane < N_EXPERTS) & ((lane // EXPERTS_PER_GROUP) == gi)
    le = jnp.where(in_grp, logits, NEG_INF)
    v1 = jnp.max(le, axis=-1, keepdims=True)
    i1 = jnp.min(jnp.where(le == v1, lane, 2 * LANES), axis=-1, keepdims=True)
    le2 = jnp.where(lane == i1, NEG_INF, le)
    v2 = jnp.max(le2, axis=-1, keepdims=True)
    i2 = jnp.min(jnp.where(le2 == v2, lane, 2 * LANES), axis=-1, keepdims=True)
    e2 = jnp.exp(v2 - v1)
    den = 1.0 + e2
    col = lax.broadcasted_iota(jnp.int32, (hn.shape[0], TOP_K), 1)
    eid_ref[...] = jnp.where(col == 0, i1, i2)
    wts_ref[...] = jnp.where(col == 0, gp / den, gp * e2 / den)
    chosen = ((lane == i1) | (lane == i2)).astype(F32)
    cnt_ref[...] = jnp.broadcast_to(jnp.sum(chosen, axis=0, keepdims=True), cnt_ref.shape)


def _residual_and_route(x, out, mod, g2_ref, wrh_ref, wrl_ref, br_ref, x_out_ref, hn_ref, eid_ref, wts_ref, cnt_ref):
    x1 = x + (1.0 + mod[2:3]) * out
    x_out_ref[...] = x1
    hn = _norm_mod(x1, g2_ref[...], mod[3:4], mod[4:5])
    hn_hi = hn.astype(BF16)
    hn_ref[...] = hn_hi
    _route(hn, hn_hi, wrh_ref, wrl_ref, br_ref, eid_ref, wts_ref, cnt_ref)


def _s5_in_kernel(x_ref, mod_ref, g_ref, wt_ref, u_ref):
    mod = mod_ref[0]
    hn = _norm_mod(x_ref[0], g_ref[...], mod[0:1], mod[1:2])
    ut = lax.dot_general(wt_ref[...], hn.astype(BF16), (((1,), (1,)), ((), ())), preferred_element_type=F32)
    u_ref[0] = ut.reshape(ut.shape[0], S5_TILE // SSM_CHUNK, SSM_CHUNK)


def _s5_in(x, mod, g, wt_bf):
    b, l, d = x.shape
    h = wt_bf.shape[0]
    cpt = S5_TILE // SSM_CHUNK
    return pl.pallas_call(
        _s5_in_kernel,
        grid=(b, l // S5_TILE),
        in_specs=[pl.BlockSpec((1, S5_TILE, d), lambda i, j: (i, j, 0)),
                  pl.BlockSpec((1, 8, d), lambda i, j: (i, 0, 0)),
                  pl.BlockSpec((1, d), lambda i, j: (0, 0)),
                  pl.BlockSpec((h, d), lambda i, j: (0, 0))],
        out_specs=pl.BlockSpec((1, h, cpt, SSM_CHUNK), lambda i, j: (i, 0, j, 0)),
        out_shape=jax.ShapeDtypeStruct((b, h, l // SSM_CHUNK, SSM_CHUNK), F32),
        compiler_params=_cparams("parallel", "parallel"),
        name="s5_in",
    )(x, mod, g, wt_bf)


def _s5_scan_kernel(u_ref, k_ref, f_ref, e_ref, a_ref, d_ref, y_ref, m_ref, s_ref, sw_ref, sp_ref):
    bsz, grp, n_chunks, tc = u_ref.shape
    srow = lax.broadcasted_iota(jnp.int32, (tc, tc), 0)
    tcol = lax.broadcasted_iota(jnp.int32, (tc, tc), 1)
    causal = tcol >= srow

    def build(i, carry):
        for j in range(grp):
            row = k_ref[0, pl.ds(i * grp + j, 1), :]
            blk = pltpu.roll(jnp.broadcast_to(row, (tc, tc)), 0, 1, stride=1, stride_axis=0)
            m_ref[pl.ds(pl.multiple_of(i * tc, tc), tc), j * tc:(j + 1) * tc] = (
                jnp.where(causal, blk, 0.0).astype(BF16))
        return carry

    lax.fori_loop(0, grp, build, 0)

    x = jnp.concatenate(
        [jnp.concatenate([u_ref[b, k] for k in range(grp)], axis=1) for b in range(bsz)], axis=0)
    xb = x.astype(BF16)
    s_loc = jnp.dot(xb, f_ref[0], preferred_element_type=F32)
    s_ref[...] = s_loc
    half = s_loc.shape[1] // 2
    sw_ref[...] = jnp.concatenate([s_loc[:, half:], s_loc[:, :half]], axis=1)
    a1 = a_ref[0, 0:1, :]
    a2 = a_ref[0, 1:2, :]
    s = [jnp.zeros((1, s_loc.shape[1]), F32) for _ in range(bsz)]
    sw = [jnp.zeros((1, s_loc.shape[1]), F32) for _ in range(bsz)]
    for c in range(n_chunks):
        for b in range(bsz):
            r = b * n_chunks + c
            sp_ref[r:r + 1, :] = s[b]
            s_new = a1 * s[b] + a2 * sw[b] + s_ref[r:r + 1, :]
            sw[b] = a1 * sw[b] - a2 * s[b] + sw_ref[r:r + 1, :]
            s[b] = s_new
    y = jnp.dot(xb, m_ref[...], preferred_element_type=F32)
    y = y + jnp.dot(sp_ref[...].astype(BF16), e_ref[0], preferred_element_type=F32)
    y = y + d_ref[0] * x
    for b in range(bsz):
        for j in range(grp):
            y_ref[b, j] = y[b * n_chunks:(b + 1) * n_chunks, j * tc:(j + 1) * tc]


def _s5_scan(u4, kern, f_bf, e_bf, a_pack, d_row):
    b, h, nc, tc = u4.shape
    g = kern.shape[0]
    grp = h // g
    k = grp * tc
    p2 = f_bf.shape[2]
    rows = b * nc
    return pl.pallas_call(
        _s5_scan_kernel,
        grid=(g,),
        in_specs=[pl.BlockSpec((b, grp, nc, tc), lambda i: (0, i, 0, 0)),
                  pl.BlockSpec((1, grp * grp, tc), lambda i: (i, 0, 0)),
                  pl.BlockSpec((1, k, p2), lambda i: (i, 0, 0)),
                  pl.BlockSpec((1, p2, k), lambda i: (i, 0, 0)),
                  pl.BlockSpec((1, 8, p2), lambda i: (i, 0, 0)),
                  pl.BlockSpec((1, 1, k), lambda i: (i, 0, 0))],
        out_specs=pl.BlockSpec((b, grp, nc, tc), lambda i: (0, i, 0, 0)),
        out_shape=jax.ShapeDtypeStruct(u4.shape, F32),
        scratch_shapes=[pltpu.VMEM((k, k), BF16), pltpu.VMEM((rows, p2), F32),
                        pltpu.VMEM((rows, p2), F32), pltpu.VMEM((rows, p2), F32)],
        compiler_params=_cparams("parallel"),
        name="s5_scan",
    )(u4, kern, f_bf, e_bf, a_pack, d_row)


def _s5_tables(lam_re, lam_im, log_dt, b_re, b_im, c_re, c_im, d_skip):
    g, p = lam_re.shape
    k = SSM_GROUP
    tc = SSM_CHUNK
    dt = jnp.exp(log_dt)[:, None]
    mag = jnp.exp(lam_re * dt)
    ab_re = mag * jnp.cos(lam_im * dt)
    ab_im = mag * jnp.sin(lam_im * dt)
    den = lam_re * lam_re + lam_im * lam_im
    cf_re = ((ab_re - 1) * lam_re + ab_im * lam_im) / den
    cf_im = (ab_im * lam_re - (ab_re - 1) * lam_im) / den
    bb_re = cf_re[..., None] * b_re - cf_im[..., None] * b_im
    bb_im = cf_re[..., None] * b_im + cf_im[..., None] * b_re
    lags = jnp.arange(tc + 1, dtype=F32)[:, None, None]
    pmag = jnp.exp(lags * (lam_re * dt)[None])
    pang = lags * (lam_im * dt)[None]
    p_re = pmag * jnp.cos(pang)
    p_im = pmag * jnp.sin(pang)
    cb_re = jnp.einsum('gjp,gpi->gpij', c_re, bb_re) - jnp.einsum('gjp,gpi->gpij', c_im, bb_im)
    cb_im = jnp.einsum('gjp,gpi->gpij', c_re, bb_im) + jnp.einsum('gjp,gpi->gpij', c_im, bb_re)
    kern = (jnp.einsum('lgp,gpij->gijl', p_re[:tc], cb_re, precision=HIGHEST)
            - jnp.einsum('lgp,gpij->gijl', p_im[:tc], cb_im, precision=HIGHEST))
    kern = kern.reshape(g, k * k, tc)
    pr = p_re[:tc][::-1]
    pi = p_im[:tc][::-1]
    f_re = jnp.einsum('sgp,gpi->gisp', pr, bb_re) - jnp.einsum('sgp,gpi->gisp', pi, bb_im)
    f_im = jnp.einsum('sgp,gpi->gisp', pr, bb_im) + jnp.einsum('sgp,gpi->gisp', pi, bb_re)
    f_mat = jnp.concatenate([f_re, f_im], axis=-1).reshape(g, k * tc, 2 * p).astype(BF16)
    qr = p_re[1:]
    qi = p_im[1:]
    e_re = jnp.einsum('gjp,tgp->gpjt', c_re, qr) - jnp.einsum('gjp,tgp->gpjt', c_im, qi)
    e_im = -(jnp.einsum('gjp,tgp->gpjt', c_re, qi) + jnp.einsum('gjp,tgp->gpjt', c_im, qr))
    e_mat = jnp.concatenate([e_re, e_im], axis=1).reshape(g, 2 * p, k * tc).astype(BF16)
    ar = p_re[tc]
    ai = p_im[tc]
    a1 = jnp.concatenate([ar, ar], axis=-1)
    a2 = jnp.concatenate([-ai, ai], axis=-1)
    a_pack = jnp.concatenate([a1[:, None], a2[:, None], jnp.zeros((g, 6, 2 * p), F32)], axis=1)
    d_row = jnp.repeat(d_skip.reshape(g, k), tc, axis=1).reshape(g, 1, k * tc)
    return kern, f_mat, e_mat, a_pack, d_row


def _gelu_tanh(x):
    return 0.5 * x * (1.0 + jnp.tanh(math.sqrt(2.0 / math.pi) * (x + 0.044715 * (x * x * x))))


def _s5_out_kernel(y_ref, x_ref, mod_ref, w_ref, g2_ref, wrh_ref, wrl_ref, br_ref,
                   x_out_ref, hn_ref, eid_ref, wts_ref, cnt_ref):
    mod = mod_ref[0]
    h = y_ref.shape[1]
    yt = _gelu_tanh(y_ref[0].reshape(h, S5_TILE))
    d = w_ref.shape[1] // 2
    for part in range(S5_TILE // MOE_BLOCK):
        rows = slice(part * MOE_BLOCK, (part + 1) * MOE_BLOCK)
        o = lax.dot_general(yt[:, rows].astype(BF16), w_ref[...], (((0,), (0,)), ((), ())),
                            preferred_element_type=F32)
        out = o[:, :d] * jax.nn.sigmoid(o[:, d:])
        _residual_and_route(x_ref[0, rows, :], out, mod, g2_ref, wrh_ref, wrl_ref, br_ref,
                            x_out_ref.at[0, rows, :], hn_ref.at[rows, :], eid_ref.at[rows, :],
                            wts_ref.at[rows, :], cnt_ref.at[part])


def _mixer_out_specs(b, l, d, tile):
    nt = l // tile
    nblk = tile // MOE_BLOCK
    specs = [pl.BlockSpec((1, tile, d), lambda i, j: (i, j, 0)),
             pl.BlockSpec((tile, d), lambda i, j: (i * nt + j, 0)),
             pl.BlockSpec((tile, TOP_K), lambda i, j: (i * nt + j, 0)),
             pl.BlockSpec((tile, TOP_K), lambda i, j: (i * nt + j, 0)),
             pl.BlockSpec((nblk, 8, LANES), lambda i, j: (i * nt + j, 0, 0))]
    shapes = [jax.ShapeDtypeStruct((b, l, d), F32),
              jax.ShapeDtypeStruct((b * l, d), BF16),
              jax.ShapeDtypeStruct((b * l, TOP_K), jnp.int32),
              jax.ShapeDtypeStruct((b * l, TOP_K), F32),
              jax.ShapeDtypeStruct((b * l // MOE_BLOCK, 8, LANES), F32)]
    return specs, shapes


def _s5_out(y4, x, mod, w_bf, g2, wrh, wrl, br):
    b, l, d = x.shape
    h = y4.shape[1]
    specs, shapes = _mixer_out_specs(b, l, d, S5_TILE)
    return pl.pallas_call(
        _s5_out_kernel,
        grid=(b, l // S5_TILE),
        in_specs=[pl.BlockSpec((1, h, S5_TILE // SSM_CHUNK, SSM_CHUNK), lambda i, j: (i, 0, j, 0)),
                  pl.BlockSpec((1, S5_TILE, d), lambda i, j: (i, j, 0)),
                  pl.BlockSpec((1, 8, d), lambda i, j: (i, 0, 0)),
                  pl.BlockSpec(w_bf.shape, lambda i, j: (0, 0), pipeline_mode=pl.Buffered(1)),
                  pl.BlockSpec((1, d), lambda i, j: (0, 0)),
                  pl.BlockSpec((d, LANES), lambda i, j: (0, 0)),
                  pl.BlockSpec((d, LANES), lambda i, j: (0, 0)),
                  pl.BlockSpec((1, LANES), lambda i, j: (0, 0))],
        out_specs=specs,
        out_shape=shapes,
        compiler_params=_cparams("parallel", "parallel"),
        name="s5_out",
    )(y4, x, mod, w_bf, g2, wrh, wrl, br)


def _conv_mixer_kernel(x_ref, mod_ref, g1_ref, win_ref, cw_ref, wout_ref, g2_ref, wrh_ref, wrl_ref, br_ref,
                       x_out_ref, hn_ref, eid_ref, wts_ref, cnt_ref, carry_ref):
    @pl.when(pl.program_id(1) == 0)
    def _():
        carry_ref[...] = jnp.zeros_like(carry_ref)

    mod = mod_ref[0]
    x = x_ref[0]
    hn = _norm_mod(x, g1_ref[...], mod[0:1], mod[1:2])
    p = jnp.dot(hn.astype(BF16), win_ref[...], preferred_element_type=F32)
    d = p.shape[1] // 3
    b_g = p[:, :d]
    u = p[:, d:2 * d] * p[:, 2 * d:]
    ext = jnp.concatenate([carry_ref[...], u], axis=0)
    t = u.shape[0]
    cw = cw_ref[...]
    z = cw[0:1] * ext[6:6 + t] + cw[1:2] * ext[7:7 + t] + cw[2:3] * u
    carry_ref[...] = u[t - 8:]
    out = jnp.dot((b_g * z).astype(BF16), wout_ref[...], preferred_element_type=F32)
    _residual_and_route(x, out, mod, g2_ref, wrh_ref, wrl_ref, br_ref, x_out_ref.at[0], hn_ref, eid_ref,
                        wts_ref, cnt_ref.at[0])


def _conv_mixer(x, mod, g1, win_bf, cw8, wout_bf, g2, wrh, wrl, br):
    b, l, d = x.shape
    specs, shapes = _mixer_out_specs(b, l, d, TOKEN_TILE)
    return pl.pallas_call(
        _conv_mixer_kernel,
        grid=(b, l // TOKEN_TILE),
        in_specs=[pl.BlockSpec((1, TOKEN_TILE, d), lambda i, j: (i, j, 0)),
                  pl.BlockSpec((1, 8, d), lambda i, j: (i, 0, 0)),
                  pl.BlockSpec((1, d), lambda i, j: (0, 0)),
                  pl.BlockSpec(win_bf.shape, lambda i, j: (0, 0)),
                  pl.BlockSpec((8, d), lambda i, j: (0, 0)),
                  pl.BlockSpec(wout_bf.shape, lambda i, j: (0, 0)),
                  pl.BlockSpec((1, d), lambda i, j: (0, 0)),
                  pl.BlockSpec((d, LANES), lambda i, j: (0, 0)),
                  pl.BlockSpec((d, LANES), lambda i, j: (0, 0)),
                  pl.BlockSpec((1, LANES), lambda i, j: (0, 0))],
        out_specs=specs,
        out_shape=shapes,
        scratch_shapes=[pltpu.VMEM((8, d), F32)],
        compiler_params=_cparams("parallel", "arbitrary"),
        name="conv_mixer",
    )(x, mod, g1, win_bf, cw8, wout_bf, g2, wrh, wrl, br)


_TN = (((0,), (0,)), ((), ()))


MAX_CHUNKS = LOCAL_ROWS // CHUNK_ROWS


def _rows_copy(src, s_row, dst, d_row, rows, sem):
    return pltpu.make_async_copy(src.at[pl.ds(s_row, rows), :], dst.at[pl.ds(d_row, rows), :], sem)


def _start_block_chunks(blk, count, crow_ref, make_copy):
    def body(q, carry):
        make_copy(pl.multiple_of(q * CHUNK_ROWS, CHUNK_ROWS),
                  pl.multiple_of(crow_ref[blk * MAX_CHUNKS + q], CHUNK_ROWS)).start()
        return carry

    lax.fori_loop(0, count, body, 0)


def _wait_chunks(count, max_count, make_wait):
    for k in range(max_count.bit_length()):
        @pl.when(((count >> k) & 1) == 1)
        def _():
            make_wait(CHUNK_ROWS << k).wait()


def _slot_masks(lp0, lp1):
    slot = lax.broadcasted_iota(jnp.int32, (lp0.shape[0], LOCAL_ROWS), 1)
    return slot == lp0, slot == lp1


def _moe_dispatch_kernel(crow_ref, tot_ref, padn_ref, pads_ref, nv_ref,
                         eid_ref, wts_ref, hn_ref, lofff_ref, xs_ref, lpos_ref, loc_ref, zero_ref, sems):
    blk = pl.program_id(0)
    d = hn_ref.shape[1]
    eid = eid_ref[...]
    t = eid.shape[0]
    lane = lax.broadcasted_iota(jnp.int32, (t, LANES), 1)
    hit0 = lane == eid[:, 0:1]
    hit1 = lane == eid[:, 1:2]
    onehot = (hit0 | hit1).astype(BF16)
    r = lax.broadcasted_iota(jnp.int32, (t, t), 0)
    c = lax.broadcasted_iota(jnp.int32, (t, t), 1)
    tri = (c < r).astype(BF16)
    before = jnp.dot(tri, onehot, preferred_element_type=F32) + lofff_ref[0]
    lp0 = jnp.sum(jnp.where(hit0, before, 0.0), axis=-1, keepdims=True).astype(jnp.int32)
    lp1 = jnp.sum(jnp.where(hit1, before, 0.0), axis=-1, keepdims=True).astype(jnp.int32)
    col = lax.broadcasted_iota(jnp.int32, (t, TOP_K), 1)
    lpos_ref[...] = jnp.where(col == 0, lp0, lp1)
    m0, m1 = _slot_masks(lp0, lp1)
    wts = wts_ref[...]
    wrow = jnp.sum(jnp.where(m0, wts[:, 0:1], 0.0) + jnp.where(m1, wts[:, 1:2], 0.0), axis=0, keepdims=True)

    buf = blk % 2
    loc = loc_ref.at[buf]

    def drain(which, count):
        _wait_chunks(count, MAX_CHUNKS,
                     lambda rows: _rows_copy(loc_ref.at[which], 0, xs_ref, 0, rows, sems.at[which]))

    @pl.when(blk >= 2)
    def _():
        drain(buf, tot_ref[blk - 2])

    loc[:, :d] = lax.dot_general((m0 | m1).astype(BF16), hn_ref[...], _TN, preferred_element_type=F32)
    loc[:, d:] = jnp.broadcast_to(wrow, (LANES, LOCAL_ROWS)).T

    _start_block_chunks(blk, tot_ref[blk], crow_ref,
                        lambda lo, go: _rows_copy(loc, lo, xs_ref, go, CHUNK_ROWS, sems.at[buf]))

    @pl.when(blk == pl.num_programs(0) - 1)
    def _():
        drain(1 - buf, tot_ref[blk - 1])
        drain(buf, tot_ref[blk])
        sem = sems.at[0]
        zero_ref[...] = jnp.zeros_like(zero_ref)
        pad_bits = (EXPERT_TILE // CHUNK_ROWS - 1).bit_length()

        def pad_copies(e, carry, *, wait):
            padn = padn_ref[e]
            for k in range(pad_bits):
                @pl.when(((padn >> k) & 1) == 1)
                def _():
                    done = (padn & ((1 << k) - 1)) * CHUNK_ROWS
                    cp = _rows_copy(zero_ref, 0, xs_ref, pl.multiple_of(pads_ref[e] + done, CHUNK_ROWS),
                                    CHUNK_ROWS << k, sem)
                    if wait:
                        cp.wait()
                    else:
                        cp.start()
            return carry

        lax.fori_loop(0, N_EXPERTS, functools.partial(pad_copies, wait=False), 0)
        lax.fori_loop(0, N_EXPERTS, functools.partial(pad_copies, wait=True), 0)

        def tile_copy(i):
            return pltpu.make_async_copy(zero_ref, xs_ref.at[pl.ds(pl.multiple_of(i * EXPERT_TILE, EXPERT_TILE),
                                                                   EXPERT_TILE), :], sem)

        n_tiles = xs_ref.shape[0] // EXPERT_TILE

        def start_tile(i, c):
            tile_copy(i).start()
            return c

        def wait_tile(i, c):
            tile_copy(i).wait()
            return c

        lax.fori_loop(nv_ref[0], n_tiles, start_tile, 0)
        lax.fori_loop(nv_ref[0], n_tiles, wait_tile, 0)


def _moe_dispatch(tables, eid, wts, hn, loff_f, n_rows):
    t, d = hn.shape
    width = d + LANES
    grid_spec = pltpu.PrefetchScalarGridSpec(
        num_scalar_prefetch=5,
        grid=(t // MOE_BLOCK,),
        in_specs=[pl.BlockSpec((MOE_BLOCK, TOP_K), lambda i, *_: (i, 0)),
                  pl.BlockSpec((MOE_BLOCK, TOP_K), lambda i, *_: (i, 0)),
                  pl.BlockSpec((MOE_BLOCK, d), lambda i, *_: (i, 0)),
                  pl.BlockSpec((1, 1, LANES), lambda i, *_: (i, 0, 0))],
        out_specs=[pl.BlockSpec(memory_space=pl.ANY),
                   pl.BlockSpec((MOE_BLOCK, TOP_K), lambda i, *_: (i, 0))],
        scratch_shapes=[pltpu.VMEM((2, LOCAL_ROWS, width), F32), pltpu.VMEM((EXPERT_TILE, width), F32),
                        pltpu.SemaphoreType.DMA((2,))],
    )
    return pl.pallas_call(
        _moe_dispatch_kernel,
        grid_spec=grid_spec,
        out_shape=[jax.ShapeDtypeStruct((n_rows, width), F32),
                   jax.ShapeDtypeStruct((t, TOP_K), jnp.int32)],
        compiler_params=_cparams("arbitrary"),
        name="moe_dispatch",
    )(*tables, eid, wts, hn, loff_f)


def _moe_experts_kernel(te_ref, tv_ref, tf_ref, ts_ref, xs_ref, w1_ref, w3_ref, w2_ref, o_ref,
                        w1b_ref, w3b_ref, w2b_ref):
    del te_ref, ts_ref
    i = pl.program_id(0)

    @pl.when(tf_ref[i] != 0)
    def _():
        def cast_rows(r, carry):
            rows = pl.ds(pl.multiple_of(r * LANES, LANES), LANES)
            w1b_ref[rows, :] = w1_ref[0, 0, rows, :].astype(BF16)
            w3b_ref[rows, :] = w3_ref[0, 0, rows, :].astype(BF16)

            @pl.when(r < w2b_ref.shape[0] // LANES)
            def _():
                w2b_ref[rows, :] = w2_ref[0, 0, rows, :].astype(BF16)

            return carry

        lax.fori_loop(0, w1b_ref.shape[0] // LANES, cast_rows, 0)

    @pl.when(tv_ref[i] != 0)
    def _():
        d = o_ref.shape[1]
        xb = xs_ref[:, :d].astype(BF16)
        w = xs_ref[:, d:d + 1]
        a = jnp.dot(xb, w1b_ref[...], preferred_element_type=F32)
        b = jnp.dot(xb, w3b_ref[...], preferred_element_type=F32)
        h = (a * jax.nn.sigmoid(a) * b).astype(BF16)
        o_ref[...] = jnp.dot(h, w2b_ref[...], preferred_element_type=F32) * w

    @pl.when(tv_ref[i] == 0)
    def _():
        o_ref[...] = jnp.zeros_like(o_ref)


def _moe_experts(tile_expert, tile_valid, tile_first, tile_src, xs, w1, w3, w2, layer):
    r, width = xs.shape
    d = width - LANES
    de = w1.shape[3]
    grid_spec = pltpu.PrefetchScalarGridSpec(
        num_scalar_prefetch=4,
        grid=(r // EXPERT_TILE,),
        in_specs=[pl.BlockSpec((EXPERT_TILE, width), lambda i, te, tv, tf, ts: (ts[i], 0)),
                  pl.BlockSpec((1, 1, d, de), lambda i, te, tv, tf, ts: (layer, te[i], 0, 0)),
                  pl.BlockSpec((1, 1, d, de), lambda i, te, tv, tf, ts: (layer, te[i], 0, 0)),
                  pl.BlockSpec((1, 1, de, d), lambda i, te, tv, tf, ts: (layer, te[i], 0, 0))],
        out_specs=pl.BlockSpec((EXPERT_TILE, d), lambda i, te, tv, tf, ts: (i, 0)),
        scratch_shapes=[pltpu.VMEM((d, de), BF16), pltpu.VMEM((d, de), BF16), pltpu.VMEM((de, d), BF16)],
    )
    return pl.pallas_call(
        _moe_experts_kernel,
        grid_spec=grid_spec,
        out_shape=jax.ShapeDtypeStruct((r, d), F32),
        compiler_params=_cparams("arbitrary"),
        name="moe_experts",
    )(tile_expert, tile_valid, tile_first, tile_src, xs, w1, w3, w2)


def _moe_combine_kernel(crow_ref, tot_ref, lpos_ref, x_ref, mod_ref, fg_ref, o_hbm_ref,
                        out_ref, loc_ref, sems, *, final_norm):
    blk = pl.program_id(0) * pl.num_programs(1) + pl.program_id(1)
    n_blk = pl.num_programs(0) * pl.num_programs(1)
    t = x_ref.shape[1]
    buf = blk % 2

    def fetch(b, which):
        loc = loc_ref.at[which]
        loc[TOP_K * t:, :] = jnp.zeros((LOCAL_ROWS - TOP_K * t, loc_ref.shape[2]), F32)
        _start_block_chunks(b, tot_ref[b], crow_ref,
                            lambda lo, go: _rows_copy(o_hbm_ref, go, loc, lo, CHUNK_ROWS, sems.at[which]))

    @pl.when(blk == 0)
    def _():
        fetch(blk, buf)

    @pl.when(blk + 1 < n_blk)
    def _():
        fetch(blk + 1, 1 - buf)

    _wait_chunks(tot_ref[blk], MAX_CHUNKS,
                 lambda rows: _rows_copy(o_hbm_ref, 0, loc_ref.at[buf], 0, rows, sems.at[buf]))
    lp = lpos_ref[...]
    m0, m1 = _slot_masks(lp[:, 0:1], lp[:, 1:2])
    y = jnp.dot((m0 | m1).astype(BF16), loc_ref[buf].astype(BF16), preferred_element_type=F32)
    x2 = x_ref[0] + (1.0 + mod_ref[0, 5:6]) * y
    if final_norm:
        x2 = (x2 * lax.rsqrt(jnp.mean(x2 * x2, axis=-1, keepdims=True) + RMS_EPS)) * fg_ref[...]
    out_ref[0] = x2


def _moe_combine(tables, lpos, x, mod, final_g, o_sorted, final_norm):
    b, l, d = x.shape
    nt = l // MOE_BLOCK
    grid_spec = pltpu.PrefetchScalarGridSpec(
        num_scalar_prefetch=2,
        grid=(b, nt),
        in_specs=[pl.BlockSpec((MOE_BLOCK, TOP_K), lambda i, j, *_: (i * nt + j, 0)),
                  pl.BlockSpec((1, MOE_BLOCK, d), lambda i, j, *_: (i, j, 0)),
                  pl.BlockSpec((1, 8, d), lambda i, j, *_: (i, 0, 0)),
                  pl.BlockSpec((1, d), lambda i, j, *_: (0, 0)),
                  pl.BlockSpec(memory_space=pl.ANY)],
        out_specs=pl.BlockSpec((1, MOE_BLOCK, d), lambda i, j, *_: (i, j, 0)),
        scratch_shapes=[pltpu.VMEM((2, LOCAL_ROWS, d), F32), pltpu.SemaphoreType.DMA((2,))],
    )
    return pl.pallas_call(
        functools.partial(_moe_combine_kernel, final_norm=final_norm),
        grid_spec=grid_spec,
        out_shape=jax.ShapeDtypeStruct((b, l, d), F32),
        compiler_params=_cparams("arbitrary", "arbitrary"),
        name="moe_combine",
    )(*tables[:2], lpos, x, mod, final_g, o_sorted)


def _moe(x, hn, eid, wts, cnt, mod, w1, w3, w2, layer, final_g, final_norm):
    t, d = hn.shape
    nblk = t // MOE_BLOCK
    i32 = jnp.int32
    n = cnt[:, 0, :N_EXPERTS].astype(i32)
    run = (n + CHUNK_ROWS - 1) // CHUNK_ROWS * CHUNK_ROWS
    loff = jnp.cumsum(run, axis=1) - run
    rows_e = jnp.sum(run, axis=0)
    tiles_e = (rows_e + EXPERT_TILE - 1) // EXPERT_TILE
    tile_end = jnp.cumsum(tiles_e)
    base = (tile_end - tiles_e) * EXPERT_TILE
    goff = base[None, :] + jnp.cumsum(run, axis=0) - run
    nch = run // CHUNK_ROWS
    tot = jnp.sum(nch, axis=1)
    ch_end = jnp.cumsum(nch, axis=1)
    q = jnp.arange(MAX_CHUNKS, dtype=i32)
    in_run = ((q[None, :, None] >= (ch_end - nch)[:, None, :]) & (q[None, :, None] < ch_end[:, None, :])).astype(i32)
    crow = jnp.sum(in_run * (goff[:, None, :] + (q[None, :, None] - (ch_end - nch)[:, None, :]) * CHUNK_ROWS), axis=-1)
    padn = (tiles_e * EXPERT_TILE - rows_e) // CHUNK_ROWS
    pads = base + rows_e
    tables = (crow.reshape(-1), tot, padn, pads, tile_end[-1:])
    loff_f = jnp.zeros((nblk, 1, LANES), F32).at[:, 0, :N_EXPERTS].set(loff.astype(F32))

    max_rows = t * TOP_K + nblk * N_EXPERTS * (CHUNK_ROWS - 1) + N_EXPERTS * (EXPERT_TILE - 1)
    max_tiles = -(-max_rows // EXPERT_TILE)
    tile_ids = jnp.arange(max_tiles, dtype=i32)
    n_valid = tile_end[-1]
    tile_src = jnp.minimum(tile_ids, n_valid - 1)
    tile_expert = jnp.sum((tile_src[:, None] >= tile_end[None, :]).astype(i32), axis=1)
    tile_valid = (tile_ids < n_valid).astype(i32)
    tile_first = jnp.concatenate([jnp.ones((1,), i32), (tile_expert[1:] != tile_expert[:-1]).astype(i32)])

    xs, lpos = _moe_dispatch(tables, eid, wts, hn, loff_f, max_tiles * EXPERT_TILE)
    o_sorted = _moe_experts(tile_expert, tile_valid, tile_first, tile_src, xs, w1, w3, w2, layer)
    return _moe_combine(tables, lpos, x, mod, final_g, o_sorted, final_norm)


def _router_pack(wg, bg, we, be):
    d = wg.shape[0]
    wr = jnp.zeros((d, LANES), F32).at[:, :N_EXPERTS].set(we).at[:, N_EXPERTS:N_EXPERTS + N_GROUPS].set(wg)
    br = jnp.zeros((1, LANES), F32).at[0, :N_EXPERTS].set(be).at[0, N_EXPERTS:N_EXPERTS + N_GROUPS].set(bg)
    wr_hi = wr.astype(BF16)
    wr_lo = (wr - wr_hi.astype(F32)).astype(BF16)
    return wr_hi, wr_lo, br


def kernel(x, c, ada_w, ada_b, norm1_g, norm2_g, ssm_w_in, ssm_lam_re, ssm_lam_im, ssm_log_dt, ssm_b_re, ssm_b_im, ssm_c_re, ssm_c_im, ssm_d, ssm_w_glu, conv_w_in, conv_w, conv_w_out, moe_wg, moe_bg, moe_we, moe_be, moe_w1, moe_w3, moe_w2, final_g):
    b, l, d = x.shape
    depth = ada_w.shape[0]
    c8 = jnp.zeros((8, d), F32).at[:b].set(c)
    mod_all = _adaln(c8, ada_w, ada_b)[:, :b].reshape(depth, b, 6, d)
    mod_all = jnp.concatenate([mod_all, jnp.zeros((depth, b, 2, d), F32)], axis=2)
    fg = final_g.reshape(1, d)

    mod = mod_all[0]
    u4 = _s5_in(x, mod, norm1_g[0:1], ssm_w_in[0].T.astype(BF16))
    kern, f_mat, e_mat, a_pack, d_row = _s5_tables(
        ssm_lam_re[0], ssm_lam_im[0], ssm_log_dt[0], ssm_b_re[0], ssm_b_im[0],
        ssm_c_re[0], ssm_c_im[0], ssm_d[0])
    y4 = _s5_scan(u4, kern, f_mat, e_mat, a_pack, d_row)
    wrh, wrl, br = _router_pack(moe_wg[0], moe_bg[0], moe_we[0], moe_be[0])
    x1, hn, eid, wts, cnt = _s5_out(y4, x, mod, ssm_w_glu[0].astype(BF16), norm2_g[0:1], wrh, wrl, br)
    x2 = _moe(x1, hn, eid, wts, cnt, mod, moe_w1, moe_w3, moe_w2, 0, fg, False)

    mod = mod_all[1]
    cw8 = jnp.zeros((8, d), F32).at[:conv_w.shape[1]].set(conv_w[0])
    wrh, wrl, br = _router_pack(moe_wg[1], moe_bg[1], moe_we[1], moe_be[1])
    x3, hn, eid, wts, cnt = _conv_mixer(x2, mod, norm1_g[1:2], conv_w_in[0].astype(BF16), cw8,
                                        conv_w_out[0].astype(BF16), norm2_g[1:2], wrh, wrl, br)
    return _moe(x3, hn, eid, wts, cnt, mod, moe_w1, moe_w3, moe_w2, 1, fg, True)
```

```python
import functools
import math

import jax
import jax.numpy as jnp
from jax import lax
from jax.experimental import pallas as pl
from jax.experimental.pallas import tpu as pltpu

F32 = jnp.float32
BF16 = jnp.bfloat16
HIGHEST = lax.Precision.HIGHEST

RMS_EPS = 1e-6
SSM_GROUP = 16
SSM_CHUNK = 128
N_GROUPS = 4
EXPERTS_PER_GROUP = 8
N_EXPERTS = N_GROUPS * EXPERTS_PER_GROUP
TOP_K = 2
LANES = 128
TOKEN_TILE = 512
S5_TILE = 1024
S5_K_BLOCKS = 2
MOE_BLOCK = TOKEN_TILE
CHUNK_ROWS = 8
LOCAL_ROWS = -(-(TOP_K * MOE_BLOCK + N_EXPERTS * (CHUNK_ROWS - 1)) // LANES) * LANES
EXPERT_TILE = 512
VMEM_LIMIT = 56 * 1024 * 1024
NEG_INF = -1e30


def _cparams(*sem):
    return pltpu.CompilerParams(dimension_semantics=sem, vmem_limit_bytes=VMEM_LIMIT)


def _adaln_kernel(c_ref, w_ref, b_ref, o_ref):
    c = c_ref[...]
    cond = c * jax.nn.sigmoid(c)
    o_ref[0] = jnp.dot(cond, w_ref[0], precision=HIGHEST, preferred_element_type=F32) + b_ref[0]


def _adaln(c8, ada_w, ada_b):
    depth, d, n = ada_w.shape
    tn = 1536
    return pl.pallas_call(
        _adaln_kernel,
        grid=(depth, n // tn),
        in_specs=[pl.BlockSpec((8, d), lambda i, j: (0, 0)),
                  pl.BlockSpec((1, d, tn), lambda i, j: (i, 0, j)),
                  pl.BlockSpec((1, 1, tn), lambda i, j: (i, 0, j))],
        out_specs=pl.BlockSpec((1, 8, tn), lambda i, j: (i, 0, j)),
        out_shape=jax.ShapeDtypeStruct((depth, 8, n), F32),
        compiler_params=_cparams("parallel", "parallel"),
        name="adaln",
    )(c8, ada_w, ada_b.reshape(depth, 1, n))


def _norm_mod(x, g, shift, scale):
    y = x * lax.rsqrt(jnp.mean(x * x, axis=-1, keepdims=True) + RMS_EPS)
    return (y * g) * (1.0 + scale) + shift


def _route(hn, hn_hi, wrh_ref, wrl_ref, br_ref, eid_ref, wts_ref, cnt_ref):
    hn_lo = (hn - hn_hi.astype(F32)).astype(BF16)
    wrh = wrh_ref[...]
    logits = (jnp.dot(hn_hi, wrh, preferred_element_type=F32)
              + jnp.dot(hn_lo, wrh, preferred_element_type=F32)
              + jnp.dot(hn_hi, wrl_ref[...], preferred_element_type=F32)) + br_ref[...]
    lane = lax.broadcasted_iota(jnp.int32, logits.shape, 1)
    is_grp = (lane >= N_EXPERTS) & (lane < N_EXPERTS + N_GROUPS)
    lg = jnp.where(is_grp, logits, NEG_INF)
    gmax = jnp.max(lg, axis=-1, keepdims=True)
    gsum = jnp.sum(jnp.where(is_grp, jnp.exp(lg - gmax), 0.0), axis=-1, keepdims=True)
    gp = 1.0 / gsum
    gi = jnp.min(jnp.where(lg == gmax, lane, 2 * LANES), axis=-1, keepdims=True) - N_EXPERTS
    in_grp = (lane < N_EXPERTS) & ((lane // EXPERTS_PER_GROUP) == gi)
    le = jnp.where(in_grp, logits, NEG_INF)
    v1 = jnp.max(le, axis=-1, keepdims=True)
    i1 = jnp.min(jnp.where(le == v1, lane, 2 * LANES), axis=-1, keepdims=True)
    le2 = jnp.where(lane == i1, NEG_INF, le)
    v2 = jnp.max(le2, axis=-1, keepdims=True)
    i2 = jnp.min(jnp.where(le2 == v2, lane, 2 * LANES), axis=-1, keepdims=True)
    e2 = jnp.exp(v2 - v1)
    den = 1.0 + e2
    col = lax.broadcasted_iota(jnp.int32, (hn.shape[0], TOP_K), 1)
    eid_ref[...] = jnp.where(col == 0, i1, i2)
    wts_ref[...] = jnp.where(col == 0, gp / den, gp * e2 / den)
    chosen = ((lane == i1) | (lane == i2)).astype(F32)
    cnt_ref[...] = jnp.broadcast_to(jnp.sum(chosen, axis=0, keepdims=True), cnt_ref.shape)


def _residual_and_route(x, out, mod, g2_ref, wrh_ref, wrl_ref, br_ref, x_out_ref, hn_ref, eid_ref, wts_ref, cnt_ref):
    x1 = x + (1.0 + mod[2:3]) * out
    x_out_ref[...] = x1
    hn = _norm_mod(x1, g2_ref[...], mod[3:4], mod[4:5])
    hn_hi = hn.astype(BF16)
    hn_ref[...] = hn_hi
    _route(hn, hn_hi, wrh_ref, wrl_ref, br_ref, eid_ref, wts_ref, cnt_ref)


def _s5_in_kernel(x_ref, mod_ref, g_ref, wt_ref, u_ref):
    mod = mod_ref[0]
    hn = _norm_mod(x_ref[0], g_ref[...], mod[0:1], mod[1:2])
    ut = lax.dot_general(wt_ref[...], hn.astype(BF16), (((1,), (1,)), ((), ())), preferred_element_type=F32)
    u_ref[0] = ut.reshape(ut.shape[0], S5_TILE // SSM_CHUNK, SSM_CHUNK)


def _s5_in(x, mod, g, wt_bf):
    b, l, d = x.shape
    h = wt_bf.shape[0]
    cpt = S5_TILE // SSM_CHUNK
    return pl.pallas_call(
        _s5_in_kernel,
        grid=(b, l // S5_TILE),
        in_specs=[pl.BlockSpec((1, S5_TILE, d), lambda i, j: (i, j, 0)),
                  pl.BlockSpec((1, 8, d), lambda i, j: (i, 0, 0)),
                  pl.BlockSpec((1, d), lambda i, j: (0, 0)),
                  pl.BlockSpec((h, d), lambda i, j: (0, 0))],
        out_specs=pl.BlockSpec((1, h, cpt, SSM_CHUNK), lambda i, j: (i, 0, j, 0)),
        out_shape=jax.ShapeDtypeStruct((b, h, l // SSM_CHUNK, SSM_CHUNK), F32),
        compiler_params=_cparams("parallel", "parallel"),
        name="s5_in",
    )(x, mod, g, wt_bf)


def _s5_scan_kernel(u_ref, k_ref, kn_ref, f_ref, e_ref, a_ref, d_ref, y_ref,
                    m0_ref, m1_ref, acc_ref, s_ref, sw_ref, sp_ref):
    bsz, grp, n_chunks, tc = u_ref.shape
    g = pl.program_id(0)
    srow = lax.broadcasted_iota(jnp.int32, (tc, tc), 0)
    tcol = lax.broadcasted_iota(jnp.int32, (tc, tc), 1)
    causal = tcol >= srow

    def build_rows(src_ref, dst_ref, i):
        blks = []
        for j in range(grp):
            row = src_ref[0, pl.ds(i * grp + j, 1), :]
            blk = pltpu.roll(jnp.broadcast_to(row, (tc, tc)), 0, 1, stride=1, stride_axis=0)
            blks.append(jnp.where(causal, blk, 0.0))
        dst_ref[pl.ds(pl.multiple_of(i * tc, tc), tc), :] = jnp.concatenate(blks, axis=1).astype(BF16)

    @pl.when(g == 0)
    def _():
        def first(i, carry):
            build_rows(k_ref, m0_ref, i)
            return carry

        lax.fori_loop(0, grp, first, 0)

    x = jnp.concatenate(
        [jnp.concatenate([u_ref[b, k] for k in range(grp)], axis=1) for b in range(bsz)], axis=0)
    xb = x.astype(BF16)
    s_loc = jnp.dot(xb, f_ref[0], preferred_element_type=F32)
    s_ref[...] = s_loc
    half = s_loc.shape[1] // 2
    sw_ref[...] = jnp.concatenate([s_loc[:, half:], s_loc[:, :half]], axis=1)
    a1 = a_ref[0, 0:1, :]
    a2 = a_ref[0, 1:2, :]
    s = [jnp.zeros((1, s_loc.shape[1]), F32) for _ in range(bsz)]
    sw = [jnp.zeros((1, s_loc.shape[1]), F32) for _ in range(bsz)]
    for c in range(n_chunks):
        for b in range(bsz):
            r = b * n_chunks + c
            sp_ref[r:r + 1, :] = s[b]
            s_new = a1 * s[b] + a2 * sw[b] + s_ref[r:r + 1, :]
            sw[b] = a1 * sw[b] - a2 * s[b] + sw_ref[r:r + 1, :]
            s[b] = s_new
    e_mat = jnp.concatenate([e_ref[0, j] for j in range(grp)], axis=1)
    acc_ref[...] = jnp.dot(sp_ref[...].astype(BF16), e_mat, preferred_element_type=F32) + d_ref[0] * x

    def run(m_cur_ref, m_next_ref):
        def step(c, carry):
            for h in range(S5_K_BLOCKS):
                build_rows(kn_ref, m_next_ref, c * S5_K_BLOCKS + h)
            xc = jnp.concatenate(
                [jnp.concatenate([u_ref[b, c * S5_K_BLOCKS + h] for h in range(S5_K_BLOCKS)], axis=1)
                 for b in range(bsz)], axis=0).astype(BF16)
            rows = pl.ds(pl.multiple_of(c * (S5_K_BLOCKS * tc), S5_K_BLOCKS * tc), S5_K_BLOCKS * tc)
            acc_ref[...] += jnp.dot(xc, m_cur_ref[rows, :], preferred_element_type=F32)
            return carry

        lax.fori_loop(0, grp // S5_K_BLOCKS, step, 0)

    @pl.when(g % 2 == 0)
    def _():
        run(m0_ref, m1_ref)

    @pl.when(g % 2 == 1)
    def _():
        run(m1_ref, m0_ref)
    for b in range(bsz):
        for j in range(grp):
            y_ref[b, j] = acc_ref[b * n_chunks:(b + 1) * n_chunks, j * tc:(j + 1) * tc]


def _s5_scan(u4, kern, f_bf, e4_bf, a_pack, d_row):
    b, h, nc, tc = u4.shape
    g = kern.shape[0]
    grp = h // g
    k = grp * tc
    p2 = f_bf.shape[2]
    rows = b * nc
    return pl.pallas_call(
        _s5_scan_kernel,
        grid=(g,),
        in_specs=[pl.BlockSpec((b, grp, nc, tc), lambda i: (0, i, 0, 0)),
                  pl.BlockSpec((1, grp * grp, tc), lambda i: (i, 0, 0)),
                  pl.BlockSpec((1, grp * grp, tc), lambda i: (jnp.minimum(i + 1, g - 1), 0, 0)),
                  pl.BlockSpec((1, k, p2), lambda i: (i, 0, 0)),
                  pl.BlockSpec((1, grp, p2, tc), lambda i: (i, 0, 0, 0)),
                  pl.BlockSpec((1, 8, p2), lambda i: (i, 0, 0)),
                  pl.BlockSpec((1, 1, k), lambda i: (i, 0, 0))],
        out_specs=pl.BlockSpec((b, grp, nc, tc), lambda i: (0, i, 0, 0)),
        out_shape=jax.ShapeDtypeStruct(u4.shape, F32),
        scratch_shapes=[pltpu.VMEM((k, k), BF16), pltpu.VMEM((k, k), BF16),
                        pltpu.VMEM((rows, k), F32), pltpu.VMEM((rows, p2), F32),
                        pltpu.VMEM((rows, p2), F32), pltpu.VMEM((rows, p2), F32)],
        compiler_params=_cparams("arbitrary"),
        name="s5_scan",
    )(u4, kern, kern, f_bf, e4_bf, a_pack, d_row)


def _s5_tables(lam_re, lam_im, log_dt, b_re, b_im, c_re, c_im, d_skip):
    g, p = lam_re.shape
    k = SSM_GROUP
    tc = SSM_CHUNK
    dt = jnp.exp(log_dt)[:, None]
    mag = jnp.exp(lam_re * dt)
    ab_re = mag * jnp.cos(lam_im * dt)
    ab_im = mag * jnp.sin(lam_im * dt)
    den = lam_re * lam_re + lam_im * lam_im
    cf_re = ((ab_re - 1) * lam_re + ab_im * lam_im) / den
    cf_im = (ab_im * lam_re - (ab_re - 1) * lam_im) / den
    bb_re = cf_re[..., None] * b_re - cf_im[..., None] * b_im
    bb_im = cf_re[..., None] * b_im + cf_im[..., None] * b_re
    lags = jnp.arange(tc + 1, dtype=F32)[:, None, None]
    pmag = jnp.exp(lags * (lam_re * dt)[None])
    pang = lags * (lam_im * dt)[None]
    p_re = pmag * jnp.cos(pang)
    p_im = pmag * jnp.sin(pang)
    cb_re = jnp.einsum('gjp,gpi->gpij', c_re, bb_re) - jnp.einsum('gjp,gpi->gpij', c_im, bb_im)
    cb_im = jnp.einsum('gjp,gpi->gpij', c_re, bb_im) + jnp.einsum('gjp,gpi->gpij', c_im, bb_re)
    kern = (jnp.einsum('lgp,gpij->gijl', p_re[:tc], cb_re, precision=HIGHEST)
            - jnp.einsum('lgp,gpij->gijl', p_im[:tc], cb_im, precision=HIGHEST))
    kern = kern.reshape(g, k * k, tc)
    lre = jnp.concatenate([lam_re * dt, lam_re * dt], axis=-1)
    lim = jnp.concatenate([lam_im * dt, lam_im * dt], axis=-1)
    sign = jnp.concatenate([-jnp.ones((p,), F32), jnp.ones((p,), F32)])
    pw_f = (tc - 1 - jnp.arange(tc, dtype=F32))[None, :, None]
    mag_f = jnp.exp(pw_f * lre[:, None, :])
    pr2 = mag_f * jnp.cos(pw_f * lim[:, None, :])
    pi2 = mag_f * jnp.sin(pw_f * lim[:, None, :]) * sign
    bb1 = jnp.concatenate([bb_re, bb_im], axis=1).transpose(0, 2, 1)
    bb2 = jnp.concatenate([bb_im, bb_re], axis=1).transpose(0, 2, 1)
    f_mat = (pr2[:, None] * bb1[:, :, None] + pi2[:, None] * bb2[:, :, None]).reshape(g, k * tc, 2 * p).astype(BF16)
    pw_e = (jnp.arange(tc, dtype=F32) + 1.0)[None, None, :]
    mag_e = jnp.exp(pw_e * lre[:, :, None])
    qr2 = mag_e * jnp.cos(pw_e * lim[:, :, None])
    qi2 = mag_e * jnp.sin(pw_e * lim[:, :, None])
    ca = jnp.concatenate([c_re, -c_im], axis=-1)
    cb = jnp.concatenate([-c_im, -c_re], axis=-1)
    e_mat = (ca[:, :, :, None] * qr2[:, None] + cb[:, :, :, None] * qi2[:, None]).astype(BF16)
    ar = p_re[tc]
    ai = p_im[tc]
    a1 = jnp.concatenate([ar, ar], axis=-1)
    a2 = jnp.concatenate([-ai, ai], axis=-1)
    a_pack = jnp.concatenate([a1[:, None], a2[:, None], jnp.zeros((g, 6, 2 * p), F32)], axis=1)
    d_row = jnp.repeat(d_skip.reshape(g, k), tc, axis=1).reshape(g, 1, k * tc)
    return kern, f_mat, e_mat, a_pack, d_row


def _gelu_tanh(x):
    return 0.5 * x * (1.0 + jnp.tanh(math.sqrt(2.0 / math.pi) * (x + 0.044715 * (x * x * x))))


def _s5_out_kernel(y_ref, x_ref, mod_ref, w_ref, g2_ref, wrh_ref, wrl_ref, br_ref,
                   x_out_ref, hn_ref, eid_ref, wts_ref, cnt_ref):
    mod = mod_ref[0]
    h = y_ref.shape[1]
    yt = _gelu_tanh(y_ref[0].reshape(h, S5_TILE))
    d = w_ref.shape[1] // 2
    for part in range(S5_TILE // MOE_BLOCK):
        rows = slice(part * MOE_BLOCK, (part + 1) * MOE_BLOCK)
        o = lax.dot_general(yt[:, rows].astype(BF16), w_ref[...], (((0,), (0,)), ((), ())),
                            preferred_element_type=F32)
        out = o[:, :d] * jax.nn.sigmoid(o[:, d:])
        _residual_and_route(x_ref[0, rows, :], out, mod, g2_ref, wrh_ref, wrl_ref, br_ref,
                            x_out_ref.at[0, rows, :], hn_ref.at[rows, :], eid_ref.at[rows, :],
                            wts_ref.at[rows, :], cnt_ref.at[part])


def _mixer_out_specs(b, l, d, tile):
    nt = l // tile
    nblk = tile // MOE_BLOCK
    specs = [pl.BlockSpec((1, tile, d), lambda i, j: (i, j, 0)),
             pl.BlockSpec((tile, d), lambda i, j: (i * nt + j, 0)),
             pl.BlockSpec((tile, TOP_K), lambda i, j: (i * nt + j, 0)),
             pl.BlockSpec((tile, TOP_K), lambda i, j: (i * nt + j, 0)),
             pl.BlockSpec((nblk, 8, LANES), lambda i, j: (i * nt + j, 0, 0))]
    shapes = [jax.ShapeDtypeStruct((b, l, d), F32),
              jax.ShapeDtypeStruct((b * l, d), BF16),
              jax.ShapeDtypeStruct((b * l, TOP_K), jnp.int32),
              jax.ShapeDtypeStruct((b * l, TOP_K), F32),
              jax.ShapeDtypeStruct((b * l // MOE_BLOCK, 8, LANES), F32)]
    return specs, shapes


def _s5_out(y4, x, mod, w_bf, g2, wrh, wrl, br):
    b, l, d = x.shape
    h = y4.shape[1]
    specs, shapes = _mixer_out_specs(b, l, d, S5_TILE)
    return pl.pallas_call(
        _s5_out_kernel,
        grid=(b, l // S5_TILE),
        in_specs=[pl.BlockSpec((1, h, S5_TILE // SSM_CHUNK, SSM_CHUNK), lambda i, j: (i, 0, j, 0)),
                  pl.BlockSpec((1, S5_TILE, d), lambda i, j: (i, j, 0)),
                  pl.BlockSpec((1, 8, d), lambda i, j: (i, 0, 0)),
                  pl.BlockSpec(w_bf.shape, lambda i, j: (0, 0), pipeline_mode=pl.Buffered(1)),
                  pl.BlockSpec((1, d), lambda i, j: (0, 0)),
                  pl.BlockSpec((d, LANES), lambda i, j: (0, 0)),
                  pl.BlockSpec((d, LANES), lambda i, j: (0, 0)),
                  pl.BlockSpec((1, LANES), lambda i, j: (0, 0))],
        out_specs=specs,
        out_shape=shapes,
        compiler_params=_cparams("parallel", "parallel"),
        name="s5_out",
    )(y4, x, mod, w_bf, g2, wrh, wrl, br)


def _conv_mixer_kernel(x_ref, mod_ref, g1_ref, win_ref, cw_ref, wout_ref, g2_ref, wrh_ref, wrl_ref, br_ref,
                       x_out_ref, hn_ref, eid_ref, wts_ref, cnt_ref, carry_ref):
    @pl.when(pl.program_id(1) == 0)
    def _():
        carry_ref[...] = jnp.zeros_like(carry_ref)

    mod = mod_ref[0]
    x = x_ref[0]
    hn = _norm_mod(x, g1_ref[...], mod[0:1], mod[1:2])
    p = jnp.dot(hn.astype(BF16), win_ref[...], preferred_element_type=F32)
    d = p.shape[1] // 3
    b_g = p[:, :d]
    u = p[:, d:2 * d] * p[:, 2 * d:]
    ext = jnp.concatenate([carry_ref[...], u], axis=0)
    t = u.shape[0]
    cw = cw_ref[...]
    z = cw[0:1] * ext[6:6 + t] + cw[1:2] * ext[7:7 + t] + cw[2:3] * u
    carry_ref[...] = u[t - 8:]
    out = jnp.dot((b_g * z).astype(BF16), wout_ref[...], preferred_element_type=F32)
    _residual_and_route(x, out, mod, g2_ref, wrh_ref, wrl_ref, br_ref, x_out_ref.at[0], hn_ref, eid_ref,
                        wts_ref, cnt_ref.at[0])


def _conv_mixer(x, mod, g1, win_bf, cw8, wout_bf, g2, wrh, wrl, br):
    b, l, d = x.shape
    specs, shapes = _mixer_out_specs(b, l, d, TOKEN_TILE)
    return pl.pallas_call(
        _conv_mixer_kernel,
        grid=(b, l // TOKEN_TILE),
        in_specs=[pl.BlockSpec((1, TOKEN_TILE, d), lambda i, j: (i, j, 0)),
                  pl.BlockSpec((1, 8, d), lambda i, j: (i, 0, 0)),
                  pl.BlockSpec((1, d), lambda i, j: (0, 0)),
                  pl.BlockSpec(win_bf.shape, lambda i, j: (0, 0)),
                  pl.BlockSpec((8, d), lambda i, j: (0, 0)),
                  pl.BlockSpec(wout_bf.shape, lambda i, j: (0, 0)),
                  pl.BlockSpec((1, d), lambda i, j: (0, 0)),
                  pl.BlockSpec((d, LANES), lambda i, j: (0, 0)),
                  pl.BlockSpec((d, LANES), lambda i, j: (0, 0)),
                  pl.BlockSpec((1, LANES), lambda i, j: (0, 0))],
        out_specs=specs,
        out_shape=shapes,
        scratch_shapes=[pltpu.VMEM((8, d), F32)],
        compiler_params=_cparams("parallel", "arbitrary"),
        name="conv_mixer",
    )(x, mod, g1, win_bf, cw8, wout_bf, g2, wrh, wrl, br)


_TN = (((0,), (0,)), ((), ()))


MAX_CHUNKS = LOCAL_ROWS // CHUNK_ROWS


def _rows_copy(src, s_row, dst, d_row, rows, sem):
    return pltpu.make_async_copy(src.at[pl.ds(s_row, rows), :], dst.at[pl.ds(d_row, rows), :], sem)


def _start_block_chunks(blk, count, crow_ref, make_copy):
    def body(q, carry):
        make_copy(pl.multiple_of(q * CHUNK_ROWS, CHUNK_ROWS),
                  pl.multiple_of(crow_ref[blk * MAX_CHUNKS + q], CHUNK_ROWS)).start()
        return carry

    lax.fori_loop(0, count, body, 0)


def _wait_chunks(count, max_count, make_wait):
    for k in range(max_count.bit_length()):
        @pl.when(((count >> k) & 1) == 1)
        def _():
            make_wait(CHUNK_ROWS << k).wait()


def _slot_masks(lp0, lp1):
    slot = lax.broadcasted_iota(jnp.int32, (lp0.shape[0], LOCAL_ROWS), 1)
    return slot == lp0, slot == lp1


def _moe_dispatch_kernel(crow_ref, tot_ref, padn_ref, pads_ref, nv_ref,
                         eid_ref, wts_ref, hn_ref, lofff_ref, xs_ref, lpos_ref, loc_ref, zero_ref, sems):
    blk = pl.program_id(0)
    d = hn_ref.shape[1]
    eid = eid_ref[...]
    t = eid.shape[0]
    lane = lax.broadcasted_iota(jnp.int32, (t, LANES), 1)
    hit0 = lane == eid[:, 0:1]
    hit1 = lane == eid[:, 1:2]
    onehot = (hit0 | hit1).astype(BF16)
    r = lax.broadcasted_iota(jnp.int32, (t, t), 0)
    c = lax.broadcasted_iota(jnp.int32, (t, t), 1)
    tri = (c < r).astype(BF16)
    before = jnp.dot(tri, onehot, preferred_element_type=F32) + lofff_ref[0]
    lp0 = jnp.sum(jnp.where(hit0, before, 0.0), axis=-1, keepdims=True).astype(jnp.int32)
    lp1 = jnp.sum(jnp.where(hit1, before, 0.0), axis=-1, keepdims=True).astype(jnp.int32)
    col = lax.broadcasted_iota(jnp.int32, (t, TOP_K), 1)
    lpos_ref[...] = jnp.where(col == 0, lp0, lp1)
    m0, m1 = _slot_masks(lp0, lp1)
    wts = wts_ref[...]
    wrow = jnp.sum(jnp.where(m0, wts[:, 0:1], 0.0) + jnp.where(m1, wts[:, 1:2], 0.0), axis=0, keepdims=True)

    buf = blk % 2
    loc = loc_ref.at[buf]

    def drain(which, count):
        _wait_chunks(count, MAX_CHUNKS,
                     lambda rows: _rows_copy(loc_ref.at[which], 0, xs_ref, 0, rows, sems.at[which]))

    @pl.when(blk >= 2)
    def _():
        drain(buf, tot_ref[blk - 2])

    loc[:, :d] = lax.dot_general((m0 | m1).astype(BF16), hn_ref[...], _TN, preferred_element_type=F32)
    loc[:, d:] = jnp.broadcast_to(wrow, (LANES, LOCAL_ROWS)).T

    _start_block_chunks(blk, tot_ref[blk], crow_ref,
                        lambda lo, go: _rows_copy(loc, lo, xs_ref, go, CHUNK_ROWS, sems.at[buf]))

    @pl.when(blk == pl.num_programs(0) - 1)
    def _():
        drain(1 - buf, tot_ref[blk - 1])
        drain(buf, tot_ref[blk])
        sem = sems.at[0]
        zero_ref[...] = jnp.zeros_like(zero_ref)
        pad_bits = (EXPERT_TILE // CHUNK_ROWS - 1).bit_length()

        def pad_copies(e, carry, *, wait):
            padn = padn_ref[e]
            for k in range(pad_bits):
                @pl.when(((padn >> k) & 1) == 1)
                def _():
                    done = (padn & ((1 << k) - 1)) * CHUNK_ROWS
                    cp = _rows_copy(zero_ref, 0, xs_ref, pl.multiple_of(pads_ref[e] + done, CHUNK_ROWS),
                                    CHUNK_ROWS << k, sem)
                    if wait:
                        cp.wait()
                    else:
                        cp.start()
            return carry

        lax.fori_loop(0, N_EXPERTS, functools.partial(pad_copies, wait=False), 0)
        lax.fori_loop(0, N_EXPERTS, functools.partial(pad_copies, wait=True), 0)

        def tile_copy(i):
            return pltpu.make_async_copy(zero_ref, xs_ref.at[pl.ds(pl.multiple_of(i * EXPERT_TILE, EXPERT_TILE),
                                                                   EXPERT_TILE), :], sem)

        n_tiles = xs_ref.shape[0] // EXPERT_TILE

        def start_tile(i, c):
            tile_copy(i).start()
            return c

        def wait_tile(i, c):
            tile_copy(i).wait()
            return c

        lax.fori_loop(nv_ref[0], n_tiles, start_tile, 0)
        lax.fori_loop(nv_ref[0], n_tiles, wait_tile, 0)


def _moe_dispatch(tables, eid, wts, hn, loff_f, n_rows):
    t, d = hn.shape
    width = d + LANES
    grid_spec = pltpu.PrefetchScalarGridSpec(
        num_scalar_prefetch=5,
        grid=(t // MOE_BLOCK,),
        in_specs=[pl.BlockSpec((MOE_BLOCK, TOP_K), lambda i, *_: (i, 0)),
                  pl.BlockSpec((MOE_BLOCK, TOP_K), lambda i, *_: (i, 0)),
                  pl.BlockSpec((MOE_BLOCK, d), lambda i, *_: (i, 0)),
                  pl.BlockSpec((1, 1, LANES), lambda i, *_: (i, 0, 0))],
        out_specs=[pl.BlockSpec(memory_space=pl.ANY),
                   pl.BlockSpec((MOE_BLOCK, TOP_K), lambda i, *_: (i, 0))],
        scratch_shapes=[pltpu.VMEM((2, LOCAL_ROWS, width), F32), pltpu.VMEM((EXPERT_TILE, width), F32),
                        pltpu.SemaphoreType.DMA((2,))],
    )
    return pl.pallas_call(
        _moe_dispatch_kernel,
        grid_spec=grid_spec,
        out_shape=[jax.ShapeDtypeStruct((n_rows, width), F32),
                   jax.ShapeDtypeStruct((t, TOP_K), jnp.int32)],
        compiler_params=_cparams("arbitrary"),
        name="moe_dispatch",
    )(*tables, eid, wts, hn, loff_f)


def _moe_experts_kernel(te_ref, tv_ref, tf_ref, ts_ref, xs_ref, w1_ref, w3_ref, w2_ref, o_ref,
                        w1b_ref, w3b_ref, w2b_ref):
    del te_ref, ts_ref
    i = pl.program_id(0)

    @pl.when(tf_ref[i] != 0)
    def _():
        def cast_rows(r, carry):
            rows = pl.ds(pl.multiple_of(r * LANES, LANES), LANES)
            w1b_ref[rows, :] = w1_ref[0, 0, rows, :].astype(BF16)
            w3b_ref[rows, :] = w3_ref[0, 0, rows, :].astype(BF16)

            @pl.when(r < w2b_ref.shape[0] // LANES)
            def _():
                w2b_ref[rows, :] = w2_ref[0, 0, rows, :].astype(BF16)

            return carry

        lax.fori_loop(0, w1b_ref.shape[0] // LANES, cast_rows, 0)

    @pl.when(tv_ref[i] != 0)
    def _():
        d = o_ref.shape[1]
        xb = xs_ref[:, :d].astype(BF16)
        w = xs_ref[:, d:d + 1]
        a = jnp.dot(xb, w1b_ref[...], preferred_element_type=F32)
        b = jnp.dot(xb, w3b_ref[...], preferred_element_type=F32)
        h = (a * jax.nn.sigmoid(a) * b).astype(BF16)
        o_ref[...] = jnp.dot(h, w2b_ref[...], preferred_element_type=F32) * w

    @pl.when(tv_ref[i] == 0)
    def _():
        o_ref[...] = jnp.zeros_like(o_ref)


def _moe_experts(tile_expert, tile_valid, tile_first, tile_src, xs, w1, w3, w2, layer):
    r, width = xs.shape
    d = width - LANES
    de = w1.shape[3]
    grid_spec = pltpu.PrefetchScalarGridSpec(
        num_scalar_prefetch=4,
        grid=(r // EXPERT_TILE,),
        in_specs=[pl.BlockSpec((EXPERT_TILE, width), lambda i, te, tv, tf, ts: (ts[i], 0)),
                  pl.BlockSpec((1, 1, d, de), lambda i, te, tv, tf, ts: (layer, te[i], 0, 0)),
                  pl.BlockSpec((1, 1, d, de), lambda i, te, tv, tf, ts: (layer, te[i], 0, 0)),
                  pl.BlockSpec((1, 1, de, d), lambda i, te, tv, tf, ts: (layer, te[i], 0, 0))],
        out_specs=pl.BlockSpec((EXPERT_TILE, d), lambda i, te, tv, tf, ts: (i, 0)),
        scratch_shapes=[pltpu.VMEM((d, de), BF16), pltpu.VMEM((d, de), BF16), pltpu.VMEM((de, d), BF16)],
    )
    return pl.pallas_call(
        _moe_experts_kernel,
        grid_spec=grid_spec,
        out_shape=jax.ShapeDtypeStruct((r, d), F32),
        compiler_params=_cparams("arbitrary"),
        name="moe_experts",
    )(tile_expert, tile_valid, tile_first, tile_src, xs, w1, w3, w2)


def _moe_combine_kernel(crow_ref, tot_ref, lpos_ref, x_ref, mod_ref, fg_ref, o_hbm_ref,
                        out_ref, loc_ref, sems, *, final_norm):
    blk = pl.program_id(0) * pl.num_programs(1) + pl.program_id(1)
    n_blk = pl.num_programs(0) * pl.num_programs(1)
    t = x_ref.shape[1]
    buf = blk % 2

    def fetch(b, which):
        loc = loc_ref.at[which]
        loc[TOP_K * t:, :] = jnp.zeros((LOCAL_ROWS - TOP_K * t, loc_ref.shape[2]), F32)
        _start_block_chunks(b, tot_ref[b], crow_ref,
                            lambda lo, go: _rows_copy(o_hbm_ref, go, loc, lo, CHUNK_ROWS, sems.at[which]))

    @pl.when(blk == 0)
    def _():
        fetch(blk, buf)

    @pl.when(blk + 1 < n_blk)
    def _():
        fetch(blk + 1, 1 - buf)

    _wait_chunks(tot_ref[blk], MAX_CHUNKS,
                 lambda rows: _rows_copy(o_hbm_ref, 0, loc_ref.at[buf], 0, rows, sems.at[buf]))
    lp = lpos_ref[...]
    m0, m1 = _slot_masks(lp[:, 0:1], lp[:, 1:2])
    y = jnp.dot((m0 | m1).astype(BF16), loc_ref[buf].astype(BF16), preferred_element_type=F32)
    x2 = x_ref[0] + (1.0 + mod_ref[0, 5:6]) * y
    if final_norm:
        x2 = (x2 * lax.rsqrt(jnp.mean(x2 * x2, axis=-1, keepdims=True) + RMS_EPS)) * fg_ref[...]
    out_ref[0] = x2


def _moe_combine(tables, lpos, x, mod, final_g, o_sorted, final_norm):
    b, l, d = x.shape
    nt = l // MOE_BLOCK
    grid_spec = pltpu.PrefetchScalarGridSpec(
        num_scalar_prefetch=2,
        grid=(b, nt),
        in_specs=[pl.BlockSpec((MOE_BLOCK, TOP_K), lambda i, j, *_: (i * nt + j, 0)),
                  pl.BlockSpec((1, MOE_BLOCK, d), lambda i, j, *_: (i, j, 0)),
                  pl.BlockSpec((1, 8, d), lambda i, j, *_: (i, 0, 0)),
                  pl.BlockSpec((1, d), lambda i, j, *_: (0, 0)),
                  pl.BlockSpec(memory_space=pl.ANY)],
        out_specs=pl.BlockSpec((1, MOE_BLOCK, d), lambda i, j, *_: (i, j, 0)),
        scratch_shapes=[pltpu.VMEM((2, LOCAL_ROWS, d), F32), pltpu.SemaphoreType.DMA((2,))],
    )
    return pl.pallas_call(
        functools.partial(_moe_combine_kernel, final_norm=final_norm),
        grid_spec=grid_spec,
        out_shape=jax.ShapeDtypeStruct((b, l, d), F32),
        compiler_params=_cparams("arbitrary", "arbitrary"),
        name="moe_combine",
    )(*tables[:2], lpos, x, mod, final_g, o_sorted)


def _moe(x, hn, eid, wts, cnt, mod, w1, w3, w2, layer, final_g, final_norm):
    t, d = hn.shape
    nblk = t // MOE_BLOCK
    i32 = jnp.int32
    n = cnt[:, 0, :N_EXPERTS].astype(i32)
    run = (n + CHUNK_ROWS - 1) // CHUNK_ROWS * CHUNK_ROWS
    loff = jnp.cumsum(run, axis=1) - run
    rows_e = jnp.sum(run, axis=0)
    tiles_e = (rows_e + EXPERT_TILE - 1) // EXPERT_TILE
    tile_end = jnp.cumsum(tiles_e)
    base = (tile_end - tiles_e) * EXPERT_TILE
    goff = base[None, :] + jnp.cumsum(run, axis=0) - run
    nch = run // CHUNK_ROWS
    tot = jnp.sum(nch, axis=1)
    ch_end = jnp.cumsum(nch, axis=1)
    q = jnp.arange(MAX_CHUNKS, dtype=i32)
    in_run = ((q[None, :, None] >= (ch_end - nch)[:, None, :]) & (q[None, :, None] < ch_end[:, None, :])).astype(i32)
    crow = jnp.sum(in_run * (goff[:, None, :] + (q[None, :, None] - (ch_end - nch)[:, None, :]) * CHUNK_ROWS), axis=-1)
    padn = (tiles_e * EXPERT_TILE - rows_e) // CHUNK_ROWS
    pads = base + rows_e
    tables = (crow.reshape(-1), tot, padn, pads, tile_end[-1:])
    loff_f = jnp.zeros((nblk, 1, LANES), F32).at[:, 0, :N_EXPERTS].set(loff.astype(F32))

    max_rows = t * TOP_K + nblk * N_EXPERTS * (CHUNK_ROWS - 1) + N_EXPERTS * (EXPERT_TILE - 1)
    max_tiles = -(-max_rows // EXPERT_TILE)
    tile_ids = jnp.arange(max_tiles, dtype=i32)
    n_valid = tile_end[-1]
    tile_src = jnp.minimum(tile_ids, n_valid - 1)
    tile_expert = jnp.sum((tile_src[:, None] >= tile_end[None, :]).astype(i32), axis=1)
    tile_valid = (tile_ids < n_valid).astype(i32)
    tile_first = jnp.concatenate([jnp.ones((1,), i32), (tile_expert[1:] != tile_expert[:-1]).astype(i32)])

    xs, lpos = _moe_dispatch(tables, eid, wts, hn, loff_f, max_tiles * EXPERT_TILE)
    o_sorted = _moe_experts(tile_expert, tile_valid, tile_first, tile_src, xs, w1, w3, w2, layer)
    return _moe_combine(tables, lpos, x, mod, final_g, o_sorted, final_norm)


def _router_pack(wg, bg, we, be):
    d = wg.shape[0]
    wr = jnp.zeros((d, LANES), F32).at[:, :N_EXPERTS].set(we).at[:, N_EXPERTS:N_EXPERTS + N_GROUPS].set(wg)
    br = jnp.zeros((1, LANES), F32).at[0, :N_EXPERTS].set(be).at[0, N_EXPERTS:N_EXPERTS + N_GROUPS].set(bg)
    wr_hi = wr.astype(BF16)
    wr_lo = (wr - wr_hi.astype(F32)).astype(BF16)
    return wr_hi, wr_lo, br


def kernel(x, c, ada_w, ada_b, norm1_g, norm2_g, ssm_w_in, ssm_lam_re, ssm_lam_im, ssm_log_dt, ssm_b_re, ssm_b_im, ssm_c_re, ssm_c_im, ssm_d, ssm_w_glu, conv_w_in, conv_w, conv_w_out, moe_wg, moe_bg, moe_we, moe_be, moe_w1, moe_w3, moe_w2, final_g):
    b, l, d = x.shape
    depth = ada_w.shape[0]
    c8 = jnp.zeros((8, d), F32).at[:b].set(c)
    mod_all = _adaln(c8, ada_w, ada_b)[:, :b].reshape(depth, b, 6, d)
    mod_all = jnp.concatenate([mod_all, jnp.zeros((depth, b, 2, d), F32)], axis=2)
    fg = final_g.reshape(1, d)

    mod = mod_all[0]
    u4 = _s5_in(x, mod, norm1_g[0:1], ssm_w_in[0].T.astype(BF16))
    kern, f_mat, e_mat, a_pack, d_row = _s5_tables(
        ssm_lam_re[0], ssm_lam_im[0], ssm_log_dt[0], ssm_b_re[0], ssm_b_im[0],
        ssm_c_re[0], ssm_c_im[0], ssm_d[0])
    y4 = _s5_scan(u4, kern, f_mat, e_mat, a_pack, d_row)
    wrh, wrl, br = _router_pack(moe_wg[0], moe_bg[0], moe_we[0], moe_be[0])
    x1, hn, eid, wts, cnt = _s5_out(y4, x, mod, ssm_w_glu[0].astype(BF16), norm2_g[0:1], wrh, wrl, br)
    x2 = _moe(x1, hn, eid, wts, cnt, mod, moe_w1, moe_w3, moe_w2, 0, fg, False)

    mod = mod_all[1]
    cw8 = jnp.zeros((8, d), F32).at[:conv_w.shape[1]].set(conv_w[0])
    wrh, wrl, br = _router_pack(moe_wg[1], moe_bg[1], moe_we[1], moe_be[1])
    x3, hn, eid, wts, cnt = _conv_mixer(x2, mod, norm1_g[1:2], conv_w_in[0].astype(BF16), cw8,
                                        conv_w_out[0].astype(BF16), norm2_g[1:2], wrh, wrl, br)
    return _moe(x3, hn, eid, wts, cnt, mod, moe_w1, moe_w3, moe_w2, 1, fg, True)
```

```python
import functools
import math

import jax
import jax.numpy as jnp
from jax import lax
from jax.experimental import pallas as pl
from jax.experimental.pallas import tpu as pltpu

F32 = jnp.float32
BF16 = jnp.bfloat16
HIGHEST = lax.Precision.HIGHEST

RMS_EPS = 1e-6
SSM_GROUP = 16
SSM_CHUNK = 128
N_GROUPS = 4
EXPERTS_PER_GROUP = 8
N_EXPERTS = N_GROUPS * EXPERTS_PER_GROUP
TOP_K = 2
LANES = 128
TOKEN_TILE = 512
S5_TILE = 1024
S5_K_BLOCKS = 2
MOE_BLOCK = TOKEN_TILE
CHUNK_ROWS = 8
LOCAL_ROWS = -(-(TOP_K * MOE_BLOCK + N_EXPERTS * (CHUNK_ROWS - 1)) // LANES) * LANES
EXPERT_TILE = 512
VMEM_LIMIT = 56 * 1024 * 1024
NEG_INF = -1e30


def _cparams(*sem):
    return pltpu.CompilerParams(dimension_semantics=sem, vmem_limit_bytes=VMEM_LIMIT)


def _adaln_kernel(c_ref, w_ref, b_ref, o_ref):
    c = c_ref[...]
    cond = c * jax.nn.sigmoid(c)
    o_ref[0] = jnp.dot(cond, w_ref[0], precision=HIGHEST, preferred_element_type=F32) + b_ref[0]


def _adaln(c8, ada_w, ada_b):
    depth, d, n = ada_w.shape
    tn = 1536
    return pl.pallas_call(
        _adaln_kernel,
        grid=(depth, n // tn),
        in_specs=[pl.BlockSpec((8, d), lambda i, j: (0, 0)),
                  pl.BlockSpec((1, d, tn), lambda i, j: (i, 0, j)),
                  pl.BlockSpec((1, 1, tn), lambda i, j: (i, 0, j))],
        out_specs=pl.BlockSpec((1, 8, tn), lambda i, j: (i, 0, j)),
        out_shape=jax.ShapeDtypeStruct((depth, 8, n), F32),
        compiler_params=_cparams("parallel", "parallel"),
        name="adaln",
    )(c8, ada_w, ada_b.reshape(depth, 1, n))


def _norm_mod(x, g, shift, scale):
    y = x * lax.rsqrt(jnp.mean(x * x, axis=-1, keepdims=True) + RMS_EPS)
    return (y * g) * (1.0 + scale) + shift


def _route(hn, hn_hi, wrh_ref, wrl_ref, br_ref, eid_ref, wts_ref, cnt_ref):
    hn_lo = (hn - hn_hi.astype(F32)).astype(BF16)
    wrh = wrh_ref[...]
    logits = (jnp.dot(hn_hi, wrh, preferred_element_type=F32)
              + jnp.dot(hn_lo, wrh, preferred_element_type=F32)
              + jnp.dot(hn_hi, wrl_ref[...], preferred_element_type=F32)) + br_ref[...]
    lane = lax.broadcasted_iota(jnp.int32, logits.shape, 1)
    is_grp = (lane >= N_EXPERTS) & (lane < N_EXPERTS + N_GROUPS)
    lg = jnp.where(is_grp, logits, NEG_INF)
    gmax = jnp.max(lg, axis=-1, keepdims=True)
    gsum = jnp.sum(jnp.where(is_grp, jnp.exp(lg - gmax), 0.0), axis=-1, keepdims=True)
    gp = 1.0 / gsum
    gi = jnp.min(jnp.where(lg == gmax, lane, 2 * LANES), axis=-1, keepdims=True) - N_EXPERTS
    in_grp = (lane < N_EXPERTS) & ((lane // EXPERTS_PER_GROUP) == gi)
    le = jnp.where(in_grp, logits, NEG_INF)
    v1 = jnp.max(le, axis=-1, keepdims=True)
    i1 = jnp.min(jnp.where(le == v1, lane, 2 * LANES), axis=-1, keepdims=True)
    le2 = jnp.where(lane == i1, NEG_INF, le)
    v2 = jnp.max(le2, axis=-1, keepdims=True)
    i2 = jnp.min(jnp.where(le2 == v2, lane, 2 * LANES), axis=-1, keepdims=True)
    e2 = jnp.exp(v2 - v1)
    den = 1.0 + e2
    col = lax.broadcasted_iota(jnp.int32, (hn.shape[0], TOP_K), 1)
    eid_ref[...] = jnp.where(col == 0, i1, i2)
    wts_ref[...] = jnp.where(col == 0, gp / den, gp * e2 / den)
    chosen = ((lane == i1) | (lane == i2)).astype(F32)
    cnt_ref[...] = jnp.broadcast_to(jnp.sum(chosen, axis=0, keepdims=True), cnt_ref.shape)


def _residual_and_route(x, out, mod, g2_ref, wrh_ref, wrl_ref, br_ref, x_out_ref, hn_ref, eid_ref, wts_ref, cnt_ref):
    x1 = x + (1.0 + mod[2:3]) * out
    x_out_ref[...] = x1
    hn = _norm_mod(x1, g2_ref[...], mod[3:4], mod[4:5])
    hn_hi = hn.astype(BF16)
    hn_ref[...] = hn_hi
    _route(hn, hn_hi, wrh_ref, wrl_ref, br_ref, eid_ref, wts_ref, cnt_ref)


def _s5_in_kernel(x_ref, mod_ref, g_ref, wt_ref, u_ref):
    mod = mod_ref[0]
    hn = _norm_mod(x_ref[0], g_ref[...], mod[0:1], mod[1:2])
    ut = lax.dot_general(wt_ref[...], hn.astype(BF16), (((1,), (1,)), ((), ())), preferred_element_type=F32)
    u_ref[0] = ut.reshape(ut.shape[0], S5_TILE // SSM_CHUNK, SSM_CHUNK)


def _s5_in(x, mod, g, wt_bf):
    b, l, d = x.shape
    h = wt_bf.shape[0]
    cpt = S5_TILE // SSM_CHUNK
    return pl.pallas_call(
        _s5_in_kernel,
        grid=(b, l // S5_TILE),
        in_specs=[pl.BlockSpec((1, S5_TILE, d), lambda i, j: (i, j, 0)),
                  pl.BlockSpec((1, 8, d), lambda i, j: (i, 0, 0)),
                  pl.BlockSpec((1, d), lambda i, j: (0, 0)),
                  pl.BlockSpec((h, d), lambda i, j: (0, 0))],
        out_specs=pl.BlockSpec((1, h, cpt, SSM_CHUNK), lambda i, j: (i, 0, j, 0)),
        out_shape=jax.ShapeDtypeStruct((b, h, l // SSM_CHUNK, SSM_CHUNK), F32),
        compiler_params=_cparams("parallel", "parallel"),
        name="s5_in",
    )(x, mod, g, wt_bf)


def _s5_scan_kernel(u_ref, k_ref, kn_ref, f_ref, e_ref, a_ref, d_ref, y_ref,
                    m0_ref, m1_ref, acc_ref, s_ref, sw_ref, sp_ref):
    bsz, grp, n_chunks, tc = u_ref.shape
    g = pl.program_id(0)
    srow = lax.broadcasted_iota(jnp.int32, (tc, tc), 0)
    tcol = lax.broadcasted_iota(jnp.int32, (tc, tc), 1)
    causal = tcol >= srow

    def build_rows(src_ref, dst_ref, i):
        blks = []
        for j in range(grp):
            row = src_ref[0, pl.ds(i * grp + j, 1), :]
            blk = pltpu.roll(jnp.broadcast_to(row, (tc, tc)), 0, 1, stride=1, stride_axis=0)
            blks.append(jnp.where(causal, blk, 0.0))
        dst_ref[pl.ds(pl.multiple_of(i * tc, tc), tc), :] = jnp.concatenate(blks, axis=1).astype(BF16)

    @pl.when(g == 0)
    def _():
        def first(i, carry):
            build_rows(k_ref, m0_ref, i)
            return carry

        lax.fori_loop(0, grp, first, 0)

    x = jnp.concatenate(
        [jnp.concatenate([u_ref[b, k] for k in range(grp)], axis=1) for b in range(bsz)], axis=0)
    xb = x.astype(BF16)
    s_loc = jnp.dot(xb, f_ref[0], preferred_element_type=F32)
    s_ref[...] = s_loc
    half = s_loc.shape[1] // 2
    sw_ref[...] = jnp.concatenate([s_loc[:, half:], s_loc[:, :half]], axis=1)
    a1 = a_ref[0, 0:1, :]
    a2 = a_ref[0, 1:2, :]
    s = [jnp.zeros((1, s_loc.shape[1]), F32) for _ in range(bsz)]
    sw = [jnp.zeros((1, s_loc.shape[1]), F32) for _ in range(bsz)]
    for c in range(n_chunks):
        for b in range(bsz):
            r = b * n_chunks + c
            sp_ref[r:r + 1, :] = s[b]
            s_new = a1 * s[b] + a2 * sw[b] + s_ref[r:r + 1, :]
            sw[b] = a1 * sw[b] - a2 * s[b] + sw_ref[r:r + 1, :]
            s[b] = s_new
    e_mat = jnp.concatenate([e_ref[0, j] for j in range(grp)], axis=1)
    acc_ref[...] = jnp.dot(sp_ref[...].astype(BF16), e_mat, preferred_element_type=F32) + d_ref[0] * x

    def run(m_cur_ref, m_next_ref):
        def step(c, carry):
            for h in range(S5_K_BLOCKS):
                build_rows(kn_ref, m_next_ref, c * S5_K_BLOCKS + h)
            xc = jnp.concatenate(
                [jnp.concatenate([u_ref[b, c * S5_K_BLOCKS + h] for h in range(S5_K_BLOCKS)], axis=1)
                 for b in range(bsz)], axis=0).astype(BF16)
            rows = pl.ds(pl.multiple_of(c * (S5_K_BLOCKS * tc), S5_K_BLOCKS * tc), S5_K_BLOCKS * tc)
            acc_ref[...] += jnp.dot(xc, m_cur_ref[rows, :], preferred_element_type=F32)
            return carry

        lax.fori_loop(0, grp // S5_K_BLOCKS, step, 0)

    @pl.when(g % 2 == 0)
    def _():
        run(m0_ref, m1_ref)

    @pl.when(g % 2 == 1)
    def _():
        run(m1_ref, m0_ref)
    for b in range(bsz):
        for j in range(grp):
            y_ref[b, j] = acc_ref[b * n_chunks:(b + 1) * n_chunks, j * tc:(j + 1) * tc]


def _s5_scan(u4, kern, f_bf, e4_bf, a_pack, d_row):
    b, h, nc, tc = u4.shape
    g = kern.shape[0]
    grp = h // g
    k = grp * tc
    p2 = f_bf.shape[2]
    rows = b * nc
    return pl.pallas_call(
        _s5_scan_kernel,
        grid=(g,),
        in_specs=[pl.BlockSpec((b, grp, nc, tc), lambda i: (0, i, 0, 0)),
                  pl.BlockSpec((1, grp * grp, tc), lambda i: (i, 0, 0)),
                  pl.BlockSpec((1, grp * grp, tc), lambda i: (jnp.minimum(i + 1, g - 1), 0, 0)),
                  pl.BlockSpec((1, k, p2), lambda i: (i, 0, 0)),
                  pl.BlockSpec((1, grp, p2, tc), lambda i: (i, 0, 0, 0)),
                  pl.BlockSpec((1, 8, p2), lambda i: (i, 0, 0)),
                  pl.BlockSpec((1, 1, k), lambda i: (i, 0, 0))],
        out_specs=pl.BlockSpec((b, grp, nc, tc), lambda i: (0, i, 0, 0)),
        out_shape=jax.ShapeDtypeStruct(u4.shape, F32),
        scratch_shapes=[pltpu.VMEM((k, k), BF16), pltpu.VMEM((k, k), BF16),
                        pltpu.VMEM((rows, k), F32), pltpu.VMEM((rows, p2), F32),
                        pltpu.VMEM((rows, p2), F32), pltpu.VMEM((rows, p2), F32)],
        compiler_params=_cparams("arbitrary"),
        name="s5_scan",
    )(u4, kern, kern, f_bf, e4_bf, a_pack, d_row)


def _s5_tables(lam_re, lam_im, log_dt, b_re, b_im, c_re, c_im, d_skip):
    g, p = lam_re.shape
    k = SSM_GROUP
    tc = SSM_CHUNK
    dt = jnp.exp(log_dt)[:, None]
    mag = jnp.exp(lam_re * dt)
    ab_re = mag * jnp.cos(lam_im * dt)
    ab_im = mag * jnp.sin(lam_im * dt)
    den = lam_re * lam_re + lam_im * lam_im
    cf_re = ((ab_re - 1) * lam_re + ab_im * lam_im) / den
    cf_im = (ab_im * lam_re - (ab_re - 1) * lam_im) / den
    bb_re = cf_re[..., None] * b_re - cf_im[..., None] * b_im
    bb_im = cf_re[..., None] * b_im + cf_im[..., None] * b_re
    lags = jnp.arange(tc + 1, dtype=F32)[:, None, None]
    pmag = jnp.exp(lags * (lam_re * dt)[None])
    pang = lags * (lam_im * dt)[None]
    p_re = pmag * jnp.cos(pang)
    p_im = pmag * jnp.sin(pang)
    cb_re = jnp.einsum('gjp,gpi->gpij', c_re, bb_re) - jnp.einsum('gjp,gpi->gpij', c_im, bb_im)
    cb_im = jnp.einsum('gjp,gpi->gpij', c_re, bb_im) + jnp.einsum('gjp,gpi->gpij', c_im, bb_re)
    kern = (jnp.einsum('lgp,gpij->gijl', p_re[:tc], cb_re, precision=HIGHEST)
            - jnp.einsum('lgp,gpij->gijl', p_im[:tc], cb_im, precision=HIGHEST))
    kern = kern.reshape(g, k * k, tc)
    lre = jnp.concatenate([lam_re * dt, lam_re * dt], axis=-1)
    lim = jnp.concatenate([lam_im * dt, lam_im * dt], axis=-1)
    sign = jnp.concatenate([-jnp.ones((p,), F32), jnp.ones((p,), F32)])
    pw_f = (tc - 1 - jnp.arange(tc, dtype=F32))[None, :, None]
    mag_f = jnp.exp(pw_f * lre[:, None, :])
    pr2 = mag_f * jnp.cos(pw_f * lim[:, None, :])
    pi2 = mag_f * jnp.sin(pw_f * lim[:, None, :]) * sign
    bb1 = jnp.concatenate([bb_re, bb_im], axis=1).transpose(0, 2, 1)
    bb2 = jnp.concatenate([bb_im, bb_re], axis=1).transpose(0, 2, 1)
    f_mat = (pr2[:, None] * bb1[:, :, None] + pi2[:, None] * bb2[:, :, None]).reshape(g, k * tc, 2 * p).astype(BF16)
    pw_e = (jnp.arange(tc, dtype=F32) + 1.0)[None, None, :]
    mag_e = jnp.exp(pw_e * lre[:, :, None])
    qr2 = mag_e * jnp.cos(pw_e * lim[:, :, None])
    qi2 = mag_e * jnp.sin(pw_e * lim[:, :, None])
    ca = jnp.concatenate([c_re, -c_im], axis=-1)
    cb = jnp.concatenate([-c_im, -c_re], axis=-1)
    e_mat = (ca[:, :, :, None] * qr2[:, None] + cb[:, :, :, None] * qi2[:, None]).astype(BF16)
    ar = p_re[tc]
    ai = p_im[tc]
    a1 = jnp.concatenate([ar, ar], axis=-1)
    a2 = jnp.concatenate([-ai, ai], axis=-1)
    a_pack = jnp.concatenate([a1[:, None], a2[:, None], jnp.zeros((g, 6, 2 * p), F32)], axis=1)
    d_row = jnp.repeat(d_skip.reshape(g, k), tc, axis=1).reshape(g, 1, k * tc)
    return kern, f_mat, e_mat, a_pack, d_row


def _gelu_tanh(x):
    return 0.5 * x * (1.0 + jnp.tanh(math.sqrt(2.0 / math.pi) * (x + 0.044715 * (x * x * x))))


def _s5_out_kernel(y_ref, x_ref, mod_ref, w_ref, g2_ref, wrh_ref, wrl_ref, br_ref,
                   x_out_ref, hn_ref, eid_ref, wts_ref, cnt_ref):
    mod = mod_ref[0]
    h = y_ref.shape[1]
    yt = _gelu_tanh(y_ref[0].reshape(h, S5_TILE))
    d = w_ref.shape[1] // 2
    for part in range(S5_TILE // MOE_BLOCK):
        rows = slice(part * MOE_BLOCK, (part + 1) * MOE_BLOCK)
        o = lax.dot_general(yt[:, rows].astype(BF16), w_ref[...], (((0,), (0,)), ((), ())),
                            preferred_element_type=F32)
        out = o[:, :d] * jax.nn.sigmoid(o[:, d:])
        _residual_and_route(x_ref[0, rows, :], out, mod, g2_ref, wrh_ref, wrl_ref, br_ref,
                            x_out_ref.at[0, rows, :], hn_ref.at[rows, :], eid_ref.at[rows, :],
                            wts_ref.at[rows, :], cnt_ref.at[part])


def _mixer_out_specs(b, l, d, tile):
    nt = l // tile
    nblk = tile // MOE_BLOCK
    specs = [pl.BlockSpec((1, tile, d), lambda i, j: (i, j, 0)),
             pl.BlockSpec((tile, d), lambda i, j: (i * nt + j, 0)),
             pl.BlockSpec((tile, TOP_K), lambda i, j: (i * nt + j, 0)),
             pl.BlockSpec((tile, TOP_K), lambda i, j: (i * nt + j, 0)),
             pl.BlockSpec((nblk, 8, LANES), lambda i, j: (i * nt + j, 0, 0))]
    shapes = [jax.ShapeDtypeStruct((b, l, d), F32),
              jax.ShapeDtypeStruct((b * l, d), BF16),
              jax.ShapeDtypeStruct((b * l, TOP_K), jnp.int32),
              jax.ShapeDtypeStruct((b * l, TOP_K), F32),
              jax.ShapeDtypeStruct((b * l // MOE_BLOCK, 8, LANES), F32)]
    return specs, shapes


def _s5_out(y4, x, mod, w_bf, g2, wrh, wrl, br):
    b, l, d = x.shape
    h = y4.shape[1]
    specs, shapes = _mixer_out_specs(b, l, d, S5_TILE)
    return pl.pallas_call(
        _s5_out_kernel,
        grid=(b, l // S5_TILE),
        in_specs=[pl.BlockSpec((1, h, S5_TILE // SSM_CHUNK, SSM_CHUNK), lambda i, j: (i, 0, j, 0)),
                  pl.BlockSpec((1, S5_TILE, d), lambda i, j: (i, j, 0)),
                  pl.BlockSpec((1, 8, d), lambda i, j: (i, 0, 0)),
                  pl.BlockSpec(w_bf.shape, lambda i, j: (0, 0), pipeline_mode=pl.Buffered(1)),
                  pl.BlockSpec((1, d), lambda i, j: (0, 0)),
                  pl.BlockSpec((d, LANES), lambda i, j: (0, 0)),
                  pl.BlockSpec((d, LANES), lambda i, j: (0, 0)),
                  pl.BlockSpec((1, LANES), lambda i, j: (0, 0))],
        out_specs=specs,
        out_shape=shapes,
        compiler_params=_cparams("parallel", "parallel"),
        name="s5_out",
    )(y4, x, mod, w_bf, g2, wrh, wrl, br)


def _conv_mixer_kernel(x_ref, mod_ref, g1_ref, win_ref, cw_ref, wout_ref, g2_ref, wrh_ref, wrl_ref, br_ref,
                       x_out_ref, hn_ref, eid_ref, wts_ref, cnt_ref, carry_ref):
    @pl.when(pl.program_id(1) == 0)
    def _():
        carry_ref[...] = jnp.zeros_like(carry_ref)

    mod = mod_ref[0]
    x = x_ref[0]
    hn = _norm_mod(x, g1_ref[...], mod[0:1], mod[1:2])
    p = jnp.dot(hn.astype(BF16), win_ref[...], preferred_element_type=F32)
    d = p.shape[1] // 3
    b_g = p[:, :d]
    u = p[:, d:2 * d] * p[:, 2 * d:]
    ext = jnp.concatenate([carry_ref[...], u], axis=0)
    t = u.shape[0]
    cw = cw_ref[...]
    z = cw[0:1] * ext[6:6 + t] + cw[1:2] * ext[7:7 + t] + cw[2:3] * u
    carry_ref[...] = u[t - 8:]
    out = jnp.dot((b_g * z).astype(BF16), wout_ref[...], preferred_element_type=F32)
    _residual_and_route(x, out, mod, g2_ref, wrh_ref, wrl_ref, br_ref, x_out_ref.at[0], hn_ref, eid_ref,
                        wts_ref, cnt_ref.at[0])


def _conv_mixer(x, mod, g1, win_bf, cw8, wout_bf, g2, wrh, wrl, br):
    b, l, d = x.shape
    specs, shapes = _mixer_out_specs(b, l, d, TOKEN_TILE)
    return pl.pallas_call(
        _conv_mixer_kernel,
        grid=(b, l // TOKEN_TILE),
        in_specs=[pl.BlockSpec((1, TOKEN_TILE, d), lambda i, j: (i, j, 0)),
                  pl.BlockSpec((1, 8, d), lambda i, j: (i, 0, 0)),
                  pl.BlockSpec((1, d), lambda i, j: (0, 0)),
                  pl.BlockSpec(win_bf.shape, lambda i, j: (0, 0)),
                  pl.BlockSpec((8, d), lambda i, j: (0, 0)),
                  pl.BlockSpec(wout_bf.shape, lambda i, j: (0, 0)),
                  pl.BlockSpec((1, d), lambda i, j: (0, 0)),
                  pl.BlockSpec((d, LANES), lambda i, j: (0, 0)),
                  pl.BlockSpec((d, LANES), lambda i, j: (0, 0)),
                  pl.BlockSpec((1, LANES), lambda i, j: (0, 0))],
        out_specs=specs,
        out_shape=shapes,
        scratch_shapes=[pltpu.VMEM((8, d), F32)],
        compiler_params=_cparams("parallel", "arbitrary"),
        name="conv_mixer",
    )(x, mod, g1, win_bf, cw8, wout_bf, g2, wrh, wrl, br)


_TN = (((0,), (0,)), ((), ()))


MAX_CHUNKS = LOCAL_ROWS // CHUNK_ROWS


def _rows_copy(src, s_row, dst, d_row, rows, sem):
    return pltpu.make_async_copy(src.at[pl.ds(s_row, rows), :], dst.at[pl.ds(d_row, rows), :], sem)


def _start_block_chunks(blk, count, crow_ref, make_copy):
    def body(q, carry):
        make_copy(pl.multiple_of(q * CHUNK_ROWS, CHUNK_ROWS),
                  pl.multiple_of(crow_ref[blk * MAX_CHUNKS + q], CHUNK_ROWS)).start()
        return carry

    lax.fori_loop(0, count, body, 0)


def _wait_chunks(count, max_count, make_wait):
    for k in range(max_count.bit_length()):
        @pl.when(((count >> k) & 1) == 1)
        def _():
            make_wait(CHUNK_ROWS << k).wait()


def _slot_masks(lp0, lp1):
    slot = lax.broadcasted_iota(jnp.int32, (lp0.shape[0], LOCAL_ROWS), 1)
    return slot == lp0, slot == lp1


_HI16 = -65536


def _pack_bf16_pairs(x):
    h = x.shape[1] // 2
    lo = lax.shift_right_logical(pltpu.bitcast(x[:, :h], jnp.int32), 16)
    hi = pltpu.bitcast(x[:, h:], jnp.int32) & _HI16
    return hi | lo


def _unpack_bf16_pairs(w):
    lo = pltpu.bitcast(w << 16, F32).astype(BF16)
    hi = pltpu.bitcast(w & _HI16, F32).astype(BF16)
    return lo, hi


def _moe_dispatch_kernel(crow_ref, tot_ref, padn_ref, pads_ref, nv_ref,
                         eid_ref, wts_ref, hn_ref, lofff_ref, xs_ref, lpos_ref, loc_ref, zero_ref, sems):
    blk = pl.program_id(0)
    d = hn_ref.shape[1]
    eid = eid_ref[...]
    t = eid.shape[0]
    lane = lax.broadcasted_iota(jnp.int32, (t, LANES), 1)
    hit0 = lane == eid[:, 0:1]
    hit1 = lane == eid[:, 1:2]
    onehot = (hit0 | hit1).astype(BF16)
    r = lax.broadcasted_iota(jnp.int32, (t, t), 0)
    c = lax.broadcasted_iota(jnp.int32, (t, t), 1)
    tri = (c < r).astype(BF16)
    before = jnp.dot(tri, onehot, preferred_element_type=F32) + lofff_ref[0]
    lp0 = jnp.sum(jnp.where(hit0, before, 0.0), axis=-1, keepdims=True).astype(jnp.int32)
    lp1 = jnp.sum(jnp.where(hit1, before, 0.0), axis=-1, keepdims=True).astype(jnp.int32)
    col = lax.broadcasted_iota(jnp.int32, (t, TOP_K), 1)
    lpos_ref[...] = jnp.where(col == 0, lp0, lp1)
    m0, m1 = _slot_masks(lp0, lp1)
    wts = wts_ref[...]
    wrow = jnp.sum(jnp.where(m0, wts[:, 0:1], 0.0) + jnp.where(m1, wts[:, 1:2], 0.0), axis=0, keepdims=True)

    buf = blk % 2
    loc = loc_ref.at[buf]

    def drain(which, count):
        _wait_chunks(count, MAX_CHUNKS,
                     lambda rows: _rows_copy(loc_ref.at[which], 0, xs_ref, 0, rows, sems.at[which]))

    @pl.when(blk >= 2)
    def _():
        drain(buf, tot_ref[blk - 2])

    sorted_rows = lax.dot_general((m0 | m1).astype(BF16), hn_ref[...], _TN, preferred_element_type=F32)
    loc[:, :d // 2] = _pack_bf16_pairs(sorted_rows)
    loc[:, d // 2:] = pltpu.bitcast(jnp.broadcast_to(wrow, (LANES, LOCAL_ROWS)).T, jnp.int32)

    _start_block_chunks(blk, tot_ref[blk], crow_ref,
                        lambda lo, go: _rows_copy(loc, lo, xs_ref, go, CHUNK_ROWS, sems.at[buf]))

    @pl.when(blk == pl.num_programs(0) - 1)
    def _():
        drain(1 - buf, tot_ref[blk - 1])
        drain(buf, tot_ref[blk])
        sem = sems.at[0]
        zero_ref[...] = jnp.zeros_like(zero_ref)
        pad_bits = (EXPERT_TILE // CHUNK_ROWS - 1).bit_length()

        def pad_copies(e, carry, *, wait):
            padn = padn_ref[e]
            for k in range(pad_bits):
                @pl.when(((padn >> k) & 1) == 1)
                def _():
                    done = (padn & ((1 << k) - 1)) * CHUNK_ROWS
                    cp = _rows_copy(zero_ref, 0, xs_ref, pl.multiple_of(pads_ref[e] + done, CHUNK_ROWS),
                                    CHUNK_ROWS << k, sem)
                    if wait:
                        cp.wait()
                    else:
                        cp.start()
            return carry

        lax.fori_loop(0, N_EXPERTS, functools.partial(pad_copies, wait=False), 0)
        lax.fori_loop(0, N_EXPERTS, functools.partial(pad_copies, wait=True), 0)

        def tile_copy(i):
            return pltpu.make_async_copy(zero_ref, xs_ref.at[pl.ds(pl.multiple_of(i * EXPERT_TILE, EXPERT_TILE),
                                                                   EXPERT_TILE), :], sem)

        n_tiles = xs_ref.shape[0] // EXPERT_TILE

        def start_tile(i, c):
            tile_copy(i).start()
            return c

        def wait_tile(i, c):
            tile_copy(i).wait()
            return c

        lax.fori_loop(nv_ref[0], n_tiles, start_tile, 0)
        lax.fori_loop(nv_ref[0], n_tiles, wait_tile, 0)


def _moe_dispatch(tables, eid, wts, hn, loff_f, n_rows):
    t, d = hn.shape
    width = d // 2 + LANES
    grid_spec = pltpu.PrefetchScalarGridSpec(
        num_scalar_prefetch=5,
        grid=(t // MOE_BLOCK,),
        in_specs=[pl.BlockSpec((MOE_BLOCK, TOP_K), lambda i, *_: (i, 0)),
                  pl.BlockSpec((MOE_BLOCK, TOP_K), lambda i, *_: (i, 0)),
                  pl.BlockSpec((MOE_BLOCK, d), lambda i, *_: (i, 0)),
                  pl.BlockSpec((1, 1, LANES), lambda i, *_: (i, 0, 0))],
        out_specs=[pl.BlockSpec(memory_space=pl.ANY),
                   pl.BlockSpec((MOE_BLOCK, TOP_K), lambda i, *_: (i, 0))],
        scratch_shapes=[pltpu.VMEM((2, LOCAL_ROWS, width), jnp.int32), pltpu.VMEM((EXPERT_TILE, width), jnp.int32),
                        pltpu.SemaphoreType.DMA((2,))],
    )
    return pl.pallas_call(
        _moe_dispatch_kernel,
        grid_spec=grid_spec,
        out_shape=[jax.ShapeDtypeStruct((n_rows, width), jnp.int32),
                   jax.ShapeDtypeStruct((t, TOP_K), jnp.int32)],
        compiler_params=_cparams("arbitrary"),
        name="moe_dispatch",
    )(*tables, eid, wts, hn, loff_f)


def _moe_experts_kernel(te_ref, tv_ref, tf_ref, ts_ref, xs_ref, w1_ref, w3_ref, w2_ref, o_ref,
                        w1b_ref, w3b_ref, w2b_ref):
    del te_ref, ts_ref
    i = pl.program_id(0)

    @pl.when(tf_ref[i] != 0)
    def _():
        def cast_rows(r, carry):
            rows = pl.ds(pl.multiple_of(r * LANES, LANES), LANES)
            w1b_ref[rows, :] = w1_ref[0, 0, rows, :].astype(BF16)
            w3b_ref[rows, :] = w3_ref[0, 0, rows, :].astype(BF16)

            @pl.when(r < w2b_ref.shape[0] // LANES)
            def _():
                w2b_ref[rows, :] = w2_ref[0, 0, rows, :].astype(BF16)

            return carry

        lax.fori_loop(0, w1b_ref.shape[0] // LANES, cast_rows, 0)

    @pl.when(tv_ref[i] != 0)
    def _():
        half = o_ref.shape[1]
        x_lo, x_hi = _unpack_bf16_pairs(xs_ref[:, :half])
        w = pltpu.bitcast(xs_ref[:, half:half + 1], F32)
        a = (jnp.dot(x_lo, w1b_ref[:half, :], preferred_element_type=F32)
             + jnp.dot(x_hi, w1b_ref[half:, :], preferred_element_type=F32))
        b = (jnp.dot(x_lo, w3b_ref[:half, :], preferred_element_type=F32)
             + jnp.dot(x_hi, w3b_ref[half:, :], preferred_element_type=F32))
        h = (a * jax.nn.sigmoid(a) * b).astype(BF16)
        o = jnp.dot(h, w2b_ref[...], preferred_element_type=F32) * w
        o_ref[...] = _pack_bf16_pairs(o.astype(BF16).astype(F32))

    @pl.when(tv_ref[i] == 0)
    def _():
        o_ref[...] = jnp.zeros_like(o_ref)


def _moe_experts(tile_expert, tile_valid, tile_first, tile_src, xs, w1, w3, w2, layer):
    r, width = xs.shape
    d = (width - LANES) * 2
    de = w1.shape[3]
    grid_spec = pltpu.PrefetchScalarGridSpec(
        num_scalar_prefetch=4,
        grid=(r // EXPERT_TILE,),
        in_specs=[pl.BlockSpec((EXPERT_TILE, width), lambda i, te, tv, tf, ts: (ts[i], 0)),
                  pl.BlockSpec((1, 1, d, de), lambda i, te, tv, tf, ts: (layer, te[i], 0, 0)),
                  pl.BlockSpec((1, 1, d, de), lambda i, te, tv, tf, ts: (layer, te[i], 0, 0)),
                  pl.BlockSpec((1, 1, de, d), lambda i, te, tv, tf, ts: (layer, te[i], 0, 0))],
        out_specs=pl.BlockSpec((EXPERT_TILE, d // 2), lambda i, te, tv, tf, ts: (i, 0)),
        scratch_shapes=[pltpu.VMEM((d, de), BF16), pltpu.VMEM((d, de), BF16), pltpu.VMEM((de, d), BF16)],
    )
    return pl.pallas_call(
        _moe_experts_kernel,
        grid_spec=grid_spec,
        out_shape=jax.ShapeDtypeStruct((r, d // 2), jnp.int32),
        compiler_params=_cparams("arbitrary"),
        name="moe_experts",
    )(tile_expert, tile_valid, tile_first, tile_src, xs, w1, w3, w2)


def _moe_combine_kernel(crow_ref, tot_ref, lpos_ref, x_ref, mod_ref, fg_ref, o_hbm_ref,
                        out_ref, loc_ref, sems, *, final_norm):
    blk = pl.program_id(0) * pl.num_programs(1) + pl.program_id(1)
    n_blk = pl.num_programs(0) * pl.num_programs(1)
    t = x_ref.shape[1]
    buf = blk % 2

    def fetch(b, which):
        loc = loc_ref.at[which]
        loc[TOP_K * t:, :] = jnp.zeros((LOCAL_ROWS - TOP_K * t, loc_ref.shape[2]), jnp.int32)
        _start_block_chunks(b, tot_ref[b], crow_ref,
                            lambda lo, go: _rows_copy(o_hbm_ref, go, loc, lo, CHUNK_ROWS, sems.at[which]))

    @pl.when(blk == 0)
    def _():
        fetch(blk, buf)

    @pl.when(blk + 1 < n_blk)
    def _():
        fetch(blk + 1, 1 - buf)

    _wait_chunks(tot_ref[blk], MAX_CHUNKS,
                 lambda rows: _rows_copy(o_hbm_ref, 0, loc_ref.at[buf], 0, rows, sems.at[buf]))
    lp = lpos_ref[...]
    m0, m1 = _slot_masks(lp[:, 0:1], lp[:, 1:2])
    pt = (m0 | m1).astype(BF16)
    o_lo, o_hi = _unpack_bf16_pairs(loc_ref[buf])
    y = jnp.concatenate([jnp.dot(pt, o_lo, preferred_element_type=F32),
                         jnp.dot(pt, o_hi, preferred_element_type=F32)], axis=1)
    x2 = x_ref[0] + (1.0 + mod_ref[0, 5:6]) * y
    if final_norm:
        x2 = (x2 * lax.rsqrt(jnp.mean(x2 * x2, axis=-1, keepdims=True) + RMS_EPS)) * fg_ref[...]
    out_ref[0] = x2


def _moe_combine(tables, lpos, x, mod, final_g, o_sorted, final_norm):
    b, l, d = x.shape
    nt = l // MOE_BLOCK
    grid_spec = pltpu.PrefetchScalarGridSpec(
        num_scalar_prefetch=2,
        grid=(b, nt),
        in_specs=[pl.BlockSpec((MOE_BLOCK, TOP_K), lambda i, j, *_: (i * nt + j, 0)),
                  pl.BlockSpec((1, MOE_BLOCK, d), lambda i, j, *_: (i, j, 0)),
                  pl.BlockSpec((1, 8, d), lambda i, j, *_: (i, 0, 0)),
                  pl.BlockSpec((1, d), lambda i, j, *_: (0, 0)),
                  pl.BlockSpec(memory_space=pl.ANY)],
        out_specs=pl.BlockSpec((1, MOE_BLOCK, d), lambda i, j, *_: (i, j, 0)),
        scratch_shapes=[pltpu.VMEM((2, LOCAL_ROWS, d // 2), jnp.int32), pltpu.SemaphoreType.DMA((2,))],
    )
    return pl.pallas_call(
        functools.partial(_moe_combine_kernel, final_norm=final_norm),
        grid_spec=grid_spec,
        out_shape=jax.ShapeDtypeStruct((b, l, d), F32),
        compiler_params=_cparams("arbitrary", "arbitrary"),
        name="moe_combine",
    )(*tables[:2], lpos, x, mod, final_g, o_sorted)


def _moe(x, hn, eid, wts, cnt, mod, w1, w3, w2, layer, final_g, final_norm):
    t, d = hn.shape
    nblk = t // MOE_BLOCK
    i32 = jnp.int32
    n = cnt[:, 0, :N_EXPERTS].astype(i32)
    run = (n + CHUNK_ROWS - 1) // CHUNK_ROWS * CHUNK_ROWS
    loff = jnp.cumsum(run, axis=1) - run
    rows_e = jnp.sum(run, axis=0)
    tiles_e = (rows_e + EXPERT_TILE - 1) // EXPERT_TILE
    tile_end = jnp.cumsum(tiles_e)
    base = (tile_end - tiles_e) * EXPERT_TILE
    goff = base[None, :] + jnp.cumsum(run, axis=0) - run
    nch = run // CHUNK_ROWS
    tot = jnp.sum(nch, axis=1)
    ch_end = jnp.cumsum(nch, axis=1)
    q = jnp.arange(MAX_CHUNKS, dtype=i32)
    in_run = ((q[None, :, None] >= (ch_end - nch)[:, None, :]) & (q[None, :, None] < ch_end[:, None, :])).astype(i32)
    crow = jnp.sum(in_run * (goff[:, None, :] + (q[None, :, None] - (ch_end - nch)[:, None, :]) * CHUNK_ROWS), axis=-1)
    padn = (tiles_e * EXPERT_TILE - rows_e) // CHUNK_ROWS
    pads = base + rows_e
    tables = (crow.reshape(-1), tot, padn, pads, tile_end[-1:])
    loff_f = jnp.zeros((nblk, 1, LANES), F32).at[:, 0, :N_EXPERTS].set(loff.astype(F32))

    max_rows = t * TOP_K + nblk * N_EXPERTS * (CHUNK_ROWS - 1) + N_EXPERTS * (EXPERT_TILE - 1)
    max_tiles = -(-max_rows // EXPERT_TILE)
    tile_ids = jnp.arange(max_tiles, dtype=i32)
    n_valid = tile_end[-1]
    tile_src = jnp.minimum(tile_ids, n_valid - 1)
    tile_expert = jnp.sum((tile_src[:, None] >= tile_end[None, :]).astype(i32), axis=1)
    tile_valid = (tile_ids < n_valid).astype(i32)
    tile_first = jnp.concatenate([jnp.ones((1,), i32), (tile_expert[1:] != tile_expert[:-1]).astype(i32)])

    xs, lpos = _moe_dispatch(tables, eid, wts, hn, loff_f, max_tiles * EXPERT_TILE)
    o_sorted = _moe_experts(tile_expert, tile_valid, tile_first, tile_src, xs, w1, w3, w2, layer)
    return _moe_combine(tables, lpos, x, mod, final_g, o_sorted, final_norm)


def _router_pack(wg, bg, we, be):
    d = wg.shape[0]
    wr = jnp.zeros((d, LANES), F32).at[:, :N_EXPERTS].set(we).at[:, N_EXPERTS:N_EXPERTS + N_GROUPS].set(wg)
    br = jnp.zeros((1, LANES), F32).at[0, :N_EXPERTS].set(be).at[0, N_EXPERTS:N_EXPERTS + N_GROUPS].set(bg)
    wr_hi = wr.astype(BF16)
    wr_lo = (wr - wr_hi.astype(F32)).astype(BF16)
    return wr_hi, wr_lo, br


def kernel(x, c, ada_w, ada_b, norm1_g, norm2_g, ssm_w_in, ssm_lam_re, ssm_lam_im, ssm_log_dt, ssm_b_re, ssm_b_im, ssm_c_re, ssm_c_im, ssm_d, ssm_w_glu, conv_w_in, conv_w, conv_w_out, moe_wg, moe_bg, moe_we, moe_be, moe_w1, moe_w3, moe_w2, final_g):
    b, l, d = x.shape
    depth = ada_w.shape[0]
    c8 = jnp.zeros((8, d), F32).at[:b].set(c)
    mod_all = _adaln(c8, ada_w, ada_b)[:, :b].reshape(depth, b, 6, d)
    mod_all = jnp.concatenate([mod_all, jnp.zeros((depth, b, 2, d), F32)], axis=2)
    fg = final_g.reshape(1, d)

    mod = mod_all[0]
    u4 = _s5_in(x, mod, norm1_g[0:1], ssm_w_in[0].T.astype(BF16))
    kern, f_mat, e_mat, a_pack, d_row = _s5_tables(
        ssm_lam_re[0], ssm_lam_im[0], ssm_log_dt[0], ssm_b_re[0], ssm_b_im[0],
        ssm_c_re[0], ssm_c_im[0], ssm_d[0])
    y4 = _s5_scan(u4, kern, f_mat, e_mat, a_pack, d_row)
    wrh, wrl, br = _router_pack(moe_wg[0], moe_bg[0], moe_we[0], moe_be[0])
    x1, hn, eid, wts, cnt = _s5_out(y4, x, mod, ssm_w_glu[0].astype(BF16), norm2_g[0:1], wrh, wrl, br)
    x2 = _moe(x1, hn, eid, wts, cnt, mod, moe_w1, moe_w3, moe_w2, 0, fg, False)

    mod = mod_all[1]
    cw8 = jnp.zeros((8, d), F32).at[:conv_w.shape[1]].set(conv_w[0])
    wrh, wrl, br = _router_pack(moe_wg[1], moe_bg[1], moe_we[1], moe_be[1])
    x3, hn, eid, wts, cnt = _conv_mixer(x2, mod, norm1_g[1:2], conv_w_in[0].astype(BF16), cw8,
                                        conv_w_out[0].astype(BF16), norm2_g[1:2], wrh, wrl, br)
    return _moe(x3, hn, eid, wts, cnt, mod, moe_w1, moe_w3, moe_w2, 1, fg, True)
```

```python
import functools
import math

import jax
import jax.numpy as jnp
from jax import lax
from jax.experimental import pallas as pl
from jax.experimental.pallas import tpu as pltpu

F32 = jnp.float32
BF16 = jnp.bfloat16
HIGHEST = lax.Precision.HIGHEST

RMS_EPS = 1e-6
SSM_GROUP = 16
SSM_CHUNK = 128
N_GROUPS = 4
EXPERTS_PER_GROUP = 8
N_EXPERTS = N_GROUPS * EXPERTS_PER_GROUP
TOP_K = 2
LANES = 128
MXU_WIDTH = 256
TOKEN_TILE = 512
S5_TILE = 1024
S5_K_BLOCKS = 2
SUB_ROWS = 256
MOE_BLOCK = TOKEN_TILE
CHUNK_ROWS = 8
LOCAL_ROWS = -(-(TOP_K * MOE_BLOCK + N_EXPERTS * (CHUNK_ROWS - 1)) // LANES) * LANES
EXPERT_TILE = 512
VMEM_LIMIT = 56 * 1024 * 1024
NEG_INF = -1e30


def _cparams(*sem):
    return pltpu.CompilerParams(dimension_semantics=sem, vmem_limit_bytes=VMEM_LIMIT)


def _adaln_kernel(c_ref, w_ref, b_ref, o_ref):
    c = c_ref[...]
    cond = c * jax.nn.sigmoid(c)
    o_ref[0] = jnp.dot(cond, w_ref[0], precision=HIGHEST, preferred_element_type=F32) + b_ref[0]


def _adaln(c8, ada_w, ada_b):
    depth, d, n = ada_w.shape
    tn = 1536
    return pl.pallas_call(
        _adaln_kernel,
        grid=(depth, n // tn),
        in_specs=[pl.BlockSpec((8, d), lambda i, j: (0, 0)),
                  pl.BlockSpec((1, d, tn), lambda i, j: (i, 0, j)),
                  pl.BlockSpec((1, 1, tn), lambda i, j: (i, 0, j))],
        out_specs=pl.BlockSpec((1, 8, tn), lambda i, j: (i, 0, j)),
        out_shape=jax.ShapeDtypeStruct((depth, 8, n), F32),
        compiler_params=_cparams("parallel", "parallel"),
        name="adaln",
    )(c8, ada_w, ada_b.reshape(depth, 1, n))


def _norm_mod(x, g, shift, scale):
    y = x * lax.rsqrt(jnp.mean(x * x, axis=-1, keepdims=True) + RMS_EPS)
    return (y * g) * (1.0 + scale) + shift


def _route(hn, hn_hi, wr_ref, br_ref, eid_ref, wts_ref):
    hn_lo = (hn - hn_hi.astype(F32)).astype(BF16)
    both = jnp.dot(hn_hi, wr_ref[...], preferred_element_type=F32)
    logits = (both[:, :LANES] + both[:, LANES:]
              + jnp.dot(hn_lo, wr_ref[:, :LANES], preferred_element_type=F32)) + br_ref[...]
    lane = lax.broadcasted_iota(jnp.int32, logits.shape, 1)
    is_grp = (lane >= N_EXPERTS) & (lane < N_EXPERTS + N_GROUPS)
    lg = jnp.where(is_grp, logits, NEG_INF)
    gmax = jnp.max(lg, axis=-1, keepdims=True)
    gsum = jnp.sum(jnp.where(is_grp, jnp.exp(lg - gmax), 0.0), axis=-1, keepdims=True)
    gp = 1.0 / gsum
    gi = jnp.min(jnp.where(lg == gmax, lane, 2 * LANES), axis=-1, keepdims=True) - N_EXPERTS
    in_grp = (lane < N_EXPERTS) & ((lane // EXPERTS_PER_GROUP) == gi)
    le = jnp.where(in_grp, logits, NEG_INF)
    v1 = jnp.max(le, axis=-1, keepdims=True)
    i1 = jnp.min(jnp.where(le == v1, lane, 2 * LANES), axis=-1, keepdims=True)
    le2 = jnp.where(lane == i1, NEG_INF, le)
    v2 = jnp.max(le2, axis=-1, keepdims=True)
    i2 = jnp.min(jnp.where(le2 == v2, lane, 2 * LANES), axis=-1, keepdims=True)
    e2 = jnp.exp(v2 - v1)
    den = 1.0 + e2
    col = lax.broadcasted_iota(jnp.int32, (hn.shape[0], TOP_K), 1)
    eid_ref[...] = jnp.where(col == 0, i1, i2)
    wts_ref[...] = jnp.where(col == 0, gp / den, gp * e2 / den)
    chosen = ((lane == i1) | (lane == i2)).astype(F32)
    return jnp.sum(chosen, axis=0, keepdims=True)


def _residual_and_route(x, out, mod, g2_ref, wr_ref, br_ref, x_out_ref, hn_ref, eid_ref, wts_ref):
    x1 = x + (1.0 + mod[2:3]) * out
    x_out_ref[...] = x1
    hn = _norm_mod(x1, g2_ref[...], mod[3:4], mod[4:5])
    hn_hi = hn.astype(BF16)
    hn_ref[...] = hn_hi
    return _route(hn, hn_hi, wr_ref, br_ref, eid_ref, wts_ref)


def _s5_in_kernel(x_ref, mod_ref, g_ref, wt_ref, u_ref):
    mod = mod_ref[0]
    hn = _norm_mod(x_ref[0], g_ref[...], mod[0:1], mod[1:2])
    ut = lax.dot_general(wt_ref[...], hn.astype(BF16), (((1,), (1,)), ((), ())), preferred_element_type=F32)
    u_ref[0] = ut.reshape(ut.shape[0], S5_TILE // SSM_CHUNK, SSM_CHUNK)


def _s5_in(x, mod, g, wt_bf):
    b, l, d = x.shape
    h = wt_bf.shape[0]
    cpt = S5_TILE // SSM_CHUNK
    return pl.pallas_call(
        _s5_in_kernel,
        grid=(b, l // S5_TILE),
        in_specs=[pl.BlockSpec((1, S5_TILE, d), lambda i, j: (i, j, 0)),
                  pl.BlockSpec((1, 8, d), lambda i, j: (i, 0, 0)),
                  pl.BlockSpec((1, d), lambda i, j: (0, 0)),
                  pl.BlockSpec((h, d), lambda i, j: (0, 0))],
        out_specs=pl.BlockSpec((1, h, cpt, SSM_CHUNK), lambda i, j: (i, 0, j, 0)),
        out_shape=jax.ShapeDtypeStruct((b, h, l // SSM_CHUNK, SSM_CHUNK), F32),
        compiler_params=_cparams("parallel", "parallel"),
        name="s5_in",
    )(x, mod, g, wt_bf)


def _s5_scan_kernel(u_ref, k_ref, kn_ref, f_ref, e_ref, a_ref, d_ref, y_ref,
                    m0_ref, m1_ref, acc_ref, s_ref, sw_ref, sp_ref):
    bsz, grp, n_chunks, tc = u_ref.shape
    g = pl.program_id(0)
    srow = lax.broadcasted_iota(jnp.int32, (tc, tc), 0)
    tcol = lax.broadcasted_iota(jnp.int32, (tc, tc), 1)
    causal = tcol >= srow

    def build_rows(src_ref, dst_ref, i):
        blks = []
        for j in range(grp):
            row = src_ref[0, pl.ds(i * grp + j, 1), :]
            blk = pltpu.roll(jnp.broadcast_to(row, (tc, tc)), 0, 1, stride=1, stride_axis=0)
            blks.append(jnp.where(causal, blk, 0.0))
        dst_ref[pl.ds(pl.multiple_of(i * tc, tc), tc), :] = jnp.concatenate(blks, axis=1).astype(BF16)

    @pl.when(g == 0)
    def _():
        def first(i, carry):
            build_rows(k_ref, m0_ref, i)
            return carry

        lax.fori_loop(0, grp, first, 0)

    x = jnp.concatenate(
        [jnp.concatenate([u_ref[b, k] for k in range(grp)], axis=1) for b in range(bsz)], axis=0)
    xb = x.astype(BF16)
    s_loc = jnp.dot(xb, f_ref[0], preferred_element_type=F32)
    s_ref[...] = s_loc
    half = s_loc.shape[1] // 2
    sw_ref[...] = jnp.concatenate([s_loc[:, half:], s_loc[:, :half]], axis=1)
    a1 = a_ref[0, 0:1, :]
    a2 = a_ref[0, 1:2, :]
    s = [jnp.zeros((1, s_loc.shape[1]), F32) for _ in range(bsz)]
    sw = [jnp.zeros((1, s_loc.shape[1]), F32) for _ in range(bsz)]
    for c in range(n_chunks):
        for b in range(bsz):
            r = b * n_chunks + c
            sp_ref[r:r + 1, :] = s[b]
            s_new = a1 * s[b] + a2 * sw[b] + s_ref[r:r + 1, :]
            sw[b] = a1 * sw[b] - a2 * s[b] + sw_ref[r:r + 1, :]
            s[b] = s_new
    e_mat = jnp.concatenate([e_ref[0, j] for j in range(grp)], axis=1)
    acc_ref[...] = jnp.dot(sp_ref[...].astype(BF16), e_mat, preferred_element_type=F32) + d_ref[0] * x

    def run(m_cur_ref, m_next_ref):
        def step(c, carry):
            for h in range(S5_K_BLOCKS):
                build_rows(kn_ref, m_next_ref, c * S5_K_BLOCKS + h)
            xc = jnp.concatenate(
                [jnp.concatenate([u_ref[b, c * S5_K_BLOCKS + h] for h in range(S5_K_BLOCKS)], axis=1)
                 for b in range(bsz)], axis=0).astype(BF16)
            rows = pl.ds(pl.multiple_of(c * (S5_K_BLOCKS * tc), S5_K_BLOCKS * tc), S5_K_BLOCKS * tc)
            acc_ref[...] += jnp.dot(xc, m_cur_ref[rows, :], preferred_element_type=F32)
            return carry

        lax.fori_loop(0, grp // S5_K_BLOCKS, step, 0)

    @pl.when(g % 2 == 0)
    def _():
        run(m0_ref, m1_ref)

    @pl.when(g % 2 == 1)
    def _():
        run(m1_ref, m0_ref)
    for b in range(bsz):
        for j in range(grp):
            y_ref[b, j] = acc_ref[b * n_chunks:(b + 1) * n_chunks, j * tc:(j + 1) * tc]


def _s5_scan(u4, kern, f_bf, e4_bf, a_pack, d_row):
    b, h, nc, tc = u4.shape
    g = kern.shape[0]
    grp = h // g
    k = grp * tc
    p2 = f_bf.shape[2]
    rows = b * nc
    return pl.pallas_call(
        _s5_scan_kernel,
        grid=(g,),
        in_specs=[pl.BlockSpec((b, grp, nc, tc), lambda i: (0, i, 0, 0)),
                  pl.BlockSpec((1, grp * grp, tc), lambda i: (i, 0, 0)),
                  pl.BlockSpec((1, grp * grp, tc), lambda i: (jnp.minimum(i + 1, g - 1), 0, 0)),
                  pl.BlockSpec((1, k, p2), lambda i: (i, 0, 0)),
                  pl.BlockSpec((1, grp, p2, tc), lambda i: (i, 0, 0, 0)),
                  pl.BlockSpec((1, 8, p2), lambda i: (i, 0, 0)),
                  pl.BlockSpec((1, 1, k), lambda i: (i, 0, 0))],
        out_specs=pl.BlockSpec((b, grp, nc, tc), lambda i: (0, i, 0, 0)),
        out_shape=jax.ShapeDtypeStruct(u4.shape, F32),
        scratch_shapes=[pltpu.VMEM((k, k), BF16), pltpu.VMEM((k, k), BF16),
                        pltpu.VMEM((rows, k), F32), pltpu.VMEM((rows, p2), F32),
                        pltpu.VMEM((rows, p2), F32), pltpu.VMEM((rows, p2), F32)],
        compiler_params=_cparams("arbitrary"),
        name="s5_scan",
    )(u4, kern, kern, f_bf, e4_bf, a_pack, d_row)


def _s5_tables(lam_re, lam_im, log_dt, b_re, b_im, c_re, c_im, d_skip):
    g, p = lam_re.shape
    k = SSM_GROUP
    tc = SSM_CHUNK
    dt = jnp.exp(log_dt)[:, None]
    mag = jnp.exp(lam_re * dt)
    ab_re = mag * jnp.cos(lam_im * dt)
    ab_im = mag * jnp.sin(lam_im * dt)
    den = lam_re * lam_re + lam_im * lam_im
    cf_re = ((ab_re - 1) * lam_re + ab_im * lam_im) / den
    cf_im = (ab_im * lam_re - (ab_re - 1) * lam_im) / den
    bb_re = cf_re[..., None] * b_re - cf_im[..., None] * b_im
    bb_im = cf_re[..., None] * b_im + cf_im[..., None] * b_re
    lags = jnp.arange(tc + 1, dtype=F32)[:, None, None]
    pmag = jnp.exp(lags * (lam_re * dt)[None])
    pang = lags * (lam_im * dt)[None]
    p_re = pmag * jnp.cos(pang)
    p_im = pmag * jnp.sin(pang)
    cb_re = jnp.einsum('gjp,gpi->gpij', c_re, bb_re) - jnp.einsum('gjp,gpi->gpij', c_im, bb_im)
    cb_im = jnp.einsum('gjp,gpi->gpij', c_re, bb_im) + jnp.einsum('gjp,gpi->gpij', c_im, bb_re)
    kern = (jnp.einsum('lgp,gpij->gijl', p_re[:tc], cb_re, precision=HIGHEST)
            - jnp.einsum('lgp,gpij->gijl', p_im[:tc], cb_im, precision=HIGHEST))
    kern = kern.reshape(g, k * k, tc)
    lre = jnp.concatenate([lam_re * dt, lam_re * dt], axis=-1)
    lim = jnp.concatenate([lam_im * dt, lam_im * dt], axis=-1)
    sign = jnp.concatenate([-jnp.ones((p,), F32), jnp.ones((p,), F32)])
    pw_f = (tc - 1 - jnp.arange(tc, dtype=F32))[None, :, None]
    mag_f = jnp.exp(pw_f * lre[:, None, :])
    pr2 = mag_f * jnp.cos(pw_f * lim[:, None, :])
    pi2 = mag_f * jnp.sin(pw_f * lim[:, None, :]) * sign
    bb1 = jnp.concatenate([bb_re, bb_im], axis=1).transpose(0, 2, 1)
    bb2 = jnp.concatenate([bb_im, bb_re], axis=1).transpose(0, 2, 1)
    f_mat = (pr2[:, None] * bb1[:, :, None] + pi2[:, None] * bb2[:, :, None]).reshape(g, k * tc, 2 * p).astype(BF16)
    pw_e = (jnp.arange(tc, dtype=F32) + 1.0)[None, None, :]
    mag_e = jnp.exp(pw_e * lre[:, :, None])
    qr2 = mag_e * jnp.cos(pw_e * lim[:, :, None])
    qi2 = mag_e * jnp.sin(pw_e * lim[:, :, None])
    ca = jnp.concatenate([c_re, -c_im], axis=-1)
    cb = jnp.concatenate([-c_im, -c_re], axis=-1)
    e_mat = (ca[:, :, :, None] * qr2[:, None] + cb[:, :, :, None] * qi2[:, None]).astype(BF16)
    ar = p_re[tc]
    ai = p_im[tc]
    a1 = jnp.concatenate([ar, ar], axis=-1)
    a2 = jnp.concatenate([-ai, ai], axis=-1)
    a_pack = jnp.concatenate([a1[:, None], a2[:, None], jnp.zeros((g, 6, 2 * p), F32)], axis=1)
    d_row = jnp.repeat(d_skip.reshape(g, k), tc, axis=1).reshape(g, 1, k * tc)
    return kern, f_mat, e_mat, a_pack, d_row


def _gelu_tanh(x):
    return 0.5 * x * (1.0 + jnp.tanh(math.sqrt(2.0 / math.pi) * (x + 0.044715 * (x * x * x))))


def _s5_out_kernel(y_ref, x_ref, mod_ref, w_ref, g2_ref, wr_ref, br_ref,
                   x_out_ref, hn_ref, eid_ref, wts_ref, cnt_ref):
    mod = mod_ref[0]
    h = y_ref.shape[1]
    yt = _gelu_tanh(y_ref[0].reshape(h, S5_TILE))
    d = w_ref.shape[1] // 2
    sub_rows = MOE_BLOCK
    n_sub = S5_TILE // sub_rows
    per_block = MOE_BLOCK // sub_rows

    def project(p):
        return lax.dot_general(yt[:, p * sub_rows:(p + 1) * sub_rows].astype(BF16), w_ref[...],
                               (((0,), (0,)), ((), ())), preferred_element_type=F32)

    o_next = project(0)
    cnt = None
    for p in range(n_sub):
        o = o_next
        if p + 1 < n_sub:
            o_next = project(p + 1)
        rows = slice(p * sub_rows, (p + 1) * sub_rows)
        out = o[:, :d] * jax.nn.sigmoid(o[:, d:])
        c = _residual_and_route(x_ref[0, rows, :], out, mod, g2_ref, wr_ref, br_ref,
                                x_out_ref.at[0, rows, :], hn_ref.at[rows, :], eid_ref.at[rows, :],
                                wts_ref.at[rows, :])
        cnt = c if p % per_block == 0 else cnt + c
        if p % per_block == per_block - 1:
            cnt_ref[p // per_block] = jnp.broadcast_to(cnt, cnt_ref.shape[1:])


def _mixer_out_specs(b, l, d, tile):
    nt = l // tile
    nblk = tile // MOE_BLOCK
    specs = [pl.BlockSpec((1, tile, d), lambda i, j: (i, j, 0)),
             pl.BlockSpec((tile, d), lambda i, j: (i * nt + j, 0)),
             pl.BlockSpec((tile, TOP_K), lambda i, j: (i * nt + j, 0)),
             pl.BlockSpec((tile, TOP_K), lambda i, j: (i * nt + j, 0)),
             pl.BlockSpec((nblk, 8, LANES), lambda i, j: (i * nt + j, 0, 0))]
    shapes = [jax.ShapeDtypeStruct((b, l, d), F32),
              jax.ShapeDtypeStruct((b * l, d), BF16),
              jax.ShapeDtypeStruct((b * l, TOP_K), jnp.int32),
              jax.ShapeDtypeStruct((b * l, TOP_K), F32),
              jax.ShapeDtypeStruct((b * l // MOE_BLOCK, 8, LANES), F32)]
    return specs, shapes


def _s5_out(y4, x, mod, w_bf, g2, wr, br):
    b, l, d = x.shape
    h = y4.shape[1]
    specs, shapes = _mixer_out_specs(b, l, d, S5_TILE)
    return pl.pallas_call(
        _s5_out_kernel,
        grid=(b, l // S5_TILE),
        in_specs=[pl.BlockSpec((1, h, S5_TILE // SSM_CHUNK, SSM_CHUNK), lambda i, j: (i, 0, j, 0)),
                  pl.BlockSpec((1, S5_TILE, d), lambda i, j: (i, j, 0)),
                  pl.BlockSpec((1, 8, d), lambda i, j: (i, 0, 0)),
                  pl.BlockSpec(w_bf.shape, lambda i, j: (0, 0), pipeline_mode=pl.Buffered(1)),
                  pl.BlockSpec((1, d), lambda i, j: (0, 0)),
                  pl.BlockSpec((d, 2 * LANES), lambda i, j: (0, 0)),
                  pl.BlockSpec((1, LANES), lambda i, j: (0, 0))],
        out_specs=specs,
        out_shape=shapes,
        compiler_params=_cparams("parallel", "parallel"),
        name="s5_out",
    )(y4, x, mod, w_bf, g2, wr, br)


def _conv_mixer_kernel(x_ref, mod_ref, g1_ref, win_ref, cw_ref, wout_ref, g2_ref, wr_ref, br_ref,
                       x_out_ref, hn_ref, eid_ref, wts_ref, cnt_ref, carry_ref):
    @pl.when(pl.program_id(1) == 0)
    def _():
        carry_ref[...] = jnp.zeros_like(carry_ref)

    mod = mod_ref[0]
    d = x_ref.shape[2]
    cw = cw_ref[...]
    n_sub = x_ref.shape[1] // SUB_ROWS
    sub = [slice(s * SUB_ROWS, (s + 1) * SUB_ROWS) for s in range(n_sub)]

    def project(s):
        hn = _norm_mod(x_ref[0, sub[s], :], g1_ref[...], mod[0:1], mod[1:2])
        return jnp.dot(hn.astype(BF16), win_ref[...], preferred_element_type=F32)

    def gate_conv(p, prev_tail):
        u = p[:, d:2 * d] * p[:, 2 * d:]
        ext = jnp.concatenate([prev_tail, u], axis=0)
        z = cw[0:1] * ext[6:6 + SUB_ROWS] + cw[1:2] * ext[7:7 + SUB_ROWS] + cw[2:3] * u
        return (p[:, :d] * z).astype(BF16), u[SUB_ROWS - 8:]

    ps = [project(s) for s in range(n_sub)]
    tail = carry_ref[...]
    outs = []
    for s in range(n_sub):
        gated, tail = gate_conv(ps[s], tail)
        outs.append(jnp.dot(gated, wout_ref[...], preferred_element_type=F32))
    carry_ref[...] = tail
    cnt = None
    for s in range(n_sub):
        c = _residual_and_route(x_ref[0, sub[s], :], outs[s], mod, g2_ref, wr_ref, br_ref,
                                x_out_ref.at[0, sub[s], :], hn_ref.at[sub[s], :], eid_ref.at[sub[s], :],
                                wts_ref.at[sub[s], :])
        cnt = c if s == 0 else cnt + c
    cnt_ref[0] = jnp.broadcast_to(cnt, cnt_ref.shape[1:])


def _conv_mixer(x, mod, g1, win_bf, cw8, wout_bf, g2, wr, br):
    b, l, d = x.shape
    specs, shapes = _mixer_out_specs(b, l, d, TOKEN_TILE)
    return pl.pallas_call(
        _conv_mixer_kernel,
        grid=(b, l // TOKEN_TILE),
        in_specs=[pl.BlockSpec((1, TOKEN_TILE, d), lambda i, j: (i, j, 0)),
                  pl.BlockSpec((1, 8, d), lambda i, j: (i, 0, 0)),
                  pl.BlockSpec((1, d), lambda i, j: (0, 0)),
                  pl.BlockSpec(win_bf.shape, lambda i, j: (0, 0)),
                  pl.BlockSpec((8, d), lambda i, j: (0, 0)),
                  pl.BlockSpec(wout_bf.shape, lambda i, j: (0, 0)),
                  pl.BlockSpec((1, d), lambda i, j: (0, 0)),
                  pl.BlockSpec((d, 2 * LANES), lambda i, j: (0, 0)),
                  pl.BlockSpec((1, LANES), lambda i, j: (0, 0))],
        out_specs=specs,
        out_shape=shapes,
        scratch_shapes=[pltpu.VMEM((8, d), F32)],
        compiler_params=_cparams("parallel", "arbitrary"),
        name="conv_mixer",
    )(x, mod, g1, win_bf, cw8, wout_bf, g2, wr, br)


_TN = (((0,), (0,)), ((), ()))


MAX_CHUNKS = LOCAL_ROWS // CHUNK_ROWS


def _rows_copy(src, s_row, dst, d_row, rows, sem):
    return pltpu.make_async_copy(src.at[pl.ds(s_row, rows), :], dst.at[pl.ds(d_row, rows), :], sem)


def _start_block_chunks(blk, count, crow_ref, make_copy):
    def body(q, carry):
        make_copy(pl.multiple_of(q * CHUNK_ROWS, CHUNK_ROWS),
                  pl.multiple_of(crow_ref[blk * MAX_CHUNKS + q], CHUNK_ROWS)).start()
        return carry

    lax.fori_loop(0, count, body, 0)


def _wait_chunks(count, max_count, make_wait):
    for k in range(max_count.bit_length()):
        @pl.when(((count >> k) & 1) == 1)
        def _():
            make_wait(CHUNK_ROWS << k).wait()


def _slot_masks(lp0, lp1):
    slot = lax.broadcasted_iota(jnp.int32, (lp0.shape[0], LOCAL_ROWS), 1)
    return slot == lp0, slot == lp1


_HI16 = -65536


def _pack_bf16_pairs(x):
    h = x.shape[1] // 2
    lo = lax.shift_right_logical(pltpu.bitcast(x[:, :h], jnp.int32), 16)
    hi = pltpu.bitcast(x[:, h:], jnp.int32) & _HI16
    return hi | lo


def _unpack_bf16_pairs(w):
    lo = pltpu.bitcast(w << 16, F32).astype(BF16)
    hi = pltpu.bitcast(w & _HI16, F32).astype(BF16)
    return lo, hi


def _moe_dispatch_kernel(crow_ref, tot_ref, padn_ref, pads_ref, nv_ref,
                         eid_ref, wts_ref, hn_ref, lofff_ref, xs_ref, lpos_ref, loc_ref, zero_ref, sems):
    blk = pl.program_id(0)
    d = hn_ref.shape[1]
    eid = eid_ref[...]
    t = eid.shape[0]
    lane = lax.broadcasted_iota(jnp.int32, (t, LANES), 1)
    hit0 = lane == eid[:, 0:1]
    hit1 = lane == eid[:, 1:2]
    onehot = (hit0 | hit1).astype(BF16)
    r = lax.broadcasted_iota(jnp.int32, (t, t), 0)
    c = lax.broadcasted_iota(jnp.int32, (t, t), 1)
    tri = (c < r).astype(BF16)
    before = jnp.dot(tri, onehot, preferred_element_type=F32) + lofff_ref[0]
    lp0 = jnp.sum(jnp.where(hit0, before, 0.0), axis=-1, keepdims=True).astype(jnp.int32)
    lp1 = jnp.sum(jnp.where(hit1, before, 0.0), axis=-1, keepdims=True).astype(jnp.int32)
    col = lax.broadcasted_iota(jnp.int32, (t, TOP_K), 1)
    lpos_ref[...] = jnp.where(col == 0, lp0, lp1)
    m0, m1 = _slot_masks(lp0, lp1)
    wts = wts_ref[...]
    wrow = jnp.sum(jnp.where(m0, wts[:, 0:1], 0.0) + jnp.where(m1, wts[:, 1:2], 0.0), axis=0, keepdims=True)

    buf = blk % 2
    loc = loc_ref.at[buf]

    def drain(which, count):
        _wait_chunks(count, MAX_CHUNKS,
                     lambda rows: _rows_copy(loc_ref.at[which], 0, xs_ref, 0, rows, sems.at[which]))

    @pl.when(blk >= 2)
    def _():
        drain(buf, tot_ref[blk - 2])

    sorted_rows = lax.dot_general((m0 | m1).astype(BF16), hn_ref[...], _TN, preferred_element_type=F32)
    loc[:, :d // 2] = _pack_bf16_pairs(sorted_rows)
    loc[:, d // 2:] = pltpu.bitcast(jnp.broadcast_to(wrow, (LANES, LOCAL_ROWS)).T, jnp.int32)

    _start_block_chunks(blk, tot_ref[blk], crow_ref,
                        lambda lo, go: _rows_copy(loc, lo, xs_ref, go, CHUNK_ROWS, sems.at[buf]))

    @pl.when(blk == pl.num_programs(0) - 1)
    def _():
        drain(1 - buf, tot_ref[blk - 1])
        drain(buf, tot_ref[blk])
        sem = sems.at[0]
        zero_ref[...] = jnp.zeros_like(zero_ref)
        pad_bits = (EXPERT_TILE // CHUNK_ROWS - 1).bit_length()

        def pad_copies(e, carry, *, wait):
            padn = padn_ref[e]
            for k in range(pad_bits):
                @pl.when(((padn >> k) & 1) == 1)
                def _():
                    done = (padn & ((1 << k) - 1)) * CHUNK_ROWS
                    cp = _rows_copy(zero_ref, 0, xs_ref, pl.multiple_of(pads_ref[e] + done, CHUNK_ROWS),
                                    CHUNK_ROWS << k, sem)
                    if wait:
                        cp.wait()
                    else:
                        cp.start()
            return carry

        lax.fori_loop(0, N_EXPERTS, functools.partial(pad_copies, wait=False), 0)
        lax.fori_loop(0, N_EXPERTS, functools.partial(pad_copies, wait=True), 0)

        def tile_copy(i):
            return pltpu.make_async_copy(zero_ref, xs_ref.at[pl.ds(pl.multiple_of(i * EXPERT_TILE, EXPERT_TILE),
                                                                   EXPERT_TILE), :], sem)

        n_tiles = xs_ref.shape[0] // EXPERT_TILE

        def start_tile(i, c):
            tile_copy(i).start()
            return c

        def wait_tile(i, c):
            tile_copy(i).wait()
            return c

        lax.fori_loop(nv_ref[0], n_tiles, start_tile, 0)
        lax.fori_loop(nv_ref[0], n_tiles, wait_tile, 0)


def _moe_dispatch(tables, eid, wts, hn, loff_f, n_rows):
    t, d = hn.shape
    width = d // 2 + LANES
    grid_spec = pltpu.PrefetchScalarGridSpec(
        num_scalar_prefetch=5,
        grid=(t // MOE_BLOCK,),
        in_specs=[pl.BlockSpec((MOE_BLOCK, TOP_K), lambda i, *_: (i, 0)),
                  pl.BlockSpec((MOE_BLOCK, TOP_K), lambda i, *_: (i, 0)),
                  pl.BlockSpec((MOE_BLOCK, d), lambda i, *_: (i, 0)),
                  pl.BlockSpec((1, 1, LANES), lambda i, *_: (i, 0, 0))],
        out_specs=[pl.BlockSpec(memory_space=pl.ANY),
                   pl.BlockSpec((MOE_BLOCK, TOP_K), lambda i, *_: (i, 0))],
        scratch_shapes=[pltpu.VMEM((2, LOCAL_ROWS, width), jnp.int32), pltpu.VMEM((EXPERT_TILE, width), jnp.int32),
                        pltpu.SemaphoreType.DMA((2,))],
    )
    return pl.pallas_call(
        _moe_dispatch_kernel,
        grid_spec=grid_spec,
        out_shape=[jax.ShapeDtypeStruct((n_rows, width), jnp.int32),
                   jax.ShapeDtypeStruct((t, TOP_K), jnp.int32)],
        compiler_params=_cparams("arbitrary"),
        name="moe_dispatch",
    )(*tables, eid, wts, hn, loff_f)


def _moe_experts_kernel(te_ref, tv_ref, tf_ref, ts_ref, xs_ref, w1_ref, w3_ref, w2_ref, o_ref,
                        w1b_ref, w3b_ref, w2b_ref):
    del te_ref, ts_ref
    i = pl.program_id(0)

    @pl.when(tf_ref[i] != 0)
    def _():
        def cast_rows(r, carry):
            rows = pl.ds(pl.multiple_of(r * LANES, LANES), LANES)
            w1b_ref[rows, :] = w1_ref[0, 0, rows, :].astype(BF16)
            w3b_ref[rows, :] = w3_ref[0, 0, rows, :].astype(BF16)

            @pl.when(r < w2b_ref.shape[0] // LANES)
            def _():
                w2b_ref[rows, :] = w2_ref[0, 0, rows, :].astype(BF16)

            return carry

        lax.fori_loop(0, w1b_ref.shape[0] // LANES, cast_rows, 0)

    @pl.when(tv_ref[i] != 0)
    def _():
        half = o_ref.shape[1]
        x_lo, x_hi = _unpack_bf16_pairs(xs_ref[:, :half])
        w = pltpu.bitcast(xs_ref[:, half:half + 1], F32)
        de = w1b_ref.shape[1]
        o = None
        for c0 in range(0, de, MXU_WIDTH):
            cols = slice(c0, c0 + MXU_WIDTH)
            a = (jnp.dot(x_lo, w1b_ref[:half, cols], preferred_element_type=F32)
                 + jnp.dot(x_hi, w1b_ref[half:, cols], preferred_element_type=F32))
            b = (jnp.dot(x_lo, w3b_ref[:half, cols], preferred_element_type=F32)
                 + jnp.dot(x_hi, w3b_ref[half:, cols], preferred_element_type=F32))
            h = (a * jax.nn.sigmoid(a) * b).astype(BF16)
            part = jnp.dot(h, w2b_ref[cols, :], preferred_element_type=F32)
            o = part if o is None else o + part
        o = o * w
        o_ref[...] = _pack_bf16_pairs(o.astype(BF16).astype(F32))

    @pl.when(tv_ref[i] == 0)
    def _():
        o_ref[...] = jnp.zeros_like(o_ref)


def _moe_experts(tile_expert, tile_valid, tile_first, tile_src, xs, w1, w3, w2, layer):
    r, width = xs.shape
    d = (width - LANES) * 2
    de = w1.shape[3]
    grid_spec = pltpu.PrefetchScalarGridSpec(
        num_scalar_prefetch=4,
        grid=(r // EXPERT_TILE,),
        in_specs=[pl.BlockSpec((EXPERT_TILE, width), lambda i, te, tv, tf, ts: (ts[i], 0)),
                  pl.BlockSpec((1, 1, d, de), lambda i, te, tv, tf, ts: (layer, te[i], 0, 0)),
                  pl.BlockSpec((1, 1, d, de), lambda i, te, tv, tf, ts: (layer, te[i], 0, 0)),
                  pl.BlockSpec((1, 1, de, d), lambda i, te, tv, tf, ts: (layer, te[i], 0, 0))],
        out_specs=pl.BlockSpec((EXPERT_TILE, d // 2), lambda i, te, tv, tf, ts: (i, 0)),
        scratch_shapes=[pltpu.VMEM((d, de), BF16), pltpu.VMEM((d, de), BF16), pltpu.VMEM((de, d), BF16)],
    )
    return pl.pallas_call(
        _moe_experts_kernel,
        grid_spec=grid_spec,
        out_shape=jax.ShapeDtypeStruct((r, d // 2), jnp.int32),
        compiler_params=_cparams("arbitrary"),
        name="moe_experts",
    )(tile_expert, tile_valid, tile_first, tile_src, xs, w1, w3, w2)


def _moe_combine_kernel(crow_ref, tot_ref, lpos_ref, x_ref, mod_ref, fg_ref, o_hbm_ref,
                        out_ref, loc_ref, sems, *, final_norm):
    blk = pl.program_id(0) * pl.num_programs(1) + pl.program_id(1)
    n_blk = pl.num_programs(0) * pl.num_programs(1)
    t = x_ref.shape[1]
    buf = blk % 2

    def fetch(b, which):
        loc = loc_ref.at[which]
        loc[TOP_K * t:, :] = jnp.zeros((LOCAL_ROWS - TOP_K * t, loc_ref.shape[2]), jnp.int32)
        _start_block_chunks(b, tot_ref[b], crow_ref,
                            lambda lo, go: _rows_copy(o_hbm_ref, go, loc, lo, CHUNK_ROWS, sems.at[which]))

    @pl.when(blk == 0)
    def _():
        fetch(blk, buf)

    @pl.when(blk + 1 < n_blk)
    def _():
        fetch(blk + 1, 1 - buf)

    _wait_chunks(tot_ref[blk], MAX_CHUNKS,
                 lambda rows: _rows_copy(o_hbm_ref, 0, loc_ref.at[buf], 0, rows, sems.at[buf]))
    lp = lpos_ref[...]
    m0, m1 = _slot_masks(lp[:, 0:1], lp[:, 1:2])
    pt = (m0 | m1).astype(BF16)
    o_lo, o_hi = _unpack_bf16_pairs(loc_ref[buf])
    y = jnp.concatenate([jnp.dot(pt, o_lo, preferred_element_type=F32),
                         jnp.dot(pt, o_hi, preferred_element_type=F32)], axis=1)
    x2 = x_ref[0] + (1.0 + mod_ref[0, 5:6]) * y
    if final_norm:
        x2 = (x2 * lax.rsqrt(jnp.mean(x2 * x2, axis=-1, keepdims=True) + RMS_EPS)) * fg_ref[...]
    out_ref[0] = x2


def _moe_combine(tables, lpos, x, mod, final_g, o_sorted, final_norm):
    b, l, d = x.shape
    nt = l // MOE_BLOCK
    grid_spec = pltpu.PrefetchScalarGridSpec(
        num_scalar_prefetch=2,
        grid=(b, nt),
        in_specs=[pl.BlockSpec((MOE_BLOCK, TOP_K), lambda i, j, *_: (i * nt + j, 0)),
                  pl.BlockSpec((1, MOE_BLOCK, d), lambda i, j, *_: (i, j, 0)),
                  pl.BlockSpec((1, 8, d), lambda i, j, *_: (i, 0, 0)),
                  pl.BlockSpec((1, d), lambda i, j, *_: (0, 0)),
                  pl.BlockSpec(memory_space=pl.ANY)],
        out_specs=pl.BlockSpec((1, MOE_BLOCK, d), lambda i, j, *_: (i, j, 0)),
        scratch_shapes=[pltpu.VMEM((2, LOCAL_ROWS, d // 2), jnp.int32), pltpu.SemaphoreType.DMA((2,))],
    )
    return pl.pallas_call(
        functools.partial(_moe_combine_kernel, final_norm=final_norm),
        grid_spec=grid_spec,
        out_shape=jax.ShapeDtypeStruct((b, l, d), F32),
        compiler_params=_cparams("arbitrary", "arbitrary"),
        name="moe_combine",
    )(*tables[:2], lpos, x, mod, final_g, o_sorted)


def _moe(x, hn, eid, wts, cnt, mod, w1, w3, w2, layer, final_g, final_norm):
    t, d = hn.shape
    nblk = t // MOE_BLOCK
    i32 = jnp.int32
    n = cnt[:, 0, :N_EXPERTS].astype(i32)
    run = (n + CHUNK_ROWS - 1) // CHUNK_ROWS * CHUNK_ROWS
    loff = jnp.cumsum(run, axis=1) - run
    rows_e = jnp.sum(run, axis=0)
    tiles_e = (rows_e + EXPERT_TILE - 1) // EXPERT_TILE
    tile_end = jnp.cumsum(tiles_e)
    base = (tile_end - tiles_e) * EXPERT_TILE
    goff = base[None, :] + jnp.cumsum(run, axis=0) - run
    nch = run // CHUNK_ROWS
    tot = jnp.sum(nch, axis=1)
    ch_end = jnp.cumsum(nch, axis=1)
    q = jnp.arange(MAX_CHUNKS, dtype=i32)
    in_run = ((q[None, :, None] >= (ch_end - nch)[:, None, :]) & (q[None, :, None] < ch_end[:, None, :])).astype(i32)
    crow = jnp.sum(in_run * (goff[:, None, :] + (q[None, :, None] - (ch_end - nch)[:, None, :]) * CHUNK_ROWS), axis=-1)
    padn = (tiles_e * EXPERT_TILE - rows_e) // CHUNK_ROWS
    pads = base + rows_e
    tables = (crow.reshape(-1), tot, padn, pads, tile_end[-1:])
    loff_f = jnp.zeros((nblk, 1, LANES), F32).at[:, 0, :N_EXPERTS].set(loff.astype(F32))

    max_rows = t * TOP_K + nblk * N_EXPERTS * (CHUNK_ROWS - 1) + N_EXPERTS * (EXPERT_TILE - 1)
    max_tiles = -(-max_rows // EXPERT_TILE)
    tile_ids = jnp.arange(max_tiles, dtype=i32)
    n_valid = tile_end[-1]
    tile_src = jnp.minimum(tile_ids, n_valid - 1)
    tile_expert = jnp.sum((tile_src[:, None] >= tile_end[None, :]).astype(i32), axis=1)
    tile_valid = (tile_ids < n_valid).astype(i32)
    tile_first = jnp.concatenate([jnp.ones((1,), i32), (tile_expert[1:] != tile_expert[:-1]).astype(i32)])

    xs, lpos = _moe_dispatch(tables, eid, wts, hn, loff_f, max_tiles * EXPERT_TILE)
    o_sorted = _moe_experts(tile_expert, tile_valid, tile_first, tile_src, xs, w1, w3, w2, layer)
    return _moe_combine(tables, lpos, x, mod, final_g, o_sorted, final_norm)


def _router_pack(wg, bg, we, be):
    d = wg.shape[0]
    wr = jnp.zeros((d, LANES), F32).at[:, :N_EXPERTS].set(we).at[:, N_EXPERTS:N_EXPERTS + N_GROUPS].set(wg)
    br = jnp.zeros((1, LANES), F32).at[0, :N_EXPERTS].set(be).at[0, N_EXPERTS:N_EXPERTS + N_GROUPS].set(bg)
    wr_hi = wr.astype(BF16)
    wr_lo = (wr - wr_hi.astype(F32)).astype(BF16)
    return jnp.concatenate([wr_hi, wr_lo], axis=1), br


def kernel(x, c, ada_w, ada_b, norm1_g, norm2_g, ssm_w_in, ssm_lam_re, ssm_lam_im, ssm_log_dt, ssm_b_re, ssm_b_im, ssm_c_re, ssm_c_im, ssm_d, ssm_w_glu, conv_w_in, conv_w, conv_w_out, moe_wg, moe_bg, moe_we, moe_be, moe_w1, moe_w3, moe_w2, final_g):
    b, l, d = x.shape
    depth = ada_w.shape[0]
    c8 = jnp.zeros((8, d), F32).at[:b].set(c)
    mod_all = _adaln(c8, ada_w, ada_b)[:, :b].reshape(depth, b, 6, d)
    mod_all = jnp.concatenate([mod_all, jnp.zeros((depth, b, 2, d), F32)], axis=2)
    fg = final_g.reshape(1, d)

    mod = mod_all[0]
    u4 = _s5_in(x, mod, norm1_g[0:1], ssm_w_in[0].T.astype(BF16))
    kern, f_mat, e_mat, a_pack, d_row = _s5_tables(
        ssm_lam_re[0], ssm_lam_im[0], ssm_log_dt[0], ssm_b_re[0], ssm_b_im[0],
        ssm_c_re[0], ssm_c_im[0], ssm_d[0])
    y4 = _s5_scan(u4, kern, f_mat, e_mat, a_pack, d_row)
    wr, br = _router_pack(moe_wg[0], moe_bg[0], moe_we[0], moe_be[0])
    x1, hn, eid, wts, cnt = _s5_out(y4, x, mod, ssm_w_glu[0].astype(BF16), norm2_g[0:1], wr, br)
    x2 = _moe(x1, hn, eid, wts, cnt, mod, moe_w1, moe_w3, moe_w2, 0, fg, False)

    mod = mod_all[1]
    cw8 = jnp.zeros((8, d), F32).at[:conv_w.shape[1]].set(conv_w[0])
    wr, br = _router_pack(moe_wg[1], moe_bg[1], moe_we[1], moe_be[1])
    x3, hn, eid, wts, cnt = _conv_mixer(x2, mod, norm1_g[1:2], conv_w_in[0].astype(BF16), cw8,
                                        conv_w_out[0].astype(BF16), norm2_g[1:2], wr, br)
    return _moe(x3, hn, eid, wts, cnt, mod, moe_w1, moe_w3, moe_w2, 1, fg, True)
```

```python
import functools
import math

import jax
import jax.numpy as jnp
from jax import lax
from jax.experimental import pallas as pl
from jax.experimental.pallas import tpu as pltpu

F32 = jnp.float32
BF16 = jnp.bfloat16
HIGHEST = lax.Precision.HIGHEST

RMS_EPS = 1e-6
SSM_GROUP = 16
SSM_CHUNK = 128
N_GROUPS = 4
EXPERTS_PER_GROUP = 8
N_EXPERTS = N_GROUPS * EXPERTS_PER_GROUP
TOP_K = 2
LANES = 128
MXU_WIDTH = 256
TOKEN_TILE = 512
S5_TILE = 1024
S5_K_BLOCKS = 2
SUB_ROWS = 256
MOE_BLOCK = TOKEN_TILE
CHUNK_ROWS = 8
LOCAL_ROWS = -(-(TOP_K * MOE_BLOCK + N_EXPERTS * (CHUNK_ROWS - 1)) // LANES) * LANES
EXPERT_TILE = 512
VMEM_LIMIT = 56 * 1024 * 1024
NEG_INF = -1e30


def _cparams(*sem):
    return pltpu.CompilerParams(dimension_semantics=sem, vmem_limit_bytes=VMEM_LIMIT)


def _adaln_kernel(c_ref, w_ref, b_ref, o_ref):
    c = c_ref[...]
    cond = c * jax.nn.sigmoid(c)
    o_ref[0] = jnp.dot(cond, w_ref[0], precision=HIGHEST, preferred_element_type=F32) + b_ref[0]


def _adaln(c8, ada_w, ada_b):
    depth, d, n = ada_w.shape
    tn = 1536
    return pl.pallas_call(
        _adaln_kernel,
        grid=(depth, n // tn),
        in_specs=[pl.BlockSpec((8, d), lambda i, j: (0, 0)),
                  pl.BlockSpec((1, d, tn), lambda i, j: (i, 0, j)),
                  pl.BlockSpec((1, 1, tn), lambda i, j: (i, 0, j))],
        out_specs=pl.BlockSpec((1, 8, tn), lambda i, j: (i, 0, j)),
        out_shape=jax.ShapeDtypeStruct((depth, 8, n), F32),
        compiler_params=_cparams("parallel", "parallel"),
        name="adaln",
    )(c8, ada_w, ada_b.reshape(depth, 1, n))


def _norm_mod(x, g, shift, scale):
    y = x * lax.rsqrt(jnp.mean(x * x, axis=-1, keepdims=True) + RMS_EPS)
    return (y * g) * (1.0 + scale) + shift


def _route(hn, hn_hi, wr_ref, br_ref, eid_ref, wts_ref):
    hn_lo = (hn - hn_hi.astype(F32)).astype(BF16)
    both = jnp.dot(hn_hi, wr_ref[...], preferred_element_type=F32)
    logits = (both[:, :LANES] + both[:, LANES:]
              + jnp.dot(hn_lo, wr_ref[:, :LANES], preferred_element_type=F32)) + br_ref[...]
    lane = lax.broadcasted_iota(jnp.int32, logits.shape, 1)
    is_grp = (lane >= N_EXPERTS) & (lane < N_EXPERTS + N_GROUPS)
    lg = jnp.where(is_grp, logits, NEG_INF)
    gmax = jnp.max(lg, axis=-1, keepdims=True)
    gsum = jnp.sum(jnp.where(is_grp, jnp.exp(lg - gmax), 0.0), axis=-1, keepdims=True)
    gp = 1.0 / gsum
    gi = jnp.min(jnp.where(lg == gmax, lane, 2 * LANES), axis=-1, keepdims=True) - N_EXPERTS
    in_grp = (lane < N_EXPERTS) & ((lane // EXPERTS_PER_GROUP) == gi)
    le = jnp.where(in_grp, logits, NEG_INF)
    v1 = jnp.max(le, axis=-1, keepdims=True)
    i1 = jnp.min(jnp.where(le == v1, lane, 2 * LANES), axis=-1, keepdims=True)
    le2 = jnp.where(lane == i1, NEG_INF, le)
    v2 = jnp.max(le2, axis=-1, keepdims=True)
    i2 = jnp.min(jnp.where(le2 == v2, lane, 2 * LANES), axis=-1, keepdims=True)
    e2 = jnp.exp(v2 - v1)
    den = 1.0 + e2
    col = lax.broadcasted_iota(jnp.int32, (hn.shape[0], TOP_K), 1)
    eid_ref[...] = jnp.where(col == 0, i1, i2)
    wts_ref[...] = jnp.where(col == 0, gp / den, gp * e2 / den)
    chosen = ((lane == i1) | (lane == i2)).astype(F32)
    return jnp.sum(chosen, axis=0, keepdims=True)


def _residual_and_route(x, out, mod, g2_ref, wr_ref, br_ref, x_out_ref, hn_ref, eid_ref, wts_ref):
    x1 = x + (1.0 + mod[2:3]) * out
    x_out_ref[...] = x1
    hn = _norm_mod(x1, g2_ref[...], mod[3:4], mod[4:5])
    hn_hi = hn.astype(BF16)
    hn_ref[...] = hn_hi
    return _route(hn, hn_hi, wr_ref, br_ref, eid_ref, wts_ref)


def _s5_in_kernel(x_ref, mod_ref, g_ref, wt_ref, u_ref):
    mod = mod_ref[0]
    hn = _norm_mod(x_ref[0], g_ref[...], mod[0:1], mod[1:2])
    ut = lax.dot_general(wt_ref[...], hn.astype(BF16), (((1,), (1,)), ((), ())), preferred_element_type=F32)
    u_ref[0] = ut.reshape(ut.shape[0], S5_TILE // SSM_CHUNK, SSM_CHUNK)


def _s5_in(x, mod, g, wt_bf):
    b, l, d = x.shape
    h = wt_bf.shape[0]
    cpt = S5_TILE // SSM_CHUNK
    return pl.pallas_call(
        _s5_in_kernel,
        grid=(b, l // S5_TILE),
        in_specs=[pl.BlockSpec((1, S5_TILE, d), lambda i, j: (i, j, 0)),
                  pl.BlockSpec((1, 8, d), lambda i, j: (i, 0, 0)),
                  pl.BlockSpec((1, d), lambda i, j: (0, 0)),
                  pl.BlockSpec((h, d), lambda i, j: (0, 0))],
        out_specs=pl.BlockSpec((1, h, cpt, SSM_CHUNK), lambda i, j: (i, 0, j, 0)),
        out_shape=jax.ShapeDtypeStruct((b, h, l // SSM_CHUNK, SSM_CHUNK), F32),
        compiler_params=_cparams("parallel", "parallel"),
        name="s5_in",
    )(x, mod, g, wt_bf)


def _s5_scan_kernel(u_ref, k_ref, kn_ref, f_ref, e_ref, a_ref, d_ref, y_ref,
                    m0_ref, m1_ref, acc_ref, s_ref, sw_ref, sp_ref):
    bsz, grp, n_chunks, tc = u_ref.shape
    g = pl.program_id(0)
    srow = lax.broadcasted_iota(jnp.int32, (tc, tc), 0)
    tcol = lax.broadcasted_iota(jnp.int32, (tc, tc), 1)
    causal = tcol >= srow

    def build_rows(src_ref, dst_ref, i):
        blks = []
        for j in range(grp):
            row = src_ref[0, pl.ds(i * grp + j, 1), :]
            blk = pltpu.roll(jnp.broadcast_to(row, (tc, tc)), 0, 1, stride=1, stride_axis=0)
            blks.append(jnp.where(causal, blk, 0.0))
        dst_ref[pl.ds(pl.multiple_of(i * tc, tc), tc), :] = jnp.concatenate(blks, axis=1).astype(BF16)

    @pl.when(g == 0)
    def _():
        def first(i, carry):
            build_rows(k_ref, m0_ref, i)
            return carry

        lax.fori_loop(0, grp, first, 0)

    x = jnp.concatenate(
        [jnp.concatenate([u_ref[b, k] for k in range(grp)], axis=1) for b in range(bsz)], axis=0)
    xb = x.astype(BF16)
    s_loc = jnp.dot(xb, f_ref[0], preferred_element_type=F32)
    s_ref[...] = s_loc
    half = s_loc.shape[1] // 2
    sw_ref[...] = jnp.concatenate([s_loc[:, half:], s_loc[:, :half]], axis=1)
    a1 = a_ref[0, 0:1, :]
    a2 = a_ref[0, 1:2, :]
    s = [jnp.zeros((1, s_loc.shape[1]), F32) for _ in range(bsz)]
    sw = [jnp.zeros((1, s_loc.shape[1]), F32) for _ in range(bsz)]
    for c in range(n_chunks):
        for b in range(bsz):
            r = b * n_chunks + c
            sp_ref[r:r + 1, :] = s[b]
            s_new = a1 * s[b] + a2 * sw[b] + s_ref[r:r + 1, :]
            sw[b] = a1 * sw[b] - a2 * s[b] + sw_ref[r:r + 1, :]
            s[b] = s_new
    e_mat = jnp.concatenate([e_ref[0, j] for j in range(grp)], axis=1)
    acc_ref[...] = jnp.dot(sp_ref[...].astype(BF16), e_mat, preferred_element_type=F32) + d_ref[0] * x

    def run(m_cur_ref, m_next_ref):
        def step(c, carry):
            for h in range(S5_K_BLOCKS):
                build_rows(kn_ref, m_next_ref, c * S5_K_BLOCKS + h)
            xc = jnp.concatenate(
                [jnp.concatenate([u_ref[b, c * S5_K_BLOCKS + h] for h in range(S5_K_BLOCKS)], axis=1)
                 for b in range(bsz)], axis=0).astype(BF16)
            rows = pl.ds(pl.multiple_of(c * (S5_K_BLOCKS * tc), S5_K_BLOCKS * tc), S5_K_BLOCKS * tc)
            acc_ref[...] += jnp.dot(xc, m_cur_ref[rows, :], preferred_element_type=F32)
            return carry

        lax.fori_loop(0, grp // S5_K_BLOCKS, step, 0)

    @pl.when(g % 2 == 0)
    def _():
        run(m0_ref, m1_ref)

    @pl.when(g % 2 == 1)
    def _():
        run(m1_ref, m0_ref)
    for b in range(bsz):
        for j in range(grp):
            y_ref[b, j] = acc_ref[b * n_chunks:(b + 1) * n_chunks, j * tc:(j + 1) * tc]


def _s5_scan(u4, kern, f_bf, e4_bf, a_pack, d_row):
    b, h, nc, tc = u4.shape
    g = kern.shape[0]
    grp = h // g
    k = grp * tc
    p2 = f_bf.shape[2]
    rows = b * nc
    return pl.pallas_call(
        _s5_scan_kernel,
        grid=(g,),
        in_specs=[pl.BlockSpec((b, grp, nc, tc), lambda i: (0, i, 0, 0)),
                  pl.BlockSpec((1, grp * grp, tc), lambda i: (i, 0, 0)),
                  pl.BlockSpec((1, grp * grp, tc), lambda i: (jnp.minimum(i + 1, g - 1), 0, 0)),
                  pl.BlockSpec((1, k, p2), lambda i: (i, 0, 0)),
                  pl.BlockSpec((1, grp, p2, tc), lambda i: (i, 0, 0, 0)),
                  pl.BlockSpec((1, 8, p2), lambda i: (i, 0, 0)),
                  pl.BlockSpec((1, 1, k), lambda i: (i, 0, 0))],
        out_specs=pl.BlockSpec((b, grp, nc, tc), lambda i: (0, i, 0, 0)),
        out_shape=jax.ShapeDtypeStruct(u4.shape, F32),
        scratch_shapes=[pltpu.VMEM((k, k), BF16), pltpu.VMEM((k, k), BF16),
                        pltpu.VMEM((rows, k), F32), pltpu.VMEM((rows, p2), F32),
                        pltpu.VMEM((rows, p2), F32), pltpu.VMEM((rows, p2), F32)],
        compiler_params=_cparams("arbitrary"),
        name="s5_scan",
    )(u4, kern, kern, f_bf, e4_bf, a_pack, d_row)


def _s5_tables(lam_re, lam_im, log_dt, b_re, b_im, c_re, c_im, d_skip):
    g, p = lam_re.shape
    k = SSM_GROUP
    tc = SSM_CHUNK
    dt = jnp.exp(log_dt)[:, None]
    mag = jnp.exp(lam_re * dt)
    ab_re = mag * jnp.cos(lam_im * dt)
    ab_im = mag * jnp.sin(lam_im * dt)
    den = lam_re * lam_re + lam_im * lam_im
    cf_re = ((ab_re - 1) * lam_re + ab_im * lam_im) / den
    cf_im = (ab_im * lam_re - (ab_re - 1) * lam_im) / den
    bb_re = cf_re[..., None] * b_re - cf_im[..., None] * b_im
    bb_im = cf_re[..., None] * b_im + cf_im[..., None] * b_re
    lags = jnp.arange(tc + 1, dtype=F32)[:, None, None]
    pmag = jnp.exp(lags * (lam_re * dt)[None])
    pang = lags * (lam_im * dt)[None]
    p_re = pmag * jnp.cos(pang)
    p_im = pmag * jnp.sin(pang)
    cb_re = jnp.einsum('gjp,gpi->gpij', c_re, bb_re) - jnp.einsum('gjp,gpi->gpij', c_im, bb_im)
    cb_im = jnp.einsum('gjp,gpi->gpij', c_re, bb_im) + jnp.einsum('gjp,gpi->gpij', c_im, bb_re)
    kern = (jnp.einsum('lgp,gpij->gijl', p_re[:tc], cb_re, precision=HIGHEST)
            - jnp.einsum('lgp,gpij->gijl', p_im[:tc], cb_im, precision=HIGHEST))
    kern = kern.reshape(g, k * k, tc)
    lre = jnp.concatenate([lam_re * dt, lam_re * dt], axis=-1)
    lim = jnp.concatenate([lam_im * dt, lam_im * dt], axis=-1)
    sign = jnp.concatenate([-jnp.ones((p,), F32), jnp.ones((p,), F32)])
    pw_f = (tc - 1 - jnp.arange(tc, dtype=F32))[None, :, None]
    mag_f = jnp.exp(pw_f * lre[:, None, :])
    pr2 = mag_f * jnp.cos(pw_f * lim[:, None, :])
    pi2 = mag_f * jnp.sin(pw_f * lim[:, None, :]) * sign
    bb1 = jnp.concatenate([bb_re, bb_im], axis=1).transpose(0, 2, 1)
    bb2 = jnp.concatenate([bb_im, bb_re], axis=1).transpose(0, 2, 1)
    f_mat = (pr2[:, None] * bb1[:, :, None] + pi2[:, None] * bb2[:, :, None]).reshape(g, k * tc, 2 * p).astype(BF16)
    pw_e = (jnp.arange(tc, dtype=F32) + 1.0)[None, None, :]
    mag_e = jnp.exp(pw_e * lre[:, :, None])
    qr2 = mag_e * jnp.cos(pw_e * lim[:, :, None])
    qi2 = mag_e * jnp.sin(pw_e * lim[:, :, None])
    ca = jnp.concatenate([c_re, -c_im], axis=-1)
    cb = jnp.concatenate([-c_im, -c_re], axis=-1)
    e_mat = (ca[:, :, :, None] * qr2[:, None] + cb[:, :, :, None] * qi2[:, None]).astype(BF16)
    ar = p_re[tc]
    ai = p_im[tc]
    a1 = jnp.concatenate([ar, ar], axis=-1)
    a2 = jnp.concatenate([-ai, ai], axis=-1)
    a_pack = jnp.concatenate([a1[:, None], a2[:, None], jnp.zeros((g, 6, 2 * p), F32)], axis=1)
    d_row = jnp.repeat(d_skip.reshape(g, k), tc, axis=1).reshape(g, 1, k * tc)
    return kern, f_mat, e_mat, a_pack, d_row


def _gelu_tanh(x):
    return 0.5 * x * (1.0 + jnp.tanh(math.sqrt(2.0 / math.pi) * (x + 0.044715 * (x * x * x))))


def _s5_out_kernel(y_ref, x_ref, mod_ref, w_ref, g2_ref, wr_ref, br_ref,
                   x_out_ref, hn_ref, eid_ref, wts_ref, cnt_ref):
    mod = mod_ref[0]
    h = y_ref.shape[1]
    yt = _gelu_tanh(y_ref[0].reshape(h, S5_TILE))
    d = w_ref.shape[1] // 2
    sub_rows = MOE_BLOCK
    n_sub = S5_TILE // sub_rows
    per_block = MOE_BLOCK // sub_rows

    def project(p):
        return lax.dot_general(yt[:, p * sub_rows:(p + 1) * sub_rows].astype(BF16), w_ref[...],
                               (((0,), (0,)), ((), ())), preferred_element_type=F32)

    o_next = project(0)
    cnt = None
    for p in range(n_sub):
        o = o_next
        if p + 1 < n_sub:
            o_next = project(p + 1)
        rows = slice(p * sub_rows, (p + 1) * sub_rows)
        out = o[:, :d] * jax.nn.sigmoid(o[:, d:])
        c = _residual_and_route(x_ref[0, rows, :], out, mod, g2_ref, wr_ref, br_ref,
                                x_out_ref.at[0, rows, :], hn_ref.at[rows, :], eid_ref.at[rows, :],
                                wts_ref.at[rows, :])
        cnt = c if p % per_block == 0 else cnt + c
        if p % per_block == per_block - 1:
            cnt_ref[p // per_block] = jnp.broadcast_to(cnt, cnt_ref.shape[1:])


def _mixer_out_specs(b, l, d, tile):
    nt = l // tile
    nblk = tile // MOE_BLOCK
    specs = [pl.BlockSpec((1, tile, d), lambda i, j: (i, j, 0)),
             pl.BlockSpec((tile, d), lambda i, j: (i * nt + j, 0)),
             pl.BlockSpec((tile, TOP_K), lambda i, j: (i * nt + j, 0)),
             pl.BlockSpec((tile, TOP_K), lambda i, j: (i * nt + j, 0)),
             pl.BlockSpec((nblk, 8, LANES), lambda i, j: (i * nt + j, 0, 0))]
    shapes = [jax.ShapeDtypeStruct((b, l, d), F32),
              jax.ShapeDtypeStruct((b * l, d), BF16),
              jax.ShapeDtypeStruct((b * l, TOP_K), jnp.int32),
              jax.ShapeDtypeStruct((b * l, TOP_K), F32),
              jax.ShapeDtypeStruct((b * l // MOE_BLOCK, 8, LANES), F32)]
    return specs, shapes


def _s5_out(y4, x, mod, w_bf, g2, wr, br):
    b, l, d = x.shape
    h = y4.shape[1]
    specs, shapes = _mixer_out_specs(b, l, d, S5_TILE)
    return pl.pallas_call(
        _s5_out_kernel,
        grid=(b, l // S5_TILE),
        in_specs=[pl.BlockSpec((1, h, S5_TILE // SSM_CHUNK, SSM_CHUNK), lambda i, j: (i, 0, j, 0)),
                  pl.BlockSpec((1, S5_TILE, d), lambda i, j: (i, j, 0)),
                  pl.BlockSpec((1, 8, d), lambda i, j: (i, 0, 0)),
                  pl.BlockSpec(w_bf.shape, lambda i, j: (0, 0), pipeline_mode=pl.Buffered(1)),
                  pl.BlockSpec((1, d), lambda i, j: (0, 0)),
                  pl.BlockSpec((d, 2 * LANES), lambda i, j: (0, 0)),
                  pl.BlockSpec((1, LANES), lambda i, j: (0, 0))],
        out_specs=specs,
        out_shape=shapes,
        compiler_params=_cparams("parallel", "parallel"),
        name="s5_out",
    )(y4, x, mod, w_bf, g2, wr, br)


def _conv_mixer_body(x_in_ref, mod_ref, g1_ref, win_ref, cw_ref, wout_ref, g2_ref, wr_ref, br_ref,
                     x_out_ref, hn_ref, eid_ref, wts_ref, cnt_ref, carry_ref):
    @pl.when(pl.program_id(1) == 0)
    def _():
        carry_ref[...] = jnp.zeros_like(carry_ref)

    mod = mod_ref[0]
    d = x_in_ref.shape[1]
    cw = cw_ref[...]
    n_sub = x_in_ref.shape[0] // SUB_ROWS
    sub = [slice(s * SUB_ROWS, (s + 1) * SUB_ROWS) for s in range(n_sub)]

    def project(s):
        hn = _norm_mod(x_in_ref[sub[s], :], g1_ref[...], mod[0:1], mod[1:2])
        return jnp.dot(hn.astype(BF16), win_ref[...], preferred_element_type=F32)

    def gate_conv(p, prev_tail):
        u = p[:, d:2 * d] * p[:, 2 * d:]
        ext = jnp.concatenate([prev_tail, u], axis=0)
        z = cw[0:1] * ext[6:6 + SUB_ROWS] + cw[1:2] * ext[7:7 + SUB_ROWS] + cw[2:3] * u
        return (p[:, :d] * z).astype(BF16), u[SUB_ROWS - 8:]

    ps = [project(s) for s in range(n_sub)]
    tail = carry_ref[...]
    outs = []
    for s in range(n_sub):
        gated, tail = gate_conv(ps[s], tail)
        outs.append(jnp.dot(gated, wout_ref[...], preferred_element_type=F32))
    carry_ref[...] = tail
    cnt = None
    for s in range(n_sub):
        c = _residual_and_route(x_in_ref[sub[s], :], outs[s], mod, g2_ref, wr_ref, br_ref,
                                x_out_ref.at[0, sub[s], :], hn_ref.at[sub[s], :], eid_ref.at[sub[s], :],
                                wts_ref.at[sub[s], :])
        cnt = c if s == 0 else cnt + c
    cnt_ref[0] = jnp.broadcast_to(cnt, cnt_ref.shape[1:])


_TN = (((0,), (0,)), ((), ()))


MAX_CHUNKS = LOCAL_ROWS // CHUNK_ROWS


def _rows_copy(src, s_row, dst, d_row, rows, sem):
    return pltpu.make_async_copy(src.at[pl.ds(s_row, rows), :], dst.at[pl.ds(d_row, rows), :], sem)


def _start_block_chunks(blk, count, crow_ref, make_copy):
    def body(q, carry):
        make_copy(pl.multiple_of(q * CHUNK_ROWS, CHUNK_ROWS),
                  pl.multiple_of(crow_ref[blk * MAX_CHUNKS + q], CHUNK_ROWS)).start()
        return carry

    lax.fori_loop(0, count, body, 0)


def _wait_chunks(count, max_count, make_wait):
    for k in range(max_count.bit_length()):
        @pl.when(((count >> k) & 1) == 1)
        def _():
            make_wait(CHUNK_ROWS << k).wait()


def _slot_masks(lp0, lp1):
    slot = lax.broadcasted_iota(jnp.int32, (lp0.shape[0], LOCAL_ROWS), 1)
    return slot == lp0, slot == lp1


_HI16 = -65536


def _pack_bf16_pairs(x):
    h = x.shape[1] // 2
    lo = lax.shift_right_logical(pltpu.bitcast(x[:, :h], jnp.int32), 16)
    hi = pltpu.bitcast(x[:, h:], jnp.int32) & _HI16
    return hi | lo


def _unpack_bf16_pairs(w):
    lo = pltpu.bitcast(w << 16, F32).astype(BF16)
    hi = pltpu.bitcast(w & _HI16, F32).astype(BF16)
    return lo, hi


def _moe_dispatch_kernel(crow_ref, tot_ref, padn_ref, pads_ref, nv_ref,
                         eid_ref, wts_ref, hn_ref, lofff_ref, xs_ref, lpos_ref, loc_ref, zero_ref, sems):
    blk = pl.program_id(0)
    d = hn_ref.shape[1]
    eid = eid_ref[...]
    t = eid.shape[0]
    lane = lax.broadcasted_iota(jnp.int32, (t, LANES), 1)
    hit0 = lane == eid[:, 0:1]
    hit1 = lane == eid[:, 1:2]
    onehot = (hit0 | hit1).astype(BF16)
    r = lax.broadcasted_iota(jnp.int32, (t, t), 0)
    c = lax.broadcasted_iota(jnp.int32, (t, t), 1)
    tri = (c < r).astype(BF16)
    before = jnp.dot(tri, onehot, preferred_element_type=F32) + lofff_ref[0]
    lp0 = jnp.sum(jnp.where(hit0, before, 0.0), axis=-1, keepdims=True).astype(jnp.int32)
    lp1 = jnp.sum(jnp.where(hit1, before, 0.0), axis=-1, keepdims=True).astype(jnp.int32)
    col = lax.broadcasted_iota(jnp.int32, (t, TOP_K), 1)
    lpos_ref[...] = jnp.where(col == 0, lp0, lp1)
    wts = wts_ref[...]

    buf = blk % 2
    loc = loc_ref.at[buf]

    def drain(which, count):
        _wait_chunks(count, MAX_CHUNKS,
                     lambda rows: _rows_copy(loc_ref.at[which], 0, xs_ref, 0, rows, sems.at[which]))

    @pl.when(blk >= 2)
    def _():
        drain(buf, tot_ref[blk - 2])

    m0, m1 = _slot_masks(lp0, lp1)
    wrow = jnp.sum(jnp.where(m0, wts[:, 0:1], 0.0) + jnp.where(m1, wts[:, 1:2], 0.0), axis=0, keepdims=True)
    sorted_rows = lax.dot_general((m0 | m1).astype(BF16), hn_ref[...], _TN, preferred_element_type=F32)
    loc[:, :d // 2] = _pack_bf16_pairs(sorted_rows)
    loc[:, d // 2:] = pltpu.bitcast(jnp.broadcast_to(wrow, (LANES, LOCAL_ROWS)).T, jnp.int32)

    _start_block_chunks(blk, tot_ref[blk], crow_ref,
                        lambda lo, go: _rows_copy(loc, lo, xs_ref, go, CHUNK_ROWS, sems.at[buf]))

    @pl.when(blk == pl.num_programs(0) - 1)
    def _():
        drain(1 - buf, tot_ref[blk - 1])
        drain(buf, tot_ref[blk])
        sem = sems.at[0]
        zero_ref[...] = jnp.zeros_like(zero_ref)
        pad_bits = (EXPERT_TILE // CHUNK_ROWS - 1).bit_length()

        def pad_copies(e, carry, *, wait):
            padn = padn_ref[e]
            for k in range(pad_bits):
                @pl.when(((padn >> k) & 1) == 1)
                def _():
                    done = (padn & ((1 << k) - 1)) * CHUNK_ROWS
                    cp = _rows_copy(zero_ref, 0, xs_ref, pl.multiple_of(pads_ref[e] + done, CHUNK_ROWS),
                                    CHUNK_ROWS << k, sem)
                    if wait:
                        cp.wait()
                    else:
                        cp.start()
            return carry

        lax.fori_loop(0, N_EXPERTS, functools.partial(pad_copies, wait=False), 0)
        lax.fori_loop(0, N_EXPERTS, functools.partial(pad_copies, wait=True), 0)

        def tile_copy(i):
            return pltpu.make_async_copy(zero_ref, xs_ref.at[pl.ds(pl.multiple_of(i * EXPERT_TILE, EXPERT_TILE),
                                                                   EXPERT_TILE), :], sem)

        n_tiles = xs_ref.shape[0] // EXPERT_TILE

        def start_tile(i, c):
            tile_copy(i).start()
            return c

        def wait_tile(i, c):
            tile_copy(i).wait()
            return c

        lax.fori_loop(nv_ref[0], n_tiles, start_tile, 0)
        lax.fori_loop(nv_ref[0], n_tiles, wait_tile, 0)


def _moe_dispatch(tables, eid, wts, hn, loff_f, n_rows):
    t, d = hn.shape
    width = d // 2 + LANES
    grid_spec = pltpu.PrefetchScalarGridSpec(
        num_scalar_prefetch=5,
        grid=(t // MOE_BLOCK,),
        in_specs=[pl.BlockSpec((MOE_BLOCK, TOP_K), lambda i, *_: (i, 0)),
                  pl.BlockSpec((MOE_BLOCK, TOP_K), lambda i, *_: (i, 0)),
                  pl.BlockSpec((MOE_BLOCK, d), lambda i, *_: (i, 0)),
                  pl.BlockSpec((1, 1, LANES), lambda i, *_: (i, 0, 0))],
        out_specs=[pl.BlockSpec(memory_space=pl.ANY),
                   pl.BlockSpec((MOE_BLOCK, TOP_K), lambda i, *_: (i, 0))],
        scratch_shapes=[pltpu.VMEM((2, LOCAL_ROWS, width), jnp.int32), pltpu.VMEM((EXPERT_TILE, width), jnp.int32),
                        pltpu.SemaphoreType.DMA((2,))],
    )
    return pl.pallas_call(
        _moe_dispatch_kernel,
        grid_spec=grid_spec,
        out_shape=[jax.ShapeDtypeStruct((n_rows, width), jnp.int32),
                   jax.ShapeDtypeStruct((t, TOP_K), jnp.int32)],
        compiler_params=_cparams("arbitrary"),
        name="moe_dispatch",
    )(*tables, eid, wts, hn, loff_f)


def _moe_experts_kernel(te_ref, tv_ref, tf_ref, ts_ref, xs_ref, w1_ref, w3_ref, w2_ref, o_ref,
                        w1b_ref, w3b_ref, w2b_ref):
    del te_ref, ts_ref
    i = pl.program_id(0)

    @pl.when(tf_ref[i] != 0)
    def _():
        def cast_rows(r, carry):
            rows = pl.ds(pl.multiple_of(r * LANES, LANES), LANES)
            w1b_ref[rows, :] = w1_ref[0, 0, rows, :].astype(BF16)
            w3b_ref[rows, :] = w3_ref[0, 0, rows, :].astype(BF16)

            @pl.when(r < w2b_ref.shape[0] // LANES)
            def _():
                w2b_ref[rows, :] = w2_ref[0, 0, rows, :].astype(BF16)

            return carry

        lax.fori_loop(0, w1b_ref.shape[0] // LANES, cast_rows, 0)

    @pl.when(tv_ref[i] != 0)
    def _():
        half = o_ref.shape[1]
        x_lo, x_hi = _unpack_bf16_pairs(xs_ref[:, :half])
        w = pltpu.bitcast(xs_ref[:, half:half + 1], F32)
        de = w1b_ref.shape[1]
        o = None
        for c0 in range(0, de, MXU_WIDTH):
            cols = slice(c0, c0 + MXU_WIDTH)
            a = (jnp.dot(x_lo, w1b_ref[:half, cols], preferred_element_type=F32)
                 + jnp.dot(x_hi, w1b_ref[half:, cols], preferred_element_type=F32))
            b = (jnp.dot(x_lo, w3b_ref[:half, cols], preferred_element_type=F32)
                 + jnp.dot(x_hi, w3b_ref[half:, cols], preferred_element_type=F32))
            h = (a * jax.nn.sigmoid(a) * b).astype(BF16)
            part = jnp.dot(h, w2b_ref[cols, :], preferred_element_type=F32)
            o = part if o is None else o + part
        o = o * w
        o_ref[...] = _pack_bf16_pairs(o.astype(BF16).astype(F32))

    @pl.when(tv_ref[i] == 0)
    def _():
        o_ref[...] = jnp.zeros_like(o_ref)


def _moe_experts(tile_expert, tile_valid, tile_first, tile_src, xs, w1, w3, w2, layer):
    r, width = xs.shape
    d = (width - LANES) * 2
    de = w1.shape[3]
    grid_spec = pltpu.PrefetchScalarGridSpec(
        num_scalar_prefetch=4,
        grid=(r // EXPERT_TILE,),
        in_specs=[pl.BlockSpec((EXPERT_TILE, width), lambda i, te, tv, tf, ts: (ts[i], 0)),
                  pl.BlockSpec((1, 1, d, de), lambda i, te, tv, tf, ts: (layer, te[i], 0, 0)),
                  pl.BlockSpec((1, 1, d, de), lambda i, te, tv, tf, ts: (layer, te[i], 0, 0)),
                  pl.BlockSpec((1, 1, de, d), lambda i, te, tv, tf, ts: (layer, te[i], 0, 0))],
        out_specs=pl.BlockSpec((EXPERT_TILE, d // 2), lambda i, te, tv, tf, ts: (i, 0)),
        scratch_shapes=[pltpu.VMEM((d, de), BF16), pltpu.VMEM((d, de), BF16), pltpu.VMEM((de, d), BF16)],
    )
    return pl.pallas_call(
        _moe_experts_kernel,
        grid_spec=grid_spec,
        out_shape=jax.ShapeDtypeStruct((r, d // 2), jnp.int32),
        compiler_params=_cparams("arbitrary"),
        name="moe_experts",
    )(tile_expert, tile_valid, tile_first, tile_src, xs, w1, w3, w2)


def _combine_rows(crow_ref, tot_ref, lpos_ref, o_hbm_ref, loc_ref, sems, t):
    blk = pl.program_id(0) * pl.num_programs(1) + pl.program_id(1)
    n_blk = pl.num_programs(0) * pl.num_programs(1)
    buf = blk % 2

    def fetch(b, which):
        loc = loc_ref.at[which]
        loc[TOP_K * t:, :] = jnp.zeros((LOCAL_ROWS - TOP_K * t, loc_ref.shape[2]), jnp.int32)
        _start_block_chunks(b, tot_ref[b], crow_ref,
                            lambda lo, go: _rows_copy(o_hbm_ref, go, loc, lo, CHUNK_ROWS, sems.at[which]))

    @pl.when(blk == 0)
    def _():
        fetch(blk, buf)

    @pl.when(blk + 1 < n_blk)
    def _():
        fetch(blk + 1, 1 - buf)

    _wait_chunks(tot_ref[blk], MAX_CHUNKS,
                 lambda rows: _rows_copy(o_hbm_ref, 0, loc_ref.at[buf], 0, rows, sems.at[buf]))
    lp = lpos_ref[...]
    m0, m1 = _slot_masks(lp[:, 0:1], lp[:, 1:2])
    pt = (m0 | m1).astype(BF16)
    o_lo, o_hi = _unpack_bf16_pairs(loc_ref[buf])
    return jnp.concatenate([jnp.dot(pt, o_lo, preferred_element_type=F32),
                            jnp.dot(pt, o_hi, preferred_element_type=F32)], axis=1)


def _moe_combine_final_kernel(crow_ref, tot_ref, lpos_ref, x_ref, mod_ref, fg_ref, o_hbm_ref,
                              out_ref, loc_ref, sems):
    y = _combine_rows(crow_ref, tot_ref, lpos_ref, o_hbm_ref, loc_ref, sems, x_ref.shape[1])
    x2 = x_ref[0] + (1.0 + mod_ref[0, 5:6]) * y
    out_ref[0] = (x2 * lax.rsqrt(jnp.mean(x2 * x2, axis=-1, keepdims=True) + RMS_EPS)) * fg_ref[...]


def _moe_combine_conv_kernel(crow_ref, tot_ref, lpos_ref, x_ref, mod_prev_ref, o_hbm_ref,
                             mod_ref, g1_ref, win_ref, cw_ref, wout_ref, g2_ref, wr_ref, br_ref,
                             x_out_ref, hn_ref, eid_ref, wts_ref, cnt_ref,
                             loc_ref, sems, xmid_ref, carry_ref):
    y = _combine_rows(crow_ref, tot_ref, lpos_ref, o_hbm_ref, loc_ref, sems, x_ref.shape[1])
    xmid_ref[...] = x_ref[0] + (1.0 + mod_prev_ref[0, 5:6]) * y
    _conv_mixer_body(xmid_ref, mod_ref, g1_ref, win_ref, cw_ref, wout_ref, g2_ref, wr_ref, br_ref,
                     x_out_ref, hn_ref, eid_ref, wts_ref, cnt_ref, carry_ref)


def _combine_in_specs(nt, d):
    return [pl.BlockSpec((MOE_BLOCK, TOP_K), lambda i, j, *_: (i * nt + j, 0)),
            pl.BlockSpec((1, MOE_BLOCK, d), lambda i, j, *_: (i, j, 0)),
            pl.BlockSpec((1, 8, d), lambda i, j, *_: (i, 0, 0))]


def _combine_scratch(d):
    return [pltpu.VMEM((2, LOCAL_ROWS, d // 2), jnp.int32), pltpu.SemaphoreType.DMA((2,))]


def _moe_combine_final(tables, lpos, x, mod, final_g, o_sorted):
    b, l, d = x.shape
    nt = l // MOE_BLOCK
    grid_spec = pltpu.PrefetchScalarGridSpec(
        num_scalar_prefetch=2,
        grid=(b, nt),
        in_specs=_combine_in_specs(nt, d) + [pl.BlockSpec((1, d), lambda i, j, *_: (0, 0)),
                                             pl.BlockSpec(memory_space=pl.ANY)],
        out_specs=pl.BlockSpec((1, MOE_BLOCK, d), lambda i, j, *_: (i, j, 0)),
        scratch_shapes=_combine_scratch(d),
    )
    return pl.pallas_call(
        _moe_combine_final_kernel,
        grid_spec=grid_spec,
        out_shape=jax.ShapeDtypeStruct((b, l, d), F32),
        compiler_params=_cparams("arbitrary", "arbitrary"),
        name="moe_combine_final",
    )(*tables[:2], lpos, x, mod, final_g, o_sorted)


def _moe_combine_conv(tables, lpos, x, mod_prev, o_sorted, mod, g1, win_bf, cw8, wout_bf, g2, wr, br):
    b, l, d = x.shape
    nt = l // MOE_BLOCK
    specs, shapes = _mixer_out_specs(b, l, d, TOKEN_TILE)
    const = lambda shape: pl.BlockSpec(shape, lambda i, j, *_: (0,) * len(shape))
    grid_spec = pltpu.PrefetchScalarGridSpec(
        num_scalar_prefetch=2,
        grid=(b, nt),
        in_specs=_combine_in_specs(nt, d) + [
            pl.BlockSpec(memory_space=pl.ANY),
            pl.BlockSpec((1, 8, d), lambda i, j, *_: (i, 0, 0)),
            const((1, d)), const(win_bf.shape), const((8, d)), const(wout_bf.shape),
            const((1, d)), const((d, 2 * LANES)), const((1, LANES))],
        out_specs=[pl.BlockSpec(s.block_shape, lambda i, j, *_, f=s.index_map: f(i, j)) for s in specs],
        scratch_shapes=_combine_scratch(d) + [pltpu.VMEM((MOE_BLOCK, d), F32), pltpu.VMEM((8, d), F32)],
    )
    return pl.pallas_call(
        _moe_combine_conv_kernel,
        grid_spec=grid_spec,
        out_shape=shapes,
        compiler_params=_cparams("arbitrary", "arbitrary"),
        name="moe_combine_conv",
    )(*tables[:2], lpos, x, mod_prev, o_sorted, mod, g1, win_bf, cw8, wout_bf, g2, wr, br)


def _moe_sorted_experts(hn, eid, wts, cnt, w1, w3, w2, layer):
    t, d = hn.shape
    nblk = t // MOE_BLOCK
    i32 = jnp.int32
    n = cnt[:, 0, :N_EXPERTS].astype(i32)
    run = (n + CHUNK_ROWS - 1) // CHUNK_ROWS * CHUNK_ROWS
    loff = jnp.cumsum(run, axis=1) - run
    rows_e = jnp.sum(run, axis=0)
    tiles_e = (rows_e + EXPERT_TILE - 1) // EXPERT_TILE
    tile_end = jnp.cumsum(tiles_e)
    base = (tile_end - tiles_e) * EXPERT_TILE
    goff = base[None, :] + jnp.cumsum(run, axis=0) - run
    nch = run // CHUNK_ROWS
    tot = jnp.sum(nch, axis=1)
    ch_end = jnp.cumsum(nch, axis=1)
    q = jnp.arange(MAX_CHUNKS, dtype=i32)
    in_run = ((q[None, :, None] >= (ch_end - nch)[:, None, :]) & (q[None, :, None] < ch_end[:, None, :])).astype(i32)
    crow = jnp.sum(in_run * (goff[:, None, :] + (q[None, :, None] - (ch_end - nch)[:, None, :]) * CHUNK_ROWS), axis=-1)
    padn = (tiles_e * EXPERT_TILE - rows_e) // CHUNK_ROWS
    pads = base + rows_e
    tables = (crow.reshape(-1), tot, padn, pads, tile_end[-1:])
    loff_f = jnp.zeros((nblk, 1, LANES), F32).at[:, 0, :N_EXPERTS].set(loff.astype(F32))

    max_rows = t * TOP_K + nblk * N_EXPERTS * (CHUNK_ROWS - 1) + N_EXPERTS * (EXPERT_TILE - 1)
    max_tiles = -(-max_rows // EXPERT_TILE)
    tile_ids = jnp.arange(max_tiles, dtype=i32)
    n_valid = tile_end[-1]
    tile_src = jnp.minimum(tile_ids, n_valid - 1)
    tile_expert = jnp.sum((tile_src[:, None] >= tile_end[None, :]).astype(i32), axis=1)
    tile_valid = (tile_ids < n_valid).astype(i32)
    tile_first = jnp.concatenate([jnp.ones((1,), i32), (tile_expert[1:] != tile_expert[:-1]).astype(i32)])

    xs, lpos = _moe_dispatch(tables, eid, wts, hn, loff_f, max_tiles * EXPERT_TILE)
    o_sorted = _moe_experts(tile_expert, tile_valid, tile_first, tile_src, xs, w1, w3, w2, layer)
    return tables, lpos, o_sorted


def _router_pack(wg, bg, we, be):
    d = wg.shape[0]
    wr = jnp.zeros((d, LANES), F32).at[:, :N_EXPERTS].set(we).at[:, N_EXPERTS:N_EXPERTS + N_GROUPS].set(wg)
    br = jnp.zeros((1, LANES), F32).at[0, :N_EXPERTS].set(be).at[0, N_EXPERTS:N_EXPERTS + N_GROUPS].set(bg)
    wr_hi = wr.astype(BF16)
    wr_lo = (wr - wr_hi.astype(F32)).astype(BF16)
    return jnp.concatenate([wr_hi, wr_lo], axis=1), br


def kernel(x, c, ada_w, ada_b, norm1_g, norm2_g, ssm_w_in, ssm_lam_re, ssm_lam_im, ssm_log_dt, ssm_b_re, ssm_b_im, ssm_c_re, ssm_c_im, ssm_d, ssm_w_glu, conv_w_in, conv_w, conv_w_out, moe_wg, moe_bg, moe_we, moe_be, moe_w1, moe_w3, moe_w2, final_g):
    b, l, d = x.shape
    depth = ada_w.shape[0]
    c8 = jnp.zeros((8, d), F32).at[:b].set(c)
    mod_all = _adaln(c8, ada_w, ada_b)[:, :b].reshape(depth, b, 6, d)
    mod_all = jnp.concatenate([mod_all, jnp.zeros((depth, b, 2, d), F32)], axis=2)
    fg = final_g.reshape(1, d)

    mod = mod_all[0]
    u4 = _s5_in(x, mod, norm1_g[0:1], ssm_w_in[0].T.astype(BF16))
    kern, f_mat, e_mat, a_pack, d_row = _s5_tables(
        ssm_lam_re[0], ssm_lam_im[0], ssm_log_dt[0], ssm_b_re[0], ssm_b_im[0],
        ssm_c_re[0], ssm_c_im[0], ssm_d[0])
    y4 = _s5_scan(u4, kern, f_mat, e_mat, a_pack, d_row)
    wr, br = _router_pack(moe_wg[0], moe_bg[0], moe_we[0], moe_be[0])
    x1, hn, eid, wts, cnt = _s5_out(y4, x, mod, ssm_w_glu[0].astype(BF16), norm2_g[0:1], wr, br)
    tables, lpos, o_sorted = _moe_sorted_experts(hn, eid, wts, cnt, moe_w1, moe_w3, moe_w2, 0)

    mod1 = mod_all[1]
    cw8 = jnp.zeros((8, d), F32).at[:conv_w.shape[1]].set(conv_w[0])
    wr, br = _router_pack(moe_wg[1], moe_bg[1], moe_we[1], moe_be[1])
    x3, hn, eid, wts, cnt = _moe_combine_conv(
        tables, lpos, x1, mod, o_sorted, mod1, norm1_g[1:2], conv_w_in[0].astype(BF16), cw8,
        conv_w_out[0].astype(BF16), norm2_g[1:2], wr, br)
    tables, lpos, o_sorted = _moe_sorted_experts(hn, eid, wts, cnt, moe_w1, moe_w3, moe_w2, 1)
    return _moe_combine_final(tables, lpos, x3, mod1, fg, o_sorted)
```

```python
import functools
import math

import jax
import jax.numpy as jnp
from jax import lax
from jax.experimental import pallas as pl
from jax.experimental.pallas import tpu as pltpu

F32 = jnp.float32
BF16 = jnp.bfloat16
HIGHEST = lax.Precision.HIGHEST

RMS_EPS = 1e-6
SSM_GROUP = 16
SSM_CHUNK = 128
N_GROUPS = 4
EXPERTS_PER_GROUP = 8
N_EXPERTS = N_GROUPS * EXPERTS_PER_GROUP
TOP_K = 2
LANES = 128
MXU_WIDTH = 256
TOKEN_TILE = 512
S5_TILE = 1024
S5_K_BLOCKS = 2
SUB_ROWS = 256
MOE_BLOCK = TOKEN_TILE
CHUNK_ROWS = 8
LOCAL_ROWS = -(-(TOP_K * MOE_BLOCK + N_EXPERTS * (CHUNK_ROWS - 1)) // LANES) * LANES
EXPERT_TILE = 512
VMEM_LIMIT = 56 * 1024 * 1024
NEG_INF = -1e30


def _cparams(*sem):
    return pltpu.CompilerParams(dimension_semantics=sem, vmem_limit_bytes=VMEM_LIMIT)


def _adaln_kernel(c_ref, w_ref, b_ref, o_ref):
    c = c_ref[...]
    cond = c * jax.nn.sigmoid(c)
    o_ref[0] = jnp.dot(cond, w_ref[0], precision=HIGHEST, preferred_element_type=F32) + b_ref[0]


def _adaln(c8, ada_w, ada_b):
    depth, d, n = ada_w.shape
    tn = 1536
    return pl.pallas_call(
        _adaln_kernel,
        grid=(depth, n // tn),
        in_specs=[pl.BlockSpec((8, d), lambda i, j: (0, 0)),
                  pl.BlockSpec((1, d, tn), lambda i, j: (i, 0, j)),
                  pl.BlockSpec((1, 1, tn), lambda i, j: (i, 0, j))],
        out_specs=pl.BlockSpec((1, 8, tn), lambda i, j: (i, 0, j)),
        out_shape=jax.ShapeDtypeStruct((depth, 8, n), F32),
        compiler_params=_cparams("parallel", "parallel"),
        name="adaln",
    )(c8, ada_w, ada_b.reshape(depth, 1, n))


def _norm_mod(x, g, shift, scale):
    y = x * lax.rsqrt(jnp.mean(x * x, axis=-1, keepdims=True) + RMS_EPS)
    return (y * g) * (1.0 + scale) + shift


def _route(hn, hn_hi, wr_ref, br_ref, eid_ref, wts_ref):
    hn_lo = (hn - hn_hi.astype(F32)).astype(BF16)
    both = jnp.dot(hn_hi, wr_ref[...], preferred_element_type=F32)
    logits = (both[:, :LANES] + both[:, LANES:]
              + jnp.dot(hn_lo, wr_ref[:, :LANES], preferred_element_type=F32)) + br_ref[...]
    lane = lax.broadcasted_iota(jnp.int32, logits.shape, 1)
    is_grp = (lane >= N_EXPERTS) & (lane < N_EXPERTS + N_GROUPS)
    lg = jnp.where(is_grp, logits, NEG_INF)
    gmax = jnp.max(lg, axis=-1, keepdims=True)
    gsum = jnp.sum(jnp.where(is_grp, jnp.exp(lg - gmax), 0.0), axis=-1, keepdims=True)
    gp = 1.0 / gsum
    gi = jnp.min(jnp.where(lg == gmax, lane, 2 * LANES), axis=-1, keepdims=True) - N_EXPERTS
    in_grp = (lane < N_EXPERTS) & ((lane // EXPERTS_PER_GROUP) == gi)
    le = jnp.where(in_grp, logits, NEG_INF)
    v1 = jnp.max(le, axis=-1, keepdims=True)
    i1 = jnp.min(jnp.where(le == v1, lane, 2 * LANES), axis=-1, keepdims=True)
    le2 = jnp.where(lane == i1, NEG_INF, le)
    v2 = jnp.max(le2, axis=-1, keepdims=True)
    i2 = jnp.min(jnp.where(le2 == v2, lane, 2 * LANES), axis=-1, keepdims=True)
    e2 = jnp.exp(v2 - v1)
    den = 1.0 + e2
    col = lax.broadcasted_iota(jnp.int32, (hn.shape[0], TOP_K), 1)
    eid_ref[...] = jnp.where(col == 0, i1, i2)
    wts_ref[...] = jnp.where(col == 0, gp / den, gp * e2 / den)
    chosen = ((lane == i1) | (lane == i2)).astype(F32)
    return jnp.sum(chosen, axis=0, keepdims=True)


def _residual_and_route(x, out, mod, g2_ref, wr_ref, br_ref, x_out_ref, hn_ref, eid_ref, wts_ref):
    x1 = x + (1.0 + mod[2:3]) * out
    x_out_ref[...] = x1
    hn = _norm_mod(x1, g2_ref[...], mod[3:4], mod[4:5])
    hn_hi = hn.astype(BF16)
    hn_ref[...] = hn_hi
    return _route(hn, hn_hi, wr_ref, br_ref, eid_ref, wts_ref)


def _s5_in_kernel(x_ref, mod_ref, g_ref, wt_ref, u_ref):
    mod = mod_ref[0]
    hn = _norm_mod(x_ref[0], g_ref[...], mod[0:1], mod[1:2])
    ut = lax.dot_general(wt_ref[...], hn.astype(BF16), (((1,), (1,)), ((), ())), preferred_element_type=F32)
    u_ref[0] = ut.reshape(ut.shape[0], S5_TILE // SSM_CHUNK, SSM_CHUNK)


def _s5_in(x, mod, g, wt_bf):
    b, l, d = x.shape
    h = wt_bf.shape[0]
    cpt = S5_TILE // SSM_CHUNK
    return pl.pallas_call(
        _s5_in_kernel,
        grid=(b, l // S5_TILE),
        in_specs=[pl.BlockSpec((1, S5_TILE, d), lambda i, j: (i, j, 0)),
                  pl.BlockSpec((1, 8, d), lambda i, j: (i, 0, 0)),
                  pl.BlockSpec((1, d), lambda i, j: (0, 0)),
                  pl.BlockSpec((h, d), lambda i, j: (0, 0))],
        out_specs=pl.BlockSpec((1, h, cpt, SSM_CHUNK), lambda i, j: (i, 0, j, 0)),
        out_shape=jax.ShapeDtypeStruct((b, h, l // SSM_CHUNK, SSM_CHUNK), F32),
        compiler_params=_cparams("parallel", "parallel"),
        name="s5_in",
    )(x, mod, g, wt_bf)


def _s5_scan_kernel(u_ref, k_ref, kn_ref, pow_ref, coef_ref, a_ref, d_ref, y_ref,
                    m0_ref, m1_ref, acc_ref, s_ref, sw_ref, sp_ref):
    bsz, grp, n_chunks, tc = u_ref.shape
    g = pl.program_id(0)
    srow = lax.broadcasted_iota(jnp.int32, (tc, tc), 0)
    tcol = lax.broadcasted_iota(jnp.int32, (tc, tc), 1)
    causal = tcol >= srow

    def build_rows(src_ref, dst_ref, i):
        blks = []
        for j in range(grp):
            row = src_ref[0, pl.ds(i * grp + j, 1), :]
            blk = pltpu.roll(jnp.broadcast_to(row, (tc, tc)), 0, 1, stride=1, stride_axis=0)
            blks.append(jnp.where(causal, blk, 0.0))
        dst_ref[pl.ds(pl.multiple_of(i * tc, tc), tc), :] = jnp.concatenate(blks, axis=1).astype(BF16)

    @pl.when(g == 0)
    def _():
        def first(i, carry):
            build_rows(k_ref, m0_ref, i)
            return carry

        lax.fori_loop(0, grp, first, 0)

    x = jnp.concatenate(
        [jnp.concatenate([u_ref[b, k] for k in range(grp)], axis=1) for b in range(bsz)], axis=0)
    xb = x.astype(BF16)
    def expand(pa, pb, ca, cb):
        return jnp.concatenate(
            [(pow_ref[0, pa] * coef_ref[0, ca, i:i + 1, :] + pow_ref[0, pb] * coef_ref[0, cb, i:i + 1, :]).astype(BF16)
             for i in range(grp)], axis=0)

    f_mat = expand(0, 1, 0, 1)
    s_loc = jnp.dot(xb, f_mat, preferred_element_type=F32)
    s_ref[...] = s_loc
    half = s_loc.shape[1] // 2
    sw_ref[...] = jnp.concatenate([s_loc[:, half:], s_loc[:, :half]], axis=1)
    a1 = a_ref[0, 0:1, :]
    a2 = a_ref[0, 1:2, :]
    s = [jnp.zeros((1, s_loc.shape[1]), F32) for _ in range(bsz)]
    sw = [jnp.zeros((1, s_loc.shape[1]), F32) for _ in range(bsz)]
    for c in range(n_chunks):
        for b in range(bsz):
            r = b * n_chunks + c
            sp_ref[r:r + 1, :] = s[b]
            s_new = a1 * s[b] + a2 * sw[b] + s_ref[r:r + 1, :]
            sw[b] = a1 * sw[b] - a2 * s[b] + sw_ref[r:r + 1, :]
            s[b] = s_new
    et_mat = expand(2, 3, 2, 3)
    acc_ref[...] = lax.dot_general(sp_ref[...].astype(BF16), et_mat, (((1,), (1,)), ((), ())),
                                   preferred_element_type=F32) + d_ref[0] * x

    def run(m_cur_ref, m_next_ref):
        def step(c, carry):
            for h in range(S5_K_BLOCKS):
                build_rows(kn_ref, m_next_ref, c * S5_K_BLOCKS + h)
            xc = jnp.concatenate(
                [jnp.concatenate([u_ref[b, c * S5_K_BLOCKS + h] for h in range(S5_K_BLOCKS)], axis=1)
                 for b in range(bsz)], axis=0).astype(BF16)
            rows = pl.ds(pl.multiple_of(c * (S5_K_BLOCKS * tc), S5_K_BLOCKS * tc), S5_K_BLOCKS * tc)
            acc_ref[...] += jnp.dot(xc, m_cur_ref[rows, :], preferred_element_type=F32)
            return carry

        lax.fori_loop(0, grp // S5_K_BLOCKS, step, 0)

    @pl.when(g % 2 == 0)
    def _():
        run(m0_ref, m1_ref)

    @pl.when(g % 2 == 1)
    def _():
        run(m1_ref, m0_ref)
    for b in range(bsz):
        for j in range(grp):
            y_ref[b, j] = acc_ref[b * n_chunks:(b + 1) * n_chunks, j * tc:(j + 1) * tc]


def _s5_scan(u4, kern, pow_tab, coef_tab, a_pack, d_row):
    b, h, nc, tc = u4.shape
    g = kern.shape[0]
    grp = h // g
    k = grp * tc
    p2 = pow_tab.shape[3]
    rows = b * nc
    return pl.pallas_call(
        _s5_scan_kernel,
        grid=(g,),
        in_specs=[pl.BlockSpec((b, grp, nc, tc), lambda i: (0, i, 0, 0)),
                  pl.BlockSpec((1, grp * grp, tc), lambda i: (i, 0, 0)),
                  pl.BlockSpec((1, grp * grp, tc), lambda i: (jnp.minimum(i + 1, g - 1), 0, 0)),
                  pl.BlockSpec((1, 4, tc, p2), lambda i: (i, 0, 0, 0)),
                  pl.BlockSpec((1, 4, grp, p2), lambda i: (i, 0, 0, 0)),
                  pl.BlockSpec((1, 8, p2), lambda i: (i, 0, 0)),
                  pl.BlockSpec((1, 1, k), lambda i: (i, 0, 0))],
        out_specs=pl.BlockSpec((b, grp, nc, tc), lambda i: (0, i, 0, 0)),
        out_shape=jax.ShapeDtypeStruct(u4.shape, F32),
        scratch_shapes=[pltpu.VMEM((k, k), BF16), pltpu.VMEM((k, k), BF16),
                        pltpu.VMEM((rows, k), F32), pltpu.VMEM((rows, p2), F32),
                        pltpu.VMEM((rows, p2), F32), pltpu.VMEM((rows, p2), F32)],
        compiler_params=_cparams("arbitrary"),
        name="s5_scan",
    )(u4, kern, kern, pow_tab, coef_tab, a_pack, d_row)


def _s5_tables(lam_re, lam_im, log_dt, b_re, b_im, c_re, c_im, d_skip):
    g, p = lam_re.shape
    k = SSM_GROUP
    tc = SSM_CHUNK
    dt = jnp.exp(log_dt)[:, None]
    mag = jnp.exp(lam_re * dt)
    ab_re = mag * jnp.cos(lam_im * dt)
    ab_im = mag * jnp.sin(lam_im * dt)
    den = lam_re * lam_re + lam_im * lam_im
    cf_re = ((ab_re - 1) * lam_re + ab_im * lam_im) / den
    cf_im = (ab_im * lam_re - (ab_re - 1) * lam_im) / den
    bb_re = cf_re[..., None] * b_re - cf_im[..., None] * b_im
    bb_im = cf_re[..., None] * b_im + cf_im[..., None] * b_re
    lags = jnp.arange(tc + 1, dtype=F32)[:, None, None]
    pmag = jnp.exp(lags * (lam_re * dt)[None])
    pang = lags * (lam_im * dt)[None]
    p_re = pmag * jnp.cos(pang)
    p_im = pmag * jnp.sin(pang)
    cb_re = jnp.einsum('gjp,gpi->gpij', c_re, bb_re) - jnp.einsum('gjp,gpi->gpij', c_im, bb_im)
    cb_im = jnp.einsum('gjp,gpi->gpij', c_re, bb_im) + jnp.einsum('gjp,gpi->gpij', c_im, bb_re)
    kern = (jnp.einsum('lgp,gpij->gijl', p_re[:tc], cb_re, precision=HIGHEST)
            - jnp.einsum('lgp,gpij->gijl', p_im[:tc], cb_im, precision=HIGHEST))
    kern = kern.reshape(g, k * k, tc)
    lre = jnp.concatenate([lam_re * dt, lam_re * dt], axis=-1)
    lim = jnp.concatenate([lam_im * dt, lam_im * dt], axis=-1)
    sign = jnp.concatenate([-jnp.ones((p,), F32), jnp.ones((p,), F32)])
    pw_f = (tc - 1 - jnp.arange(tc, dtype=F32))[None, :, None]
    mag_f = jnp.exp(pw_f * lre[:, None, :])
    pr2 = mag_f * jnp.cos(pw_f * lim[:, None, :])
    pi2 = mag_f * jnp.sin(pw_f * lim[:, None, :]) * sign
    bb1 = jnp.concatenate([bb_re, bb_im], axis=1).transpose(0, 2, 1)
    bb2 = jnp.concatenate([bb_im, bb_re], axis=1).transpose(0, 2, 1)
    pw_e = (jnp.arange(tc, dtype=F32) + 1.0)[None, :, None]
    mag_e = jnp.exp(pw_e * lre[:, None, :])
    qr2 = mag_e * jnp.cos(pw_e * lim[:, None, :])
    qi2 = mag_e * jnp.sin(pw_e * lim[:, None, :])
    ca = jnp.concatenate([c_re, -c_im], axis=-1)
    cb = jnp.concatenate([-c_im, -c_re], axis=-1)
    pow_tab = jnp.stack([pr2, pi2, qr2, qi2], axis=1)
    coef_tab = jnp.stack([bb1, bb2, ca, cb], axis=1)
    ar = p_re[tc]
    ai = p_im[tc]
    a1 = jnp.concatenate([ar, ar], axis=-1)
    a2 = jnp.concatenate([-ai, ai], axis=-1)
    a_pack = jnp.concatenate([a1[:, None], a2[:, None], jnp.zeros((g, 6, 2 * p), F32)], axis=1)
    d_row = jnp.repeat(d_skip.reshape(g, k), tc, axis=1).reshape(g, 1, k * tc)
    return kern, pow_tab, coef_tab, a_pack, d_row


def _gelu_tanh(x):
    return 0.5 * x * (1.0 + jnp.tanh(math.sqrt(2.0 / math.pi) * (x + 0.044715 * (x * x * x))))


def _s5_out_kernel(y_ref, x_ref, mod_ref, w_ref, g2_ref, wr_ref, br_ref,
                   x_out_ref, hn_ref, eid_ref, wts_ref, cnt_ref):
    mod = mod_ref[0]
    h = y_ref.shape[1]
    yt = _gelu_tanh(y_ref[0].reshape(h, S5_TILE))
    d = w_ref.shape[1] // 2
    sub_rows = MOE_BLOCK
    n_sub = S5_TILE // sub_rows
    per_block = MOE_BLOCK // sub_rows

    def project(p):
        return lax.dot_general(yt[:, p * sub_rows:(p + 1) * sub_rows].astype(BF16), w_ref[...],
                               (((0,), (0,)), ((), ())), preferred_element_type=F32)

    o_next = project(0)
    cnt = None
    for p in range(n_sub):
        o = o_next
        if p + 1 < n_sub:
            o_next = project(p + 1)
        rows = slice(p * sub_rows, (p + 1) * sub_rows)
        out = o[:, :d] * jax.nn.sigmoid(o[:, d:])
        c = _residual_and_route(x_ref[0, rows, :], out, mod, g2_ref, wr_ref, br_ref,
                                x_out_ref.at[0, rows, :], hn_ref.at[rows, :], eid_ref.at[rows, :],
                                wts_ref.at[rows, :])
        cnt = c if p % per_block == 0 else cnt + c
        if p % per_block == per_block - 1:
            cnt_ref[p // per_block] = jnp.broadcast_to(cnt, cnt_ref.shape[1:])


def _mixer_out_specs(b, l, d, tile):
    nt = l // tile
    nblk = tile // MOE_BLOCK
    specs = [pl.BlockSpec((1, tile, d), lambda i, j: (i, j, 0)),
             pl.BlockSpec((tile, d), lambda i, j: (i * nt + j, 0)),
             pl.BlockSpec((tile, TOP_K), lambda i, j: (i * nt + j, 0)),
             pl.BlockSpec((tile, TOP_K), lambda i, j: (i * nt + j, 0)),
             pl.BlockSpec((nblk, 8, LANES), lambda i, j: (i * nt + j, 0, 0))]
    shapes = [jax.ShapeDtypeStruct((b, l, d), F32),
              jax.ShapeDtypeStruct((b * l, d), BF16),
              jax.ShapeDtypeStruct((b * l, TOP_K), jnp.int32),
              jax.ShapeDtypeStruct((b * l, TOP_K), F32),
              jax.ShapeDtypeStruct((b * l // MOE_BLOCK, 8, LANES), F32)]
    return specs, shapes


def _s5_out(y4, x, mod, w_bf, g2, wr, br):
    b, l, d = x.shape
    h = y4.shape[1]
    specs, shapes = _mixer_out_specs(b, l, d, S5_TILE)
    return pl.pallas_call(
        _s5_out_kernel,
        grid=(b, l // S5_TILE),
        in_specs=[pl.BlockSpec((1, h, S5_TILE // SSM_CHUNK, SSM_CHUNK), lambda i, j: (i, 0, j, 0)),
                  pl.BlockSpec((1, S5_TILE, d), lambda i, j: (i, j, 0)),
                  pl.BlockSpec((1, 8, d), lambda i, j: (i, 0, 0)),
                  pl.BlockSpec(w_bf.shape, lambda i, j: (0, 0), pipeline_mode=pl.Buffered(1)),
                  pl.BlockSpec((1, d), lambda i, j: (0, 0)),
                  pl.BlockSpec((d, 2 * LANES), lambda i, j: (0, 0)),
                  pl.BlockSpec((1, LANES), lambda i, j: (0, 0))],
        out_specs=specs,
        out_shape=shapes,
        compiler_params=_cparams("parallel", "parallel"),
        name="s5_out",
    )(y4, x, mod, w_bf, g2, wr, br)


def _conv_mixer_body(x_in_ref, mod_ref, g1_ref, win_ref, cw_ref, wout_ref, g2_ref, wr_ref, br_ref,
                     x_out_ref, hn_ref, eid_ref, wts_ref, cnt_ref, carry_ref):
    @pl.when(pl.program_id(1) == 0)
    def _():
        carry_ref[...] = jnp.zeros_like(carry_ref)

    mod = mod_ref[0]
    d = x_in_ref.shape[1]
    cw = cw_ref[...]
    n_sub = x_in_ref.shape[0] // SUB_ROWS
    sub = [slice(s * SUB_ROWS, (s + 1) * SUB_ROWS) for s in range(n_sub)]

    def project(s):
        hn = _norm_mod(x_in_ref[sub[s], :], g1_ref[...], mod[0:1], mod[1:2])
        return jnp.dot(hn.astype(BF16), win_ref[...], preferred_element_type=F32)

    def gate_conv(p, prev_tail):
        u = p[:, d:2 * d] * p[:, 2 * d:]
        ext = jnp.concatenate([prev_tail, u], axis=0)
        z = cw[0:1] * ext[6:6 + SUB_ROWS] + cw[1:2] * ext[7:7 + SUB_ROWS] + cw[2:3] * u
        return (p[:, :d] * z).astype(BF16), u[SUB_ROWS - 8:]

    ps = [project(s) for s in range(n_sub)]
    tail = carry_ref[...]
    outs = []
    for s in range(n_sub):
        gated, tail = gate_conv(ps[s], tail)
        outs.append(jnp.dot(gated, wout_ref[...], preferred_element_type=F32))
    carry_ref[...] = tail
    cnt = None
    for s in range(n_sub):
        c = _residual_and_route(x_in_ref[sub[s], :], outs[s], mod, g2_ref, wr_ref, br_ref,
                                x_out_ref.at[0, sub[s], :], hn_ref.at[sub[s], :], eid_ref.at[sub[s], :],
                                wts_ref.at[sub[s], :])
        cnt = c if s == 0 else cnt + c
    cnt_ref[0] = jnp.broadcast_to(cnt, cnt_ref.shape[1:])


_TN = (((0,), (0,)), ((), ()))


MAX_CHUNKS = LOCAL_ROWS // CHUNK_ROWS


def _rows_copy(src, s_row, dst, d_row, rows, sem):
    return pltpu.make_async_copy(src.at[pl.ds(s_row, rows), :], dst.at[pl.ds(d_row, rows), :], sem)


def _start_block_chunks(blk, count, crow_ref, make_copy):
    def body(q, carry):
        make_copy(pl.multiple_of(q * CHUNK_ROWS, CHUNK_ROWS),
                  pl.multiple_of(crow_ref[blk * MAX_CHUNKS + q], CHUNK_ROWS)).start()
        return carry

    lax.fori_loop(0, count, body, 0)


def _wait_chunks(count, max_count, make_wait):
    for k in range(max_count.bit_length()):
        @pl.when(((count >> k) & 1) == 1)
        def _():
            make_wait(CHUNK_ROWS << k).wait()


def _slot_masks(lp0, lp1):
    slot = lax.broadcasted_iota(jnp.int32, (lp0.shape[0], LOCAL_ROWS), 1)
    return slot == lp0, slot == lp1


_HI16 = -65536


def _pack_bf16_pairs(x):
    h = x.shape[1] // 2
    lo = lax.shift_right_logical(pltpu.bitcast(x[:, :h], jnp.int32), 16)
    hi = pltpu.bitcast(x[:, h:], jnp.int32) & _HI16
    return hi | lo


def _unpack_bf16_pairs(w):
    lo = pltpu.bitcast(w << 16, F32).astype(BF16)
    hi = pltpu.bitcast(w & _HI16, F32).astype(BF16)
    return lo, hi


def _moe_dispatch_kernel(crow_ref, tot_ref, padn_ref, pads_ref, nv_ref,
                         eid_ref, wts_ref, hn_ref, lofff_ref, xs_ref, lpos_ref, loc_ref, zero_ref, sems):
    blk = pl.program_id(0)
    d = hn_ref.shape[1]
    eid = eid_ref[...]
    t = eid.shape[0]
    lane = lax.broadcasted_iota(jnp.int32, (t, LANES), 1)
    hit0 = lane == eid[:, 0:1]
    hit1 = lane == eid[:, 1:2]
    onehot = (hit0 | hit1).astype(BF16)
    r = lax.broadcasted_iota(jnp.int32, (t, t), 0)
    c = lax.broadcasted_iota(jnp.int32, (t, t), 1)
    tri = (c < r).astype(BF16)
    before = jnp.dot(tri, onehot, preferred_element_type=F32) + lofff_ref[0]
    lp0 = jnp.sum(jnp.where(hit0, before, 0.0), axis=-1, keepdims=True).astype(jnp.int32)
    lp1 = jnp.sum(jnp.where(hit1, before, 0.0), axis=-1, keepdims=True).astype(jnp.int32)
    col = lax.broadcasted_iota(jnp.int32, (t, TOP_K), 1)
    lpos_ref[...] = jnp.where(col == 0, lp0, lp1)
    wts = wts_ref[...]

    buf = blk % 2
    loc = loc_ref.at[buf]

    def drain(which, count):
        _wait_chunks(count, MAX_CHUNKS,
                     lambda rows: _rows_copy(loc_ref.at[which], 0, xs_ref, 0, rows, sems.at[which]))

    @pl.when(blk >= 2)
    def _():
        drain(buf, tot_ref[blk - 2])

    m0, m1 = _slot_masks(lp0, lp1)
    wrow = jnp.sum(jnp.where(m0, wts[:, 0:1], 0.0) + jnp.where(m1, wts[:, 1:2], 0.0), axis=0, keepdims=True)
    sorted_rows = lax.dot_general((m0 | m1).astype(BF16), hn_ref[...], _TN, preferred_element_type=F32)
    loc[:, :d // 2] = _pack_bf16_pairs(sorted_rows)
    loc[:, d // 2:] = pltpu.bitcast(jnp.broadcast_to(wrow, (LANES, LOCAL_ROWS)).T, jnp.int32)

    _start_block_chunks(blk, tot_ref[blk], crow_ref,
                        lambda lo, go: _rows_copy(loc, lo, xs_ref, go, CHUNK_ROWS, sems.at[buf]))

    @pl.when(blk == pl.num_programs(0) - 1)
    def _():
        drain(1 - buf, tot_ref[blk - 1])
        drain(buf, tot_ref[blk])
        sem = sems.at[0]
        zero_ref[...] = jnp.zeros_like(zero_ref)
        pad_bits = (EXPERT_TILE // CHUNK_ROWS - 1).bit_length()

        def pad_copies(e, carry, *, wait):
            padn = padn_ref[e]
            for k in range(pad_bits):
                @pl.when(((padn >> k) & 1) == 1)
                def _():
                    done = (padn & ((1 << k) - 1)) * CHUNK_ROWS
                    cp = _rows_copy(zero_ref, 0, xs_ref, pl.multiple_of(pads_ref[e] + done, CHUNK_ROWS),
                                    CHUNK_ROWS << k, sem)
                    if wait:
                        cp.wait()
                    else:
                        cp.start()
            return carry

        lax.fori_loop(0, N_EXPERTS, functools.partial(pad_copies, wait=False), 0)
        lax.fori_loop(0, N_EXPERTS, functools.partial(pad_copies, wait=True), 0)

        def tile_copy(i):
            return pltpu.make_async_copy(zero_ref, xs_ref.at[pl.ds(pl.multiple_of(i * EXPERT_TILE, EXPERT_TILE),
                                                                   EXPERT_TILE), :], sem)

        n_tiles = xs_ref.shape[0] // EXPERT_TILE

        def start_tile(i, c):
            tile_copy(i).start()
            return c

        def wait_tile(i, c):
            tile_copy(i).wait()
            return c

        lax.fori_loop(nv_ref[0], n_tiles, start_tile, 0)
        lax.fori_loop(nv_ref[0], n_tiles, wait_tile, 0)


def _moe_dispatch(tables, eid, wts, hn, loff_f, n_rows):
    t, d = hn.shape
    width = d // 2 + LANES
    grid_spec = pltpu.PrefetchScalarGridSpec(
        num_scalar_prefetch=5,
        grid=(t // MOE_BLOCK,),
        in_specs=[pl.BlockSpec((MOE_BLOCK, TOP_K), lambda i, *_: (i, 0)),
                  pl.BlockSpec((MOE_BLOCK, TOP_K), lambda i, *_: (i, 0)),
                  pl.BlockSpec((MOE_BLOCK, d), lambda i, *_: (i, 0)),
                  pl.BlockSpec((1, 1, LANES), lambda i, *_: (i, 0, 0))],
        out_specs=[pl.BlockSpec(memory_space=pl.ANY),
                   pl.BlockSpec((MOE_BLOCK, TOP_K), lambda i, *_: (i, 0))],
        scratch_shapes=[pltpu.VMEM((2, LOCAL_ROWS, width), jnp.int32), pltpu.VMEM((EXPERT_TILE, width), jnp.int32),
                        pltpu.SemaphoreType.DMA((2,))],
    )
    return pl.pallas_call(
        _moe_dispatch_kernel,
        grid_spec=grid_spec,
        out_shape=[jax.ShapeDtypeStruct((n_rows, width), jnp.int32),
                   jax.ShapeDtypeStruct((t, TOP_K), jnp.int32)],
        compiler_params=_cparams("arbitrary"),
        name="moe_dispatch",
    )(*tables, eid, wts, hn, loff_f)


def _moe_experts_kernel(te_ref, tv_ref, tf_ref, ts_ref, xs_ref, w1_ref, w3_ref, w2_ref, o_ref,
                        w1b_ref, w3b_ref, w2b_ref):
    del te_ref, ts_ref
    i = pl.program_id(0)

    @pl.when(tf_ref[i] != 0)
    def _():
        def cast_rows(r, carry):
            rows = pl.ds(pl.multiple_of(r * LANES, LANES), LANES)
            w1b_ref[rows, :] = w1_ref[0, 0, rows, :].astype(BF16)
            w3b_ref[rows, :] = w3_ref[0, 0, rows, :].astype(BF16)

            @pl.when(r < w2b_ref.shape[0] // LANES)
            def _():
                w2b_ref[rows, :] = w2_ref[0, 0, rows, :].astype(BF16)

            return carry

        lax.fori_loop(0, w1b_ref.shape[0] // LANES, cast_rows, 0)

    @pl.when(tv_ref[i] != 0)
    def _():
        half = o_ref.shape[1]
        x_lo, x_hi = _unpack_bf16_pairs(xs_ref[:, :half])
        w = pltpu.bitcast(xs_ref[:, half:half + 1], F32)
        de = w1b_ref.shape[1]
        o = None
        for c0 in range(0, de, MXU_WIDTH):
            cols = slice(c0, c0 + MXU_WIDTH)
            a = (jnp.dot(x_lo, w1b_ref[:half, cols], preferred_element_type=F32)
                 + jnp.dot(x_hi, w1b_ref[half:, cols], preferred_element_type=F32))
            b = (jnp.dot(x_lo, w3b_ref[:half, cols], preferred_element_type=F32)
                 + jnp.dot(x_hi, w3b_ref[half:, cols], preferred_element_type=F32))
            h = (a * jax.nn.sigmoid(a) * b).astype(BF16)
            part = jnp.dot(h, w2b_ref[cols, :], preferred_element_type=F32)
            o = part if o is None else o + part
        o = o * w
        o_ref[...] = _pack_bf16_pairs(o.astype(BF16).astype(F32))

    @pl.when(tv_ref[i] == 0)
    def _():
        o_ref[...] = jnp.zeros_like(o_ref)


def _moe_experts(tile_expert, tile_valid, tile_first, tile_src, xs, w1, w3, w2, layer):
    r, width = xs.shape
    d = (width - LANES) * 2
    de = w1.shape[3]
    grid_spec = pltpu.PrefetchScalarGridSpec(
        num_scalar_prefetch=4,
        grid=(r // EXPERT_TILE,),
        in_specs=[pl.BlockSpec((EXPERT_TILE, width), lambda i, te, tv, tf, ts: (ts[i], 0)),
                  pl.BlockSpec((1, 1, d, de), lambda i, te, tv, tf, ts: (layer, te[i], 0, 0)),
                  pl.BlockSpec((1, 1, d, de), lambda i, te, tv, tf, ts: (layer, te[i], 0, 0)),
                  pl.BlockSpec((1, 1, de, d), lambda i, te, tv, tf, ts: (layer, te[i], 0, 0))],
        out_specs=pl.BlockSpec((EXPERT_TILE, d // 2), lambda i, te, tv, tf, ts: (i, 0)),
        scratch_shapes=[pltpu.VMEM((d, de), BF16), pltpu.VMEM((d, de), BF16), pltpu.VMEM((de, d), BF16)],
    )
    return pl.pallas_call(
        _moe_experts_kernel,
        grid_spec=grid_spec,
        out_shape=jax.ShapeDtypeStruct((r, d // 2), jnp.int32),
        compiler_params=_cparams("arbitrary"),
        name="moe_experts",
    )(tile_expert, tile_valid, tile_first, tile_src, xs, w1, w3, w2)


def _combine_rows(crow_ref, tot_ref, lpos_ref, o_hbm_ref, loc_ref, sems, t):
    blk = pl.program_id(0) * pl.num_programs(1) + pl.program_id(1)
    n_blk = pl.num_programs(0) * pl.num_programs(1)
    buf = blk % 2

    def fetch(b, which):
        loc = loc_ref.at[which]
        loc[TOP_K * t:, :] = jnp.zeros((LOCAL_ROWS - TOP_K * t, loc_ref.shape[2]), jnp.int32)
        _start_block_chunks(b, tot_ref[b], crow_ref,
                            lambda lo, go: _rows_copy(o_hbm_ref, go, loc, lo, CHUNK_ROWS, sems.at[which]))

    @pl.when(blk == 0)
    def _():
        fetch(blk, buf)

    @pl.when(blk + 1 < n_blk)
    def _():
        fetch(blk + 1, 1 - buf)

    _wait_chunks(tot_ref[blk], MAX_CHUNKS,
                 lambda rows: _rows_copy(o_hbm_ref, 0, loc_ref.at[buf], 0, rows, sems.at[buf]))
    lp = lpos_ref[...]
    m0, m1 = _slot_masks(lp[:, 0:1], lp[:, 1:2])
    pt = (m0 | m1).astype(BF16)
    o_lo, o_hi = _unpack_bf16_pairs(loc_ref[buf])
    return jnp.concatenate([jnp.dot(pt, o_lo, preferred_element_type=F32),
                            jnp.dot(pt, o_hi, preferred_element_type=F32)], axis=1)


def _moe_combine_final_kernel(crow_ref, tot_ref, lpos_ref, x_ref, mod_ref, fg_ref, o_hbm_ref,
                              out_ref, loc_ref, sems):
    y = _combine_rows(crow_ref, tot_ref, lpos_ref, o_hbm_ref, loc_ref, sems, x_ref.shape[1])
    x2 = x_ref[0] + (1.0 + mod_ref[0, 5:6]) * y
    out_ref[0] = (x2 * lax.rsqrt(jnp.mean(x2 * x2, axis=-1, keepdims=True) + RMS_EPS)) * fg_ref[...]


def _moe_combine_conv_kernel(crow_ref, tot_ref, lpos_ref, x_ref, mod_prev_ref, o_hbm_ref,
                             mod_ref, g1_ref, win_ref, cw_ref, wout_ref, g2_ref, wr_ref, br_ref,
                             x_out_ref, hn_ref, eid_ref, wts_ref, cnt_ref,
                             loc_ref, sems, xmid_ref, carry_ref):
    y = _combine_rows(crow_ref, tot_ref, lpos_ref, o_hbm_ref, loc_ref, sems, x_ref.shape[1])
    xmid_ref[...] = x_ref[0] + (1.0 + mod_prev_ref[0, 5:6]) * y
    _conv_mixer_body(xmid_ref, mod_ref, g1_ref, win_ref, cw_ref, wout_ref, g2_ref, wr_ref, br_ref,
                     x_out_ref, hn_ref, eid_ref, wts_ref, cnt_ref, carry_ref)


def _combine_in_specs(nt, d):
    return [pl.BlockSpec((MOE_BLOCK, TOP_K), lambda i, j, *_: (i * nt + j, 0)),
            pl.BlockSpec((1, MOE_BLOCK, d), lambda i, j, *_: (i, j, 0)),
            pl.BlockSpec((1, 8, d), lambda i, j, *_: (i, 0, 0))]


def _combine_scratch(d):
    return [pltpu.VMEM((2, LOCAL_ROWS, d // 2), jnp.int32), pltpu.SemaphoreType.DMA((2,))]


def _moe_combine_final(tables, lpos, x, mod, final_g, o_sorted):
    b, l, d = x.shape
    nt = l // MOE_BLOCK
    grid_spec = pltpu.PrefetchScalarGridSpec(
        num_scalar_prefetch=2,
        grid=(b, nt),
        in_specs=_combine_in_specs(nt, d) + [pl.BlockSpec((1, d), lambda i, j, *_: (0, 0)),
                                             pl.BlockSpec(memory_space=pl.ANY)],
        out_specs=pl.BlockSpec((1, MOE_BLOCK, d), lambda i, j, *_: (i, j, 0)),
        scratch_shapes=_combine_scratch(d),
    )
    return pl.pallas_call(
        _moe_combine_final_kernel,
        grid_spec=grid_spec,
        out_shape=jax.ShapeDtypeStruct((b, l, d), F32),
        compiler_params=_cparams("arbitrary", "arbitrary"),
        name="moe_combine_final",
    )(*tables[:2], lpos, x, mod, final_g, o_sorted)


def _moe_combine_conv(tables, lpos, x, mod_prev, o_sorted, mod, g1, win_bf, cw8, wout_bf, g2, wr, br):
    b, l, d = x.shape
    nt = l // MOE_BLOCK
    specs, shapes = _mixer_out_specs(b, l, d, TOKEN_TILE)
    const = lambda shape: pl.BlockSpec(shape, lambda i, j, *_: (0,) * len(shape))
    grid_spec = pltpu.PrefetchScalarGridSpec(
        num_scalar_prefetch=2,
        grid=(b, nt),
        in_specs=_combine_in_specs(nt, d) + [
            pl.BlockSpec(memory_space=pl.ANY),
            pl.BlockSpec((1, 8, d), lambda i, j, *_: (i, 0, 0)),
            const((1, d)), const(win_bf.shape), const((8, d)), const(wout_bf.shape),
            const((1, d)), const((d, 2 * LANES)), const((1, LANES))],
        out_specs=[pl.BlockSpec(s.block_shape, lambda i, j, *_, f=s.index_map: f(i, j)) for s in specs],
        scratch_shapes=_combine_scratch(d) + [pltpu.VMEM((MOE_BLOCK, d), F32), pltpu.VMEM((8, d), F32)],
    )
    return pl.pallas_call(
        _moe_combine_conv_kernel,
        grid_spec=grid_spec,
        out_shape=shapes,
        compiler_params=_cparams("arbitrary", "arbitrary"),
        name="moe_combine_conv",
    )(*tables[:2], lpos, x, mod_prev, o_sorted, mod, g1, win_bf, cw8, wout_bf, g2, wr, br)


def _moe_sorted_experts(hn, eid, wts, cnt, w1, w3, w2, layer):
    t, d = hn.shape
    nblk = t // MOE_BLOCK
    i32 = jnp.int32
    n = cnt[:, 0, :N_EXPERTS].astype(i32)
    run = (n + CHUNK_ROWS - 1) // CHUNK_ROWS * CHUNK_ROWS
    loff = jnp.cumsum(run, axis=1) - run
    rows_e = jnp.sum(run, axis=0)
    tiles_e = (rows_e + EXPERT_TILE - 1) // EXPERT_TILE
    tile_end = jnp.cumsum(tiles_e)
    base = (tile_end - tiles_e) * EXPERT_TILE
    goff = base[None, :] + jnp.cumsum(run, axis=0) - run
    nch = run // CHUNK_ROWS
    tot = jnp.sum(nch, axis=1)
    ch_end = jnp.cumsum(nch, axis=1)
    q = jnp.arange(MAX_CHUNKS, dtype=i32)
    in_run = ((q[None, :, None] >= (ch_end - nch)[:, None, :]) & (q[None, :, None] < ch_end[:, None, :])).astype(i32)
    crow = jnp.sum(in_run * (goff[:, None, :] + (q[None, :, None] - (ch_end - nch)[:, None, :]) * CHUNK_ROWS), axis=-1)
    padn = (tiles_e * EXPERT_TILE - rows_e) // CHUNK_ROWS
    pads = base + rows_e
    tables = (crow.reshape(-1), tot, padn, pads, tile_end[-1:])
    loff_f = jnp.zeros((nblk, 1, LANES), F32).at[:, 0, :N_EXPERTS].set(loff.astype(F32))

    max_rows = t * TOP_K + nblk * N_EXPERTS * (CHUNK_ROWS - 1) + N_EXPERTS * (EXPERT_TILE - 1)
    max_tiles = -(-max_rows // EXPERT_TILE)
    tile_ids = jnp.arange(max_tiles, dtype=i32)
    n_valid = tile_end[-1]
    tile_src = jnp.minimum(tile_ids, n_valid - 1)
    tile_expert = jnp.sum((tile_src[:, None] >= tile_end[None, :]).astype(i32), axis=1)
    tile_valid = (tile_ids < n_valid).astype(i32)
    tile_first = jnp.concatenate([jnp.ones((1,), i32), (tile_expert[1:] != tile_expert[:-1]).astype(i32)])

    xs, lpos = _moe_dispatch(tables, eid, wts, hn, loff_f, max_tiles * EXPERT_TILE)
    o_sorted = _moe_experts(tile_expert, tile_valid, tile_first, tile_src, xs, w1, w3, w2, layer)
    return tables, lpos, o_sorted


def _router_pack(wg, bg, we, be):
    d = wg.shape[0]
    wr = jnp.zeros((d, LANES), F32).at[:, :N_EXPERTS].set(we).at[:, N_EXPERTS:N_EXPERTS + N_GROUPS].set(wg)
    br = jnp.zeros((1, LANES), F32).at[0, :N_EXPERTS].set(be).at[0, N_EXPERTS:N_EXPERTS + N_GROUPS].set(bg)
    wr_hi = wr.astype(BF16)
    wr_lo = (wr - wr_hi.astype(F32)).astype(BF16)
    return jnp.concatenate([wr_hi, wr_lo], axis=1), br


def kernel(x, c, ada_w, ada_b, norm1_g, norm2_g, ssm_w_in, ssm_lam_re, ssm_lam_im, ssm_log_dt, ssm_b_re, ssm_b_im, ssm_c_re, ssm_c_im, ssm_d, ssm_w_glu, conv_w_in, conv_w, conv_w_out, moe_wg, moe_bg, moe_we, moe_be, moe_w1, moe_w3, moe_w2, final_g):
    b, l, d = x.shape
    depth = ada_w.shape[0]
    c8 = jnp.zeros((8, d), F32).at[:b].set(c)
    mod_all = _adaln(c8, ada_w, ada_b)[:, :b].reshape(depth, b, 6, d)
    mod_all = jnp.concatenate([mod_all, jnp.zeros((depth, b, 2, d), F32)], axis=2)
    fg = final_g.reshape(1, d)

    mod = mod_all[0]
    u4 = _s5_in(x, mod, norm1_g[0:1], ssm_w_in[0].T.astype(BF16))
    kern, pow_tab, coef_tab, a_pack, d_row = _s5_tables(
        ssm_lam_re[0], ssm_lam_im[0], ssm_log_dt[0], ssm_b_re[0], ssm_b_im[0],
        ssm_c_re[0], ssm_c_im[0], ssm_d[0])
    y4 = _s5_scan(u4, kern, pow_tab, coef_tab, a_pack, d_row)
    wr, br = _router_pack(moe_wg[0], moe_bg[0], moe_we[0], moe_be[0])
    x1, hn, eid, wts, cnt = _s5_out(y4, x, mod, ssm_w_glu[0].astype(BF16), norm2_g[0:1], wr, br)
    tables, lpos, o_sorted = _moe_sorted_experts(hn, eid, wts, cnt, moe_w1, moe_w3, moe_w2, 0)

    mod1 = mod_all[1]
    cw8 = jnp.zeros((8, d), F32).at[:conv_w.shape[1]].set(conv_w[0])
    wr, br = _router_pack(moe_wg[1], moe_bg[1], moe_we[1], moe_be[1])
    x3, hn, eid, wts, cnt = _moe_combine_conv(
        tables, lpos, x1, mod, o_sorted, mod1, norm1_g[1:2], conv_w_in[0].astype(BF16), cw8,
        conv_w_out[0].astype(BF16), norm2_g[1:2], wr, br)
    tables, lpos, o_sorted = _moe_sorted_experts(hn, eid, wts, cnt, moe_w1, moe_w3, moe_w2, 1)
    return _moe_combine_final(tables, lpos, x3, mod1, fg, o_sorted)
```

```python
import functools
import math

import jax
import jax.numpy as jnp
from jax import lax
from jax.experimental import pallas as pl
from jax.experimental.pallas import tpu as pltpu

F32 = jnp.float32
BF16 = jnp.bfloat16
HIGHEST = lax.Precision.HIGHEST

RMS_EPS = 1e-6
SSM_GROUP = 16
SSM_CHUNK = 128
N_GROUPS = 4
EXPERTS_PER_GROUP = 8
N_EXPERTS = N_GROUPS * EXPERTS_PER_GROUP
TOP_K = 2
LANES = 128
MXU_WIDTH = 256
TOKEN_TILE = 512
S5_TILE = 1024
S5_K_BLOCKS = 2
SUB_ROWS = 256
MOE_BLOCK = TOKEN_TILE
CHUNK_ROWS = 8
LOCAL_ROWS = -(-(TOP_K * MOE_BLOCK + N_EXPERTS * (CHUNK_ROWS - 1)) // LANES) * LANES
EXPERT_TILE = 512
VMEM_LIMIT = 56 * 1024 * 1024
NEG_INF = -1e30


def _cparams(*sem):
    return pltpu.CompilerParams(dimension_semantics=sem, vmem_limit_bytes=VMEM_LIMIT)


def _adaln_kernel(c_ref, w_ref, b_ref, o_ref):
    c = c_ref[...]
    cond = c * jax.nn.sigmoid(c)
    o_ref[0] = jnp.dot(cond, w_ref[0], precision=HIGHEST, preferred_element_type=F32) + b_ref[0]


def _adaln(c8, ada_w, ada_b):
    depth, d, n = ada_w.shape
    tn = 1536
    return pl.pallas_call(
        _adaln_kernel,
        grid=(depth, n // tn),
        in_specs=[pl.BlockSpec((8, d), lambda i, j: (0, 0)),
                  pl.BlockSpec((1, d, tn), lambda i, j: (i, 0, j)),
                  pl.BlockSpec((1, 1, tn), lambda i, j: (i, 0, j))],
        out_specs=pl.BlockSpec((1, 8, tn), lambda i, j: (i, 0, j)),
        out_shape=jax.ShapeDtypeStruct((depth, 8, n), F32),
        compiler_params=_cparams("parallel", "parallel"),
        name="adaln",
    )(c8, ada_w, ada_b.reshape(depth, 1, n))


def _norm_mod(x, g, shift, scale):
    y = x * lax.rsqrt(jnp.mean(x * x, axis=-1, keepdims=True) + RMS_EPS)
    return (y * g) * (1.0 + scale) + shift


def _route(hn, hn_hi, wr_ref, br_ref, eid_ref, wts_ref):
    hn_lo = (hn - hn_hi.astype(F32)).astype(BF16)
    both = jnp.dot(hn_hi, wr_ref[...], preferred_element_type=F32)
    logits = (both[:, :LANES] + both[:, LANES:]
              + jnp.dot(hn_lo, wr_ref[:, :LANES], preferred_element_type=F32)) + br_ref[...]
    lane = lax.broadcasted_iota(jnp.int32, logits.shape, 1)
    is_grp = (lane >= N_EXPERTS) & (lane < N_EXPERTS + N_GROUPS)
    lg = jnp.where(is_grp, logits, NEG_INF)
    gmax = jnp.max(lg, axis=-1, keepdims=True)
    gsum = jnp.sum(jnp.where(is_grp, jnp.exp(lg - gmax), 0.0), axis=-1, keepdims=True)
    gp = 1.0 / gsum
    gi = jnp.min(jnp.where(lg == gmax, lane, 2 * LANES), axis=-1, keepdims=True) - N_EXPERTS
    in_grp = (lane < N_EXPERTS) & ((lane // EXPERTS_PER_GROUP) == gi)
    le = jnp.where(in_grp, logits, NEG_INF)
    v1 = jnp.max(le, axis=-1, keepdims=True)
    i1 = jnp.min(jnp.where(le == v1, lane, 2 * LANES), axis=-1, keepdims=True)
    le2 = jnp.where(lane == i1, NEG_INF, le)
    v2 = jnp.max(le2, axis=-1, keepdims=True)
    i2 = jnp.min(jnp.where(le2 == v2, lane, 2 * LANES), axis=-1, keepdims=True)
    e2 = jnp.exp(v2 - v1)
    den = 1.0 + e2
    col = lax.broadcasted_iota(jnp.int32, (hn.shape[0], TOP_K), 1)
    eid_ref[...] = jnp.where(col == 0, i1, i2)
    wts_ref[...] = jnp.where(col == 0, gp / den, gp * e2 / den)
    chosen = ((lane == i1) | (lane == i2)).astype(F32)
    return jnp.sum(chosen, axis=0, keepdims=True)


def _residual_and_route(x, out, mod, g2_ref, wr_ref, br_ref, x_out_ref, hn_ref, eid_ref, wts_ref):
    x1 = x + (1.0 + mod[2:3]) * out
    x_out_ref[...] = x1
    hn = _norm_mod(x1, g2_ref[...], mod[3:4], mod[4:5])
    hn_hi = hn.astype(BF16)
    hn_ref[...] = hn_hi
    return _route(hn, hn_hi, wr_ref, br_ref, eid_ref, wts_ref)


def _s5_in_kernel(x_ref, mod_ref, g_ref, wt_ref, u_ref):
    mod = mod_ref[0]
    hn = _norm_mod(x_ref[0], g_ref[...], mod[0:1], mod[1:2])
    ut = lax.dot_general(wt_ref[...], hn.astype(BF16), (((1,), (1,)), ((), ())), preferred_element_type=F32)
    u_ref[0] = ut.reshape(ut.shape[0], S5_TILE // SSM_CHUNK, SSM_CHUNK)


def _s5_in(x, mod, g, wt_bf):
    b, l, d = x.shape
    h = wt_bf.shape[0]
    cpt = S5_TILE // SSM_CHUNK
    return pl.pallas_call(
        _s5_in_kernel,
        grid=(b, l // S5_TILE),
        in_specs=[pl.BlockSpec((1, S5_TILE, d), lambda i, j: (i, j, 0)),
                  pl.BlockSpec((1, 8, d), lambda i, j: (i, 0, 0)),
                  pl.BlockSpec((1, d), lambda i, j: (0, 0)),
                  pl.BlockSpec((h, d), lambda i, j: (0, 0))],
        out_specs=pl.BlockSpec((1, h, cpt, SSM_CHUNK), lambda i, j: (i, 0, j, 0)),
        out_shape=jax.ShapeDtypeStruct((b, h, l // SSM_CHUNK, SSM_CHUNK), F32),
        compiler_params=_cparams("parallel", "parallel"),
        name="s5_in",
    )(x, mod, g, wt_bf)


def _s5_scan_kernel(u_ref, k_ref, kn_ref, pow_ref, coef_ref, a_ref, d_ref, y_ref,
                    m0_ref, m1_ref, acc_ref, s_ref, sw_ref, sp_ref):
    bsz, grp, n_chunks, tc = u_ref.shape
    g = pl.program_id(0)
    srow = lax.broadcasted_iota(jnp.int32, (tc, tc), 0)
    tcol = lax.broadcasted_iota(jnp.int32, (tc, tc), 1)
    causal = tcol >= srow

    def build_rows(src_ref, dst_ref, i):
        blks = []
        for j in range(grp):
            row = src_ref[0, pl.ds(i * grp + j, 1), :]
            blk = pltpu.roll(jnp.broadcast_to(row, (tc, tc)), 0, 1, stride=1, stride_axis=0)
            blks.append(jnp.where(causal, blk, 0.0))
        dst_ref[pl.ds(pl.multiple_of(i * tc, tc), tc), :] = jnp.concatenate(blks, axis=1).astype(BF16)

    @pl.when(g == 0)
    def _():
        def first(i, carry):
            build_rows(k_ref, m0_ref, i)
            return carry

        lax.fori_loop(0, grp, first, 0)

    x = jnp.concatenate(
        [jnp.concatenate([u_ref[b, k] for k in range(grp)], axis=1) for b in range(bsz)], axis=0)
    xb = x.astype(BF16)
    def expand(pa, pb, ca, cb):
        return jnp.concatenate(
            [(pow_ref[0, pa] * coef_ref[0, ca, i:i + 1, :] + pow_ref[0, pb] * coef_ref[0, cb, i:i + 1, :]).astype(BF16)
             for i in range(grp)], axis=0)

    f_mat = expand(0, 1, 0, 1)
    s_loc = jnp.dot(xb, f_mat, preferred_element_type=F32)
    s_ref[...] = s_loc
    half = s_loc.shape[1] // 2
    sw_ref[...] = jnp.concatenate([s_loc[:, half:], s_loc[:, :half]], axis=1)
    a1 = a_ref[0, 0:1, :]
    a2 = a_ref[0, 1:2, :]
    s = [jnp.zeros((1, s_loc.shape[1]), F32) for _ in range(bsz)]
    sw = [jnp.zeros((1, s_loc.shape[1]), F32) for _ in range(bsz)]
    for c in range(n_chunks):
        for b in range(bsz):
            r = b * n_chunks + c
            sp_ref[r:r + 1, :] = s[b]
            s_new = a1 * s[b] + a2 * sw[b] + s_ref[r:r + 1, :]
            sw[b] = a1 * sw[b] - a2 * s[b] + sw_ref[r:r + 1, :]
            s[b] = s_new
    et_mat = expand(2, 3, 2, 3)
    acc_ref[...] = lax.dot_general(sp_ref[...].astype(BF16), et_mat, (((1,), (1,)), ((), ())),
                                   preferred_element_type=F32) + d_ref[0] * x

    def run(m_cur_ref, m_next_ref):
        def step(c, carry):
            for h in range(S5_K_BLOCKS):
                build_rows(kn_ref, m_next_ref, c * S5_K_BLOCKS + h)
            xc = jnp.concatenate(
                [jnp.concatenate([u_ref[b, c * S5_K_BLOCKS + h] for h in range(S5_K_BLOCKS)], axis=1)
                 for b in range(bsz)], axis=0).astype(BF16)
            rows = pl.ds(pl.multiple_of(c * (S5_K_BLOCKS * tc), S5_K_BLOCKS * tc), S5_K_BLOCKS * tc)
            acc_ref[...] += jnp.dot(xc, m_cur_ref[rows, :], preferred_element_type=F32)
            return carry

        lax.fori_loop(0, grp // S5_K_BLOCKS, step, 0)

    @pl.when(g % 2 == 0)
    def _():
        run(m0_ref, m1_ref)

    @pl.when(g % 2 == 1)
    def _():
        run(m1_ref, m0_ref)
    for b in range(bsz):
        for j in range(grp):
            y_ref[b, j] = acc_ref[b * n_chunks:(b + 1) * n_chunks, j * tc:(j + 1) * tc]


def _s5_scan(u4, kern, pow_tab, coef_tab, a_pack, d_row):
    b, h, nc, tc = u4.shape
    g = kern.shape[0]
    grp = h // g
    k = grp * tc
    p2 = pow_tab.shape[3]
    rows = b * nc
    return pl.pallas_call(
        _s5_scan_kernel,
        grid=(g,),
        in_specs=[pl.BlockSpec((b, grp, nc, tc), lambda i: (0, i, 0, 0)),
                  pl.BlockSpec((1, grp * grp, tc), lambda i: (i, 0, 0)),
                  pl.BlockSpec((1, grp * grp, tc), lambda i: (jnp.minimum(i + 1, g - 1), 0, 0)),
                  pl.BlockSpec((1, 4, tc, p2), lambda i: (i, 0, 0, 0)),
                  pl.BlockSpec((1, 4, grp, p2), lambda i: (i, 0, 0, 0)),
                  pl.BlockSpec((1, 8, p2), lambda i: (i, 0, 0)),
                  pl.BlockSpec((1, 1, k), lambda i: (i, 0, 0))],
        out_specs=pl.BlockSpec((b, grp, nc, tc), lambda i: (0, i, 0, 0)),
        out_shape=jax.ShapeDtypeStruct(u4.shape, F32),
        scratch_shapes=[pltpu.VMEM((k, k), BF16), pltpu.VMEM((k, k), BF16),
                        pltpu.VMEM((rows, k), F32), pltpu.VMEM((rows, p2), F32),
                        pltpu.VMEM((rows, p2), F32), pltpu.VMEM((rows, p2), F32)],
        compiler_params=_cparams("arbitrary"),
        name="s5_scan",
    )(u4, kern, kern, pow_tab, coef_tab, a_pack, d_row)


def _s5_tables(lam_re, lam_im, log_dt, b_re, b_im, c_re, c_im, d_skip):
    g, p = lam_re.shape
    k = SSM_GROUP
    tc = SSM_CHUNK
    dt = jnp.exp(log_dt)[:, None]
    mag = jnp.exp(lam_re * dt)
    ab_re = mag * jnp.cos(lam_im * dt)
    ab_im = mag * jnp.sin(lam_im * dt)
    den = lam_re * lam_re + lam_im * lam_im
    cf_re = ((ab_re - 1) * lam_re + ab_im * lam_im) / den
    cf_im = (ab_im * lam_re - (ab_re - 1) * lam_im) / den
    bb_re = cf_re[..., None] * b_re - cf_im[..., None] * b_im
    bb_im = cf_re[..., None] * b_im + cf_im[..., None] * b_re
    lre = jnp.concatenate([lam_re * dt, lam_re * dt], axis=-1)
    lim = jnp.concatenate([lam_im * dt, lam_im * dt], axis=-1)
    lags = jnp.arange(tc + 1, dtype=F32)[None, :, None]
    pmag = jnp.exp(lags * lre[:, None, :])
    pw_re = pmag * jnp.cos(lags * lim[:, None, :])
    pw_im = pmag * jnp.sin(lags * lim[:, None, :])
    cb_re = jnp.einsum('gjp,gpi->gpij', c_re, bb_re) - jnp.einsum('gjp,gpi->gpij', c_im, bb_im)
    cb_im = jnp.einsum('gjp,gpi->gpij', c_re, bb_im) + jnp.einsum('gjp,gpi->gpij', c_im, bb_re)
    kern = (jnp.einsum('glp,gpij->gijl', pw_re[:, :tc, :p], cb_re, precision=HIGHEST)
            - jnp.einsum('glp,gpij->gijl', pw_im[:, :tc, :p], cb_im, precision=HIGHEST))
    kern = kern.reshape(g, k * k, tc)
    sign = jnp.concatenate([-jnp.ones((p,), F32), jnp.ones((p,), F32)])
    pr2 = pw_re[:, :tc][:, ::-1]
    pi2 = pw_im[:, :tc][:, ::-1] * sign
    bb1 = jnp.concatenate([bb_re, bb_im], axis=1).transpose(0, 2, 1)
    bb2 = jnp.concatenate([bb_im, bb_re], axis=1).transpose(0, 2, 1)
    qr2 = pw_re[:, 1:]
    qi2 = pw_im[:, 1:]
    ca = jnp.concatenate([c_re, -c_im], axis=-1)
    cb = jnp.concatenate([-c_im, -c_re], axis=-1)
    pow_tab = jnp.stack([pr2, pi2, qr2, qi2], axis=1)
    coef_tab = jnp.stack([bb1, bb2, ca, cb], axis=1)
    a1 = pw_re[:, tc]
    a2 = pw_im[:, tc] * sign
    a_pack = jnp.concatenate([a1[:, None], a2[:, None], jnp.zeros((g, 6, 2 * p), F32)], axis=1)
    d_row = jnp.repeat(d_skip.reshape(g, k), tc, axis=1).reshape(g, 1, k * tc)
    return kern, pow_tab, coef_tab, a_pack, d_row


def _gelu_tanh(x):
    return 0.5 * x * (1.0 + jnp.tanh(math.sqrt(2.0 / math.pi) * (x + 0.044715 * (x * x * x))))


def _s5_out_kernel(y_ref, x_ref, mod_ref, w_ref, g2_ref, wr_ref, br_ref,
                   x_out_ref, hn_ref, eid_ref, wts_ref, cnt_ref):
    mod = mod_ref[0]
    h = y_ref.shape[1]
    yt = _gelu_tanh(y_ref[0].reshape(h, S5_TILE))
    d = w_ref.shape[1] // 2
    sub_rows = MOE_BLOCK
    n_sub = S5_TILE // sub_rows
    per_block = MOE_BLOCK // sub_rows

    def project(p):
        return lax.dot_general(yt[:, p * sub_rows:(p + 1) * sub_rows].astype(BF16), w_ref[...],
                               (((0,), (0,)), ((), ())), preferred_element_type=F32)

    o_next = project(0)
    cnt = None
    for p in range(n_sub):
        o = o_next
        if p + 1 < n_sub:
            o_next = project(p + 1)
        rows = slice(p * sub_rows, (p + 1) * sub_rows)
        out = o[:, :d] * jax.nn.sigmoid(o[:, d:])
        c = _residual_and_route(x_ref[0, rows, :], out, mod, g2_ref, wr_ref, br_ref,
                                x_out_ref.at[0, rows, :], hn_ref.at[rows, :], eid_ref.at[rows, :],
                                wts_ref.at[rows, :])
        cnt = c if p % per_block == 0 else cnt + c
        if p % per_block == per_block - 1:
            cnt_ref[p // per_block] = jnp.broadcast_to(cnt, cnt_ref.shape[1:])


def _mixer_out_specs(b, l, d, tile):
    nt = l // tile
    nblk = tile // MOE_BLOCK
    specs = [pl.BlockSpec((1, tile, d), lambda i, j: (i, j, 0)),
             pl.BlockSpec((tile, d), lambda i, j: (i * nt + j, 0)),
             pl.BlockSpec((tile, TOP_K), lambda i, j: (i * nt + j, 0)),
             pl.BlockSpec((tile, TOP_K), lambda i, j: (i * nt + j, 0)),
             pl.BlockSpec((nblk, 8, LANES), lambda i, j: (i * nt + j, 0, 0))]
    shapes = [jax.ShapeDtypeStruct((b, l, d), F32),
              jax.ShapeDtypeStruct((b * l, d), BF16),
              jax.ShapeDtypeStruct((b * l, TOP_K), jnp.int32),
              jax.ShapeDtypeStruct((b * l, TOP_K), F32),
              jax.ShapeDtypeStruct((b * l // MOE_BLOCK, 8, LANES), F32)]
    return specs, shapes


def _s5_out(y4, x, mod, w_bf, g2, wr, br):
    b, l, d = x.shape
    h = y4.shape[1]
    specs, shapes = _mixer_out_specs(b, l, d, S5_TILE)
    return pl.pallas_call(
        _s5_out_kernel,
        grid=(b, l // S5_TILE),
        in_specs=[pl.BlockSpec((1, h, S5_TILE // SSM_CHUNK, SSM_CHUNK), lambda i, j: (i, 0, j, 0)),
                  pl.BlockSpec((1, S5_TILE, d), lambda i, j: (i, j, 0)),
                  pl.BlockSpec((1, 8, d), lambda i, j: (i, 0, 0)),
                  pl.BlockSpec(w_bf.shape, lambda i, j: (0, 0), pipeline_mode=pl.Buffered(1)),
                  pl.BlockSpec((1, d), lambda i, j: (0, 0)),
                  pl.BlockSpec((d, 2 * LANES), lambda i, j: (0, 0)),
                  pl.BlockSpec((1, LANES), lambda i, j: (0, 0))],
        out_specs=specs,
        out_shape=shapes,
        compiler_params=_cparams("parallel", "parallel"),
        name="s5_out",
    )(y4, x, mod, w_bf, g2, wr, br)


def _conv_mixer_body(x_in_ref, mod_ref, g1_ref, win_ref, cw_ref, wout_ref, g2_ref, wr_ref, br_ref,
                     x_out_ref, hn_ref, eid_ref, wts_ref, cnt_ref, carry_ref):
    @pl.when(pl.program_id(1) == 0)
    def _():
        carry_ref[...] = jnp.zeros_like(carry_ref)

    mod = mod_ref[0]
    d = x_in_ref.shape[1]
    cw = cw_ref[...]
    n_sub = x_in_ref.shape[0] // SUB_ROWS
    sub = [slice(s * SUB_ROWS, (s + 1) * SUB_ROWS) for s in range(n_sub)]

    def project(s):
        hn = _norm_mod(x_in_ref[sub[s], :], g1_ref[...], mod[0:1], mod[1:2])
        return jnp.dot(hn.astype(BF16), win_ref[...], preferred_element_type=F32)

    def gate_conv(p, prev_tail):
        u = p[:, d:2 * d] * p[:, 2 * d:]
        ext = jnp.concatenate([prev_tail, u], axis=0)
        z = cw[0:1] * ext[6:6 + SUB_ROWS] + cw[1:2] * ext[7:7 + SUB_ROWS] + cw[2:3] * u
        return (p[:, :d] * z).astype(BF16), u[SUB_ROWS - 8:]

    ps = [project(s) for s in range(n_sub)]
    tail = carry_ref[...]
    outs = []
    for s in range(n_sub):
        gated, tail = gate_conv(ps[s], tail)
        outs.append(jnp.dot(gated, wout_ref[...], preferred_element_type=F32))
    carry_ref[...] = tail
    cnt = None
    for s in range(n_sub):
        c = _residual_and_route(x_in_ref[sub[s], :], outs[s], mod, g2_ref, wr_ref, br_ref,
                                x_out_ref.at[0, sub[s], :], hn_ref.at[sub[s], :], eid_ref.at[sub[s], :],
                                wts_ref.at[sub[s], :])
        cnt = c if s == 0 else cnt + c
    cnt_ref[0] = jnp.broadcast_to(cnt, cnt_ref.shape[1:])


_TN = (((0,), (0,)), ((), ()))


MAX_CHUNKS = LOCAL_ROWS // CHUNK_ROWS


def _rows_copy(src, s_row, dst, d_row, rows, sem):
    return pltpu.make_async_copy(src.at[pl.ds(s_row, rows), :], dst.at[pl.ds(d_row, rows), :], sem)


def _start_block_chunks(blk, count, crow_ref, make_copy):
    def body(q, carry):
        make_copy(pl.multiple_of(q * CHUNK_ROWS, CHUNK_ROWS),
                  pl.multiple_of(crow_ref[blk * MAX_CHUNKS + q], CHUNK_ROWS)).start()
        return carry

    lax.fori_loop(0, count, body, 0)


def _wait_chunks(count, max_count, make_wait):
    for k in range(max_count.bit_length()):
        @pl.when(((count >> k) & 1) == 1)
        def _():
            make_wait(CHUNK_ROWS << k).wait()


def _slot_masks(lp0, lp1):
    slot = lax.broadcasted_iota(jnp.int32, (lp0.shape[0], LOCAL_ROWS), 1)
    return slot == lp0, slot == lp1


_HI16 = -65536


def _pack_bf16_pairs(x):
    h = x.shape[1] // 2
    lo = lax.shift_right_logical(pltpu.bitcast(x[:, :h], jnp.int32), 16)
    hi = pltpu.bitcast(x[:, h:], jnp.int32) & _HI16
    return hi | lo


def _unpack_bf16_pairs(w):
    lo = pltpu.bitcast(w << 16, F32).astype(BF16)
    hi = pltpu.bitcast(w & _HI16, F32).astype(BF16)
    return lo, hi


def _moe_dispatch_kernel(crow_ref, tot_ref, padn_ref, pads_ref, nv_ref,
                         eid_ref, wts_ref, hn_ref, lofff_ref, tri_ref, xs_ref, lpos_ref, loc_ref, zero_ref, sems):
    blk = pl.program_id(0)
    d = hn_ref.shape[1]
    eid = eid_ref[...]
    t = eid.shape[0]
    lane = lax.broadcasted_iota(jnp.int32, (t, LANES), 1)
    hit0 = lane == eid[:, 0:1]
    hit1 = lane == eid[:, 1:2]
    onehot = (hit0 | hit1).astype(BF16)
    before = jnp.dot(tri_ref[...], onehot, preferred_element_type=F32) + lofff_ref[0]
    lp0 = jnp.sum(jnp.where(hit0, before, 0.0), axis=-1, keepdims=True).astype(jnp.int32)
    lp1 = jnp.sum(jnp.where(hit1, before, 0.0), axis=-1, keepdims=True).astype(jnp.int32)
    col = lax.broadcasted_iota(jnp.int32, (t, TOP_K), 1)
    lpos_ref[...] = jnp.where(col == 0, lp0, lp1)
    wts = wts_ref[...]

    buf = blk % 2
    loc = loc_ref.at[buf]

    def drain(which, count):
        _wait_chunks(count, MAX_CHUNKS,
                     lambda rows: _rows_copy(loc_ref.at[which], 0, xs_ref, 0, rows, sems.at[which]))

    @pl.when(blk >= 2)
    def _():
        drain(buf, tot_ref[blk - 2])

    m0, m1 = _slot_masks(lp0, lp1)
    wrow = jnp.sum(jnp.where(m0, wts[:, 0:1], 0.0) + jnp.where(m1, wts[:, 1:2], 0.0), axis=0, keepdims=True)
    sorted_rows = lax.dot_general((m0 | m1).astype(BF16), hn_ref[...], _TN, preferred_element_type=F32)
    loc[:, :d // 2] = _pack_bf16_pairs(sorted_rows)
    loc[:, d // 2:] = pltpu.bitcast(jnp.broadcast_to(wrow, (LANES, LOCAL_ROWS)).T, jnp.int32)

    _start_block_chunks(blk, tot_ref[blk], crow_ref,
                        lambda lo, go: _rows_copy(loc, lo, xs_ref, go, CHUNK_ROWS, sems.at[buf]))

    @pl.when(blk == pl.num_programs(0) - 1)
    def _():
        drain(1 - buf, tot_ref[blk - 1])
        drain(buf, tot_ref[blk])
        sem = sems.at[0]
        zero_ref[...] = jnp.zeros_like(zero_ref)
        pad_bits = (EXPERT_TILE // CHUNK_ROWS - 1).bit_length()

        def pad_copies(e, carry, *, wait):
            padn = padn_ref[e]
            for k in range(pad_bits):
                @pl.when(((padn >> k) & 1) == 1)
                def _():
                    done = (padn & ((1 << k) - 1)) * CHUNK_ROWS
                    cp = _rows_copy(zero_ref, 0, xs_ref, pl.multiple_of(pads_ref[e] + done, CHUNK_ROWS),
                                    CHUNK_ROWS << k, sem)
                    if wait:
                        cp.wait()
                    else:
                        cp.start()
            return carry

        lax.fori_loop(0, N_EXPERTS, functools.partial(pad_copies, wait=False), 0)
        lax.fori_loop(0, N_EXPERTS, functools.partial(pad_copies, wait=True), 0)

        def tile_copy(i):
            return pltpu.make_async_copy(zero_ref, xs_ref.at[pl.ds(pl.multiple_of(i * EXPERT_TILE, EXPERT_TILE),
                                                                   EXPERT_TILE), :], sem)

        n_tiles = xs_ref.shape[0] // EXPERT_TILE

        def start_tile(i, c):
            tile_copy(i).start()
            return c

        def wait_tile(i, c):
            tile_copy(i).wait()
            return c

        lax.fori_loop(nv_ref[0], n_tiles, start_tile, 0)
        lax.fori_loop(nv_ref[0], n_tiles, wait_tile, 0)


def _moe_dispatch(tables, eid, wts, hn, loff_f, n_rows):
    t, d = hn.shape
    pos = jnp.arange(MOE_BLOCK, dtype=jnp.int32)
    tri = (pos[None, :] < pos[:, None]).astype(BF16)
    width = d // 2 + LANES
    grid_spec = pltpu.PrefetchScalarGridSpec(
        num_scalar_prefetch=5,
        grid=(t // MOE_BLOCK,),
        in_specs=[pl.BlockSpec((MOE_BLOCK, TOP_K), lambda i, *_: (i, 0)),
                  pl.BlockSpec((MOE_BLOCK, TOP_K), lambda i, *_: (i, 0)),
                  pl.BlockSpec((MOE_BLOCK, d), lambda i, *_: (i, 0)),
                  pl.BlockSpec((1, 1, LANES), lambda i, *_: (i, 0, 0)),
                  pl.BlockSpec((MOE_BLOCK, MOE_BLOCK), lambda i, *_: (0, 0))],
        out_specs=[pl.BlockSpec(memory_space=pl.ANY),
                   pl.BlockSpec((MOE_BLOCK, TOP_K), lambda i, *_: (i, 0))],
        scratch_shapes=[pltpu.VMEM((2, LOCAL_ROWS, width), jnp.int32), pltpu.VMEM((EXPERT_TILE, width), jnp.int32),
                        pltpu.SemaphoreType.DMA((2,))],
    )
    return pl.pallas_call(
        _moe_dispatch_kernel,
        grid_spec=grid_spec,
        out_shape=[jax.ShapeDtypeStruct((n_rows, width), jnp.int32),
                   jax.ShapeDtypeStruct((t, TOP_K), jnp.int32)],
        compiler_params=_cparams("arbitrary"),
        name="moe_dispatch",
    )(*tables, eid, wts, hn, loff_f, tri)


def _moe_experts_kernel(te_ref, tv_ref, tf_ref, ts_ref, xs_ref, w1_ref, w3_ref, w2_ref, o_ref,
                        w1b_ref, w3b_ref, w2b_ref):
    del te_ref, ts_ref
    i = pl.program_id(0)

    @pl.when(tf_ref[i] != 0)
    def _():
        def cast_rows(r, carry):
            rows = pl.ds(pl.multiple_of(r * LANES, LANES), LANES)
            w1b_ref[rows, :] = w1_ref[0, 0, rows, :].astype(BF16)
            w3b_ref[rows, :] = w3_ref[0, 0, rows, :].astype(BF16)

            @pl.when(r < w2b_ref.shape[0] // LANES)
            def _():
                w2b_ref[rows, :] = w2_ref[0, 0, rows, :].astype(BF16)

            return carry

        lax.fori_loop(0, w1b_ref.shape[0] // LANES, cast_rows, 0)

    @pl.when(tv_ref[i] != 0)
    def _():
        half = o_ref.shape[1]
        x_lo, x_hi = _unpack_bf16_pairs(xs_ref[:, :half])
        w = pltpu.bitcast(xs_ref[:, half:half + 1], F32)
        de = w1b_ref.shape[1]
        o = None
        for c0 in range(0, de, MXU_WIDTH):
            cols = slice(c0, c0 + MXU_WIDTH)
            a = (jnp.dot(x_lo, w1b_ref[:half, cols], preferred_element_type=F32)
                 + jnp.dot(x_hi, w1b_ref[half:, cols], preferred_element_type=F32))
            b = (jnp.dot(x_lo, w3b_ref[:half, cols], preferred_element_type=F32)
                 + jnp.dot(x_hi, w3b_ref[half:, cols], preferred_element_type=F32))
            h = (a * jax.nn.sigmoid(a) * b).astype(BF16)
            part = jnp.dot(h, w2b_ref[cols, :], preferred_element_type=F32)
            o = part if o is None else o + part
        o = o * w
        o_ref[...] = _pack_bf16_pairs(o.astype(BF16).astype(F32))

    @pl.when(tv_ref[i] == 0)
    def _():
        o_ref[...] = jnp.zeros_like(o_ref)


def _moe_experts(tile_expert, tile_valid, tile_first, tile_src, xs, w1, w3, w2, layer):
    r, width = xs.shape
    d = (width - LANES) * 2
    de = w1.shape[3]
    grid_spec = pltpu.PrefetchScalarGridSpec(
        num_scalar_prefetch=4,
        grid=(r // EXPERT_TILE,),
        in_specs=[pl.BlockSpec((EXPERT_TILE, width), lambda i, te, tv, tf, ts: (ts[i], 0)),
                  pl.BlockSpec((1, 1, d, de), lambda i, te, tv, tf, ts: (layer, te[i], 0, 0)),
                  pl.BlockSpec((1, 1, d, de), lambda i, te, tv, tf, ts: (layer, te[i], 0, 0)),
                  pl.BlockSpec((1, 1, de, d), lambda i, te, tv, tf, ts: (layer, te[i], 0, 0))],
        out_specs=pl.BlockSpec((EXPERT_TILE, d // 2), lambda i, te, tv, tf, ts: (i, 0)),
        scratch_shapes=[pltpu.VMEM((d, de), BF16), pltpu.VMEM((d, de), BF16), pltpu.VMEM((de, d), BF16)],
    )
    return pl.pallas_call(
        _moe_experts_kernel,
        grid_spec=grid_spec,
        out_shape=jax.ShapeDtypeStruct((r, d // 2), jnp.int32),
        compiler_params=_cparams("arbitrary"),
        name="moe_experts",
    )(tile_expert, tile_valid, tile_first, tile_src, xs, w1, w3, w2)


def _combine_rows(crow_ref, tot_ref, lpos_ref, o_hbm_ref, loc_ref, sems, t):
    blk = pl.program_id(0) * pl.num_programs(1) + pl.program_id(1)
    n_blk = pl.num_programs(0) * pl.num_programs(1)
    buf = blk % 2

    def fetch(b, which):
        loc = loc_ref.at[which]
        loc[TOP_K * t:, :] = jnp.zeros((LOCAL_ROWS - TOP_K * t, loc_ref.shape[2]), jnp.int32)
        _start_block_chunks(b, tot_ref[b], crow_ref,
                            lambda lo, go: _rows_copy(o_hbm_ref, go, loc, lo, CHUNK_ROWS, sems.at[which]))

    @pl.when(blk == 0)
    def _():
        fetch(blk, buf)

    @pl.when(blk + 1 < n_blk)
    def _():
        fetch(blk + 1, 1 - buf)

    _wait_chunks(tot_ref[blk], MAX_CHUNKS,
                 lambda rows: _rows_copy(o_hbm_ref, 0, loc_ref.at[buf], 0, rows, sems.at[buf]))
    lp = lpos_ref[...]
    m0, m1 = _slot_masks(lp[:, 0:1], lp[:, 1:2])
    pt = (m0 | m1).astype(BF16)
    o_lo, o_hi = _unpack_bf16_pairs(loc_ref[buf])
    return jnp.concatenate([jnp.dot(pt, o_lo, preferred_element_type=F32),
                            jnp.dot(pt, o_hi, preferred_element_type=F32)], axis=1)


def _moe_combine_final_kernel(crow_ref, tot_ref, lpos_ref, x_ref, mod_ref, fg_ref, o_hbm_ref,
                              out_ref, loc_ref, sems):
    y = _combine_rows(crow_ref, tot_ref, lpos_ref, o_hbm_ref, loc_ref, sems, x_ref.shape[1])
    x2 = x_ref[0] + (1.0 + mod_ref[0, 5:6]) * y
    out_ref[0] = (x2 * lax.rsqrt(jnp.mean(x2 * x2, axis=-1, keepdims=True) + RMS_EPS)) * fg_ref[...]


def _moe_combine_conv_kernel(crow_ref, tot_ref, lpos_ref, x_ref, mod_prev_ref, o_hbm_ref,
                             mod_ref, g1_ref, win_ref, cw_ref, wout_ref, g2_ref, wr_ref, br_ref,
                             x_out_ref, hn_ref, eid_ref, wts_ref, cnt_ref,
                             loc_ref, sems, xmid_ref, carry_ref):
    y = _combine_rows(crow_ref, tot_ref, lpos_ref, o_hbm_ref, loc_ref, sems, x_ref.shape[1])
    xmid_ref[...] = x_ref[0] + (1.0 + mod_prev_ref[0, 5:6]) * y
    _conv_mixer_body(xmid_ref, mod_ref, g1_ref, win_ref, cw_ref, wout_ref, g2_ref, wr_ref, br_ref,
                     x_out_ref, hn_ref, eid_ref, wts_ref, cnt_ref, carry_ref)


def _combine_in_specs(nt, d):
    return [pl.BlockSpec((MOE_BLOCK, TOP_K), lambda i, j, *_: (i * nt + j, 0)),
            pl.BlockSpec((1, MOE_BLOCK, d), lambda i, j, *_: (i, j, 0)),
            pl.BlockSpec((1, 8, d), lambda i, j, *_: (i, 0, 0))]


def _combine_scratch(d):
    return [pltpu.VMEM((2, LOCAL_ROWS, d // 2), jnp.int32), pltpu.SemaphoreType.DMA((2,))]


def _moe_combine_final(tables, lpos, x, mod, final_g, o_sorted):
    b, l, d = x.shape
    nt = l // MOE_BLOCK
    grid_spec = pltpu.PrefetchScalarGridSpec(
        num_scalar_prefetch=2,
        grid=(b, nt),
        in_specs=_combine_in_specs(nt, d) + [pl.BlockSpec((1, d), lambda i, j, *_: (0, 0)),
                                             pl.BlockSpec(memory_space=pl.ANY)],
        out_specs=pl.BlockSpec((1, MOE_BLOCK, d), lambda i, j, *_: (i, j, 0)),
        scratch_shapes=_combine_scratch(d),
    )
    return pl.pallas_call(
        _moe_combine_final_kernel,
        grid_spec=grid_spec,
        out_shape=jax.ShapeDtypeStruct((b, l, d), F32),
        compiler_params=_cparams("arbitrary", "arbitrary"),
        name="moe_combine_final",
    )(*tables[:2], lpos, x, mod, final_g, o_sorted)


def _moe_combine_conv(tables, lpos, x, mod_prev, o_sorted, mod, g1, win_bf, cw8, wout_bf, g2, wr, br):
    b, l, d = x.shape
    nt = l // MOE_BLOCK
    specs, shapes = _mixer_out_specs(b, l, d, TOKEN_TILE)
    const = lambda shape: pl.BlockSpec(shape, lambda i, j, *_: (0,) * len(shape))
    grid_spec = pltpu.PrefetchScalarGridSpec(
        num_scalar_prefetch=2,
        grid=(b, nt),
        in_specs=_combine_in_specs(nt, d) + [
            pl.BlockSpec(memory_space=pl.ANY),
            pl.BlockSpec((1, 8, d), lambda i, j, *_: (i, 0, 0)),
            const((1, d)), const(win_bf.shape), const((8, d)), const(wout_bf.shape),
            const((1, d)), const((d, 2 * LANES)), const((1, LANES))],
        out_specs=[pl.BlockSpec(s.block_shape, lambda i, j, *_, f=s.index_map: f(i, j)) for s in specs],
        scratch_shapes=_combine_scratch(d) + [pltpu.VMEM((MOE_BLOCK, d), F32), pltpu.VMEM((8, d), F32)],
    )
    return pl.pallas_call(
        _moe_combine_conv_kernel,
        grid_spec=grid_spec,
        out_shape=shapes,
        compiler_params=_cparams("arbitrary", "arbitrary"),
        name="moe_combine_conv",
    )(*tables[:2], lpos, x, mod_prev, o_sorted, mod, g1, win_bf, cw8, wout_bf, g2, wr, br)


def _moe_sorted_experts(hn, eid, wts, cnt, w1, w3, w2, layer):
    t, d = hn.shape
    nblk = t // MOE_BLOCK
    i32 = jnp.int32
    n = cnt[:, 0, :N_EXPERTS].astype(i32)
    run = (n + CHUNK_ROWS - 1) // CHUNK_ROWS * CHUNK_ROWS
    loff = jnp.cumsum(run, axis=1) - run
    rows_e = jnp.sum(run, axis=0)
    tiles_e = (rows_e + EXPERT_TILE - 1) // EXPERT_TILE
    tile_end = jnp.cumsum(tiles_e)
    base = (tile_end - tiles_e) * EXPERT_TILE
    goff = base[None, :] + jnp.cumsum(run, axis=0) - run
    nch = run // CHUNK_ROWS
    tot = jnp.sum(nch, axis=1)
    ch_end = jnp.cumsum(nch, axis=1)
    q = jnp.arange(MAX_CHUNKS, dtype=i32)
    in_run = ((q[None, :, None] >= (ch_end - nch)[:, None, :]) & (q[None, :, None] < ch_end[:, None, :])).astype(i32)
    crow = jnp.sum(in_run * (goff[:, None, :] + (q[None, :, None] - (ch_end - nch)[:, None, :]) * CHUNK_ROWS), axis=-1)
    padn = (tiles_e * EXPERT_TILE - rows_e) // CHUNK_ROWS
    pads = base + rows_e
    tables = (crow.reshape(-1), tot, padn, pads, tile_end[-1:])
    loff_f = jnp.zeros((nblk, 1, LANES), F32).at[:, 0, :N_EXPERTS].set(loff.astype(F32))

    max_rows = t * TOP_K + nblk * N_EXPERTS * (CHUNK_ROWS - 1) + N_EXPERTS * (EXPERT_TILE - 1)
    max_tiles = -(-max_rows // EXPERT_TILE)
    tile_ids = jnp.arange(max_tiles, dtype=i32)
    n_valid = tile_end[-1]
    tile_src = jnp.minimum(tile_ids, n_valid - 1)
    tile_expert = jnp.sum((tile_src[:, None] >= tile_end[None, :]).astype(i32), axis=1)
    tile_valid = (tile_ids < n_valid).astype(i32)
    tile_first = jnp.concatenate([jnp.ones((1,), i32), (tile_expert[1:] != tile_expert[:-1]).astype(i32)])

    xs, lpos = _moe_dispatch(tables, eid, wts, hn, loff_f, max_tiles * EXPERT_TILE)
    o_sorted = _moe_experts(tile_expert, tile_valid, tile_first, tile_src, xs, w1, w3, w2, layer)
    return tables, lpos, o_sorted


def _router_pack(wg, bg, we, be):
    d = wg.shape[0]
    wr = jnp.zeros((d, LANES), F32).at[:, :N_EXPERTS].set(we).at[:, N_EXPERTS:N_EXPERTS + N_GROUPS].set(wg)
    br = jnp.zeros((1, LANES), F32).at[0, :N_EXPERTS].set(be).at[0, N_EXPERTS:N_EXPERTS + N_GROUPS].set(bg)
    wr_hi = wr.astype(BF16)
    wr_lo = (wr - wr_hi.astype(F32)).astype(BF16)
    return jnp.concatenate([wr_hi, wr_lo], axis=1), br


def kernel(x, c, ada_w, ada_b, norm1_g, norm2_g, ssm_w_in, ssm_lam_re, ssm_lam_im, ssm_log_dt, ssm_b_re, ssm_b_im, ssm_c_re, ssm_c_im, ssm_d, ssm_w_glu, conv_w_in, conv_w, conv_w_out, moe_wg, moe_bg, moe_we, moe_be, moe_w1, moe_w3, moe_w2, final_g):
    b, l, d = x.shape
    depth = ada_w.shape[0]
    c8 = jnp.zeros((8, d), F32).at[:b].set(c)
    mod_all = _adaln(c8, ada_w, ada_b)[:, :b].reshape(depth, b, 6, d)
    mod_all = jnp.concatenate([mod_all, jnp.zeros((depth, b, 2, d), F32)], axis=2)
    fg = final_g.reshape(1, d)

    mod = mod_all[0]
    u4 = _s5_in(x, mod, norm1_g[0:1], ssm_w_in[0].T.astype(BF16))
    kern, pow_tab, coef_tab, a_pack, d_row = _s5_tables(
        ssm_lam_re[0], ssm_lam_im[0], ssm_log_dt[0], ssm_b_re[0], ssm_b_im[0],
        ssm_c_re[0], ssm_c_im[0], ssm_d[0])
    y4 = _s5_scan(u4, kern, pow_tab, coef_tab, a_pack, d_row)
    wr, br = _router_pack(moe_wg[0], moe_bg[0], moe_we[0], moe_be[0])
    x1, hn, eid, wts, cnt = _s5_out(y4, x, mod, ssm_w_glu[0].astype(BF16), norm2_g[0:1], wr, br)
    tables, lpos, o_sorted = _moe_sorted_experts(hn, eid, wts, cnt, moe_w1, moe_w3, moe_w2, 0)

    mod1 = mod_all[1]
    cw8 = jnp.zeros((8, d), F32).at[:conv_w.shape[1]].set(conv_w[0])
    wr, br = _router_pack(moe_wg[1], moe_bg[1], moe_we[1], moe_be[1])
    x3, hn, eid, wts, cnt = _moe_combine_conv(
        tables, lpos, x1, mod, o_sorted, mod1, norm1_g[1:2], conv_w_in[0].astype(BF16), cw8,
        conv_w_out[0].astype(BF16), norm2_g[1:2], wr, br)
    tables, lpos, o_sorted = _moe_sorted_experts(hn, eid, wts, cnt, moe_w1, moe_w3, moe_w2, 1)
    return _moe_combine_final(tables, lpos, x3, mod1, fg, o_sorted)
```

```python
import functools
import math

import jax
import jax.numpy as jnp
from jax import lax
from jax.experimental import pallas as pl
from jax.experimental.pallas import tpu as pltpu

F32 = jnp.float32
BF16 = jnp.bfloat16
HIGHEST = lax.Precision.HIGHEST

RMS_EPS = 1e-6
SSM_GROUP = 16
SSM_CHUNK = 128
N_GROUPS = 4
EXPERTS_PER_GROUP = 8
N_EXPERTS = N_GROUPS * EXPERTS_PER_GROUP
TOP_K = 2
LANES = 128
MXU_WIDTH = 256
TOKEN_TILE = 512
S5_TILE = 1024
S5_K_BLOCKS = 2
SUB_ROWS = 256
MOE_BLOCK = TOKEN_TILE
CHUNK_ROWS = 8
LOCAL_ROWS = -(-(TOP_K * MOE_BLOCK + N_EXPERTS * (CHUNK_ROWS - 1)) // LANES) * LANES
EXPERT_TILE = 512
VMEM_LIMIT = 56 * 1024 * 1024
NEG_INF = -1e30


def _cparams(*sem):
    return pltpu.CompilerParams(dimension_semantics=sem, vmem_limit_bytes=VMEM_LIMIT)


def _adaln_kernel(c_ref, w_ref, b_ref, o_ref):
    c = c_ref[...]
    cond = c * jax.nn.sigmoid(c)
    o_ref[0] = jnp.dot(cond, w_ref[0], precision=HIGHEST, preferred_element_type=F32) + b_ref[0]


def _adaln(c8, ada_w, ada_b):
    depth, d, n = ada_w.shape
    tn = 1536
    return pl.pallas_call(
        _adaln_kernel,
        grid=(depth, n // tn),
        in_specs=[pl.BlockSpec((8, d), lambda i, j: (0, 0)),
                  pl.BlockSpec((1, d, tn), lambda i, j: (i, 0, j)),
                  pl.BlockSpec((1, 1, tn), lambda i, j: (i, 0, j))],
        out_specs=pl.BlockSpec((1, 8, tn), lambda i, j: (i, 0, j)),
        out_shape=jax.ShapeDtypeStruct((depth, 8, n), F32),
        compiler_params=_cparams("parallel", "parallel"),
        name="adaln",
    )(c8, ada_w, ada_b.reshape(depth, 1, n))


def _norm_mod(x, g, shift, scale):
    y = x * lax.rsqrt(jnp.mean(x * x, axis=-1, keepdims=True) + RMS_EPS)
    return (y * g) * (1.0 + scale) + shift


def _route(hn, hn_hi, wr_ref, br_ref, eid_ref, wts_ref):
    hn_lo = (hn - hn_hi.astype(F32)).astype(BF16)
    both = jnp.dot(hn_hi, wr_ref[...], preferred_element_type=F32)
    logits = (both[:, :LANES] + both[:, LANES:]
              + jnp.dot(hn_lo, wr_ref[:, :LANES], preferred_element_type=F32)) + br_ref[...]
    lane = lax.broadcasted_iota(jnp.int32, logits.shape, 1)
    is_grp = (lane >= N_EXPERTS) & (lane < N_EXPERTS + N_GROUPS)
    lg = jnp.where(is_grp, logits, NEG_INF)
    gmax = jnp.max(lg, axis=-1, keepdims=True)
    gsum = jnp.sum(jnp.where(is_grp, jnp.exp(lg - gmax), 0.0), axis=-1, keepdims=True)
    gp = 1.0 / gsum
    gi = jnp.min(jnp.where(lg == gmax, lane, 2 * LANES), axis=-1, keepdims=True) - N_EXPERTS
    in_grp = (lane < N_EXPERTS) & ((lane // EXPERTS_PER_GROUP) == gi)
    le = jnp.where(in_grp, logits, NEG_INF)
    v1 = jnp.max(le, axis=-1, keepdims=True)
    i1 = jnp.min(jnp.where(le == v1, lane, 2 * LANES), axis=-1, keepdims=True)
    le2 = jnp.where(lane == i1, NEG_INF, le)
    v2 = jnp.max(le2, axis=-1, keepdims=True)
    i2 = jnp.min(jnp.where(le2 == v2, lane, 2 * LANES), axis=-1, keepdims=True)
    e2 = jnp.exp(v2 - v1)
    den = 1.0 + e2
    col = lax.broadcasted_iota(jnp.int32, (hn.shape[0], TOP_K), 1)
    eid_ref[...] = jnp.where(col == 0, i1, i2)
    wts_ref[...] = jnp.where(col == 0, gp / den, gp * e2 / den)
    chosen = ((lane == i1) | (lane == i2)).astype(F32)
    return jnp.sum(chosen, axis=0, keepdims=True)


def _residual_and_route(x, out, mod, g2_ref, wr_ref, br_ref, x_out_ref, hn_ref, eid_ref, wts_ref):
    x1 = x + (1.0 + mod[2:3]) * out
    x_out_ref[...] = x1
    hn = _norm_mod(x1, g2_ref[...], mod[3:4], mod[4:5])
    hn_hi = hn.astype(BF16)
    hn_ref[...] = hn_hi
    return _route(hn, hn_hi, wr_ref, br_ref, eid_ref, wts_ref)


def _s5_in_kernel(x_ref, mod_ref, g_ref, wt_ref, u_ref):
    mod = mod_ref[0]
    hn = _norm_mod(x_ref[0], g_ref[...], mod[0:1], mod[1:2])
    ut = lax.dot_general(wt_ref[...], hn.astype(BF16), (((1,), (1,)), ((), ())), preferred_element_type=F32)
    u_ref[0] = ut.reshape(ut.shape[0], S5_TILE // SSM_CHUNK, SSM_CHUNK)


def _s5_in(x, mod, g, wt_bf):
    b, l, d = x.shape
    h = wt_bf.shape[0]
    cpt = S5_TILE // SSM_CHUNK
    return pl.pallas_call(
        _s5_in_kernel,
        grid=(b, l // S5_TILE),
        in_specs=[pl.BlockSpec((1, S5_TILE, d), lambda i, j: (i, j, 0)),
                  pl.BlockSpec((1, 8, d), lambda i, j: (i, 0, 0)),
                  pl.BlockSpec((1, d), lambda i, j: (0, 0)),
                  pl.BlockSpec((h, d), lambda i, j: (0, 0))],
        out_specs=pl.BlockSpec((1, h, cpt, SSM_CHUNK), lambda i, j: (i, 0, j, 0)),
        out_shape=jax.ShapeDtypeStruct((b, h, l // SSM_CHUNK, SSM_CHUNK), F32),
        compiler_params=_cparams("parallel", "parallel"),
        name="s5_in",
    )(x, mod, g, wt_bf)


def _s5_scan_kernel(u_ref, k_ref, kn_ref, pow_ref, coef_ref, a_ref, d_ref, y_ref,
                    m0_ref, m1_ref, acc_ref, s_ref, sw_ref, sp_ref):
    bsz, grp, n_chunks, tc = u_ref.shape
    g = pl.program_id(0)
    srow = lax.broadcasted_iota(jnp.int32, (tc, tc), 0)
    tcol = lax.broadcasted_iota(jnp.int32, (tc, tc), 1)
    causal = tcol >= srow

    def build_rows(src_ref, dst_ref, i):
        blks = []
        for j in range(grp):
            row = src_ref[0, pl.ds(i * grp + j, 1), :]
            blk = pltpu.roll(jnp.broadcast_to(row, (tc, tc)), 0, 1, stride=1, stride_axis=0)
            blks.append(jnp.where(causal, blk, 0.0))
        dst_ref[pl.ds(pl.multiple_of(i * tc, tc), tc), :] = jnp.concatenate(blks, axis=1).astype(BF16)

    @pl.when(g == 0)
    def _():
        def first(i, carry):
            build_rows(k_ref, m0_ref, i)
            return carry

        lax.fori_loop(0, grp, first, 0)

    x = jnp.concatenate(
        [jnp.concatenate([u_ref[b, k] for k in range(grp)], axis=1) for b in range(bsz)], axis=0)
    xb = x.astype(BF16)
    def expand(pa, pb, ca, cb):
        return jnp.concatenate(
            [(pow_ref[0, pa] * coef_ref[0, ca, i:i + 1, :] + pow_ref[0, pb] * coef_ref[0, cb, i:i + 1, :]).astype(BF16)
             for i in range(grp)], axis=0)

    f_mat = expand(0, 1, 0, 1)
    s_loc = jnp.dot(xb, f_mat, preferred_element_type=F32)
    s_ref[...] = s_loc
    half = s_loc.shape[1] // 2
    sw_ref[...] = jnp.concatenate([s_loc[:, half:], s_loc[:, :half]], axis=1)
    a1 = a_ref[0, 0:1, :]
    a2 = a_ref[0, 1:2, :]
    s = [jnp.zeros((1, s_loc.shape[1]), F32) for _ in range(bsz)]
    sw = [jnp.zeros((1, s_loc.shape[1]), F32) for _ in range(bsz)]
    for c in range(n_chunks):
        for b in range(bsz):
            r = b * n_chunks + c
            sp_ref[r:r + 1, :] = s[b]
            s_new = a1 * s[b] + a2 * sw[b] + s_ref[r:r + 1, :]
            sw[b] = a1 * sw[b] - a2 * s[b] + sw_ref[r:r + 1, :]
            s[b] = s_new
    et_mat = expand(2, 3, 2, 3)
    acc_ref[...] = lax.dot_general(sp_ref[...].astype(BF16), et_mat, (((1,), (1,)), ((), ())),
                                   preferred_element_type=F32) + d_ref[0] * x

    def run(m_cur_ref, m_next_ref):
        def step(c, carry):
            for h in range(S5_K_BLOCKS):
                build_rows(kn_ref, m_next_ref, c * S5_K_BLOCKS + h)
            xc = jnp.concatenate(
                [jnp.concatenate([u_ref[b, c * S5_K_BLOCKS + h] for h in range(S5_K_BLOCKS)], axis=1)
                 for b in range(bsz)], axis=0).astype(BF16)
            rows = pl.ds(pl.multiple_of(c * (S5_K_BLOCKS * tc), S5_K_BLOCKS * tc), S5_K_BLOCKS * tc)
            acc_ref[...] += jnp.dot(xc, m_cur_ref[rows, :], preferred_element_type=F32)
            return carry

        lax.fori_loop(0, grp // S5_K_BLOCKS, step, 0)

    @pl.when(g % 2 == 0)
    def _():
        run(m0_ref, m1_ref)

    @pl.when(g % 2 == 1)
    def _():
        run(m1_ref, m0_ref)
    for b in range(bsz):
        for j in range(grp):
            y_ref[b, j] = acc_ref[b * n_chunks:(b + 1) * n_chunks, j * tc:(j + 1) * tc]


def _s5_scan(u4, kern, pow_tab, coef_tab, a_pack, d_row):
    b, h, nc, tc = u4.shape
    g = kern.shape[0]
    grp = h // g
    k = grp * tc
    p2 = pow_tab.shape[3]
    rows = b * nc
    return pl.pallas_call(
        _s5_scan_kernel,
        grid=(g,),
        in_specs=[pl.BlockSpec((b, grp, nc, tc), lambda i: (0, i, 0, 0)),
                  pl.BlockSpec((1, grp * grp, tc), lambda i: (i, 0, 0)),
                  pl.BlockSpec((1, grp * grp, tc), lambda i: (jnp.minimum(i + 1, g - 1), 0, 0)),
                  pl.BlockSpec((1, 4, tc, p2), lambda i: (i, 0, 0, 0)),
                  pl.BlockSpec((1, 4, grp, p2), lambda i: (i, 0, 0, 0)),
                  pl.BlockSpec((1, 8, p2), lambda i: (i, 0, 0)),
                  pl.BlockSpec((1, 1, k), lambda i: (i, 0, 0))],
        out_specs=pl.BlockSpec((b, grp, nc, tc), lambda i: (0, i, 0, 0)),
        out_shape=jax.ShapeDtypeStruct(u4.shape, F32),
        scratch_shapes=[pltpu.VMEM((k, k), BF16), pltpu.VMEM((k, k), BF16),
                        pltpu.VMEM((rows, k), F32), pltpu.VMEM((rows, p2), F32),
                        pltpu.VMEM((rows, p2), F32), pltpu.VMEM((rows, p2), F32)],
        compiler_params=_cparams("arbitrary"),
        name="s5_scan",
    )(u4, kern, kern, pow_tab, coef_tab, a_pack, d_row)


def _s5_tables(lam_re, lam_im, log_dt, b_re, b_im, c_re, c_im, d_skip):
    g, p = lam_re.shape
    k = SSM_GROUP
    tc = SSM_CHUNK
    dt = jnp.exp(log_dt)[:, None]
    mag = jnp.exp(lam_re * dt)
    ab_re = mag * jnp.cos(lam_im * dt)
    ab_im = mag * jnp.sin(lam_im * dt)
    den = lam_re * lam_re + lam_im * lam_im
    cf_re = ((ab_re - 1) * lam_re + ab_im * lam_im) / den
    cf_im = (ab_im * lam_re - (ab_re - 1) * lam_im) / den
    bb_re = cf_re[..., None] * b_re - cf_im[..., None] * b_im
    bb_im = cf_re[..., None] * b_im + cf_im[..., None] * b_re
    lre = jnp.concatenate([lam_re * dt, lam_re * dt], axis=-1)
    lim = jnp.concatenate([lam_im * dt, lam_im * dt], axis=-1)
    lags = jnp.arange(tc + 1, dtype=F32)[None, :, None]
    pmag = jnp.exp(lags * lre[:, None, :])
    pw_re = pmag * jnp.cos(lags * lim[:, None, :])
    pw_im = pmag * jnp.sin(lags * lim[:, None, :])
    cb_re = jnp.einsum('gjp,gpi->gpij', c_re, bb_re) - jnp.einsum('gjp,gpi->gpij', c_im, bb_im)
    cb_im = jnp.einsum('gjp,gpi->gpij', c_re, bb_im) + jnp.einsum('gjp,gpi->gpij', c_im, bb_re)
    kern = (jnp.einsum('glp,gpij->gijl', pw_re[:, :tc, :p], cb_re, precision=HIGHEST)
            - jnp.einsum('glp,gpij->gijl', pw_im[:, :tc, :p], cb_im, precision=HIGHEST))
    kern = kern.reshape(g, k * k, tc)
    sign = jnp.concatenate([-jnp.ones((p,), F32), jnp.ones((p,), F32)])
    flip = (jnp.arange(tc)[:, None] + jnp.arange(tc)[None, :] == tc - 1).astype(F32)
    pr2 = jnp.einsum('st,gtq->gsq', flip, pw_re[:, :tc], precision=HIGHEST)
    pi2 = jnp.einsum('st,gtq->gsq', flip, pw_im[:, :tc], precision=HIGHEST) * sign
    bb1 = jnp.concatenate([bb_re, bb_im], axis=1).transpose(0, 2, 1)
    bb2 = jnp.concatenate([bb_im, bb_re], axis=1).transpose(0, 2, 1)
    qr2 = pw_re[:, 1:]
    qi2 = pw_im[:, 1:]
    ca = jnp.concatenate([c_re, -c_im], axis=-1)
    cb = jnp.concatenate([-c_im, -c_re], axis=-1)
    pow_tab = jnp.stack([pr2, pi2, qr2, qi2], axis=1)
    coef_tab = jnp.stack([bb1, bb2, ca, cb], axis=1)
    a1 = pw_re[:, tc]
    a2 = pw_im[:, tc] * sign
    a_pack = jnp.concatenate([a1[:, None], a2[:, None], jnp.zeros((g, 6, 2 * p), F32)], axis=1)
    d_row = jnp.repeat(d_skip.reshape(g, k), tc, axis=1).reshape(g, 1, k * tc)
    return kern, pow_tab, coef_tab, a_pack, d_row


def _gelu_tanh(x):
    return 0.5 * x * (1.0 + jnp.tanh(math.sqrt(2.0 / math.pi) * (x + 0.044715 * (x * x * x))))


def _s5_out_kernel(y_ref, x_ref, mod_ref, w_ref, g2_ref, wr_ref, br_ref,
                   x_out_ref, hn_ref, eid_ref, wts_ref, cnt_ref):
    mod = mod_ref[0]
    h = y_ref.shape[1]
    yt = _gelu_tanh(y_ref[0].reshape(h, S5_TILE))
    d = w_ref.shape[1] // 2
    sub_rows = MOE_BLOCK
    n_sub = S5_TILE // sub_rows
    per_block = MOE_BLOCK // sub_rows

    def project(p):
        return lax.dot_general(yt[:, p * sub_rows:(p + 1) * sub_rows].astype(BF16), w_ref[...],
                               (((0,), (0,)), ((), ())), preferred_element_type=F32)

    o_next = project(0)
    cnt = None
    for p in range(n_sub):
        o = o_next
        if p + 1 < n_sub:
            o_next = project(p + 1)
        rows = slice(p * sub_rows, (p + 1) * sub_rows)
        out = o[:, :d] * jax.nn.sigmoid(o[:, d:])
        c = _residual_and_route(x_ref[0, rows, :], out, mod, g2_ref, wr_ref, br_ref,
                                x_out_ref.at[0, rows, :], hn_ref.at[rows, :], eid_ref.at[rows, :],
                                wts_ref.at[rows, :])
        cnt = c if p % per_block == 0 else cnt + c
        if p % per_block == per_block - 1:
            cnt_ref[p // per_block] = jnp.broadcast_to(cnt, cnt_ref.shape[1:])


def _mixer_out_specs(b, l, d, tile):
    nt = l // tile
    nblk = tile // MOE_BLOCK
    specs = [pl.BlockSpec((1, tile, d), lambda i, j: (i, j, 0)),
             pl.BlockSpec((tile, d), lambda i, j: (i * nt + j, 0)),
             pl.BlockSpec((tile, TOP_K), lambda i, j: (i * nt + j, 0)),
             pl.BlockSpec((tile, TOP_K), lambda i, j: (i * nt + j, 0)),
             pl.BlockSpec((nblk, 8, LANES), lambda i, j: (i * nt + j, 0, 0))]
    shapes = [jax.ShapeDtypeStruct((b, l, d), F32),
              jax.ShapeDtypeStruct((b * l, d), BF16),
              jax.ShapeDtypeStruct((b * l, TOP_K), jnp.int32),
              jax.ShapeDtypeStruct((b * l, TOP_K), F32),
              jax.ShapeDtypeStruct((b * l // MOE_BLOCK, 8, LANES), F32)]
    return specs, shapes


def _s5_out(y4, x, mod, w_bf, g2, wr, br):
    b, l, d = x.shape
    h = y4.shape[1]
    specs, shapes = _mixer_out_specs(b, l, d, S5_TILE)
    return pl.pallas_call(
        _s5_out_kernel,
        grid=(b, l // S5_TILE),
        in_specs=[pl.BlockSpec((1, h, S5_TILE // SSM_CHUNK, SSM_CHUNK), lambda i, j: (i, 0, j, 0)),
                  pl.BlockSpec((1, S5_TILE, d), lambda i, j: (i, j, 0)),
                  pl.BlockSpec((1, 8, d), lambda i, j: (i, 0, 0)),
                  pl.BlockSpec(w_bf.shape, lambda i, j: (0, 0), pipeline_mode=pl.Buffered(1)),
                  pl.BlockSpec((1, d), lambda i, j: (0, 0)),
                  pl.BlockSpec((d, 2 * LANES), lambda i, j: (0, 0)),
                  pl.BlockSpec((1, LANES), lambda i, j: (0, 0))],
        out_specs=specs,
        out_shape=shapes,
        compiler_params=_cparams("parallel", "parallel"),
        name="s5_out",
    )(y4, x, mod, w_bf, g2, wr, br)


def _conv_mixer_body(x_in_ref, mod_ref, g1_ref, win_ref, cw_ref, wout_ref, g2_ref, wr_ref, br_ref,
                     x_out_ref, hn_ref, eid_ref, wts_ref, cnt_ref, carry_ref):
    @pl.when(pl.program_id(1) == 0)
    def _():
        carry_ref[...] = jnp.zeros_like(carry_ref)

    mod = mod_ref[0]
    d = x_in_ref.shape[1]
    cw = cw_ref[...]
    n_sub = x_in_ref.shape[0] // SUB_ROWS
    sub = [slice(s * SUB_ROWS, (s + 1) * SUB_ROWS) for s in range(n_sub)]

    def project(s):
        hn = _norm_mod(x_in_ref[sub[s], :], g1_ref[...], mod[0:1], mod[1:2])
        return jnp.dot(hn.astype(BF16), win_ref[...], preferred_element_type=F32)

    def gate_conv(p, prev_tail):
        u = p[:, d:2 * d] * p[:, 2 * d:]
        ext = jnp.concatenate([prev_tail, u], axis=0)
        z = cw[0:1] * ext[6:6 + SUB_ROWS] + cw[1:2] * ext[7:7 + SUB_ROWS] + cw[2:3] * u
        return (p[:, :d] * z).astype(BF16), u[SUB_ROWS - 8:]

    ps = [project(s) for s in range(n_sub)]
    tail = carry_ref[...]
    outs = []
    for s in range(n_sub):
        gated, tail = gate_conv(ps[s], tail)
        outs.append(jnp.dot(gated, wout_ref[...], preferred_element_type=F32))
    carry_ref[...] = tail
    cnt = None
    for s in range(n_sub):
        c = _residual_and_route(x_in_ref[sub[s], :], outs[s], mod, g2_ref, wr_ref, br_ref,
                                x_out_ref.at[0, sub[s], :], hn_ref.at[sub[s], :], eid_ref.at[sub[s], :],
                                wts_ref.at[sub[s], :])
        cnt = c if s == 0 else cnt + c
    cnt_ref[0] = jnp.broadcast_to(cnt, cnt_ref.shape[1:])


_TN = (((0,), (0,)), ((), ()))


MAX_CHUNKS = LOCAL_ROWS // CHUNK_ROWS


def _rows_copy(src, s_row, dst, d_row, rows, sem):
    return pltpu.make_async_copy(src.at[pl.ds(s_row, rows), :], dst.at[pl.ds(d_row, rows), :], sem)


def _start_block_chunks(blk, count, crow_ref, make_copy):
    def body(q, carry):
        make_copy(pl.multiple_of(q * CHUNK_ROWS, CHUNK_ROWS),
                  pl.multiple_of(crow_ref[blk * MAX_CHUNKS + q], CHUNK_ROWS)).start()
        return carry

    lax.fori_loop(0, count, body, 0)


def _wait_chunks(count, max_count, make_wait):
    for k in range(max_count.bit_length()):
        @pl.when(((count >> k) & 1) == 1)
        def _():
            make_wait(CHUNK_ROWS << k).wait()


def _slot_masks(lp0, lp1):
    slot = lax.broadcasted_iota(jnp.int32, (lp0.shape[0], LOCAL_ROWS), 1)
    return slot == lp0, slot == lp1


_HI16 = -65536


def _pack_bf16_pairs(x):
    h = x.shape[1] // 2
    lo = lax.shift_right_logical(pltpu.bitcast(x[:, :h], jnp.int32), 16)
    hi = pltpu.bitcast(x[:, h:], jnp.int32) & _HI16
    return hi | lo


def _unpack_bf16_pairs(w):
    lo = pltpu.bitcast(w << 16, F32).astype(BF16)
    hi = pltpu.bitcast(w & _HI16, F32).astype(BF16)
    return lo, hi


def _moe_dispatch_kernel(crow_ref, tot_ref, padn_ref, pads_ref, nv_ref,
                         eid_ref, wts_ref, hn_ref, lofff_ref, tri_ref, xs_ref, lpos_ref, loc_ref, zero_ref, sems):
    blk = pl.program_id(0)
    d = hn_ref.shape[1]
    eid = eid_ref[...]
    t = eid.shape[0]
    lane = lax.broadcasted_iota(jnp.int32, (t, LANES), 1)
    hit0 = lane == eid[:, 0:1]
    hit1 = lane == eid[:, 1:2]
    onehot = (hit0 | hit1).astype(BF16)
    before = jnp.dot(tri_ref[...], onehot, preferred_element_type=F32) + lofff_ref[0]
    lp0 = jnp.sum(jnp.where(hit0, before, 0.0), axis=-1, keepdims=True).astype(jnp.int32)
    lp1 = jnp.sum(jnp.where(hit1, before, 0.0), axis=-1, keepdims=True).astype(jnp.int32)
    col = lax.broadcasted_iota(jnp.int32, (t, TOP_K), 1)
    lpos_ref[...] = jnp.where(col == 0, lp0, lp1)
    wts = wts_ref[...]

    buf = blk % 2
    loc = loc_ref.at[buf]

    def drain(which, count):
        _wait_chunks(count, MAX_CHUNKS,
                     lambda rows: _rows_copy(loc_ref.at[which], 0, xs_ref, 0, rows, sems.at[which]))

    @pl.when(blk >= 2)
    def _():
        drain(buf, tot_ref[blk - 2])

    m0, m1 = _slot_masks(lp0, lp1)
    wrow = jnp.sum(jnp.where(m0, wts[:, 0:1], 0.0) + jnp.where(m1, wts[:, 1:2], 0.0), axis=0, keepdims=True)
    sorted_rows = lax.dot_general((m0 | m1).astype(BF16), hn_ref[...], _TN, preferred_element_type=F32)
    loc[:, :d // 2] = _pack_bf16_pairs(sorted_rows)
    loc[:, d // 2:] = pltpu.bitcast(jnp.broadcast_to(wrow, (LANES, LOCAL_ROWS)).T, jnp.int32)

    _start_block_chunks(blk, tot_ref[blk], crow_ref,
                        lambda lo, go: _rows_copy(loc, lo, xs_ref, go, CHUNK_ROWS, sems.at[buf]))

    @pl.when(blk == pl.num_programs(0) - 1)
    def _():
        drain(1 - buf, tot_ref[blk - 1])
        drain(buf, tot_ref[blk])
        sem = sems.at[0]
        zero_ref[...] = jnp.zeros_like(zero_ref)
        pad_bits = (EXPERT_TILE // CHUNK_ROWS - 1).bit_length()

        def pad_copies(e, carry, *, wait):
            padn = padn_ref[e]
            for k in range(pad_bits):
                @pl.when(((padn >> k) & 1) == 1)
                def _():
                    done = (padn & ((1 << k) - 1)) * CHUNK_ROWS
                    cp = _rows_copy(zero_ref, 0, xs_ref, pl.multiple_of(pads_ref[e] + done, CHUNK_ROWS),
                                    CHUNK_ROWS << k, sem)
                    if wait:
                        cp.wait()
                    else:
                        cp.start()
            return carry

        lax.fori_loop(0, N_EXPERTS, functools.partial(pad_copies, wait=False), 0)
        lax.fori_loop(0, N_EXPERTS, functools.partial(pad_copies, wait=True), 0)

        def tile_copy(i):
            return pltpu.make_async_copy(zero_ref, xs_ref.at[pl.ds(pl.multiple_of(i * EXPERT_TILE, EXPERT_TILE),
                                                                   EXPERT_TILE), :], sem)

        n_tiles = xs_ref.shape[0] // EXPERT_TILE

        def start_tile(i, c):
            tile_copy(i).start()
            return c

        def wait_tile(i, c):
            tile_copy(i).wait()
            return c

        lax.fori_loop(nv_ref[0], n_tiles, start_tile, 0)
        lax.fori_loop(nv_ref[0], n_tiles, wait_tile, 0)


def _moe_dispatch(tables, eid, wts, hn, loff_f, n_rows):
    t, d = hn.shape
    pos = jnp.arange(MOE_BLOCK, dtype=jnp.int32)
    tri = (pos[None, :] < pos[:, None]).astype(BF16)
    width = d // 2 + LANES
    grid_spec = pltpu.PrefetchScalarGridSpec(
        num_scalar_prefetch=5,
        grid=(t // MOE_BLOCK,),
        in_specs=[pl.BlockSpec((MOE_BLOCK, TOP_K), lambda i, *_: (i, 0)),
                  pl.BlockSpec((MOE_BLOCK, TOP_K), lambda i, *_: (i, 0)),
                  pl.BlockSpec((MOE_BLOCK, d), lambda i, *_: (i, 0)),
                  pl.BlockSpec((1, 1, LANES), lambda i, *_: (i, 0, 0)),
                  pl.BlockSpec((MOE_BLOCK, MOE_BLOCK), lambda i, *_: (0, 0))],
        out_specs=[pl.BlockSpec(memory_space=pl.ANY),
                   pl.BlockSpec((MOE_BLOCK, TOP_K), lambda i, *_: (i, 0))],
        scratch_shapes=[pltpu.VMEM((2, LOCAL_ROWS, width), jnp.int32), pltpu.VMEM((EXPERT_TILE, width), jnp.int32),
                        pltpu.SemaphoreType.DMA((2,))],
    )
    return pl.pallas_call(
        _moe_dispatch_kernel,
        grid_spec=grid_spec,
        out_shape=[jax.ShapeDtypeStruct((n_rows, width), jnp.int32),
                   jax.ShapeDtypeStruct((t, TOP_K), jnp.int32)],
        compiler_params=_cparams("arbitrary"),
        name="moe_dispatch",
    )(*tables, eid, wts, hn, loff_f, tri)


def _moe_experts_kernel(te_ref, tv_ref, tf_ref, ts_ref, xs_ref, w1_ref, w3_ref, w2_ref, o_ref,
                        w1b_ref, w3b_ref, w2b_ref):
    del te_ref, ts_ref
    i = pl.program_id(0)

    @pl.when(tf_ref[i] != 0)
    def _():
        def cast_rows(r, carry):
            rows = pl.ds(pl.multiple_of(r * LANES, LANES), LANES)
            w1b_ref[rows, :] = w1_ref[0, 0, rows, :].astype(BF16)
            w3b_ref[rows, :] = w3_ref[0, 0, rows, :].astype(BF16)

            @pl.when(r < w2b_ref.shape[0] // LANES)
            def _():
                w2b_ref[rows, :] = w2_ref[0, 0, rows, :].astype(BF16)

            return carry

        lax.fori_loop(0, w1b_ref.shape[0] // LANES, cast_rows, 0)

    @pl.when(tv_ref[i] != 0)
    def _():
        half = o_ref.shape[1]
        x_lo, x_hi = _unpack_bf16_pairs(xs_ref[:, :half])
        w = pltpu.bitcast(xs_ref[:, half:half + 1], F32)
        de = w1b_ref.shape[1]
        o = None
        for c0 in range(0, de, MXU_WIDTH):
            cols = slice(c0, c0 + MXU_WIDTH)
            a = (jnp.dot(x_lo, w1b_ref[:half, cols], preferred_element_type=F32)
                 + jnp.dot(x_hi, w1b_ref[half:, cols], preferred_element_type=F32))
            b = (jnp.dot(x_lo, w3b_ref[:half, cols], preferred_element_type=F32)
                 + jnp.dot(x_hi, w3b_ref[half:, cols], preferred_element_type=F32))
            h = (a * jax.nn.sigmoid(a) * b).astype(BF16)
            part = jnp.dot(h, w2b_ref[cols, :], preferred_element_type=F32)
            o = part if o is None else o + part
        o = o * w
        o_ref[...] = _pack_bf16_pairs(o.astype(BF16).astype(F32))

    @pl.when(tv_ref[i] == 0)
    def _():
        o_ref[...] = jnp.zeros_like(o_ref)


def _moe_experts(tile_expert, tile_valid, tile_first, tile_src, xs, w1, w3, w2, layer):
    r, width = xs.shape
    d = (width - LANES) * 2
    de = w1.shape[3]
    grid_spec = pltpu.PrefetchScalarGridSpec(
        num_scalar_prefetch=4,
        grid=(r // EXPERT_TILE,),
        in_specs=[pl.BlockSpec((EXPERT_TILE, width), lambda i, te, tv, tf, ts: (ts[i], 0)),
                  pl.BlockSpec((1, 1, d, de), lambda i, te, tv, tf, ts: (layer, te[i], 0, 0)),
                  pl.BlockSpec((1, 1, d, de), lambda i, te, tv, tf, ts: (layer, te[i], 0, 0)),
                  pl.BlockSpec((1, 1, de, d), lambda i, te, tv, tf, ts: (layer, te[i], 0, 0))],
        out_specs=pl.BlockSpec((EXPERT_TILE, d // 2), lambda i, te, tv, tf, ts: (i, 0)),
        scratch_shapes=[pltpu.VMEM((d, de), BF16), pltpu.VMEM((d, de), BF16), pltpu.VMEM((de, d), BF16)],
    )
    return pl.pallas_call(
        _moe_experts_kernel,
        grid_spec=grid_spec,
        out_shape=jax.ShapeDtypeStruct((r, d // 2), jnp.int32),
        compiler_params=_cparams("arbitrary"),
        name="moe_experts",
    )(tile_expert, tile_valid, tile_first, tile_src, xs, w1, w3, w2)


def _combine_rows(crow_ref, tot_ref, lpos_ref, o_hbm_ref, loc_ref, sems, t):
    blk = pl.program_id(0) * pl.num_programs(1) + pl.program_id(1)
    n_blk = pl.num_programs(0) * pl.num_programs(1)
    buf = blk % 2

    def fetch(b, which):
        loc = loc_ref.at[which]
        loc[TOP_K * t:, :] = jnp.zeros((LOCAL_ROWS - TOP_K * t, loc_ref.shape[2]), jnp.int32)
        _start_block_chunks(b, tot_ref[b], crow_ref,
                            lambda lo, go: _rows_copy(o_hbm_ref, go, loc, lo, CHUNK_ROWS, sems.at[which]))

    @pl.when(blk == 0)
    def _():
        fetch(blk, buf)

    @pl.when(blk + 1 < n_blk)
    def _():
        fetch(blk + 1, 1 - buf)

    _wait_chunks(tot_ref[blk], MAX_CHUNKS,
                 lambda rows: _rows_copy(o_hbm_ref, 0, loc_ref.at[buf], 0, rows, sems.at[buf]))
    lp = lpos_ref[...]
    m0, m1 = _slot_masks(lp[:, 0:1], lp[:, 1:2])
    pt = (m0 | m1).astype(BF16)
    o_lo, o_hi = _unpack_bf16_pairs(loc_ref[buf])
    return jnp.concatenate([jnp.dot(pt, o_lo, preferred_element_type=F32),
                            jnp.dot(pt, o_hi, preferred_element_type=F32)], axis=1)


def _moe_combine_final_kernel(crow_ref, tot_ref, lpos_ref, x_ref, mod_ref, fg_ref, o_hbm_ref,
                              out_ref, loc_ref, sems):
    y = _combine_rows(crow_ref, tot_ref, lpos_ref, o_hbm_ref, loc_ref, sems, x_ref.shape[1])
    x2 = x_ref[0] + (1.0 + mod_ref[0, 5:6]) * y
    out_ref[0] = (x2 * lax.rsqrt(jnp.mean(x2 * x2, axis=-1, keepdims=True) + RMS_EPS)) * fg_ref[...]


def _moe_combine_conv_kernel(crow_ref, tot_ref, lpos_ref, x_ref, mod_prev_ref, o_hbm_ref,
                             mod_ref, g1_ref, win_ref, cw_ref, wout_ref, g2_ref, wr_ref, br_ref,
                             x_out_ref, hn_ref, eid_ref, wts_ref, cnt_ref,
                             loc_ref, sems, xmid_ref, carry_ref):
    y = _combine_rows(crow_ref, tot_ref, lpos_ref, o_hbm_ref, loc_ref, sems, x_ref.shape[1])
    xmid_ref[...] = x_ref[0] + (1.0 + mod_prev_ref[0, 5:6]) * y
    _conv_mixer_body(xmid_ref, mod_ref, g1_ref, win_ref, cw_ref, wout_ref, g2_ref, wr_ref, br_ref,
                     x_out_ref, hn_ref, eid_ref, wts_ref, cnt_ref, carry_ref)


def _combine_in_specs(nt, d):
    return [pl.BlockSpec((MOE_BLOCK, TOP_K), lambda i, j, *_: (i * nt + j, 0)),
            pl.BlockSpec((1, MOE_BLOCK, d), lambda i, j, *_: (i, j, 0)),
            pl.BlockSpec((1, 8, d), lambda i, j, *_: (i, 0, 0))]


def _combine_scratch(d):
    return [pltpu.VMEM((2, LOCAL_ROWS, d // 2), jnp.int32), pltpu.SemaphoreType.DMA((2,))]


def _moe_combine_final(tables, lpos, x, mod, final_g, o_sorted):
    b, l, d = x.shape
    nt = l // MOE_BLOCK
    grid_spec = pltpu.PrefetchScalarGridSpec(
        num_scalar_prefetch=2,
        grid=(b, nt),
        in_specs=_combine_in_specs(nt, d) + [pl.BlockSpec((1, d), lambda i, j, *_: (0, 0)),
                                             pl.BlockSpec(memory_space=pl.ANY)],
        out_specs=pl.BlockSpec((1, MOE_BLOCK, d), lambda i, j, *_: (i, j, 0)),
        scratch_shapes=_combine_scratch(d),
    )
    return pl.pallas_call(
        _moe_combine_final_kernel,
        grid_spec=grid_spec,
        out_shape=jax.ShapeDtypeStruct((b, l, d), F32),
        compiler_params=_cparams("arbitrary", "arbitrary"),
        name="moe_combine_final",
    )(*tables[:2], lpos, x, mod, final_g, o_sorted)


def _moe_combine_conv(tables, lpos, x, mod_prev, o_sorted, mod, g1, win_bf, cw8, wout_bf, g2, wr, br):
    b, l, d = x.shape
    nt = l // MOE_BLOCK
    specs, shapes = _mixer_out_specs(b, l, d, TOKEN_TILE)
    const = lambda shape: pl.BlockSpec(shape, lambda i, j, *_: (0,) * len(shape))
    grid_spec = pltpu.PrefetchScalarGridSpec(
        num_scalar_prefetch=2,
        grid=(b, nt),
        in_specs=_combine_in_specs(nt, d) + [
            pl.BlockSpec(memory_space=pl.ANY),
            pl.BlockSpec((1, 8, d), lambda i, j, *_: (i, 0, 0)),
            const((1, d)), const(win_bf.shape), const((8, d)), const(wout_bf.shape),
            const((1, d)), const((d, 2 * LANES)), const((1, LANES))],
        out_specs=[pl.BlockSpec(s.block_shape, lambda i, j, *_, f=s.index_map: f(i, j)) for s in specs],
        scratch_shapes=_combine_scratch(d) + [pltpu.VMEM((MOE_BLOCK, d), F32), pltpu.VMEM((8, d), F32)],
    )
    return pl.pallas_call(
        _moe_combine_conv_kernel,
        grid_spec=grid_spec,
        out_shape=shapes,
        compiler_params=_cparams("arbitrary", "arbitrary"),
        name="moe_combine_conv",
    )(*tables[:2], lpos, x, mod_prev, o_sorted, mod, g1, win_bf, cw8, wout_bf, g2, wr, br)


def _moe_sorted_experts(hn, eid, wts, cnt, w1, w3, w2, layer):
    t, d = hn.shape
    nblk = t // MOE_BLOCK
    i32 = jnp.int32
    n = cnt[:, 0, :N_EXPERTS].astype(i32)
    run = (n + CHUNK_ROWS - 1) // CHUNK_ROWS * CHUNK_ROWS
    loff = jnp.cumsum(run, axis=1) - run
    rows_e = jnp.sum(run, axis=0)
    tiles_e = (rows_e + EXPERT_TILE - 1) // EXPERT_TILE
    tile_end = jnp.cumsum(tiles_e)
    base = (tile_end - tiles_e) * EXPERT_TILE
    goff = base[None, :] + jnp.cumsum(run, axis=0) - run
    nch = run // CHUNK_ROWS
    tot = jnp.sum(nch, axis=1)
    ch_end = jnp.cumsum(nch, axis=1)
    q = jnp.arange(MAX_CHUNKS, dtype=i32)
    in_run = ((q[None, :, None] >= (ch_end - nch)[:, None, :]) & (q[None, :, None] < ch_end[:, None, :])).astype(i32)
    crow = jnp.sum(in_run * (goff[:, None, :] + (q[None, :, None] - (ch_end - nch)[:, None, :]) * CHUNK_ROWS), axis=-1)
    padn = (tiles_e * EXPERT_TILE - rows_e) // CHUNK_ROWS
    pads = base + rows_e
    tables = (crow.reshape(-1), tot, padn, pads, tile_end[-1:])
    loff_f = jnp.zeros((nblk, 1, LANES), F32).at[:, 0, :N_EXPERTS].set(loff.astype(F32))

    max_rows = t * TOP_K + nblk * N_EXPERTS * (CHUNK_ROWS - 1) + N_EXPERTS * (EXPERT_TILE - 1)
    max_tiles = -(-max_rows // EXPERT_TILE)
    tile_ids = jnp.arange(max_tiles, dtype=i32)
    n_valid = tile_end[-1]
    tile_src = jnp.minimum(tile_ids, n_valid - 1)
    tile_expert = jnp.sum((tile_src[:, None] >= tile_end[None, :]).astype(i32), axis=1)
    tile_valid = (tile_ids < n_valid).astype(i32)
    tile_first = jnp.concatenate([jnp.ones((1,), i32), (tile_expert[1:] != tile_expert[:-1]).astype(i32)])

    xs, lpos = _moe_dispatch(tables, eid, wts, hn, loff_f, max_tiles * EXPERT_TILE)
    o_sorted = _moe_experts(tile_expert, tile_valid, tile_first, tile_src, xs, w1, w3, w2, layer)
    return tables, lpos, o_sorted


def _router_pack(wg, bg, we, be):
    d = wg.shape[0]
    wr = jnp.zeros((d, LANES), F32).at[:, :N_EXPERTS].set(we).at[:, N_EXPERTS:N_EXPERTS + N_GROUPS].set(wg)
    br = jnp.zeros((1, LANES), F32).at[0, :N_EXPERTS].set(be).at[0, N_EXPERTS:N_EXPERTS + N_GROUPS].set(bg)
    wr_hi = wr.astype(BF16)
    wr_lo = (wr - wr_hi.astype(F32)).astype(BF16)
    return jnp.concatenate([wr_hi, wr_lo], axis=1), br


def kernel(x, c, ada_w, ada_b, norm1_g, norm2_g, ssm_w_in, ssm_lam_re, ssm_lam_im, ssm_log_dt, ssm_b_re, ssm_b_im, ssm_c_re, ssm_c_im, ssm_d, ssm_w_glu, conv_w_in, conv_w, conv_w_out, moe_wg, moe_bg, moe_we, moe_be, moe_w1, moe_w3, moe_w2, final_g):
    b, l, d = x.shape
    depth = ada_w.shape[0]
    c8 = jnp.zeros((8, d), F32).at[:b].set(c)
    mod_all = _adaln(c8, ada_w, ada_b)[:, :b].reshape(depth, b, 6, d)
    mod_all = jnp.concatenate([mod_all, jnp.zeros((depth, b, 2, d), F32)], axis=2)
    fg = final_g.reshape(1, d)

    mod = mod_all[0]
    u4 = _s5_in(x, mod, norm1_g[0:1], ssm_w_in[0].T.astype(BF16))
    kern, pow_tab, coef_tab, a_pack, d_row = _s5_tables(
        ssm_lam_re[0], ssm_lam_im[0], ssm_log_dt[0], ssm_b_re[0], ssm_b_im[0],
        ssm_c_re[0], ssm_c_im[0], ssm_d[0])
    y4 = _s5_scan(u4, kern, pow_tab, coef_tab, a_pack, d_row)
    wr, br = _router_pack(moe_wg[0], moe_bg[0], moe_we[0], moe_be[0])
    x1, hn, eid, wts, cnt = _s5_out(y4, x, mod, ssm_w_glu[0].astype(BF16), norm2_g[0:1], wr, br)
    tables, lpos, o_sorted = _moe_sorted_experts(hn, eid, wts, cnt, moe_w1, moe_w3, moe_w2, 0)

    mod1 = mod_all[1]
    cw8 = jnp.zeros((8, d), F32).at[:conv_w.shape[1]].set(conv_w[0])
    wr, br = _router_pack(moe_wg[1], moe_bg[1], moe_we[1], moe_be[1])
    x3, hn, eid, wts, cnt = _moe_combine_conv(
        tables, lpos, x1, mod, o_sorted, mod1, norm1_g[1:2], conv_w_in[0].astype(BF16), cw8,
        conv_w_out[0].astype(BF16), norm2_g[1:2], wr, br)
    tables, lpos, o_sorted = _moe_sorted_experts(hn, eid, wts, cnt, moe_w1, moe_w3, moe_w2, 1)
    return _moe_combine_final(tables, lpos, x3, mod1, fg, o_sorted)
```

```python
import functools
import math

import jax
import jax.numpy as jnp
from jax import lax
from jax.experimental import pallas as pl
from jax.experimental.pallas import tpu as pltpu

F32 = jnp.float32
BF16 = jnp.bfloat16
HIGHEST = lax.Precision.HIGHEST

RMS_EPS = 1e-6
SSM_GROUP = 16
SSM_CHUNK = 128
N_GROUPS = 4
EXPERTS_PER_GROUP = 8
N_EXPERTS = N_GROUPS * EXPERTS_PER_GROUP
TOP_K = 2
LANES = 128
MXU_WIDTH = 256
TOKEN_TILE = 512
S5_TILE = 1024
S5_K_BLOCKS = 2
SUB_ROWS = 256
MOE_BLOCK = TOKEN_TILE
CHUNK_ROWS = 8
LOCAL_ROWS = -(-(TOP_K * MOE_BLOCK + N_EXPERTS * (CHUNK_ROWS - 1)) // LANES) * LANES
EXPERT_TILE = 512
VMEM_LIMIT = 56 * 1024 * 1024
NEG_INF = -1e30


def _cparams(*sem):
    return pltpu.CompilerParams(dimension_semantics=sem, vmem_limit_bytes=VMEM_LIMIT)


def _adaln_kernel(c_ref, w_ref, b_ref, o_ref):
    c = c_ref[...]
    cond = c * jax.nn.sigmoid(c)
    o_ref[0] = jnp.dot(cond, w_ref[0], precision=HIGHEST, preferred_element_type=F32) + b_ref[0]


def _adaln(c8, ada_w, ada_b):
    depth, d, n = ada_w.shape
    tn = 1536
    return pl.pallas_call(
        _adaln_kernel,
        grid=(depth, n // tn),
        in_specs=[pl.BlockSpec((8, d), lambda i, j: (0, 0)),
                  pl.BlockSpec((1, d, tn), lambda i, j: (i, 0, j)),
                  pl.BlockSpec((1, 1, tn), lambda i, j: (i, 0, j))],
        out_specs=pl.BlockSpec((1, 8, tn), lambda i, j: (i, 0, j)),
        out_shape=jax.ShapeDtypeStruct((depth, 8, n), F32),
        compiler_params=_cparams("parallel", "parallel"),
        name="adaln",
    )(c8, ada_w, ada_b.reshape(depth, 1, n))


def _norm_mod(x, g, shift, scale):
    y = x * lax.rsqrt(jnp.mean(x * x, axis=-1, keepdims=True) + RMS_EPS)
    return (y * g) * (1.0 + scale) + shift


def _route(hn, hn_hi, wr_ref, br_ref, eid_ref, wts_ref):
    hn_lo = (hn - hn_hi.astype(F32)).astype(BF16)
    both = jnp.dot(hn_hi, wr_ref[...], preferred_element_type=F32)
    logits = (both[:, :LANES] + both[:, LANES:]
              + jnp.dot(hn_lo, wr_ref[:, :LANES], preferred_element_type=F32)) + br_ref[...]
    lane = lax.broadcasted_iota(jnp.int32, logits.shape, 1)
    is_grp = (lane >= N_EXPERTS) & (lane < N_EXPERTS + N_GROUPS)
    lg = jnp.where(is_grp, logits, NEG_INF)
    gmax = jnp.max(lg, axis=-1, keepdims=True)
    gsum = jnp.sum(jnp.where(is_grp, jnp.exp(lg - gmax), 0.0), axis=-1, keepdims=True)
    gp = 1.0 / gsum
    gi = jnp.min(jnp.where(lg == gmax, lane, 2 * LANES), axis=-1, keepdims=True) - N_EXPERTS
    in_grp = (lane < N_EXPERTS) & ((lane // EXPERTS_PER_GROUP) == gi)
    le = jnp.where(in_grp, logits, NEG_INF)
    v1 = jnp.max(le, axis=-1, keepdims=True)
    i1 = jnp.min(jnp.where(le == v1, lane, 2 * LANES), axis=-1, keepdims=True)
    le2 = jnp.where(lane == i1, NEG_INF, le)
    v2 = jnp.max(le2, axis=-1, keepdims=True)
    i2 = jnp.min(jnp.where(le2 == v2, lane, 2 * LANES), axis=-1, keepdims=True)
    e2 = jnp.exp(v2 - v1)
    den = 1.0 + e2
    col = lax.broadcasted_iota(jnp.int32, (hn.shape[0], TOP_K), 1)
    eid_ref[...] = jnp.where(col == 0, i1, i2)
    wts_ref[...] = jnp.where(col == 0, gp / den, gp * e2 / den)
    chosen = ((lane == i1) | (lane == i2)).astype(F32)
    return jnp.sum(chosen, axis=0, keepdims=True)


def _residual_and_route(x, out, mod, g2_ref, wr_ref, br_ref, x_out_ref, hn_ref, eid_ref, wts_ref):
    x1 = x + (1.0 + mod[2:3]) * out
    x_out_ref[...] = x1
    hn = _norm_mod(x1, g2_ref[...], mod[3:4], mod[4:5])
    hn_hi = hn.astype(BF16)
    hn_ref[...] = hn_hi
    return _route(hn, hn_hi, wr_ref, br_ref, eid_ref, wts_ref)


def _s5_in_kernel(x_ref, mod_ref, g_ref, wt_ref, u_ref):
    mod = mod_ref[0]
    hn = _norm_mod(x_ref[0], g_ref[...], mod[0:1], mod[1:2])
    ut = lax.dot_general(wt_ref[...], hn.astype(BF16), (((1,), (1,)), ((), ())), preferred_element_type=F32)
    u_ref[0] = ut.reshape(ut.shape[0], S5_TILE // SSM_CHUNK, SSM_CHUNK)


def _s5_in(x, mod, g, wt_bf):
    b, l, d = x.shape
    h = wt_bf.shape[0]
    cpt = S5_TILE // SSM_CHUNK
    return pl.pallas_call(
        _s5_in_kernel,
        grid=(b, l // S5_TILE),
        in_specs=[pl.BlockSpec((1, S5_TILE, d), lambda i, j: (i, j, 0)),
                  pl.BlockSpec((1, 8, d), lambda i, j: (i, 0, 0)),
                  pl.BlockSpec((1, d), lambda i, j: (0, 0)),
                  pl.BlockSpec((h, d), lambda i, j: (0, 0))],
        out_specs=pl.BlockSpec((1, h, cpt, SSM_CHUNK), lambda i, j: (i, 0, j, 0)),
        out_shape=jax.ShapeDtypeStruct((b, h, l // SSM_CHUNK, SSM_CHUNK), F32),
        compiler_params=_cparams("parallel", "parallel"),
        name="s5_in",
    )(x, mod, g, wt_bf)


def _s5_scan_kernel(u_ref, k_ref, kn_ref, pow_ref, coef_ref, a_ref, d_ref, y_ref,
                    m0_ref, m1_ref, acc_ref, s_ref, sw_ref, sp_ref):
    bsz, grp, n_chunks, tc = u_ref.shape
    g = pl.program_id(0)
    srow = lax.broadcasted_iota(jnp.int32, (tc, tc), 0)
    tcol = lax.broadcasted_iota(jnp.int32, (tc, tc), 1)
    causal = tcol >= srow

    def build_rows(src_ref, dst_ref, i):
        blks = []
        for j in range(grp):
            row = src_ref[0, pl.ds(i * grp + j, 1), :]
            blk = pltpu.roll(jnp.broadcast_to(row, (tc, tc)), 0, 1, stride=1, stride_axis=0)
            blks.append(jnp.where(causal, blk, 0.0))
        dst_ref[pl.ds(pl.multiple_of(i * tc, tc), tc), :] = jnp.concatenate(blks, axis=1).astype(BF16)

    @pl.when(g == 0)
    def _():
        def first(i, carry):
            build_rows(k_ref, m0_ref, i)
            return carry

        lax.fori_loop(0, grp, first, 0)

    x = jnp.concatenate(
        [jnp.concatenate([u_ref[b, k] for k in range(grp)], axis=1) for b in range(bsz)], axis=0)
    xb = x.astype(BF16)
    def expand(pa, pb, ca, cb):
        return jnp.concatenate(
            [(pow_ref[0, pa] * coef_ref[0, ca, i:i + 1, :] + pow_ref[0, pb] * coef_ref[0, cb, i:i + 1, :]).astype(BF16)
             for i in range(grp)], axis=0)

    f_mat = expand(0, 1, 0, 1)
    s_loc = jnp.dot(xb, f_mat, preferred_element_type=F32)
    s_ref[...] = s_loc
    half = s_loc.shape[1] // 2
    sw_ref[...] = jnp.concatenate([s_loc[:, half:], s_loc[:, :half]], axis=1)
    a1 = a_ref[0, 0:1, :]
    a2 = a_ref[0, 1:2, :]
    s = [jnp.zeros((1, s_loc.shape[1]), F32) for _ in range(bsz)]
    sw = [jnp.zeros((1, s_loc.shape[1]), F32) for _ in range(bsz)]
    for c in range(n_chunks):
        for b in range(bsz):
            r = b * n_chunks + c
            sp_ref[r:r + 1, :] = s[b]
            s_new = a1 * s[b] + a2 * sw[b] + s_ref[r:r + 1, :]
            sw[b] = a1 * sw[b] - a2 * s[b] + sw_ref[r:r + 1, :]
            s[b] = s_new
    et_mat = expand(2, 3, 2, 3)
    acc_ref[...] = lax.dot_general(sp_ref[...].astype(BF16), et_mat, (((1,), (1,)), ((), ())),
                                   preferred_element_type=F32) + d_ref[0] * x

    def run(m_cur_ref, m_next_ref):
        def step(c, carry):
            for h in range(S5_K_BLOCKS):
                build_rows(kn_ref, m_next_ref, c * S5_K_BLOCKS + h)
            xc = jnp.concatenate(
                [jnp.concatenate([u_ref[b, c * S5_K_BLOCKS + h] for h in range(S5_K_BLOCKS)], axis=1)
                 for b in range(bsz)], axis=0).astype(BF16)
            rows = pl.ds(pl.multiple_of(c * (S5_K_BLOCKS * tc), S5_K_BLOCKS * tc), S5_K_BLOCKS * tc)
            acc_ref[...] += jnp.dot(xc, m_cur_ref[rows, :], preferred_element_type=F32)
            return carry

        lax.fori_loop(0, grp // S5_K_BLOCKS, step, 0)

    @pl.when(g % 2 == 0)
    def _():
        run(m0_ref, m1_ref)

    @pl.when(g % 2 == 1)
    def _():
        run(m1_ref, m0_ref)
    for b in range(bsz):
        for j in range(grp):
            y_ref[b, j] = acc_ref[b * n_chunks:(b + 1) * n_chunks, j * tc:(j + 1) * tc]


def _s5_scan(u4, kern, pow_tab, coef_tab, a_pack, d_row):
    b, h, nc, tc = u4.shape
    g = kern.shape[0]
    grp = h // g
    k = grp * tc
    p2 = pow_tab.shape[3]
    rows = b * nc
    return pl.pallas_call(
        _s5_scan_kernel,
        grid=(g,),
        in_specs=[pl.BlockSpec((b, grp, nc, tc), lambda i: (0, i, 0, 0)),
                  pl.BlockSpec((1, grp * grp, tc), lambda i: (i, 0, 0)),
                  pl.BlockSpec((1, grp * grp, tc), lambda i: (jnp.minimum(i + 1, g - 1), 0, 0)),
                  pl.BlockSpec((1, 4, tc, p2), lambda i: (i, 0, 0, 0)),
                  pl.BlockSpec((1, 4, grp, p2), lambda i: (i, 0, 0, 0)),
                  pl.BlockSpec((1, 8, p2), lambda i: (i, 0, 0)),
                  pl.BlockSpec((1, 1, k), lambda i: (i, 0, 0))],
        out_specs=pl.BlockSpec((b, grp, nc, tc), lambda i: (0, i, 0, 0)),
        out_shape=jax.ShapeDtypeStruct(u4.shape, F32),
        scratch_shapes=[pltpu.VMEM((k, k), BF16), pltpu.VMEM((k, k), BF16),
                        pltpu.VMEM((rows, k), F32), pltpu.VMEM((rows, p2), F32),
                        pltpu.VMEM((rows, p2), F32), pltpu.VMEM((rows, p2), F32)],
        compiler_params=_cparams("arbitrary"),
        name="s5_scan",
    )(u4, kern, kern, pow_tab, coef_tab, a_pack, d_row)


def _s5_tables(lam_re, lam_im, log_dt, b_re, b_im, c_re, c_im, d_skip):
    g, p = lam_re.shape
    k = SSM_GROUP
    tc = SSM_CHUNK
    dt = jnp.exp(log_dt)[:, None]
    mag = jnp.exp(lam_re * dt)
    ab_re = mag * jnp.cos(lam_im * dt)
    ab_im = mag * jnp.sin(lam_im * dt)
    den = lam_re * lam_re + lam_im * lam_im
    cf_re = ((ab_re - 1) * lam_re + ab_im * lam_im) / den
    cf_im = (ab_im * lam_re - (ab_re - 1) * lam_im) / den
    bb_re = cf_re[..., None] * b_re - cf_im[..., None] * b_im
    bb_im = cf_re[..., None] * b_im + cf_im[..., None] * b_re
    lre = jnp.concatenate([lam_re * dt, lam_re * dt], axis=-1)
    lim = jnp.concatenate([lam_im * dt, lam_im * dt], axis=-1)
    lags = jnp.arange(tc + 1, dtype=F32)[None, :, None]
    pmag = jnp.exp(lags * lre[:, None, :])
    pw_re = pmag * jnp.cos(lags * lim[:, None, :])
    pw_im = pmag * jnp.sin(lags * lim[:, None, :])
    cb_re = jnp.einsum('gjp,gpi->gpij', c_re, bb_re) - jnp.einsum('gjp,gpi->gpij', c_im, bb_im)
    cb_im = jnp.einsum('gjp,gpi->gpij', c_re, bb_im) + jnp.einsum('gjp,gpi->gpij', c_im, bb_re)
    kern = (jnp.einsum('glp,gpij->gijl', pw_re[:, :tc, :p], cb_re, precision=HIGHEST)
            - jnp.einsum('glp,gpij->gijl', pw_im[:, :tc, :p], cb_im, precision=HIGHEST))
    kern = kern.reshape(g, k * k, tc)
    sign = jnp.concatenate([-jnp.ones((p,), F32), jnp.ones((p,), F32)])
    flip = (jnp.arange(tc)[:, None] + jnp.arange(tc)[None, :] == tc - 1).astype(F32)
    pr2 = jnp.einsum('st,gtq->gsq', flip, pw_re[:, :tc], precision=HIGHEST)
    pi2 = jnp.einsum('st,gtq->gsq', flip, pw_im[:, :tc], precision=HIGHEST) * sign
    bb1 = jnp.concatenate([bb_re, bb_im], axis=1).transpose(0, 2, 1)
    bb2 = jnp.concatenate([bb_im, bb_re], axis=1).transpose(0, 2, 1)
    qr2 = pw_re[:, 1:]
    qi2 = pw_im[:, 1:]
    ca = jnp.concatenate([c_re, -c_im], axis=-1)
    cb = jnp.concatenate([-c_im, -c_re], axis=-1)
    pow_tab = jnp.stack([pr2, pi2, qr2, qi2], axis=1)
    coef_tab = jnp.stack([bb1, bb2, ca, cb], axis=1)
    a1 = pw_re[:, tc]
    a2 = pw_im[:, tc] * sign
    a_pack = jnp.concatenate([a1[:, None], a2[:, None], jnp.zeros((g, 6, 2 * p), F32)], axis=1)
    d_row = jnp.repeat(d_skip.reshape(g, k), tc, axis=1).reshape(g, 1, k * tc)
    return kern, pow_tab, coef_tab, a_pack, d_row


def _gelu_tanh(x):
    return 0.5 * x * (1.0 + jnp.tanh(math.sqrt(2.0 / math.pi) * (x + 0.044715 * (x * x * x))))


def _s5_out_kernel(y_ref, x_ref, mod_ref, w_ref, g2_ref, wr_ref, br_ref,
                   x_out_ref, hn_ref, eid_ref, wts_ref, cnt_ref):
    mod = mod_ref[0]
    h = y_ref.shape[1]
    yt = _gelu_tanh(y_ref[0].reshape(h, S5_TILE))
    d = w_ref.shape[1] // 2
    sub_rows = MOE_BLOCK
    n_sub = S5_TILE // sub_rows
    per_block = MOE_BLOCK // sub_rows

    def project(p):
        return lax.dot_general(yt[:, p * sub_rows:(p + 1) * sub_rows].astype(BF16), w_ref[...],
                               (((0,), (0,)), ((), ())), preferred_element_type=F32)

    o_next = project(0)
    cnt = None
    for p in range(n_sub):
        o = o_next
        if p + 1 < n_sub:
            o_next = project(p + 1)
        rows = slice(p * sub_rows, (p + 1) * sub_rows)
        out = o[:, :d] * jax.nn.sigmoid(o[:, d:])
        c = _residual_and_route(x_ref[0, rows, :], out, mod, g2_ref, wr_ref, br_ref,
                                x_out_ref.at[0, rows, :], hn_ref.at[rows, :], eid_ref.at[rows, :],
                                wts_ref.at[rows, :])
        cnt = c if p % per_block == 0 else cnt + c
        if p % per_block == per_block - 1:
            cnt_ref[p // per_block] = jnp.broadcast_to(cnt, cnt_ref.shape[1:])


def _mixer_out_specs(b, l, d, tile):
    nt = l // tile
    nblk = tile // MOE_BLOCK
    specs = [pl.BlockSpec((1, tile, d), lambda i, j: (i, j, 0)),
             pl.BlockSpec((tile, d), lambda i, j: (i * nt + j, 0)),
             pl.BlockSpec((tile, TOP_K), lambda i, j: (i * nt + j, 0)),
             pl.BlockSpec((tile, TOP_K), lambda i, j: (i * nt + j, 0)),
             pl.BlockSpec((nblk, 8, LANES), lambda i, j: (i * nt + j, 0, 0))]
    shapes = [jax.ShapeDtypeStruct((b, l, d), F32),
              jax.ShapeDtypeStruct((b * l, d), BF16),
              jax.ShapeDtypeStruct((b * l, TOP_K), jnp.int32),
              jax.ShapeDtypeStruct((b * l, TOP_K), F32),
              jax.ShapeDtypeStruct((b * l // MOE_BLOCK, 8, LANES), F32)]
    return specs, shapes


def _s5_out(y4, x, mod, w_bf, g2, wr, br):
    b, l, d = x.shape
    h = y4.shape[1]
    specs, shapes = _mixer_out_specs(b, l, d, S5_TILE)
    return pl.pallas_call(
        _s5_out_kernel,
        grid=(b, l // S5_TILE),
        in_specs=[pl.BlockSpec((1, h, S5_TILE // SSM_CHUNK, SSM_CHUNK), lambda i, j: (i, 0, j, 0)),
                  pl.BlockSpec((1, S5_TILE, d), lambda i, j: (i, j, 0)),
                  pl.BlockSpec((1, 8, d), lambda i, j: (i, 0, 0)),
                  pl.BlockSpec(w_bf.shape, lambda i, j: (0, 0), pipeline_mode=pl.Buffered(1)),
                  pl.BlockSpec((1, d), lambda i, j: (0, 0)),
                  pl.BlockSpec((d, 2 * LANES), lambda i, j: (0, 0)),
                  pl.BlockSpec((1, LANES), lambda i, j: (0, 0))],
        out_specs=specs,
        out_shape=shapes,
        compiler_params=_cparams("parallel", "parallel"),
        name="s5_out",
    )(y4, x, mod, w_bf, g2, wr, br)


def _conv_mixer_body(x_in_ref, mod_ref, g1_ref, win_ref, cw_ref, wout_ref, g2_ref, wr_ref, br_ref,
                     x_out_ref, hn_ref, eid_ref, wts_ref, cnt_ref, carry_ref):
    @pl.when(pl.program_id(1) == 0)
    def _():
        carry_ref[...] = jnp.zeros_like(carry_ref)

    mod = mod_ref[0]
    d = x_in_ref.shape[1]
    cw = cw_ref[...]
    n_sub = x_in_ref.shape[0] // SUB_ROWS
    sub = [slice(s * SUB_ROWS, (s + 1) * SUB_ROWS) for s in range(n_sub)]

    def project(s):
        hn = _norm_mod(x_in_ref[sub[s], :], g1_ref[...], mod[0:1], mod[1:2])
        return jnp.dot(hn.astype(BF16), win_ref[...], preferred_element_type=F32)

    def gate_conv(p, prev_tail):
        u = p[:, d:2 * d] * p[:, 2 * d:]
        ext = jnp.concatenate([prev_tail, u], axis=0)
        z = cw[0:1] * ext[6:6 + SUB_ROWS] + cw[1:2] * ext[7:7 + SUB_ROWS] + cw[2:3] * u
        return (p[:, :d] * z).astype(BF16), u[SUB_ROWS - 8:]

    ps = [project(s) for s in range(n_sub)]
    tail = carry_ref[...]
    outs = []
    for s in range(n_sub):
        gated, tail = gate_conv(ps[s], tail)
        outs.append(jnp.dot(gated, wout_ref[...], preferred_element_type=F32))
    carry_ref[...] = tail
    cnt = None
    for s in range(n_sub):
        c = _residual_and_route(x_in_ref[sub[s], :], outs[s], mod, g2_ref, wr_ref, br_ref,
                                x_out_ref.at[0, sub[s], :], hn_ref.at[sub[s], :], eid_ref.at[sub[s], :],
                                wts_ref.at[sub[s], :])
        cnt = c if s == 0 else cnt + c
    cnt_ref[0] = jnp.broadcast_to(cnt, cnt_ref.shape[1:])


_TN = (((0,), (0,)), ((), ()))


MAX_CHUNKS = LOCAL_ROWS // CHUNK_ROWS


def _rows_copy(src, s_row, dst, d_row, rows, sem):
    return pltpu.make_async_copy(src.at[pl.ds(s_row, rows), :], dst.at[pl.ds(d_row, rows), :], sem)


def _start_block_chunks(blk, count, crow_ref, make_copy):
    def body(q, carry):
        make_copy(pl.multiple_of(q * CHUNK_ROWS, CHUNK_ROWS),
                  pl.multiple_of(crow_ref[blk * MAX_CHUNKS + q], CHUNK_ROWS)).start()
        return carry

    lax.fori_loop(0, count, body, 0)


def _wait_chunks(count, max_count, make_wait):
    for k in range(max_count.bit_length()):
        @pl.when(((count >> k) & 1) == 1)
        def _():
            make_wait(CHUNK_ROWS << k).wait()


def _slot_masks(lp0, lp1):
    slot = lax.broadcasted_iota(jnp.int32, (lp0.shape[0], LOCAL_ROWS), 1)
    return slot == lp0, slot == lp1


_HI16 = -65536


def _pack_bf16_pairs(x):
    h = x.shape[1] // 2
    lo = lax.shift_right_logical(pltpu.bitcast(x[:, :h], jnp.int32), 16)
    hi = pltpu.bitcast(x[:, h:], jnp.int32) & _HI16
    return hi | lo


def _unpack_bf16_pairs(w):
    lo = pltpu.bitcast(w << 16, F32).astype(BF16)
    hi = pltpu.bitcast(w & _HI16, F32).astype(BF16)
    return lo, hi


def _moe_dispatch_kernel(crow_ref, tot_ref, padn_ref, pads_ref, nv_ref,
                         eid_ref, wts_ref, hn_ref, lofff_ref, tri_ref, xs_ref, lpos_ref, loc_ref, zero_ref, sems):
    blk = pl.program_id(0)
    d = hn_ref.shape[1]
    eid = eid_ref[...]
    t = eid.shape[0]
    lane = lax.broadcasted_iota(jnp.int32, (t, LANES), 1)
    hit0 = lane == eid[:, 0:1]
    hit1 = lane == eid[:, 1:2]
    onehot = (hit0 | hit1).astype(BF16)
    before = jnp.dot(tri_ref[...], onehot, preferred_element_type=F32) + lofff_ref[0]
    lp0 = jnp.sum(jnp.where(hit0, before, 0.0), axis=-1, keepdims=True).astype(jnp.int32)
    lp1 = jnp.sum(jnp.where(hit1, before, 0.0), axis=-1, keepdims=True).astype(jnp.int32)
    col = lax.broadcasted_iota(jnp.int32, (t, TOP_K), 1)
    lpos_ref[...] = jnp.where(col == 0, lp0, lp1)
    wts = wts_ref[...]

    buf = blk % 2
    loc = loc_ref.at[buf]

    def drain(which, count):
        _wait_chunks(count, MAX_CHUNKS,
                     lambda rows: _rows_copy(loc_ref.at[which], 0, xs_ref, 0, rows, sems.at[which]))

    @pl.when(blk >= 2)
    def _():
        drain(buf, tot_ref[blk - 2])

    m0, m1 = _slot_masks(lp0, lp1)
    wrow = jnp.sum(jnp.where(m0, wts[:, 0:1], 0.0) + jnp.where(m1, wts[:, 1:2], 0.0), axis=0, keepdims=True)
    sorted_rows = lax.dot_general((m0 | m1).astype(BF16), hn_ref[...], _TN, preferred_element_type=F32)
    loc[:, :d // 2] = _pack_bf16_pairs(sorted_rows)
    loc[:, d // 2:] = pltpu.bitcast(jnp.broadcast_to(wrow, (LANES, LOCAL_ROWS)).T, jnp.int32)

    _start_block_chunks(blk, tot_ref[blk], crow_ref,
                        lambda lo, go: _rows_copy(loc, lo, xs_ref, go, CHUNK_ROWS, sems.at[buf]))

    @pl.when(blk == pl.num_programs(0) - 1)
    def _():
        drain(1 - buf, tot_ref[blk - 1])
        drain(buf, tot_ref[blk])
        sem = sems.at[0]
        zero_ref[...] = jnp.zeros_like(zero_ref)
        pad_bits = (EXPERT_TILE // CHUNK_ROWS - 1).bit_length()

        def pad_copies(e, carry, *, wait):
            padn = padn_ref[e]
            for k in range(pad_bits):
                @pl.when(((padn >> k) & 1) == 1)
                def _():
                    done = (padn & ((1 << k) - 1)) * CHUNK_ROWS
                    cp = _rows_copy(zero_ref, 0, xs_ref, pl.multiple_of(pads_ref[e] + done, CHUNK_ROWS),
                                    CHUNK_ROWS << k, sem)
                    if wait:
                        cp.wait()
                    else:
                        cp.start()
            return carry

        lax.fori_loop(0, N_EXPERTS, functools.partial(pad_copies, wait=False), 0)
        lax.fori_loop(0, N_EXPERTS, functools.partial(pad_copies, wait=True), 0)

        def tile_copy(i):
            return pltpu.make_async_copy(zero_ref, xs_ref.at[pl.ds(pl.multiple_of(i * EXPERT_TILE, EXPERT_TILE),
                                                                   EXPERT_TILE), :], sem)

        n_tiles = xs_ref.shape[0] // EXPERT_TILE

        def start_tile(i, c):
            tile_copy(i).start()
            return c

        def wait_tile(i, c):
            tile_copy(i).wait()
            return c

        lax.fori_loop(nv_ref[0], n_tiles, start_tile, 0)
        lax.fori_loop(nv_ref[0], n_tiles, wait_tile, 0)


def _moe_dispatch(tables, eid, wts, hn, loff_f, n_rows):
    t, d = hn.shape
    pos = jnp.arange(MOE_BLOCK, dtype=jnp.int32)
    tri = (pos[None, :] < pos[:, None]).astype(BF16)
    width = d // 2 + LANES
    grid_spec = pltpu.PrefetchScalarGridSpec(
        num_scalar_prefetch=5,
        grid=(t // MOE_BLOCK,),
        in_specs=[pl.BlockSpec((MOE_BLOCK, TOP_K), lambda i, *_: (i, 0)),
                  pl.BlockSpec((MOE_BLOCK, TOP_K), lambda i, *_: (i, 0)),
                  pl.BlockSpec((MOE_BLOCK, d), lambda i, *_: (i, 0)),
                  pl.BlockSpec((1, 1, LANES), lambda i, *_: (i, 0, 0)),
                  pl.BlockSpec((MOE_BLOCK, MOE_BLOCK), lambda i, *_: (0, 0))],
        out_specs=[pl.BlockSpec(memory_space=pl.ANY),
                   pl.BlockSpec((MOE_BLOCK, TOP_K), lambda i, *_: (i, 0))],
        scratch_shapes=[pltpu.VMEM((2, LOCAL_ROWS, width), jnp.int32), pltpu.VMEM((EXPERT_TILE, width), jnp.int32),
                        pltpu.SemaphoreType.DMA((2,))],
    )
    return pl.pallas_call(
        _moe_dispatch_kernel,
        grid_spec=grid_spec,
        out_shape=[jax.ShapeDtypeStruct((n_rows, width), jnp.int32),
                   jax.ShapeDtypeStruct((t, TOP_K), jnp.int32)],
        compiler_params=_cparams("arbitrary"),
        name="moe_dispatch",
    )(*tables, eid, wts, hn, loff_f, tri)


def _moe_experts_kernel(te_ref, tv_ref, tf_ref, ts_ref, xs_ref, w1_ref, w3_ref, w2_ref, o_ref,
                        w1b_ref, w3b_ref, w2b_ref):
    del te_ref, ts_ref
    i = pl.program_id(0)

    @pl.when(tf_ref[i] != 0)
    def _():
        def cast_rows(r, carry):
            rows = pl.ds(pl.multiple_of(r * LANES, LANES), LANES)
            w1b_ref[rows, :] = w1_ref[0, 0, rows, :].astype(BF16)
            w3b_ref[rows, :] = w3_ref[0, 0, rows, :].astype(BF16)

            @pl.when(r < w2b_ref.shape[0] // LANES)
            def _():
                w2b_ref[rows, :] = w2_ref[0, 0, rows, :].astype(BF16)

            return carry

        lax.fori_loop(0, w1b_ref.shape[0] // LANES, cast_rows, 0)

    @pl.when(tv_ref[i] != 0)
    def _():
        half = o_ref.shape[1]
        x_lo, x_hi = _unpack_bf16_pairs(xs_ref[:, :half])
        w = pltpu.bitcast(xs_ref[:, half:half + 1], F32)
        de = w1b_ref.shape[1]
        o = None
        for c0 in range(0, de, MXU_WIDTH):
            cols = slice(c0, c0 + MXU_WIDTH)
            a = (jnp.dot(x_lo, w1b_ref[:half, cols], preferred_element_type=F32)
                 + jnp.dot(x_hi, w1b_ref[half:, cols], preferred_element_type=F32))
            b = (jnp.dot(x_lo, w3b_ref[:half, cols], preferred_element_type=F32)
                 + jnp.dot(x_hi, w3b_ref[half:, cols], preferred_element_type=F32))
            h = (a * jax.nn.sigmoid(a) * b).astype(BF16)
            part = jnp.dot(h, w2b_ref[cols, :], preferred_element_type=F32)
            o = part if o is None else o + part
        o = o * w
        o_ref[...] = _pack_bf16_pairs(o.astype(BF16).astype(F32))

    @pl.when(tv_ref[i] == 0)
    def _():
        o_ref[...] = jnp.zeros_like(o_ref)


def _moe_experts(tile_expert, tile_valid, tile_first, tile_src, xs, w1, w3, w2, layer):
    r, width = xs.shape
    d = (width - LANES) * 2
    de = w1.shape[3]
    grid_spec = pltpu.PrefetchScalarGridSpec(
        num_scalar_prefetch=4,
        grid=(r // EXPERT_TILE,),
        in_specs=[pl.BlockSpec((EXPERT_TILE, width), lambda i, te, tv, tf, ts: (ts[i], 0)),
                  pl.BlockSpec((1, 1, d, de), lambda i, te, tv, tf, ts: (layer, te[i], 0, 0)),
                  pl.BlockSpec((1, 1, d, de), lambda i, te, tv, tf, ts: (layer, te[i], 0, 0)),
                  pl.BlockSpec((1, 1, de, d), lambda i, te, tv, tf, ts: (layer, te[i], 0, 0))],
        out_specs=pl.BlockSpec((EXPERT_TILE, d // 2), lambda i, te, tv, tf, ts: (i, 0)),
        scratch_shapes=[pltpu.VMEM((d, de), BF16), pltpu.VMEM((d, de), BF16), pltpu.VMEM((de, d), BF16)],
    )
    return pl.pallas_call(
        _moe_experts_kernel,
        grid_spec=grid_spec,
        out_shape=jax.ShapeDtypeStruct((r, d // 2), jnp.int32),
        compiler_params=_cparams("arbitrary"),
        name="moe_experts",
    )(tile_expert, tile_valid, tile_first, tile_src, xs, w1, w3, w2)


def _combine_rows(crow_ref, lpos_ref, o_hbm_ref, loc_a_ref, loc_b_ref, sems, consume):
    blk = pl.program_id(0) * pl.num_programs(1) + pl.program_id(1)
    n_blk = pl.num_programs(0) * pl.num_programs(1)

    def fetch(b, loc, sem):
        for q in range(MAX_CHUNKS):
            _rows_copy(o_hbm_ref, pl.multiple_of(crow_ref[b * MAX_CHUNKS + q], CHUNK_ROWS), loc,
                       q * CHUNK_ROWS, CHUNK_ROWS, sem).start()

    def wait_all(loc, sem):
        _rows_copy(o_hbm_ref, 0, loc, 0, LOCAL_ROWS, sem).wait()

    @pl.when(blk == 0)
    def _():
        fetch(blk, loc_a_ref, sems.at[0])

    def run(cur, cur_sem, nxt, nxt_sem):
        wait_all(cur, cur_sem)
        fetch(jnp.minimum(blk + 1, n_blk - 1), nxt, nxt_sem)
        lp = lpos_ref[...]
        m0, m1 = _slot_masks(lp[:, 0:1], lp[:, 1:2])
        pt = (m0 | m1).astype(BF16)
        o_lo, o_hi = _unpack_bf16_pairs(cur[...])
        consume(jnp.concatenate([jnp.dot(pt, o_lo, preferred_element_type=F32),
                                 jnp.dot(pt, o_hi, preferred_element_type=F32)], axis=1))

        @pl.when(blk == n_blk - 1)
        def _():
            wait_all(nxt, nxt_sem)

    @pl.when(blk % 2 == 0)
    def _():
        run(loc_a_ref, sems.at[0], loc_b_ref, sems.at[1])

    @pl.when(blk % 2 == 1)
    def _():
        run(loc_b_ref, sems.at[1], loc_a_ref, sems.at[0])


def _moe_combine_final_kernel(crow_ref, tot_ref, lpos_ref, x_ref, mod_ref, fg_ref, o_hbm_ref,
                              out_ref, loc_a_ref, loc_b_ref, sems):
    del tot_ref

    def finish(y):
        x2 = x_ref[0] + (1.0 + mod_ref[0, 5:6]) * y
        out_ref[0] = (x2 * lax.rsqrt(jnp.mean(x2 * x2, axis=-1, keepdims=True) + RMS_EPS)) * fg_ref[...]

    _combine_rows(crow_ref, lpos_ref, o_hbm_ref, loc_a_ref, loc_b_ref, sems, finish)


def _moe_combine_conv_kernel(crow_ref, tot_ref, lpos_ref, x_ref, mod_prev_ref, o_hbm_ref,
                             mod_ref, g1_ref, win_ref, cw_ref, wout_ref, g2_ref, wr_ref, br_ref,
                             x_out_ref, hn_ref, eid_ref, wts_ref, cnt_ref,
                             loc_a_ref, loc_b_ref, sems, xmid_ref, carry_ref):
    del tot_ref

    def residual(y):
        xmid_ref[...] = x_ref[0] + (1.0 + mod_prev_ref[0, 5:6]) * y

    _combine_rows(crow_ref, lpos_ref, o_hbm_ref, loc_a_ref, loc_b_ref, sems, residual)
    _conv_mixer_body(xmid_ref, mod_ref, g1_ref, win_ref, cw_ref, wout_ref, g2_ref, wr_ref, br_ref,
                     x_out_ref, hn_ref, eid_ref, wts_ref, cnt_ref, carry_ref)


def _combine_in_specs(nt, d):
    return [pl.BlockSpec((MOE_BLOCK, TOP_K), lambda i, j, *_: (i * nt + j, 0)),
            pl.BlockSpec((1, MOE_BLOCK, d), lambda i, j, *_: (i, j, 0)),
            pl.BlockSpec((1, 8, d), lambda i, j, *_: (i, 0, 0))]


def _combine_scratch(d):
    return [pltpu.VMEM((LOCAL_ROWS, d // 2), jnp.int32), pltpu.VMEM((LOCAL_ROWS, d // 2), jnp.int32),
            pltpu.SemaphoreType.DMA((2,))]


def _moe_combine_final(tables, lpos, x, mod, final_g, o_sorted):
    b, l, d = x.shape
    nt = l // MOE_BLOCK
    grid_spec = pltpu.PrefetchScalarGridSpec(
        num_scalar_prefetch=2,
        grid=(b, nt),
        in_specs=_combine_in_specs(nt, d) + [pl.BlockSpec((1, d), lambda i, j, *_: (0, 0)),
                                             pl.BlockSpec(memory_space=pl.ANY)],
        out_specs=pl.BlockSpec((1, MOE_BLOCK, d), lambda i, j, *_: (i, j, 0)),
        scratch_shapes=_combine_scratch(d),
    )
    return pl.pallas_call(
        _moe_combine_final_kernel,
        grid_spec=grid_spec,
        out_shape=jax.ShapeDtypeStruct((b, l, d), F32),
        compiler_params=_cparams("arbitrary", "arbitrary"),
        name="moe_combine_final",
    )(*tables[:2], lpos, x, mod, final_g, o_sorted)


def _moe_combine_conv(tables, lpos, x, mod_prev, o_sorted, mod, g1, win_bf, cw8, wout_bf, g2, wr, br):
    b, l, d = x.shape
    nt = l // MOE_BLOCK
    specs, shapes = _mixer_out_specs(b, l, d, TOKEN_TILE)
    const = lambda shape: pl.BlockSpec(shape, lambda i, j, *_: (0,) * len(shape))
    grid_spec = pltpu.PrefetchScalarGridSpec(
        num_scalar_prefetch=2,
        grid=(b, nt),
        in_specs=_combine_in_specs(nt, d) + [
            pl.BlockSpec(memory_space=pl.ANY),
            pl.BlockSpec((1, 8, d), lambda i, j, *_: (i, 0, 0)),
            const((1, d)), const(win_bf.shape), const((8, d)), const(wout_bf.shape),
            const((1, d)), const((d, 2 * LANES)), const((1, LANES))],
        out_specs=[pl.BlockSpec(s.block_shape, lambda i, j, *_, f=s.index_map: f(i, j)) for s in specs],
        scratch_shapes=_combine_scratch(d) + [pltpu.VMEM((MOE_BLOCK, d), F32), pltpu.VMEM((8, d), F32)],
    )
    return pl.pallas_call(
        _moe_combine_conv_kernel,
        grid_spec=grid_spec,
        out_shape=shapes,
        compiler_params=_cparams("arbitrary", "arbitrary"),
        name="moe_combine_conv",
    )(*tables[:2], lpos, x, mod_prev, o_sorted, mod, g1, win_bf, cw8, wout_bf, g2, wr, br)


def _moe_sorted_experts(hn, eid, wts, cnt, w1, w3, w2, layer):
    t, d = hn.shape
    nblk = t // MOE_BLOCK
    i32 = jnp.int32
    n = cnt[:, 0, :N_EXPERTS].astype(i32)
    run = (n + CHUNK_ROWS - 1) // CHUNK_ROWS * CHUNK_ROWS
    loff = jnp.cumsum(run, axis=1) - run
    rows_e = jnp.sum(run, axis=0)
    tiles_e = (rows_e + EXPERT_TILE - 1) // EXPERT_TILE
    tile_end = jnp.cumsum(tiles_e)
    base = (tile_end - tiles_e) * EXPERT_TILE
    goff = base[None, :] + jnp.cumsum(run, axis=0) - run
    nch = run // CHUNK_ROWS
    tot = jnp.sum(nch, axis=1)
    ch_end = jnp.cumsum(nch, axis=1)
    q = jnp.arange(MAX_CHUNKS, dtype=i32)
    in_run = ((q[None, :, None] >= (ch_end - nch)[:, None, :]) & (q[None, :, None] < ch_end[:, None, :])).astype(i32)
    crow = jnp.sum(in_run * (goff[:, None, :] + (q[None, :, None] - (ch_end - nch)[:, None, :]) * CHUNK_ROWS), axis=-1)
    padn = (tiles_e * EXPERT_TILE - rows_e) // CHUNK_ROWS
    pads = base + rows_e
    tables = (crow.reshape(-1), tot, padn, pads, tile_end[-1:])
    loff_f = jnp.zeros((nblk, 1, LANES), F32).at[:, 0, :N_EXPERTS].set(loff.astype(F32))

    max_rows = t * TOP_K + nblk * N_EXPERTS * (CHUNK_ROWS - 1) + N_EXPERTS * (EXPERT_TILE - 1)
    max_tiles = -(-max_rows // EXPERT_TILE)
    tile_ids = jnp.arange(max_tiles, dtype=i32)
    n_valid = tile_end[-1]
    tile_src = jnp.minimum(tile_ids, n_valid - 1)
    tile_expert = jnp.sum((tile_src[:, None] >= tile_end[None, :]).astype(i32), axis=1)
    tile_valid = (tile_ids < n_valid).astype(i32)
    tile_first = jnp.concatenate([jnp.ones((1,), i32), (tile_expert[1:] != tile_expert[:-1]).astype(i32)])

    xs, lpos = _moe_dispatch(tables, eid, wts, hn, loff_f, max_tiles * EXPERT_TILE)
    o_sorted = _moe_experts(tile_expert, tile_valid, tile_first, tile_src, xs, w1, w3, w2, layer)
    return tables, lpos, o_sorted


def _router_pack(wg, bg, we, be):
    d = wg.shape[0]
    wr = jnp.zeros((d, LANES), F32).at[:, :N_EXPERTS].set(we).at[:, N_EXPERTS:N_EXPERTS + N_GROUPS].set(wg)
    br = jnp.zeros((1, LANES), F32).at[0, :N_EXPERTS].set(be).at[0, N_EXPERTS:N_EXPERTS + N_GROUPS].set(bg)
    wr_hi = wr.astype(BF16)
    wr_lo = (wr - wr_hi.astype(F32)).astype(BF16)
    return jnp.concatenate([wr_hi, wr_lo], axis=1), br


def kernel(x, c, ada_w, ada_b, norm1_g, norm2_g, ssm_w_in, ssm_lam_re, ssm_lam_im, ssm_log_dt, ssm_b_re, ssm_b_im, ssm_c_re, ssm_c_im, ssm_d, ssm_w_glu, conv_w_in, conv_w, conv_w_out, moe_wg, moe_bg, moe_we, moe_be, moe_w1, moe_w3, moe_w2, final_g):
    b, l, d = x.shape
    depth = ada_w.shape[0]
    c8 = jnp.zeros((8, d), F32).at[:b].set(c)
    mod_all = _adaln(c8, ada_w, ada_b)[:, :b].reshape(depth, b, 6, d)
    mod_all = jnp.concatenate([mod_all, jnp.zeros((depth, b, 2, d), F32)], axis=2)
    fg = final_g.reshape(1, d)

    mod = mod_all[0]
    u4 = _s5_in(x, mod, norm1_g[0:1], ssm_w_in[0].T.astype(BF16))
    kern, pow_tab, coef_tab, a_pack, d_row = _s5_tables(
        ssm_lam_re[0], ssm_lam_im[0], ssm_log_dt[0], ssm_b_re[0], ssm_b_im[0],
        ssm_c_re[0], ssm_c_im[0], ssm_d[0])
    y4 = _s5_scan(u4, kern, pow_tab, coef_tab, a_pack, d_row)
    wr, br = _router_pack(moe_wg[0], moe_bg[0], moe_we[0], moe_be[0])
    x1, hn, eid, wts, cnt = _s5_out(y4, x, mod, ssm_w_glu[0].astype(BF16), norm2_g[0:1], wr, br)
    tables, lpos, o_sorted = _moe_sorted_experts(hn, eid, wts, cnt, moe_w1, moe_w3, moe_w2, 0)

    mod1 = mod_all[1]
    cw8 = jnp.zeros((8, d), F32).at[:conv_w.shape[1]].set(conv_w[0])
    wr, br = _router_pack(moe_wg[1], moe_bg[1], moe_we[1], moe_be[1])
    x3, hn, eid, wts, cnt = _moe_combine_conv(
        tables, lpos, x1, mod, o_sorted, mod1, norm1_g[1:2], conv_w_in[0].astype(BF16), cw8,
        conv_w_out[0].astype(BF16), norm2_g[1:2], wr, br)
    tables, lpos, o_sorted = _moe_sorted_experts(hn, eid, wts, cnt, moe_w1, moe_w3, moe_w2, 1)
    return _moe_combine_final(tables, lpos, x3, mod1, fg, o_sorted)
```

```python
import functools
import math

import jax
import jax.numpy as jnp
from jax import lax
from jax.experimental import pallas as pl
from jax.experimental.pallas import tpu as pltpu

F32 = jnp.float32
BF16 = jnp.bfloat16
HIGHEST = lax.Precision.HIGHEST

RMS_EPS = 1e-6
SSM_GROUP = 16
SSM_CHUNK = 128
N_GROUPS = 4
EXPERTS_PER_GROUP = 8
N_EXPERTS = N_GROUPS * EXPERTS_PER_GROUP
TOP_K = 2
LANES = 128
MXU_WIDTH = 256
TOKEN_TILE = 512
S5_TILE = 1024
S5_K_BLOCKS = 2
SUB_ROWS = 256
MOE_BLOCK = TOKEN_TILE
CHUNK_ROWS = 8
LOCAL_ROWS = -(-(TOP_K * MOE_BLOCK + N_EXPERTS * (CHUNK_ROWS - 1)) // LANES) * LANES
EXPERT_TILE = 512
VMEM_LIMIT = 56 * 1024 * 1024
NEG_INF = -1e30


def _cparams(*sem):
    return pltpu.CompilerParams(dimension_semantics=sem, vmem_limit_bytes=VMEM_LIMIT)


def _adaln_kernel(c_ref, w_ref, b_ref, o_ref):
    c = c_ref[...]
    cond = c * jax.nn.sigmoid(c)
    o_ref[0] = jnp.dot(cond, w_ref[0], precision=HIGHEST, preferred_element_type=F32) + b_ref[0]


def _adaln(c8, ada_w, ada_b):
    depth, d, n = ada_w.shape
    tn = 1536
    return pl.pallas_call(
        _adaln_kernel,
        grid=(depth, n // tn),
        in_specs=[pl.BlockSpec((8, d), lambda i, j: (0, 0)),
                  pl.BlockSpec((1, d, tn), lambda i, j: (i, 0, j)),
                  pl.BlockSpec((1, 1, tn), lambda i, j: (i, 0, j))],
        out_specs=pl.BlockSpec((1, 8, tn), lambda i, j: (i, 0, j)),
        out_shape=jax.ShapeDtypeStruct((depth, 8, n), F32),
        compiler_params=_cparams("parallel", "parallel"),
        name="adaln",
    )(c8, ada_w, ada_b.reshape(depth, 1, n))


def _norm_mod(x, g, shift, scale):
    y = x * lax.rsqrt(jnp.mean(x * x, axis=-1, keepdims=True) + RMS_EPS)
    return (y * g) * (1.0 + scale) + shift


def _route(hn, hn_hi, wr_ref, br_ref, eid_ref, wts_ref):
    hn_lo = (hn - hn_hi.astype(F32)).astype(BF16)
    both = jnp.dot(hn_hi, wr_ref[...], preferred_element_type=F32)
    logits = (both[:, :LANES] + both[:, LANES:]
              + jnp.dot(hn_lo, wr_ref[:, :LANES], preferred_element_type=F32)) + br_ref[...]
    lane = lax.broadcasted_iota(jnp.int32, logits.shape, 1)
    is_grp = (lane >= N_EXPERTS) & (lane < N_EXPERTS + N_GROUPS)
    lg = jnp.where(is_grp, logits, NEG_INF)
    gmax = jnp.max(lg, axis=-1, keepdims=True)
    gsum = jnp.sum(jnp.where(is_grp, jnp.exp(lg - gmax), 0.0), axis=-1, keepdims=True)
    gp = 1.0 / gsum
    gi = jnp.min(jnp.where(lg == gmax, lane, 2 * LANES), axis=-1, keepdims=True) - N_EXPERTS
    in_grp = (lane < N_EXPERTS) & ((lane // EXPERTS_PER_GROUP) == gi)
    le = jnp.where(in_grp, logits, NEG_INF)
    v1 = jnp.max(le, axis=-1, keepdims=True)
    i1 = jnp.min(jnp.where(le == v1, lane, 2 * LANES), axis=-1, keepdims=True)
    le2 = jnp.where(lane == i1, NEG_INF, le)
    v2 = jnp.max(le2, axis=-1, keepdims=True)
    i2 = jnp.min(jnp.where(le2 == v2, lane, 2 * LANES), axis=-1, keepdims=True)
    e2 = jnp.exp(v2 - v1)
    den = 1.0 + e2
    col = lax.broadcasted_iota(jnp.int32, (hn.shape[0], TOP_K), 1)
    eid_ref[...] = jnp.where(col == 0, i1, i2)
    wts_ref[...] = jnp.where(col == 0, gp / den, gp * e2 / den)
    chosen = ((lane == i1) | (lane == i2)).astype(F32)
    return jnp.sum(chosen, axis=0, keepdims=True)


def _residual_and_route(x, out, mod, g2_ref, wr_ref, br_ref, x_out_ref, hn_ref, eid_ref, wts_ref):
    x1 = x + (1.0 + mod[2:3]) * out
    x_out_ref[...] = x1
    hn = _norm_mod(x1, g2_ref[...], mod[3:4], mod[4:5])
    hn_hi = hn.astype(BF16)
    hn_ref[...] = hn_hi
    return _route(hn, hn_hi, wr_ref, br_ref, eid_ref, wts_ref)


def _s5_in_kernel(x_ref, mod_ref, g_ref, wt_ref, u_ref):
    mod = mod_ref[0]
    hn = _norm_mod(x_ref[0], g_ref[...], mod[0:1], mod[1:2])
    ut = lax.dot_general(wt_ref[...], hn.astype(BF16), (((1,), (1,)), ((), ())), preferred_element_type=F32)
    u_ref[0] = ut.reshape(ut.shape[0], S5_TILE // SSM_CHUNK, SSM_CHUNK)


def _s5_in(x, mod, g, wt_bf):
    b, l, d = x.shape
    h = wt_bf.shape[0]
    cpt = S5_TILE // SSM_CHUNK
    return pl.pallas_call(
        _s5_in_kernel,
        grid=(b, l // S5_TILE),
        in_specs=[pl.BlockSpec((1, S5_TILE, d), lambda i, j: (i, j, 0)),
                  pl.BlockSpec((1, 8, d), lambda i, j: (i, 0, 0)),
                  pl.BlockSpec((1, d), lambda i, j: (0, 0)),
                  pl.BlockSpec((h, d), lambda i, j: (0, 0))],
        out_specs=pl.BlockSpec((1, h, cpt, SSM_CHUNK), lambda i, j: (i, 0, j, 0)),
        out_shape=jax.ShapeDtypeStruct((b, h, l // SSM_CHUNK, SSM_CHUNK), F32),
        compiler_params=_cparams("parallel", "parallel"),
        name="s5_in",
    )(x, mod, g, wt_bf)


def _s5_scan_kernel(u_ref, k_ref, kn_ref, pow_ref, coef_ref, a_ref, d_ref, y_ref,
                    m0_ref, m1_ref, acc_ref, s_ref, sw_ref, sp_ref):
    bsz, grp, n_chunks, tc = u_ref.shape
    g = pl.program_id(0)
    srow = lax.broadcasted_iota(jnp.int32, (tc, tc), 0)
    tcol = lax.broadcasted_iota(jnp.int32, (tc, tc), 1)
    causal = tcol >= srow

    def build_rows(src_ref, dst_ref, i):
        blks = []
        for j in range(grp):
            row = src_ref[0, pl.ds(i * grp + j, 1), :]
            blk = pltpu.roll(jnp.broadcast_to(row, (tc, tc)), 0, 1, stride=1, stride_axis=0)
            blks.append(jnp.where(causal, blk, 0.0))
        dst_ref[pl.ds(pl.multiple_of(i * tc, tc), tc), :] = jnp.concatenate(blks, axis=1).astype(BF16)

    @pl.when(g == 0)
    def _():
        def first(i, carry):
            build_rows(k_ref, m0_ref, i)
            return carry

        lax.fori_loop(0, grp, first, 0)

    x = jnp.concatenate(
        [jnp.concatenate([u_ref[b, k] for k in range(grp)], axis=1) for b in range(bsz)], axis=0)
    xb = x.astype(BF16)
    def expand(pa, pb, ca, cb):
        return jnp.concatenate(
            [(pow_ref[0, pa] * coef_ref[0, ca, i:i + 1, :] + pow_ref[0, pb] * coef_ref[0, cb, i:i + 1, :]).astype(BF16)
             for i in range(grp)], axis=0)

    f_mat = expand(0, 1, 0, 1)
    s_loc = jnp.dot(xb, f_mat, preferred_element_type=F32)
    s_ref[...] = s_loc
    half = s_loc.shape[1] // 2
    sw_ref[...] = jnp.concatenate([s_loc[:, half:], s_loc[:, :half]], axis=1)
    a1 = a_ref[0, 0:1, :]
    a2 = a_ref[0, 1:2, :]
    s = [jnp.zeros((1, s_loc.shape[1]), F32) for _ in range(bsz)]
    sw = [jnp.zeros((1, s_loc.shape[1]), F32) for _ in range(bsz)]
    for c in range(n_chunks):
        for b in range(bsz):
            r = b * n_chunks + c
            sp_ref[r:r + 1, :] = s[b]
            s_new = a1 * s[b] + a2 * sw[b] + s_ref[r:r + 1, :]
            sw[b] = a1 * sw[b] - a2 * s[b] + sw_ref[r:r + 1, :]
            s[b] = s_new
    et_mat = expand(2, 3, 2, 3)
    acc_ref[...] = lax.dot_general(sp_ref[...].astype(BF16), et_mat, (((1,), (1,)), ((), ())),
                                   preferred_element_type=F32) + d_ref[0] * x

    def run(m_cur_ref, m_next_ref):
        def step(c, carry):
            for h in range(S5_K_BLOCKS):
                build_rows(kn_ref, m_next_ref, c * S5_K_BLOCKS + h)
            xc = jnp.concatenate(
                [jnp.concatenate([u_ref[b, c * S5_K_BLOCKS + h] for h in range(S5_K_BLOCKS)], axis=1)
                 for b in range(bsz)], axis=0).astype(BF16)
            rows = pl.ds(pl.multiple_of(c * (S5_K_BLOCKS * tc), S5_K_BLOCKS * tc), S5_K_BLOCKS * tc)
            acc_ref[...] += jnp.dot(xc, m_cur_ref[rows, :], preferred_element_type=F32)
            return carry

        lax.fori_loop(0, grp // S5_K_BLOCKS, step, 0)

    @pl.when(g % 2 == 0)
    def _():
        run(m0_ref, m1_ref)

    @pl.when(g % 2 == 1)
    def _():
        run(m1_ref, m0_ref)
    for b in range(bsz):
        for j in range(grp):
            y_ref[b, j] = acc_ref[b * n_chunks:(b + 1) * n_chunks, j * tc:(j + 1) * tc]


def _s5_scan(u4, kern, pow_tab, coef_tab, a_pack, d_row):
    b, h, nc, tc = u4.shape
    g = kern.shape[0]
    grp = h // g
    k = grp * tc
    p2 = pow_tab.shape[3]
    rows = b * nc
    return pl.pallas_call(
        _s5_scan_kernel,
        grid=(g,),
        in_specs=[pl.BlockSpec((b, grp, nc, tc), lambda i: (0, i, 0, 0)),
                  pl.BlockSpec((1, grp * grp, tc), lambda i: (i, 0, 0)),
                  pl.BlockSpec((1, grp * grp, tc), lambda i: (jnp.minimum(i + 1, g - 1), 0, 0)),
                  pl.BlockSpec((1, 4, tc, p2), lambda i: (i, 0, 0, 0)),
                  pl.BlockSpec((1, 4, grp, p2), lambda i: (i, 0, 0, 0)),
                  pl.BlockSpec((1, 8, p2), lambda i: (i, 0, 0)),
                  pl.BlockSpec((1, 1, k), lambda i: (i, 0, 0))],
        out_specs=pl.BlockSpec((b, grp, nc, tc), lambda i: (0, i, 0, 0)),
        out_shape=jax.ShapeDtypeStruct(u4.shape, F32),
        scratch_shapes=[pltpu.VMEM((k, k), BF16), pltpu.VMEM((k, k), BF16),
                        pltpu.VMEM((rows, k), F32), pltpu.VMEM((rows, p2), F32),
                        pltpu.VMEM((rows, p2), F32), pltpu.VMEM((rows, p2), F32)],
        compiler_params=_cparams("arbitrary"),
        name="s5_scan",
    )(u4, kern, kern, pow_tab, coef_tab, a_pack, d_row)


def _s5_tables(lam_re, lam_im, log_dt, b_re, b_im, c_re, c_im, d_skip):
    g, p = lam_re.shape
    k = SSM_GROUP
    tc = SSM_CHUNK
    dt = jnp.exp(log_dt)[:, None]
    mag = jnp.exp(lam_re * dt)
    ab_re = mag * jnp.cos(lam_im * dt)
    ab_im = mag * jnp.sin(lam_im * dt)
    den = lam_re * lam_re + lam_im * lam_im
    cf_re = ((ab_re - 1) * lam_re + ab_im * lam_im) / den
    cf_im = (ab_im * lam_re - (ab_re - 1) * lam_im) / den
    bb_re = cf_re[..., None] * b_re - cf_im[..., None] * b_im
    bb_im = cf_re[..., None] * b_im + cf_im[..., None] * b_re
    lre = jnp.concatenate([lam_re * dt, lam_re * dt], axis=-1)
    lim = jnp.concatenate([lam_im * dt, lam_im * dt], axis=-1)
    lags = jnp.arange(tc + 1, dtype=F32)[None, :, None]
    pmag = jnp.exp(lags * lre[:, None, :])
    pw_re = pmag * jnp.cos(lags * lim[:, None, :])
    pw_im = pmag * jnp.sin(lags * lim[:, None, :])
    cb_re = jnp.einsum('gjp,gpi->gpij', c_re, bb_re) - jnp.einsum('gjp,gpi->gpij', c_im, bb_im)
    cb_im = jnp.einsum('gjp,gpi->gpij', c_re, bb_im) + jnp.einsum('gjp,gpi->gpij', c_im, bb_re)
    kern = (jnp.einsum('glp,gpij->gijl', pw_re[:, :tc, :p], cb_re, precision=HIGHEST)
            - jnp.einsum('glp,gpij->gijl', pw_im[:, :tc, :p], cb_im, precision=HIGHEST))
    kern = kern.reshape(g, k * k, tc)
    sign = jnp.concatenate([-jnp.ones((p,), F32), jnp.ones((p,), F32)])
    flip = (jnp.arange(tc)[:, None] + jnp.arange(tc)[None, :] == tc - 1).astype(F32)
    pr2 = jnp.einsum('st,gtq->gsq', flip, pw_re[:, :tc], precision=HIGHEST)
    pi2 = jnp.einsum('st,gtq->gsq', flip, pw_im[:, :tc], precision=HIGHEST) * sign
    bb1 = jnp.concatenate([bb_re, bb_im], axis=1).transpose(0, 2, 1)
    bb2 = jnp.concatenate([bb_im, bb_re], axis=1).transpose(0, 2, 1)
    qr2 = pw_re[:, 1:]
    qi2 = pw_im[:, 1:]
    ca = jnp.concatenate([c_re, -c_im], axis=-1)
    cb = jnp.concatenate([-c_im, -c_re], axis=-1)
    pow_tab = jnp.stack([pr2, pi2, qr2, qi2], axis=1)
    coef_tab = jnp.stack([bb1, bb2, ca, cb], axis=1)
    a1 = pw_re[:, tc]
    a2 = pw_im[:, tc] * sign
    a_pack = jnp.concatenate([a1[:, None], a2[:, None], jnp.zeros((g, 6, 2 * p), F32)], axis=1)
    d_row = jnp.repeat(d_skip.reshape(g, k), tc, axis=1).reshape(g, 1, k * tc)
    return kern, pow_tab, coef_tab, a_pack, d_row


def _gelu_tanh(x):
    return 0.5 * x * (1.0 + jnp.tanh(math.sqrt(2.0 / math.pi) * (x + 0.044715 * (x * x * x))))


def _s5_out_kernel(y_ref, x_ref, mod_ref, w_ref, g2_ref, wr_ref, br_ref,
                   x_out_ref, hn_ref, eid_ref, wts_ref, cnt_ref):
    mod = mod_ref[0]
    h = y_ref.shape[1]
    yt = _gelu_tanh(y_ref[0].reshape(h, S5_TILE))
    d = w_ref.shape[1] // 2
    sub_rows = MOE_BLOCK
    n_sub = S5_TILE // sub_rows
    per_block = MOE_BLOCK // sub_rows

    def project(p):
        return lax.dot_general(yt[:, p * sub_rows:(p + 1) * sub_rows].astype(BF16), w_ref[...],
                               (((0,), (0,)), ((), ())), preferred_element_type=F32)

    o_next = project(0)
    cnt = None
    for p in range(n_sub):
        o = o_next
        if p + 1 < n_sub:
            o_next = project(p + 1)
        rows = slice(p * sub_rows, (p + 1) * sub_rows)
        out = o[:, :d] * jax.nn.sigmoid(o[:, d:])
        c = _residual_and_route(x_ref[0, rows, :], out, mod, g2_ref, wr_ref, br_ref,
                                x_out_ref.at[0, rows, :], hn_ref.at[rows, :], eid_ref.at[rows, :],
                                wts_ref.at[rows, :])
        cnt = c if p % per_block == 0 else cnt + c
        if p % per_block == per_block - 1:
            cnt_ref[p // per_block] = jnp.broadcast_to(cnt, cnt_ref.shape[1:])


def _mixer_out_specs(b, l, d, tile):
    nt = l // tile
    nblk = tile // MOE_BLOCK
    specs = [pl.BlockSpec((1, tile, d), lambda i, j: (i, j, 0)),
             pl.BlockSpec((tile, d), lambda i, j: (i * nt + j, 0)),
             pl.BlockSpec((tile, TOP_K), lambda i, j: (i * nt + j, 0)),
             pl.BlockSpec((tile, TOP_K), lambda i, j: (i * nt + j, 0)),
             pl.BlockSpec((nblk, 8, LANES), lambda i, j: (i * nt + j, 0, 0))]
    shapes = [jax.ShapeDtypeStruct((b, l, d), F32),
              jax.ShapeDtypeStruct((b * l, d), BF16),
              jax.ShapeDtypeStruct((b * l, TOP_K), jnp.int32),
              jax.ShapeDtypeStruct((b * l, TOP_K), F32),
              jax.ShapeDtypeStruct((b * l // MOE_BLOCK, 8, LANES), F32)]
    return specs, shapes


def _s5_out(y4, x, mod, w_bf, g2, wr, br):
    b, l, d = x.shape
    h = y4.shape[1]
    specs, shapes = _mixer_out_specs(b, l, d, S5_TILE)
    return pl.pallas_call(
        _s5_out_kernel,
        grid=(b, l // S5_TILE),
        in_specs=[pl.BlockSpec((1, h, S5_TILE // SSM_CHUNK, SSM_CHUNK), lambda i, j: (i, 0, j, 0)),
                  pl.BlockSpec((1, S5_TILE, d), lambda i, j: (i, j, 0)),
                  pl.BlockSpec((1, 8, d), lambda i, j: (i, 0, 0)),
                  pl.BlockSpec(w_bf.shape, lambda i, j: (0, 0), pipeline_mode=pl.Buffered(1)),
                  pl.BlockSpec((1, d), lambda i, j: (0, 0)),
                  pl.BlockSpec((d, 2 * LANES), lambda i, j: (0, 0)),
                  pl.BlockSpec((1, LANES), lambda i, j: (0, 0))],
        out_specs=specs,
        out_shape=shapes,
        compiler_params=_cparams("parallel", "parallel"),
        name="s5_out",
    )(y4, x, mod, w_bf, g2, wr, br)


def _conv_mixer_body(x_in_ref, mod_ref, g1_ref, win_ref, cw_ref, wout_ref, g2_ref, wr_ref, br_ref,
                     x_out_ref, hn_ref, eid_ref, wts_ref, cnt_ref, carry_ref):
    @pl.when(pl.program_id(1) == 0)
    def _():
        carry_ref[...] = jnp.zeros_like(carry_ref)

    mod = mod_ref[0]
    d = x_in_ref.shape[1]
    cw = cw_ref[...]
    n_sub = x_in_ref.shape[0] // SUB_ROWS
    sub = [slice(s * SUB_ROWS, (s + 1) * SUB_ROWS) for s in range(n_sub)]

    def project(s):
        hn = _norm_mod(x_in_ref[sub[s], :], g1_ref[...], mod[0:1], mod[1:2])
        return jnp.dot(hn.astype(BF16), win_ref[...], preferred_element_type=F32)

    def gate_conv(p, prev_tail):
        u = p[:, d:2 * d] * p[:, 2 * d:]
        ext = jnp.concatenate([prev_tail, u], axis=0)
        z = cw[0:1] * ext[6:6 + SUB_ROWS] + cw[1:2] * ext[7:7 + SUB_ROWS] + cw[2:3] * u
        return (p[:, :d] * z).astype(BF16), u[SUB_ROWS - 8:]

    ps = [project(s) for s in range(n_sub)]
    tail = carry_ref[...]
    outs = []
    for s in range(n_sub):
        gated, tail = gate_conv(ps[s], tail)
        outs.append(jnp.dot(gated, wout_ref[...], preferred_element_type=F32))
    carry_ref[...] = tail
    cnt = None
    for s in range(n_sub):
        c = _residual_and_route(x_in_ref[sub[s], :], outs[s], mod, g2_ref, wr_ref, br_ref,
                                x_out_ref.at[0, sub[s], :], hn_ref.at[sub[s], :], eid_ref.at[sub[s], :],
                                wts_ref.at[sub[s], :])
        cnt = c if s == 0 else cnt + c
    cnt_ref[0] = jnp.broadcast_to(cnt, cnt_ref.shape[1:])


_TN = (((0,), (0,)), ((), ()))


MAX_CHUNKS = LOCAL_ROWS // CHUNK_ROWS


def _rows_copy(src, s_row, dst, d_row, rows, sem):
    return pltpu.make_async_copy(src.at[pl.ds(s_row, rows), :], dst.at[pl.ds(d_row, rows), :], sem)


def _start_block_chunks(blk, count, crow_ref, make_copy):
    def body(q, carry):
        make_copy(pl.multiple_of(q * CHUNK_ROWS, CHUNK_ROWS),
                  pl.multiple_of(crow_ref[blk * MAX_CHUNKS + q], CHUNK_ROWS)).start()
        return carry

    lax.fori_loop(0, count, body, 0)


def _wait_chunks(count, max_count, make_wait):
    for k in range(max_count.bit_length()):
        @pl.when(((count >> k) & 1) == 1)
        def _():
            make_wait(CHUNK_ROWS << k).wait()


def _slot_masks(lp0, lp1):
    slot = lax.broadcasted_iota(jnp.int32, (lp0.shape[0], LOCAL_ROWS), 1)
    return slot == lp0, slot == lp1


_HI16 = -65536


def _pack_bf16_pairs(x):
    h = x.shape[1] // 2
    lo = lax.shift_right_logical(pltpu.bitcast(x[:, :h], jnp.int32), 16)
    hi = pltpu.bitcast(x[:, h:], jnp.int32) & _HI16
    return hi | lo


def _unpack_bf16_pairs(w):
    lo = pltpu.bitcast(w << 16, F32).astype(BF16)
    hi = pltpu.bitcast(w & _HI16, F32).astype(BF16)
    return lo, hi


def _moe_dispatch_kernel(crow_ref, tot_ref, padn_ref, pads_ref, nv_ref,
                         eid_ref, wts_ref, hn_ref, lofff_ref, tri_ref, xs_ref, lpos_ref, loc_ref, zero_ref, sems):
    blk = pl.program_id(0)
    d = hn_ref.shape[1]
    eid = eid_ref[...]
    t = eid.shape[0]
    lane = lax.broadcasted_iota(jnp.int32, (t, LANES), 1)
    hit0 = lane == eid[:, 0:1]
    hit1 = lane == eid[:, 1:2]
    onehot = (hit0 | hit1).astype(BF16)
    before = jnp.dot(tri_ref[...], onehot, preferred_element_type=F32) + lofff_ref[0]
    lp0 = jnp.sum(jnp.where(hit0, before, 0.0), axis=-1, keepdims=True).astype(jnp.int32)
    lp1 = jnp.sum(jnp.where(hit1, before, 0.0), axis=-1, keepdims=True).astype(jnp.int32)
    col = lax.broadcasted_iota(jnp.int32, (t, TOP_K), 1)
    lpos_ref[...] = jnp.where(col == 0, lp0, lp1)
    wts = wts_ref[...]

    buf = blk % 2
    loc = loc_ref.at[buf]

    def drain(which, count):
        _wait_chunks(count, MAX_CHUNKS,
                     lambda rows: _rows_copy(loc_ref.at[which], 0, xs_ref, 0, rows, sems.at[which]))

    @pl.when(blk >= 2)
    def _():
        drain(buf, tot_ref[blk - 2])

    m0, m1 = _slot_masks(lp0, lp1)
    wrow = jnp.sum(jnp.where(m0, wts[:, 0:1], 0.0) + jnp.where(m1, wts[:, 1:2], 0.0), axis=0, keepdims=True)
    sorted_rows = lax.dot_general((m0 | m1).astype(BF16), hn_ref[...], _TN, preferred_element_type=F32)
    loc[:, :d // 2] = _pack_bf16_pairs(sorted_rows)
    loc[:, d // 2:] = pltpu.bitcast(jnp.broadcast_to(wrow, (LANES, LOCAL_ROWS)).T, jnp.int32)

    _start_block_chunks(blk, tot_ref[blk], crow_ref,
                        lambda lo, go: _rows_copy(loc, lo, xs_ref, go, CHUNK_ROWS, sems.at[buf]))

    @pl.when(blk == pl.num_programs(0) - 1)
    def _():
        drain(1 - buf, tot_ref[blk - 1])
        drain(buf, tot_ref[blk])
        sem = sems.at[0]
        zero_ref[...] = jnp.zeros_like(zero_ref)
        pad_bits = (EXPERT_TILE // CHUNK_ROWS - 1).bit_length()

        def pad_copies(e, carry, *, wait):
            padn = padn_ref[e]
            for k in range(pad_bits):
                @pl.when(((padn >> k) & 1) == 1)
                def _():
                    done = (padn & ((1 << k) - 1)) * CHUNK_ROWS
                    cp = _rows_copy(zero_ref, 0, xs_ref, pl.multiple_of(pads_ref[e] + done, CHUNK_ROWS),
                                    CHUNK_ROWS << k, sem)
                    if wait:
                        cp.wait()
                    else:
                        cp.start()
            return carry

        lax.fori_loop(0, N_EXPERTS, functools.partial(pad_copies, wait=False), 0)
        lax.fori_loop(0, N_EXPERTS, functools.partial(pad_copies, wait=True), 0)

        def tile_copy(i):
            return pltpu.make_async_copy(zero_ref, xs_ref.at[pl.ds(pl.multiple_of(i * EXPERT_TILE, EXPERT_TILE),
                                                                   EXPERT_TILE), :], sem)

        n_tiles = xs_ref.shape[0] // EXPERT_TILE

        def start_tile(i, c):
            tile_copy(i).start()
            return c

        def wait_tile(i, c):
            tile_copy(i).wait()
            return c

        lax.fori_loop(nv_ref[0], n_tiles, start_tile, 0)
        lax.fori_loop(nv_ref[0], n_tiles, wait_tile, 0)


def _moe_dispatch(tables, eid, wts, hn, loff_f, n_rows):
    t, d = hn.shape
    pos = jnp.arange(MOE_BLOCK, dtype=jnp.int32)
    tri = (pos[None, :] < pos[:, None]).astype(BF16)
    width = d // 2 + LANES
    grid_spec = pltpu.PrefetchScalarGridSpec(
        num_scalar_prefetch=5,
        grid=(t // MOE_BLOCK,),
        in_specs=[pl.BlockSpec((MOE_BLOCK, TOP_K), lambda i, *_: (i, 0)),
                  pl.BlockSpec((MOE_BLOCK, TOP_K), lambda i, *_: (i, 0)),
                  pl.BlockSpec((MOE_BLOCK, d), lambda i, *_: (i, 0)),
                  pl.BlockSpec((1, 1, LANES), lambda i, *_: (i, 0, 0)),
                  pl.BlockSpec((MOE_BLOCK, MOE_BLOCK), lambda i, *_: (0, 0))],
        out_specs=[pl.BlockSpec(memory_space=pl.ANY),
                   pl.BlockSpec((MOE_BLOCK, TOP_K), lambda i, *_: (i, 0))],
        scratch_shapes=[pltpu.VMEM((2, LOCAL_ROWS, width), jnp.int32), pltpu.VMEM((EXPERT_TILE, width), jnp.int32),
                        pltpu.SemaphoreType.DMA((2,))],
    )
    return pl.pallas_call(
        _moe_dispatch_kernel,
        grid_spec=grid_spec,
        out_shape=[jax.ShapeDtypeStruct((n_rows, width), jnp.int32),
                   jax.ShapeDtypeStruct((t, TOP_K), jnp.int32)],
        compiler_params=_cparams("arbitrary"),
        name="moe_dispatch",
    )(*tables, eid, wts, hn, loff_f, tri)


def _moe_experts_kernel(te_ref, tv_ref, tf_ref, ts_ref, xs_ref, w1_ref, w3_ref, w2_ref, o_ref,
                        w1b_ref, w3b_ref, w2b_ref):
    del te_ref, ts_ref
    i = pl.program_id(0)

    @pl.when(tf_ref[i] != 0)
    def _():
        def cast_rows(r, carry):
            rows = pl.ds(pl.multiple_of(r * LANES, LANES), LANES)
            w1b_ref[rows, :] = w1_ref[0, 0, rows, :].astype(BF16)
            w3b_ref[rows, :] = w3_ref[0, 0, rows, :].astype(BF16)

            @pl.when(r < w2b_ref.shape[0] // LANES)
            def _():
                w2b_ref[rows, :] = w2_ref[0, 0, rows, :].astype(BF16)

            return carry

        lax.fori_loop(0, w1b_ref.shape[0] // LANES, cast_rows, 0)

    @pl.when(tv_ref[i] != 0)
    def _():
        half = o_ref.shape[1]
        x_lo, x_hi = _unpack_bf16_pairs(xs_ref[:, :half])
        w = pltpu.bitcast(xs_ref[:, half:half + 1], F32)
        de = w1b_ref.shape[1]
        o = None
        for c0 in range(0, de, MXU_WIDTH):
            cols = slice(c0, c0 + MXU_WIDTH)
            a = (jnp.dot(x_lo, w1b_ref[:half, cols], preferred_element_type=F32)
                 + jnp.dot(x_hi, w1b_ref[half:, cols], preferred_element_type=F32))
            b = (jnp.dot(x_lo, w3b_ref[:half, cols], preferred_element_type=F32)
                 + jnp.dot(x_hi, w3b_ref[half:, cols], preferred_element_type=F32))
            h = (a * jax.nn.sigmoid(a) * b).astype(BF16)
            part = jnp.dot(h, w2b_ref[cols, :], preferred_element_type=F32)
            o = part if o is None else o + part
        o = o * w
        o_ref[...] = _pack_bf16_pairs(o.astype(BF16).astype(F32))

    @pl.when(tv_ref[i] == 0)
    def _():
        o_ref[...] = jnp.zeros_like(o_ref)


def _moe_experts(tile_expert, tile_valid, tile_first, tile_src, xs, w1, w3, w2, layer):
    r, width = xs.shape
    d = (width - LANES) * 2
    de = w1.shape[3]
    grid_spec = pltpu.PrefetchScalarGridSpec(
        num_scalar_prefetch=4,
        grid=(r // EXPERT_TILE,),
        in_specs=[pl.BlockSpec((EXPERT_TILE, width), lambda i, te, tv, tf, ts: (ts[i], 0)),
                  pl.BlockSpec((1, 1, d, de), lambda i, te, tv, tf, ts: (layer, te[i], 0, 0)),
                  pl.BlockSpec((1, 1, d, de), lambda i, te, tv, tf, ts: (layer, te[i], 0, 0)),
                  pl.BlockSpec((1, 1, de, d), lambda i, te, tv, tf, ts: (layer, te[i], 0, 0))],
        out_specs=pl.BlockSpec((EXPERT_TILE, d // 2), lambda i, te, tv, tf, ts: (i, 0)),
        scratch_shapes=[pltpu.VMEM((d, de), BF16), pltpu.VMEM((d, de), BF16), pltpu.VMEM((de, d), BF16)],
    )
    return pl.pallas_call(
        _moe_experts_kernel,
        grid_spec=grid_spec,
        out_shape=jax.ShapeDtypeStruct((r, d // 2), jnp.int32),
        compiler_params=_cparams("arbitrary"),
        name="moe_experts",
    )(tile_expert, tile_valid, tile_first, tile_src, xs, w1, w3, w2)


def _combine_rows(crow_ref, lpos_ref, o_hbm_ref, loc_a_ref, loc_b_ref, sems, consume, fetch_first):
    blk = pl.program_id(0) * pl.num_programs(1) + pl.program_id(1)
    n_blk = pl.num_programs(0) * pl.num_programs(1)

    def fetch(b, loc, sem):
        for q in range(MAX_CHUNKS):
            _rows_copy(o_hbm_ref, pl.multiple_of(crow_ref[b * MAX_CHUNKS + q], CHUNK_ROWS), loc,
                       q * CHUNK_ROWS, CHUNK_ROWS, sem).start()

    def wait_all(loc, sem):
        _rows_copy(o_hbm_ref, 0, loc, 0, LOCAL_ROWS, sem).wait()

    @pl.when(blk == 0)
    def _():
        fetch(blk, loc_a_ref, sems.at[0])

    def run(cur, cur_sem, nxt, nxt_sem):
        if fetch_first:
            fetch(jnp.minimum(blk + 1, n_blk - 1), nxt, nxt_sem)
        wait_all(cur, cur_sem)
        if not fetch_first:
            fetch(jnp.minimum(blk + 1, n_blk - 1), nxt, nxt_sem)
        lp = lpos_ref[...]
        m0, m1 = _slot_masks(lp[:, 0:1], lp[:, 1:2])
        pt = (m0 | m1).astype(BF16)
        o_lo, o_hi = _unpack_bf16_pairs(cur[...])
        consume(jnp.concatenate([jnp.dot(pt, o_lo, preferred_element_type=F32),
                                 jnp.dot(pt, o_hi, preferred_element_type=F32)], axis=1))

        @pl.when(blk == n_blk - 1)
        def _():
            wait_all(nxt, nxt_sem)

    @pl.when(blk % 2 == 0)
    def _():
        run(loc_a_ref, sems.at[0], loc_b_ref, sems.at[1])

    @pl.when(blk % 2 == 1)
    def _():
        run(loc_b_ref, sems.at[1], loc_a_ref, sems.at[0])


def _moe_combine_final_kernel(crow_ref, tot_ref, lpos_ref, x_ref, mod_ref, fg_ref, o_hbm_ref,
                              out_ref, loc_a_ref, loc_b_ref, sems):
    del tot_ref

    def finish(y):
        x2 = x_ref[0] + (1.0 + mod_ref[0, 5:6]) * y
        out_ref[0] = (x2 * lax.rsqrt(jnp.mean(x2 * x2, axis=-1, keepdims=True) + RMS_EPS)) * fg_ref[...]

    _combine_rows(crow_ref, lpos_ref, o_hbm_ref, loc_a_ref, loc_b_ref, sems, finish, fetch_first=True)


def _moe_combine_conv_kernel(crow_ref, tot_ref, lpos_ref, x_ref, mod_prev_ref, o_hbm_ref,
                             mod_ref, g1_ref, win_ref, cw_ref, wout_ref, g2_ref, wr_ref, br_ref,
                             x_out_ref, hn_ref, eid_ref, wts_ref, cnt_ref,
                             loc_a_ref, loc_b_ref, sems, xmid_ref, carry_ref):
    del tot_ref

    def residual(y):
        xmid_ref[...] = x_ref[0] + (1.0 + mod_prev_ref[0, 5:6]) * y

    _combine_rows(crow_ref, lpos_ref, o_hbm_ref, loc_a_ref, loc_b_ref, sems, residual, fetch_first=False)
    _conv_mixer_body(xmid_ref, mod_ref, g1_ref, win_ref, cw_ref, wout_ref, g2_ref, wr_ref, br_ref,
                     x_out_ref, hn_ref, eid_ref, wts_ref, cnt_ref, carry_ref)


def _combine_in_specs(nt, d):
    return [pl.BlockSpec((MOE_BLOCK, TOP_K), lambda i, j, *_: (i * nt + j, 0)),
            pl.BlockSpec((1, MOE_BLOCK, d), lambda i, j, *_: (i, j, 0)),
            pl.BlockSpec((1, 8, d), lambda i, j, *_: (i, 0, 0))]


def _combine_scratch(d):
    return [pltpu.VMEM((LOCAL_ROWS, d // 2), jnp.int32), pltpu.VMEM((LOCAL_ROWS, d // 2), jnp.int32),
            pltpu.SemaphoreType.DMA((2,))]


def _moe_combine_final(tables, lpos, x, mod, final_g, o_sorted):
    b, l, d = x.shape
    nt = l // MOE_BLOCK
    grid_spec = pltpu.PrefetchScalarGridSpec(
        num_scalar_prefetch=2,
        grid=(b, nt),
        in_specs=_combine_in_specs(nt, d) + [pl.BlockSpec((1, d), lambda i, j, *_: (0, 0)),
                                             pl.BlockSpec(memory_space=pl.ANY)],
        out_specs=pl.BlockSpec((1, MOE_BLOCK, d), lambda i, j, *_: (i, j, 0)),
        scratch_shapes=_combine_scratch(d),
    )
    return pl.pallas_call(
        _moe_combine_final_kernel,
        grid_spec=grid_spec,
        out_shape=jax.ShapeDtypeStruct((b, l, d), F32),
        compiler_params=_cparams("arbitrary", "arbitrary"),
        name="moe_combine_final",
    )(*tables[:2], lpos, x, mod, final_g, o_sorted)


def _moe_combine_conv(tables, lpos, x, mod_prev, o_sorted, mod, g1, win_bf, cw8, wout_bf, g2, wr, br):
    b, l, d = x.shape
    nt = l // MOE_BLOCK
    specs, shapes = _mixer_out_specs(b, l, d, TOKEN_TILE)
    const = lambda shape: pl.BlockSpec(shape, lambda i, j, *_: (0,) * len(shape))
    grid_spec = pltpu.PrefetchScalarGridSpec(
        num_scalar_prefetch=2,
        grid=(b, nt),
        in_specs=_combine_in_specs(nt, d) + [
            pl.BlockSpec(memory_space=pl.ANY),
            pl.BlockSpec((1, 8, d), lambda i, j, *_: (i, 0, 0)),
            const((1, d)), const(win_bf.shape), const((8, d)), const(wout_bf.shape),
            const((1, d)), const((d, 2 * LANES)), const((1, LANES))],
        out_specs=[pl.BlockSpec(s.block_shape, lambda i, j, *_, f=s.index_map: f(i, j)) for s in specs],
        scratch_shapes=_combine_scratch(d) + [pltpu.VMEM((MOE_BLOCK, d), F32), pltpu.VMEM((8, d), F32)],
    )
    return pl.pallas_call(
        _moe_combine_conv_kernel,
        grid_spec=grid_spec,
        out_shape=shapes,
        compiler_params=_cparams("arbitrary", "arbitrary"),
        name="moe_combine_conv",
    )(*tables[:2], lpos, x, mod_prev, o_sorted, mod, g1, win_bf, cw8, wout_bf, g2, wr, br)


def _moe_sorted_experts(hn, eid, wts, cnt, w1, w3, w2, layer):
    t, d = hn.shape
    nblk = t // MOE_BLOCK
    i32 = jnp.int32
    n = cnt[:, 0, :N_EXPERTS].astype(i32)
    run = (n + CHUNK_ROWS - 1) // CHUNK_ROWS * CHUNK_ROWS
    loff = jnp.cumsum(run, axis=1) - run
    rows_e = jnp.sum(run, axis=0)
    tiles_e = (rows_e + EXPERT_TILE - 1) // EXPERT_TILE
    tile_end = jnp.cumsum(tiles_e)
    base = (tile_end - tiles_e) * EXPERT_TILE
    goff = base[None, :] + jnp.cumsum(run, axis=0) - run
    nch = run // CHUNK_ROWS
    tot = jnp.sum(nch, axis=1)
    ch_end = jnp.cumsum(nch, axis=1)
    q = jnp.arange(MAX_CHUNKS, dtype=i32)
    in_run = ((q[None, :, None] >= (ch_end - nch)[:, None, :]) & (q[None, :, None] < ch_end[:, None, :])).astype(i32)
    crow = jnp.sum(in_run * (goff[:, None, :] + (q[None, :, None] - (ch_end - nch)[:, None, :]) * CHUNK_ROWS), axis=-1)
    padn = (tiles_e * EXPERT_TILE - rows_e) // CHUNK_ROWS
    pads = base + rows_e
    tables = (crow.reshape(-1), tot, padn, pads, tile_end[-1:])
    loff_f = jnp.zeros((nblk, 1, LANES), F32).at[:, 0, :N_EXPERTS].set(loff.astype(F32))

    max_rows = t * TOP_K + nblk * N_EXPERTS * (CHUNK_ROWS - 1) + N_EXPERTS * (EXPERT_TILE - 1)
    max_tiles = -(-max_rows // EXPERT_TILE)
    tile_ids = jnp.arange(max_tiles, dtype=i32)
    n_valid = tile_end[-1]
    tile_src = jnp.minimum(tile_ids, n_valid - 1)
    tile_expert = jnp.sum((tile_src[:, None] >= tile_end[None, :]).astype(i32), axis=1)
    tile_valid = (tile_ids < n_valid).astype(i32)
    tile_first = jnp.concatenate([jnp.ones((1,), i32), (tile_expert[1:] != tile_expert[:-1]).astype(i32)])

    xs, lpos = _moe_dispatch(tables, eid, wts, hn, loff_f, max_tiles * EXPERT_TILE)
    o_sorted = _moe_experts(tile_expert, tile_valid, tile_first, tile_src, xs, w1, w3, w2, layer)
    return tables, lpos, o_sorted


def _router_pack(wg, bg, we, be):
    d = wg.shape[0]
    wr = jnp.zeros((d, LANES), F32).at[:, :N_EXPERTS].set(we).at[:, N_EXPERTS:N_EXPERTS + N_GROUPS].set(wg)
    br = jnp.zeros((1, LANES), F32).at[0, :N_EXPERTS].set(be).at[0, N_EXPERTS:N_EXPERTS + N_GROUPS].set(bg)
    wr_hi = wr.astype(BF16)
    wr_lo = (wr - wr_hi.astype(F32)).astype(BF16)
    return jnp.concatenate([wr_hi, wr_lo], axis=1), br


def kernel(x, c, ada_w, ada_b, norm1_g, norm2_g, ssm_w_in, ssm_lam_re, ssm_lam_im, ssm_log_dt, ssm_b_re, ssm_b_im, ssm_c_re, ssm_c_im, ssm_d, ssm_w_glu, conv_w_in, conv_w, conv_w_out, moe_wg, moe_bg, moe_we, moe_be, moe_w1, moe_w3, moe_w2, final_g):
    b, l, d = x.shape
    depth = ada_w.shape[0]
    c8 = jnp.zeros((8, d), F32).at[:b].set(c)
    mod_all = _adaln(c8, ada_w, ada_b)[:, :b].reshape(depth, b, 6, d)
    mod_all = jnp.concatenate([mod_all, jnp.zeros((depth, b, 2, d), F32)], axis=2)
    fg = final_g.reshape(1, d)

    mod = mod_all[0]
    u4 = _s5_in(x, mod, norm1_g[0:1], ssm_w_in[0].T.astype(BF16))
    kern, pow_tab, coef_tab, a_pack, d_row = _s5_tables(
        ssm_lam_re[0], ssm_lam_im[0], ssm_log_dt[0], ssm_b_re[0], ssm_b_im[0],
        ssm_c_re[0], ssm_c_im[0], ssm_d[0])
    y4 = _s5_scan(u4, kern, pow_tab, coef_tab, a_pack, d_row)
    wr, br = _router_pack(moe_wg[0], moe_bg[0], moe_we[0], moe_be[0])
    x1, hn, eid, wts, cnt = _s5_out(y4, x, mod, ssm_w_glu[0].astype(BF16), norm2_g[0:1], wr, br)
    tables, lpos, o_sorted = _moe_sorted_experts(hn, eid, wts, cnt, moe_w1, moe_w3, moe_w2, 0)

    mod1 = mod_all[1]
    cw8 = jnp.zeros((8, d), F32).at[:conv_w.shape[1]].set(conv_w[0])
    wr, br = _router_pack(moe_wg[1], moe_bg[1], moe_we[1], moe_be[1])
    x3, hn, eid, wts, cnt = _moe_combine_conv(
        tables, lpos, x1, mod, o_sorted, mod1, norm1_g[1:2], conv_w_in[0].astype(BF16), cw8,
        conv_w_out[0].astype(BF16), norm2_g[1:2], wr, br)
    tables, lpos, o_sorted = _moe_sorted_experts(hn, eid, wts, cnt, moe_w1, moe_w3, moe_w2, 1)
    return _moe_combine_final(tables, lpos, x3, mod1, fg, o_sorted)
```

```python
import functools
import math

import jax
import jax.numpy as jnp
from jax import lax
from jax.experimental import pallas as pl
from jax.experimental.pallas import tpu as pltpu

F32 = jnp.float32
BF16 = jnp.bfloat16
HIGHEST = lax.Precision.HIGHEST

RMS_EPS = 1e-6
SSM_GROUP = 16
SSM_CHUNK = 128
N_GROUPS = 4
EXPERTS_PER_GROUP = 8
N_EXPERTS = N_GROUPS * EXPERTS_PER_GROUP
TOP_K = 2
LANES = 128
MXU_WIDTH = 256
TOKEN_TILE = 512
S5_TILE = 1024
S5_K_BLOCKS = 2
SUB_ROWS = 256
MOE_BLOCK = TOKEN_TILE
CHUNK_ROWS = 8
LOCAL_ROWS = -(-(TOP_K * MOE_BLOCK + N_EXPERTS * (CHUNK_ROWS - 1)) // LANES) * LANES
EXPERT_TILE = 512
VMEM_LIMIT = 56 * 1024 * 1024
NEG_INF = -1e30


def _cparams(*sem):
    return pltpu.CompilerParams(dimension_semantics=sem, vmem_limit_bytes=VMEM_LIMIT)


def _adaln_kernel(c_ref, w_ref, b_ref, o_ref):
    c = c_ref[...]
    cond = c * jax.nn.sigmoid(c)
    o_ref[0] = jnp.dot(cond, w_ref[0], precision=HIGHEST, preferred_element_type=F32) + b_ref[0]


def _adaln(c8, ada_w, ada_b):
    depth, d, n = ada_w.shape
    tn = 1536
    return pl.pallas_call(
        _adaln_kernel,
        grid=(depth, n // tn),
        in_specs=[pl.BlockSpec((8, d), lambda i, j: (0, 0)),
                  pl.BlockSpec((1, d, tn), lambda i, j: (i, 0, j)),
                  pl.BlockSpec((1, 1, tn), lambda i, j: (i, 0, j))],
        out_specs=pl.BlockSpec((1, 8, tn), lambda i, j: (i, 0, j)),
        out_shape=jax.ShapeDtypeStruct((depth, 8, n), F32),
        compiler_params=_cparams("parallel", "parallel"),
        name="adaln",
    )(c8, ada_w, ada_b.reshape(depth, 1, n))


def _norm_mod(x, g, shift, scale):
    y = x * lax.rsqrt(jnp.mean(x * x, axis=-1, keepdims=True) + RMS_EPS)
    return (y * g) * (1.0 + scale) + shift


def _route(hn, hn_hi, wr_ref, br_ref, eid_ref, wts_ref):
    hn_lo = (hn - hn_hi.astype(F32)).astype(BF16)
    both = jnp.dot(hn_hi, wr_ref[...], preferred_element_type=F32)
    logits = (both[:, :LANES] + both[:, LANES:]
              + jnp.dot(hn_lo, wr_ref[:, :LANES], preferred_element_type=F32)) + br_ref[...]
    lane = lax.broadcasted_iota(jnp.int32, logits.shape, 1)
    is_grp = (lane >= N_EXPERTS) & (lane < N_EXPERTS + N_GROUPS)
    lg = jnp.where(is_grp, logits, NEG_INF)
    gmax = jnp.max(lg, axis=-1, keepdims=True)
    gsum = jnp.sum(jnp.where(is_grp, jnp.exp(lg - gmax), 0.0), axis=-1, keepdims=True)
    gp = 1.0 / gsum
    gi = jnp.min(jnp.where(lg == gmax, lane, 2 * LANES), axis=-1, keepdims=True) - N_EXPERTS
    in_grp = (lane < N_EXPERTS) & ((lane // EXPERTS_PER_GROUP) == gi)
    le = jnp.where(in_grp, logits, NEG_INF)
    v1 = jnp.max(le, axis=-1, keepdims=True)
    i1 = jnp.min(jnp.where(le == v1, lane, 2 * LANES), axis=-1, keepdims=True)
    le2 = jnp.where(lane == i1, NEG_INF, le)
    v2 = jnp.max(le2, axis=-1, keepdims=True)
    i2 = jnp.min(jnp.where(le2 == v2, lane, 2 * LANES), axis=-1, keepdims=True)
    e2 = jnp.exp(v2 - v1)
    den = 1.0 + e2
    col = lax.broadcasted_iota(jnp.int32, (hn.shape[0], TOP_K), 1)
    eid_ref[...] = jnp.where(col == 0, i1, i2)
    wts_ref[...] = jnp.where(col == 0, gp / den, gp * e2 / den)
    chosen = ((lane == i1) | (lane == i2)).astype(F32)
    return jnp.sum(chosen, axis=0, keepdims=True)


def _residual_and_route(x, out, mod, g2_ref, wr_ref, br_ref, x_out_ref, hn_ref, eid_ref, wts_ref):
    x1 = x + (1.0 + mod[2:3]) * out
    x_out_ref[...] = x1
    hn = _norm_mod(x1, g2_ref[...], mod[3:4], mod[4:5])
    hn_hi = hn.astype(BF16)
    hn_ref[...] = hn_hi
    return _route(hn, hn_hi, wr_ref, br_ref, eid_ref, wts_ref)


def _s5_in_kernel(x_ref, mod_ref, g_ref, wt_ref, u_ref):
    mod = mod_ref[0]
    hn = _norm_mod(x_ref[0], g_ref[...], mod[0:1], mod[1:2])
    ut = lax.dot_general(wt_ref[...], hn.astype(BF16), (((1,), (1,)), ((), ())), preferred_element_type=F32)
    u_ref[0] = ut.reshape(ut.shape[0], S5_TILE // SSM_CHUNK, SSM_CHUNK)


def _s5_in(x, mod, g, wt_bf):
    b, l, d = x.shape
    h = wt_bf.shape[0]
    cpt = S5_TILE // SSM_CHUNK
    return pl.pallas_call(
        _s5_in_kernel,
        grid=(b, l // S5_TILE),
        in_specs=[pl.BlockSpec((1, S5_TILE, d), lambda i, j: (i, j, 0)),
                  pl.BlockSpec((1, 8, d), lambda i, j: (i, 0, 0)),
                  pl.BlockSpec((1, d), lambda i, j: (0, 0)),
                  pl.BlockSpec((h, d), lambda i, j: (0, 0))],
        out_specs=pl.BlockSpec((1, h, cpt, SSM_CHUNK), lambda i, j: (i, 0, j, 0)),
        out_shape=jax.ShapeDtypeStruct((b, h, l // SSM_CHUNK, SSM_CHUNK), F32),
        compiler_params=_cparams("parallel", "parallel"),
        name="s5_in",
    )(x, mod, g, wt_bf)


def _s5_scan_kernel(u_ref, k_ref, kn_ref, pow_ref, coef_ref, a_ref, d_ref, y_ref,
                    m0_ref, m1_ref, acc_ref, s_ref, sw_ref, sp_ref):
    bsz, grp, n_chunks, tc = u_ref.shape
    g = pl.program_id(0)
    srow = lax.broadcasted_iota(jnp.int32, (tc, tc), 0)
    tcol = lax.broadcasted_iota(jnp.int32, (tc, tc), 1)
    causal = tcol >= srow

    def build_rows(src_ref, dst_ref, i):
        blks = []
        for j in range(grp):
            row = src_ref[0, pl.ds(i * grp + j, 1), :]
            blk = pltpu.roll(jnp.broadcast_to(row, (tc, tc)), 0, 1, stride=1, stride_axis=0)
            blks.append(jnp.where(causal, blk, 0.0))
        dst_ref[pl.ds(pl.multiple_of(i * tc, tc), tc), :] = jnp.concatenate(blks, axis=1).astype(BF16)

    @pl.when(g == 0)
    def _():
        def first(i, carry):
            build_rows(k_ref, m0_ref, i)
            return carry

        lax.fori_loop(0, grp, first, 0)

    x = jnp.concatenate(
        [jnp.concatenate([u_ref[b, k] for k in range(grp)], axis=1) for b in range(bsz)], axis=0)
    xb = x.astype(BF16)
    def expand(pa, pb, ca, cb):
        return jnp.concatenate(
            [(pow_ref[0, pa] * coef_ref[0, ca, i:i + 1, :] + pow_ref[0, pb] * coef_ref[0, cb, i:i + 1, :]).astype(BF16)
             for i in range(grp)], axis=0)

    f_mat = expand(0, 1, 0, 1)
    s_loc = jnp.dot(xb, f_mat, preferred_element_type=F32)
    s_ref[...] = s_loc
    half = s_loc.shape[1] // 2
    sw_ref[...] = jnp.concatenate([s_loc[:, half:], s_loc[:, :half]], axis=1)
    a1 = a_ref[0, 0:1, :]
    a2 = a_ref[0, 1:2, :]
    s = [jnp.zeros((1, s_loc.shape[1]), F32) for _ in range(bsz)]
    sw = [jnp.zeros((1, s_loc.shape[1]), F32) for _ in range(bsz)]
    for c in range(n_chunks):
        for b in range(bsz):
            r = b * n_chunks + c
            sp_ref[r:r + 1, :] = s[b]
            s_new = a1 * s[b] + a2 * sw[b] + s_ref[r:r + 1, :]
            sw[b] = a1 * sw[b] - a2 * s[b] + sw_ref[r:r + 1, :]
            s[b] = s_new
    et_mat = expand(2, 3, 2, 3)
    acc_ref[...] = lax.dot_general(sp_ref[...].astype(BF16), et_mat, (((1,), (1,)), ((), ())),
                                   preferred_element_type=F32) + d_ref[0] * x

    def run(m_cur_ref, m_next_ref):
        def step(c, carry):
            for h in range(S5_K_BLOCKS):
                build_rows(kn_ref, m_next_ref, c * S5_K_BLOCKS + h)
            xc = jnp.concatenate(
                [jnp.concatenate([u_ref[b, c * S5_K_BLOCKS + h] for h in range(S5_K_BLOCKS)], axis=1)
                 for b in range(bsz)], axis=0).astype(BF16)
            rows = pl.ds(pl.multiple_of(c * (S5_K_BLOCKS * tc), S5_K_BLOCKS * tc), S5_K_BLOCKS * tc)
            acc_ref[...] += jnp.dot(xc, m_cur_ref[rows, :], preferred_element_type=F32)
            return carry

        lax.fori_loop(0, grp // S5_K_BLOCKS, step, 0)

    @pl.when(g % 2 == 0)
    def _():
        run(m0_ref, m1_ref)

    @pl.when(g % 2 == 1)
    def _():
        run(m1_ref, m0_ref)
    for b in range(bsz):
        for j in range(grp):
            y_ref[b, j] = acc_ref[b * n_chunks:(b + 1) * n_chunks, j * tc:(j + 1) * tc]


def _s5_scan(u4, kern, pow_tab, coef_tab, a_pack, d_row):
    b, h, nc, tc = u4.shape
    g = kern.shape[0]
    grp = h // g
    k = grp * tc
    p2 = pow_tab.shape[3]
    rows = b * nc
    return pl.pallas_call(
        _s5_scan_kernel,
        grid=(g,),
        in_specs=[pl.BlockSpec((b, grp, nc, tc), lambda i: (0, i, 0, 0)),
                  pl.BlockSpec((1, grp * grp, tc), lambda i: (i, 0, 0)),
                  pl.BlockSpec((1, grp * grp, tc), lambda i: (jnp.minimum(i + 1, g - 1), 0, 0)),
                  pl.BlockSpec((1, 4, tc, p2), lambda i: (i, 0, 0, 0)),
                  pl.BlockSpec((1, 4, grp, p2), lambda i: (i, 0, 0, 0)),
                  pl.BlockSpec((1, 8, p2), lambda i: (i, 0, 0)),
                  pl.BlockSpec((1, 1, k), lambda i: (i, 0, 0))],
        out_specs=pl.BlockSpec((b, grp, nc, tc), lambda i: (0, i, 0, 0)),
        out_shape=jax.ShapeDtypeStruct(u4.shape, F32),
        scratch_shapes=[pltpu.VMEM((k, k), BF16), pltpu.VMEM((k, k), BF16),
                        pltpu.VMEM((rows, k), F32), pltpu.VMEM((rows, p2), F32),
                        pltpu.VMEM((rows, p2), F32), pltpu.VMEM((rows, p2), F32)],
        compiler_params=_cparams("arbitrary"),
        name="s5_scan",
    )(u4, kern, kern, pow_tab, coef_tab, a_pack, d_row)


def _s5_tables(lam_re, lam_im, log_dt, b_re, b_im, c_re, c_im, d_skip):
    g, p = lam_re.shape
    k = SSM_GROUP
    tc = SSM_CHUNK
    dt = jnp.exp(log_dt)[:, None]
    mag = jnp.exp(lam_re * dt)
    ab_re = mag * jnp.cos(lam_im * dt)
    ab_im = mag * jnp.sin(lam_im * dt)
    den = lam_re * lam_re + lam_im * lam_im
    cf_re = ((ab_re - 1) * lam_re + ab_im * lam_im) / den
    cf_im = (ab_im * lam_re - (ab_re - 1) * lam_im) / den
    bb_re = cf_re[..., None] * b_re - cf_im[..., None] * b_im
    bb_im = cf_re[..., None] * b_im + cf_im[..., None] * b_re
    lre = jnp.concatenate([lam_re * dt, lam_re * dt], axis=-1)
    lim = jnp.concatenate([lam_im * dt, lam_im * dt], axis=-1)
    lags = jnp.arange(tc + 1, dtype=F32)[None, :, None]
    pmag = jnp.exp(lags * lre[:, None, :])
    pw_re = pmag * jnp.cos(lags * lim[:, None, :])
    pw_im = pmag * jnp.sin(lags * lim[:, None, :])
    cb_re = jnp.einsum('gjp,gpi->gpij', c_re, bb_re) - jnp.einsum('gjp,gpi->gpij', c_im, bb_im)
    cb_im = jnp.einsum('gjp,gpi->gpij', c_re, bb_im) + jnp.einsum('gjp,gpi->gpij', c_im, bb_re)
    kern = (jnp.einsum('glp,gpij->gijl', pw_re[:, :tc, :p], cb_re, precision=HIGHEST)
            - jnp.einsum('glp,gpij->gijl', pw_im[:, :tc, :p], cb_im, precision=HIGHEST))
    kern = kern.reshape(g, k * k, tc)
    sign = jnp.concatenate([-jnp.ones((p,), F32), jnp.ones((p,), F32)])
    flip = (jnp.arange(tc)[:, None] + jnp.arange(tc)[None, :] == tc - 1).astype(F32)
    pr2 = jnp.einsum('st,gtq->gsq', flip, pw_re[:, :tc], precision=HIGHEST)
    pi2 = jnp.einsum('st,gtq->gsq', flip, pw_im[:, :tc], precision=HIGHEST) * sign
    bb1 = jnp.concatenate([bb_re, bb_im], axis=1).transpose(0, 2, 1)
    bb2 = jnp.concatenate([bb_im, bb_re], axis=1).transpose(0, 2, 1)
    qr2 = pw_re[:, 1:]
    qi2 = pw_im[:, 1:]
    ca = jnp.concatenate([c_re, -c_im], axis=-1)
    cb = jnp.concatenate([-c_im, -c_re], axis=-1)
    pow_tab = jnp.stack([pr2, pi2, qr2, qi2], axis=1)
    coef_tab = jnp.stack([bb1, bb2, ca, cb], axis=1)
    a1 = pw_re[:, tc]
    a2 = pw_im[:, tc] * sign
    a_pack = jnp.concatenate([a1[:, None], a2[:, None], jnp.zeros((g, 6, 2 * p), F32)], axis=1)
    d_row = jnp.repeat(d_skip.reshape(g, k), tc, axis=1).reshape(g, 1, k * tc)
    return kern, pow_tab, coef_tab, a_pack, d_row


def _gelu_tanh(x):
    return 0.5 * x * (1.0 + jnp.tanh(math.sqrt(2.0 / math.pi) * (x + 0.044715 * (x * x * x))))


def _s5_out_kernel(y_ref, x_ref, mod_ref, w_ref, g2_ref, wr_ref, br_ref,
                   x_out_ref, hn_ref, eid_ref, wts_ref, cnt_ref):
    mod = mod_ref[0]
    h = y_ref.shape[1]
    yt = _gelu_tanh(y_ref[0].reshape(h, S5_TILE))
    d = w_ref.shape[1] // 2
    sub_rows = MOE_BLOCK
    n_sub = S5_TILE // sub_rows
    per_block = MOE_BLOCK // sub_rows

    def project(p):
        return lax.dot_general(yt[:, p * sub_rows:(p + 1) * sub_rows].astype(BF16), w_ref[...],
                               (((0,), (0,)), ((), ())), preferred_element_type=F32)

    o_next = project(0)
    cnt = None
    for p in range(n_sub):
        o = o_next
        if p + 1 < n_sub:
            o_next = project(p + 1)
        rows = slice(p * sub_rows, (p + 1) * sub_rows)
        out = o[:, :d] * jax.nn.sigmoid(o[:, d:])
        c = _residual_and_route(x_ref[0, rows, :], out, mod, g2_ref, wr_ref, br_ref,
                                x_out_ref.at[0, rows, :], hn_ref.at[rows, :], eid_ref.at[rows, :],
                                wts_ref.at[rows, :])
        cnt = c if p % per_block == 0 else cnt + c
        if p % per_block == per_block - 1:
            cnt_ref[p // per_block] = jnp.broadcast_to(cnt, cnt_ref.shape[1:])


def _mixer_out_specs(b, l, d, tile):
    nt = l // tile
    nblk = tile // MOE_BLOCK
    specs = [pl.BlockSpec((1, tile, d), lambda i, j: (i, j, 0)),
             pl.BlockSpec((tile, d), lambda i, j: (i * nt + j, 0)),
             pl.BlockSpec((tile, TOP_K), lambda i, j: (i * nt + j, 0)),
             pl.BlockSpec((tile, TOP_K), lambda i, j: (i * nt + j, 0)),
             pl.BlockSpec((nblk, 8, LANES), lambda i, j: (i * nt + j, 0, 0))]
    shapes = [jax.ShapeDtypeStruct((b, l, d), F32),
              jax.ShapeDtypeStruct((b * l, d), BF16),
              jax.ShapeDtypeStruct((b * l, TOP_K), jnp.int32),
              jax.ShapeDtypeStruct((b * l, TOP_K), F32),
              jax.ShapeDtypeStruct((b * l // MOE_BLOCK, 8, LANES), F32)]
    return specs, shapes


def _s5_out(y4, x, mod, w_bf, g2, wr, br):
    b, l, d = x.shape
    h = y4.shape[1]
    specs, shapes = _mixer_out_specs(b, l, d, S5_TILE)
    return pl.pallas_call(
        _s5_out_kernel,
        grid=(b, l // S5_TILE),
        in_specs=[pl.BlockSpec((1, h, S5_TILE // SSM_CHUNK, SSM_CHUNK), lambda i, j: (i, 0, j, 0)),
                  pl.BlockSpec((1, S5_TILE, d), lambda i, j: (i, j, 0)),
                  pl.BlockSpec((1, 8, d), lambda i, j: (i, 0, 0)),
                  pl.BlockSpec(w_bf.shape, lambda i, j: (0, 0), pipeline_mode=pl.Buffered(1)),
                  pl.BlockSpec((1, d), lambda i, j: (0, 0)),
                  pl.BlockSpec((d, 2 * LANES), lambda i, j: (0, 0)),
                  pl.BlockSpec((1, LANES), lambda i, j: (0, 0))],
        out_specs=specs,
        out_shape=shapes,
        compiler_params=_cparams("parallel", "parallel"),
        name="s5_out",
    )(y4, x, mod, w_bf, g2, wr, br)


def _conv_mixer_body(x_in_ref, mod_ref, g1_ref, win_ref, cw_ref, wout_ref, g2_ref, wr_ref, br_ref,
                     x_out_ref, hn_ref, eid_ref, wts_ref, cnt_ref, carry_ref):
    @pl.when(pl.program_id(1) == 0)
    def _():
        carry_ref[...] = jnp.zeros_like(carry_ref)

    mod = mod_ref[0]
    d = x_in_ref.shape[1]
    cw = cw_ref[...]
    n_sub = x_in_ref.shape[0] // SUB_ROWS
    sub = [slice(s * SUB_ROWS, (s + 1) * SUB_ROWS) for s in range(n_sub)]

    def project(s):
        hn = _norm_mod(x_in_ref[sub[s], :], g1_ref[...], mod[0:1], mod[1:2])
        return jnp.dot(hn.astype(BF16), win_ref[...], preferred_element_type=F32)

    def gate_conv(p, prev_tail):
        u = p[:, d:2 * d] * p[:, 2 * d:]
        ext = jnp.concatenate([prev_tail, u], axis=0)
        z = cw[0:1] * ext[6:6 + SUB_ROWS] + cw[1:2] * ext[7:7 + SUB_ROWS] + cw[2:3] * u
        return (p[:, :d] * z).astype(BF16), u[SUB_ROWS - 8:]

    ps = [project(s) for s in range(n_sub)]
    tail = carry_ref[...]
    outs = []
    for s in range(n_sub):
        gated, tail = gate_conv(ps[s], tail)
        outs.append(jnp.dot(gated, wout_ref[...], preferred_element_type=F32))
    carry_ref[...] = tail
    cnt = None
    for s in range(n_sub):
        c = _residual_and_route(x_in_ref[sub[s], :], outs[s], mod, g2_ref, wr_ref, br_ref,
                                x_out_ref.at[0, sub[s], :], hn_ref.at[sub[s], :], eid_ref.at[sub[s], :],
                                wts_ref.at[sub[s], :])
        cnt = c if s == 0 else cnt + c
    cnt_ref[0] = jnp.broadcast_to(cnt, cnt_ref.shape[1:])


_TN = (((0,), (0,)), ((), ()))


MAX_CHUNKS = LOCAL_ROWS // CHUNK_ROWS


def _rows_copy(src, s_row, dst, d_row, rows, sem):
    return pltpu.make_async_copy(src.at[pl.ds(s_row, rows), :], dst.at[pl.ds(d_row, rows), :], sem)


def _slot_masks(lp0, lp1):
    slot = lax.broadcasted_iota(jnp.int32, (lp0.shape[0], LOCAL_ROWS), 1)
    return slot == lp0, slot == lp1


_HI16 = -65536


def _pack_bf16_pairs(x):
    h = x.shape[1] // 2
    lo = lax.shift_right_logical(pltpu.bitcast(x[:, :h], jnp.int32), 16)
    hi = pltpu.bitcast(x[:, h:], jnp.int32) & _HI16
    return hi | lo


def _unpack_bf16_pairs(w):
    lo = pltpu.bitcast(w << 16, F32).astype(BF16)
    hi = pltpu.bitcast(w & _HI16, F32).astype(BF16)
    return lo, hi


def _moe_dispatch_kernel(crow_ref, tot_ref, padn_ref, pads_ref, nv_ref,
                         eid_ref, wts_ref, hn_ref, lofff_ref, tri_ref, xs_ref, lpos_ref,
                         loc_a_ref, loc_b_ref, zero_ref, sems):
    blk = pl.program_id(0)
    d = hn_ref.shape[1]
    eid = eid_ref[...]
    t = eid.shape[0]
    lane = lax.broadcasted_iota(jnp.int32, (t, LANES), 1)
    hit0 = lane == eid[:, 0:1]
    hit1 = lane == eid[:, 1:2]
    onehot = (hit0 | hit1).astype(BF16)
    before = jnp.dot(tri_ref[...], onehot, preferred_element_type=F32) + lofff_ref[0]
    lp0 = jnp.sum(jnp.where(hit0, before, 0.0), axis=-1, keepdims=True).astype(jnp.int32)
    lp1 = jnp.sum(jnp.where(hit1, before, 0.0), axis=-1, keepdims=True).astype(jnp.int32)
    col = lax.broadcasted_iota(jnp.int32, (t, TOP_K), 1)
    lpos_ref[...] = jnp.where(col == 0, lp0, lp1)
    wts = wts_ref[...]

    del tot_ref
    n_blk = pl.num_programs(0)

    @pl.when(blk == 0)
    def _():
        loc_b_ref[...] = jnp.zeros_like(loc_b_ref)

    def starts(table_row, src, sem):
        for q in range(MAX_CHUNKS):
            _rows_copy(src, q * CHUNK_ROWS, xs_ref,
                       pl.multiple_of(crow_ref[table_row * MAX_CHUNKS + q], CHUNK_ROWS), CHUNK_ROWS, sem).start()

    def wait_all(src, sem):
        _rows_copy(src, 0, xs_ref, 0, LOCAL_ROWS, sem).wait()

    def run(cur, cur_sem, prev, prev_sem):
        @pl.when(blk >= 1)
        def _():
            wait_all(cur, cur_sem)

        starts(blk, prev, prev_sem)
        m0, m1 = _slot_masks(lp0, lp1)
        wrow = jnp.sum(jnp.where(m0, wts[:, 0:1], 0.0) + jnp.where(m1, wts[:, 1:2], 0.0), axis=0, keepdims=True)
        sorted_rows = lax.dot_general((m0 | m1).astype(BF16), hn_ref[...], _TN, preferred_element_type=F32)
        cur[:, :d // 2] = _pack_bf16_pairs(sorted_rows)
        cur[:, d // 2:] = pltpu.bitcast(jnp.broadcast_to(wrow, (LANES, LOCAL_ROWS)).T, jnp.int32)

        @pl.when(blk == n_blk - 1)
        def _():
            starts(blk + 1, cur, cur_sem)
            wait_all(prev, prev_sem)
            wait_all(cur, cur_sem)

    @pl.when(blk % 2 == 0)
    def _():
        run(loc_a_ref, sems.at[0], loc_b_ref, sems.at[1])

    @pl.when(blk % 2 == 1)
    def _():
        run(loc_b_ref, sems.at[1], loc_a_ref, sems.at[0])

    @pl.when(blk == n_blk - 1)
    def _():
        sem = sems.at[0]
        zero_ref[...] = jnp.zeros_like(zero_ref)
        pad_bits = (EXPERT_TILE // CHUNK_ROWS - 1).bit_length()

        def pad_copies(e, carry, *, wait):
            padn = padn_ref[e]
            for k in range(pad_bits):
                @pl.when(((padn >> k) & 1) == 1)
                def _():
                    done = (padn & ((1 << k) - 1)) * CHUNK_ROWS
                    cp = _rows_copy(zero_ref, 0, xs_ref, pl.multiple_of(pads_ref[e] + done, CHUNK_ROWS),
                                    CHUNK_ROWS << k, sem)
                    if wait:
                        cp.wait()
                    else:
                        cp.start()
            return carry

        lax.fori_loop(0, N_EXPERTS, functools.partial(pad_copies, wait=False), 0)
        lax.fori_loop(0, N_EXPERTS, functools.partial(pad_copies, wait=True), 0)

        def tile_copy(i):
            return pltpu.make_async_copy(zero_ref, xs_ref.at[pl.ds(pl.multiple_of(i * EXPERT_TILE, EXPERT_TILE),
                                                                   EXPERT_TILE), :], sem)

        n_tiles = xs_ref.shape[0] // EXPERT_TILE

        def start_tile(i, c):
            tile_copy(i).start()
            return c

        def wait_tile(i, c):
            tile_copy(i).wait()
            return c

        lax.fori_loop(nv_ref[0], n_tiles, start_tile, 0)
        lax.fori_loop(nv_ref[0], n_tiles, wait_tile, 0)


def _moe_dispatch(tables, eid, wts, hn, loff_f, n_rows):
    t, d = hn.shape
    pos = jnp.arange(MOE_BLOCK, dtype=jnp.int32)
    tri = (pos[None, :] < pos[:, None]).astype(BF16)
    width = d // 2 + LANES
    grid_spec = pltpu.PrefetchScalarGridSpec(
        num_scalar_prefetch=5,
        grid=(t // MOE_BLOCK,),
        in_specs=[pl.BlockSpec((MOE_BLOCK, TOP_K), lambda i, *_: (i, 0)),
                  pl.BlockSpec((MOE_BLOCK, TOP_K), lambda i, *_: (i, 0)),
                  pl.BlockSpec((MOE_BLOCK, d), lambda i, *_: (i, 0)),
                  pl.BlockSpec((1, 1, LANES), lambda i, *_: (i, 0, 0)),
                  pl.BlockSpec((MOE_BLOCK, MOE_BLOCK), lambda i, *_: (0, 0))],
        out_specs=[pl.BlockSpec(memory_space=pl.ANY),
                   pl.BlockSpec((MOE_BLOCK, TOP_K), lambda i, *_: (i, 0))],
        scratch_shapes=[pltpu.VMEM((LOCAL_ROWS, width), jnp.int32), pltpu.VMEM((LOCAL_ROWS, width), jnp.int32),
                        pltpu.VMEM((EXPERT_TILE, width), jnp.int32), pltpu.SemaphoreType.DMA((2,))],
    )
    return pl.pallas_call(
        _moe_dispatch_kernel,
        grid_spec=grid_spec,
        out_shape=[jax.ShapeDtypeStruct((n_rows, width), jnp.int32),
                   jax.ShapeDtypeStruct((t, TOP_K), jnp.int32)],
        compiler_params=_cparams("arbitrary"),
        name="moe_dispatch",
    )(*tables, eid, wts, hn, loff_f, tri)


def _moe_experts_kernel(te_ref, tv_ref, tf_ref, ts_ref, xs_ref, w1_ref, w3_ref, w2_ref, o_ref,
                        w1b_ref, w3b_ref, w2b_ref):
    del te_ref, ts_ref
    i = pl.program_id(0)

    @pl.when(tf_ref[i] != 0)
    def _():
        def cast_rows(r, carry):
            rows = pl.ds(pl.multiple_of(r * LANES, LANES), LANES)
            w1b_ref[rows, :] = w1_ref[0, 0, rows, :].astype(BF16)
            w3b_ref[rows, :] = w3_ref[0, 0, rows, :].astype(BF16)

            @pl.when(r < w2b_ref.shape[0] // LANES)
            def _():
                w2b_ref[rows, :] = w2_ref[0, 0, rows, :].astype(BF16)

            return carry

        lax.fori_loop(0, w1b_ref.shape[0] // LANES, cast_rows, 0)

    @pl.when(tv_ref[i] != 0)
    def _():
        half = o_ref.shape[1]
        x_lo, x_hi = _unpack_bf16_pairs(xs_ref[:, :half])
        w = pltpu.bitcast(xs_ref[:, half:half + 1], F32)
        de = w1b_ref.shape[1]
        o = None
        for c0 in range(0, de, MXU_WIDTH):
            cols = slice(c0, c0 + MXU_WIDTH)
            a = (jnp.dot(x_lo, w1b_ref[:half, cols], preferred_element_type=F32)
                 + jnp.dot(x_hi, w1b_ref[half:, cols], preferred_element_type=F32))
            b = (jnp.dot(x_lo, w3b_ref[:half, cols], preferred_element_type=F32)
                 + jnp.dot(x_hi, w3b_ref[half:, cols], preferred_element_type=F32))
            h = (a * jax.nn.sigmoid(a) * b).astype(BF16)
            part = jnp.dot(h, w2b_ref[cols, :], preferred_element_type=F32)
            o = part if o is None else o + part
        o = o * w
        o_ref[...] = _pack_bf16_pairs(o.astype(BF16).astype(F32))

    @pl.when(tv_ref[i] == 0)
    def _():
        o_ref[...] = jnp.zeros_like(o_ref)


def _moe_experts(tile_expert, tile_valid, tile_first, tile_src, xs, w1, w3, w2, layer):
    r, width = xs.shape
    d = (width - LANES) * 2
    de = w1.shape[3]
    grid_spec = pltpu.PrefetchScalarGridSpec(
        num_scalar_prefetch=4,
        grid=(r // EXPERT_TILE,),
        in_specs=[pl.BlockSpec((EXPERT_TILE, width), lambda i, te, tv, tf, ts: (ts[i], 0)),
                  pl.BlockSpec((1, 1, d, de), lambda i, te, tv, tf, ts: (layer, te[i], 0, 0)),
                  pl.BlockSpec((1, 1, d, de), lambda i, te, tv, tf, ts: (layer, te[i], 0, 0)),
                  pl.BlockSpec((1, 1, de, d), lambda i, te, tv, tf, ts: (layer, te[i], 0, 0))],
        out_specs=pl.BlockSpec((EXPERT_TILE, d // 2), lambda i, te, tv, tf, ts: (i, 0)),
        scratch_shapes=[pltpu.VMEM((d, de), BF16), pltpu.VMEM((d, de), BF16), pltpu.VMEM((de, d), BF16)],
    )
    return pl.pallas_call(
        _moe_experts_kernel,
        grid_spec=grid_spec,
        out_shape=jax.ShapeDtypeStruct((r, d // 2), jnp.int32),
        compiler_params=_cparams("arbitrary"),
        name="moe_experts",
    )(tile_expert, tile_valid, tile_first, tile_src, xs, w1, w3, w2)


def _combine_rows(crow_ref, lpos_ref, o_hbm_ref, loc_a_ref, loc_b_ref, sems, consume, fetch_first):
    blk = pl.program_id(0) * pl.num_programs(1) + pl.program_id(1)
    n_blk = pl.num_programs(0) * pl.num_programs(1)

    def fetch(b, loc, sem):
        for q in range(MAX_CHUNKS):
            _rows_copy(o_hbm_ref, pl.multiple_of(crow_ref[b * MAX_CHUNKS + q], CHUNK_ROWS), loc,
                       q * CHUNK_ROWS, CHUNK_ROWS, sem).start()

    def wait_all(loc, sem):
        _rows_copy(o_hbm_ref, 0, loc, 0, LOCAL_ROWS, sem).wait()

    @pl.when(blk == 0)
    def _():
        fetch(blk, loc_a_ref, sems.at[0])

    def run(cur, cur_sem, nxt, nxt_sem):
        if fetch_first:
            fetch(jnp.minimum(blk + 1, n_blk - 1), nxt, nxt_sem)
        wait_all(cur, cur_sem)
        if not fetch_first:
            fetch(jnp.minimum(blk + 1, n_blk - 1), nxt, nxt_sem)
        lp = lpos_ref[...]
        m0, m1 = _slot_masks(lp[:, 0:1], lp[:, 1:2])
        pt = (m0 | m1).astype(BF16)
        o_lo, o_hi = _unpack_bf16_pairs(cur[...])
        consume(jnp.concatenate([jnp.dot(pt, o_lo, preferred_element_type=F32),
                                 jnp.dot(pt, o_hi, preferred_element_type=F32)], axis=1))

        @pl.when(blk == n_blk - 1)
        def _():
            wait_all(nxt, nxt_sem)

    @pl.when(blk % 2 == 0)
    def _():
        run(loc_a_ref, sems.at[0], loc_b_ref, sems.at[1])

    @pl.when(blk % 2 == 1)
    def _():
        run(loc_b_ref, sems.at[1], loc_a_ref, sems.at[0])


def _moe_combine_final_kernel(crow_ref, tot_ref, lpos_ref, x_ref, mod_ref, fg_ref, o_hbm_ref,
                              out_ref, loc_a_ref, loc_b_ref, sems):
    del tot_ref

    def finish(y):
        x2 = x_ref[0] + (1.0 + mod_ref[0, 5:6]) * y
        out_ref[0] = (x2 * lax.rsqrt(jnp.mean(x2 * x2, axis=-1, keepdims=True) + RMS_EPS)) * fg_ref[...]

    _combine_rows(crow_ref, lpos_ref, o_hbm_ref, loc_a_ref, loc_b_ref, sems, finish, fetch_first=True)


def _moe_combine_conv_kernel(crow_ref, tot_ref, lpos_ref, x_ref, mod_prev_ref, o_hbm_ref,
                             mod_ref, g1_ref, win_ref, cw_ref, wout_ref, g2_ref, wr_ref, br_ref,
                             x_out_ref, hn_ref, eid_ref, wts_ref, cnt_ref,
                             loc_a_ref, loc_b_ref, sems, xmid_ref, carry_ref):
    del tot_ref

    def residual(y):
        xmid_ref[...] = x_ref[0] + (1.0 + mod_prev_ref[0, 5:6]) * y

    _combine_rows(crow_ref, lpos_ref, o_hbm_ref, loc_a_ref, loc_b_ref, sems, residual, fetch_first=False)
    _conv_mixer_body(xmid_ref, mod_ref, g1_ref, win_ref, cw_ref, wout_ref, g2_ref, wr_ref, br_ref,
                     x_out_ref, hn_ref, eid_ref, wts_ref, cnt_ref, carry_ref)


def _combine_in_specs(nt, d):
    return [pl.BlockSpec((MOE_BLOCK, TOP_K), lambda i, j, *_: (i * nt + j, 0)),
            pl.BlockSpec((1, MOE_BLOCK, d), lambda i, j, *_: (i, j, 0)),
            pl.BlockSpec((1, 8, d), lambda i, j, *_: (i, 0, 0))]


def _combine_scratch(d):
    return [pltpu.VMEM((LOCAL_ROWS, d // 2), jnp.int32), pltpu.VMEM((LOCAL_ROWS, d // 2), jnp.int32),
            pltpu.SemaphoreType.DMA((2,))]


def _moe_combine_final(tables, lpos, x, mod, final_g, o_sorted):
    b, l, d = x.shape
    nt = l // MOE_BLOCK
    grid_spec = pltpu.PrefetchScalarGridSpec(
        num_scalar_prefetch=2,
        grid=(b, nt),
        in_specs=_combine_in_specs(nt, d) + [pl.BlockSpec((1, d), lambda i, j, *_: (0, 0)),
                                             pl.BlockSpec(memory_space=pl.ANY)],
        out_specs=pl.BlockSpec((1, MOE_BLOCK, d), lambda i, j, *_: (i, j, 0)),
        scratch_shapes=_combine_scratch(d),
    )
    return pl.pallas_call(
        _moe_combine_final_kernel,
        grid_spec=grid_spec,
        out_shape=jax.ShapeDtypeStruct((b, l, d), F32),
        compiler_params=_cparams("arbitrary", "arbitrary"),
        name="moe_combine_final",
    )(*tables[:2], lpos, x, mod, final_g, o_sorted)


def _moe_combine_conv(tables, lpos, x, mod_prev, o_sorted, mod, g1, win_bf, cw8, wout_bf, g2, wr, br):
    b, l, d = x.shape
    nt = l // MOE_BLOCK
    specs, shapes = _mixer_out_specs(b, l, d, TOKEN_TILE)
    const = lambda shape: pl.BlockSpec(shape, lambda i, j, *_: (0,) * len(shape))
    grid_spec = pltpu.PrefetchScalarGridSpec(
        num_scalar_prefetch=2,
        grid=(b, nt),
        in_specs=_combine_in_specs(nt, d) + [
            pl.BlockSpec(memory_space=pl.ANY),
            pl.BlockSpec((1, 8, d), lambda i, j, *_: (i, 0, 0)),
            const((1, d)), const(win_bf.shape), const((8, d)), const(wout_bf.shape),
            const((1, d)), const((d, 2 * LANES)), const((1, LANES))],
        out_specs=[pl.BlockSpec(s.block_shape, lambda i, j, *_, f=s.index_map: f(i, j)) for s in specs],
        scratch_shapes=_combine_scratch(d) + [pltpu.VMEM((MOE_BLOCK, d), F32), pltpu.VMEM((8, d), F32)],
    )
    return pl.pallas_call(
        _moe_combine_conv_kernel,
        grid_spec=grid_spec,
        out_shape=shapes,
        compiler_params=_cparams("arbitrary", "arbitrary"),
        name="moe_combine_conv",
    )(*tables[:2], lpos, x, mod_prev, o_sorted, mod, g1, win_bf, cw8, wout_bf, g2, wr, br)


def _moe_sorted_experts(hn, eid, wts, cnt, w1, w3, w2, layer):
    t, d = hn.shape
    nblk = t // MOE_BLOCK
    i32 = jnp.int32
    n = cnt[:, 0, :N_EXPERTS].astype(i32)
    run = (n + CHUNK_ROWS - 1) // CHUNK_ROWS * CHUNK_ROWS
    loff = jnp.cumsum(run, axis=1) - run
    rows_e = jnp.sum(run, axis=0)
    tiles_e = (rows_e + EXPERT_TILE - 1) // EXPERT_TILE
    tile_end = jnp.cumsum(tiles_e)
    base = (tile_end - tiles_e) * EXPERT_TILE
    goff = base[None, :] + jnp.cumsum(run, axis=0) - run
    nch = run // CHUNK_ROWS
    tot = jnp.sum(nch, axis=1)
    ch_end = jnp.cumsum(nch, axis=1)
    q = jnp.arange(MAX_CHUNKS, dtype=i32)
    in_run = ((q[None, :, None] >= (ch_end - nch)[:, None, :]) & (q[None, :, None] < ch_end[:, None, :])).astype(i32)
    crow = jnp.sum(in_run * (goff[:, None, :] + (q[None, :, None] - (ch_end - nch)[:, None, :]) * CHUNK_ROWS), axis=-1)
    padn = (tiles_e * EXPERT_TILE - rows_e) // CHUNK_ROWS
    pads = base + rows_e
    loff_f = jnp.zeros((nblk, 1, LANES), F32).at[:, 0, :N_EXPERTS].set(loff.astype(F32))

    max_rows = t * TOP_K + nblk * N_EXPERTS * (CHUNK_ROWS - 1) + N_EXPERTS * (EXPERT_TILE - 1)
    real_tiles = -(-max_rows // EXPERT_TILE)
    scratch_tiles = -(-2 * LOCAL_ROWS // EXPERT_TILE)
    max_tiles = real_tiles + scratch_tiles
    scratch_row = real_tiles * EXPERT_TILE + q[None, :] * CHUNK_ROWS
    parity = (jnp.arange(nblk, dtype=i32) % 2)[:, None]
    crow_out = jnp.where(q[None, :] < tot[:, None], crow, scratch_row + parity * LOCAL_ROWS)
    crow_out = jnp.concatenate([scratch_row + LOCAL_ROWS, crow_out], axis=0)
    tables = (crow.reshape(-1), tot)
    dispatch_tables = (crow_out.reshape(-1), tot, padn, pads, tile_end[-1:])
    tile_ids = jnp.arange(max_tiles, dtype=i32)
    n_valid = tile_end[-1]
    tile_src = jnp.minimum(tile_ids, n_valid - 1)
    tile_expert = jnp.sum((tile_src[:, None] >= tile_end[None, :]).astype(i32), axis=1)
    tile_valid = (tile_ids < n_valid).astype(i32)
    tile_first = jnp.concatenate([jnp.ones((1,), i32), (tile_expert[1:] != tile_expert[:-1]).astype(i32)])

    xs, lpos = _moe_dispatch(dispatch_tables, eid, wts, hn, loff_f, max_tiles * EXPERT_TILE)
    o_sorted = _moe_experts(tile_expert, tile_valid, tile_first, tile_src, xs, w1, w3, w2, layer)
    return tables, lpos, o_sorted


def _router_pack(wg, bg, we, be):
    d = wg.shape[0]
    wr = jnp.zeros((d, LANES), F32).at[:, :N_EXPERTS].set(we).at[:, N_EXPERTS:N_EXPERTS + N_GROUPS].set(wg)
    br = jnp.zeros((1, LANES), F32).at[0, :N_EXPERTS].set(be).at[0, N_EXPERTS:N_EXPERTS + N_GROUPS].set(bg)
    wr_hi = wr.astype(BF16)
    wr_lo = (wr - wr_hi.astype(F32)).astype(BF16)
    return jnp.concatenate([wr_hi, wr_lo], axis=1), br


def kernel(x, c, ada_w, ada_b, norm1_g, norm2_g, ssm_w_in, ssm_lam_re, ssm_lam_im, ssm_log_dt, ssm_b_re, ssm_b_im, ssm_c_re, ssm_c_im, ssm_d, ssm_w_glu, conv_w_in, conv_w, conv_w_out, moe_wg, moe_bg, moe_we, moe_be, moe_w1, moe_w3, moe_w2, final_g):
    b, l, d = x.shape
    depth = ada_w.shape[0]
    c8 = jnp.zeros((8, d), F32).at[:b].set(c)
    mod_all = _adaln(c8, ada_w, ada_b)[:, :b].reshape(depth, b, 6, d)
    mod_all = jnp.concatenate([mod_all, jnp.zeros((depth, b, 2, d), F32)], axis=2)
    fg = final_g.reshape(1, d)

    mod = mod_all[0]
    u4 = _s5_in(x, mod, norm1_g[0:1], ssm_w_in[0].T.astype(BF16))
    kern, pow_tab, coef_tab, a_pack, d_row = _s5_tables(
        ssm_lam_re[0], ssm_lam_im[0], ssm_log_dt[0], ssm_b_re[0], ssm_b_im[0],
        ssm_c_re[0], ssm_c_im[0], ssm_d[0])
    y4 = _s5_scan(u4, kern, pow_tab, coef_tab, a_pack, d_row)
    wr, br = _router_pack(moe_wg[0], moe_bg[0], moe_we[0], moe_be[0])
    x1, hn, eid, wts, cnt = _s5_out(y4, x, mod, ssm_w_glu[0].astype(BF16), norm2_g[0:1], wr, br)
    tables, lpos, o_sorted = _moe_sorted_experts(hn, eid, wts, cnt, moe_w1, moe_w3, moe_w2, 0)

    mod1 = mod_all[1]
    cw8 = jnp.zeros((8, d), F32).at[:conv_w.shape[1]].set(conv_w[0])
    wr, br = _router_pack(moe_wg[1], moe_bg[1], moe_we[1], moe_be[1])
    x3, hn, eid, wts, cnt = _moe_combine_conv(
        tables, lpos, x1, mod, o_sorted, mod1, norm1_g[1:2], conv_w_in[0].astype(BF16), cw8,
        conv_w_out[0].astype(BF16), norm2_g[1:2], wr, br)
    tables, lpos, o_sorted = _moe_sorted_experts(hn, eid, wts, cnt, moe_w1, moe_w3, moe_w2, 1)
    return _moe_combine_final(tables, lpos, x3, mod1, fg, o_sorted)
```

```python
import functools
import math

import jax
import jax.numpy as jnp
from jax import lax
from jax.experimental import pallas as pl
from jax.experimental.pallas import tpu as pltpu

F32 = jnp.float32
BF16 = jnp.bfloat16
HIGHEST = lax.Precision.HIGHEST

RMS_EPS = 1e-6
SSM_GROUP = 16
SSM_CHUNK = 128
N_GROUPS = 4
EXPERTS_PER_GROUP = 8
N_EXPERTS = N_GROUPS * EXPERTS_PER_GROUP
TOP_K = 2
LANES = 128
MXU_WIDTH = 256
TOKEN_TILE = 512
S5_TILE = 1024
S5_K_BLOCKS = 2
SUB_ROWS = 256
MOE_BLOCK = TOKEN_TILE
CHUNK_ROWS = 8
LOCAL_ROWS = -(-(TOP_K * MOE_BLOCK + N_EXPERTS * (CHUNK_ROWS - 1)) // LANES) * LANES
EXPERT_TILE = 512
VMEM_LIMIT = 56 * 1024 * 1024
NEG_INF = -1e30


def _cparams(*sem):
    return pltpu.CompilerParams(dimension_semantics=sem, vmem_limit_bytes=VMEM_LIMIT)


def _adaln_kernel(c_ref, w_ref, b_ref, o_ref):
    c = c_ref[...]
    cond = c * jax.nn.sigmoid(c)
    o_ref[0] = jnp.dot(cond, w_ref[0], precision=HIGHEST, preferred_element_type=F32) + b_ref[0]


def _adaln(c8, ada_w, ada_b):
    depth, d, n = ada_w.shape
    tn = 1536
    return pl.pallas_call(
        _adaln_kernel,
        grid=(depth, n // tn),
        in_specs=[pl.BlockSpec((8, d), lambda i, j: (0, 0)),
                  pl.BlockSpec((1, d, tn), lambda i, j: (i, 0, j)),
                  pl.BlockSpec((1, 1, tn), lambda i, j: (i, 0, j))],
        out_specs=pl.BlockSpec((1, 8, tn), lambda i, j: (i, 0, j)),
        out_shape=jax.ShapeDtypeStruct((depth, 8, n), F32),
        compiler_params=_cparams("parallel", "parallel"),
        name="adaln",
    )(c8, ada_w, ada_b.reshape(depth, 1, n))


def _norm_mod(x, g, shift, scale):
    y = x * lax.rsqrt(jnp.mean(x * x, axis=-1, keepdims=True) + RMS_EPS)
    return (y * g) * (1.0 + scale) + shift


def _route(hn, hn_hi, wr_ref, br_ref, eid_ref, wts_ref):
    hn_lo = (hn - hn_hi.astype(F32)).astype(BF16)
    both = jnp.dot(hn_hi, wr_ref[...], preferred_element_type=F32)
    logits = (both[:, :LANES] + both[:, LANES:]
              + jnp.dot(hn_lo, wr_ref[:, :LANES], preferred_element_type=F32)) + br_ref[...]
    lane = lax.broadcasted_iota(jnp.int32, logits.shape, 1)
    is_grp = (lane >= N_EXPERTS) & (lane < N_EXPERTS + N_GROUPS)
    lg = jnp.where(is_grp, logits, NEG_INF)
    gmax = jnp.max(lg, axis=-1, keepdims=True)
    gsum = jnp.sum(jnp.where(is_grp, jnp.exp(lg - gmax), 0.0), axis=-1, keepdims=True)
    gp = 1.0 / gsum
    gi = jnp.min(jnp.where(lg == gmax, lane, 2 * LANES), axis=-1, keepdims=True) - N_EXPERTS
    in_grp = (lane < N_EXPERTS) & ((lane // EXPERTS_PER_GROUP) == gi)
    le = jnp.where(in_grp, logits, NEG_INF)
    v1 = jnp.max(le, axis=-1, keepdims=True)
    i1 = jnp.min(jnp.where(le == v1, lane, 2 * LANES), axis=-1, keepdims=True)
    le2 = jnp.where(lane == i1, NEG_INF, le)
    v2 = jnp.max(le2, axis=-1, keepdims=True)
    i2 = jnp.min(jnp.where(le2 == v2, lane, 2 * LANES), axis=-1, keepdims=True)
    e2 = jnp.exp(v2 - v1)
    den = 1.0 + e2
    col = lax.broadcasted_iota(jnp.int32, (hn.shape[0], TOP_K), 1)
    eid_ref[...] = jnp.where(col == 0, i1, i2)
    wts_ref[...] = jnp.where(col == 0, gp / den, gp * e2 / den)
    chosen = ((lane == i1) | (lane == i2)).astype(F32)
    return jnp.sum(chosen, axis=0, keepdims=True)


def _residual_and_route(x, out, mod, g2_ref, wr_ref, br_ref, x_out_ref, hn_ref, eid_ref, wts_ref):
    x1 = x + (1.0 + mod[2:3]) * out
    x_out_ref[...] = x1
    hn = _norm_mod(x1, g2_ref[...], mod[3:4], mod[4:5])
    hn_hi = hn.astype(BF16)
    hn_ref[...] = hn_hi
    return _route(hn, hn_hi, wr_ref, br_ref, eid_ref, wts_ref)


def _s5_in_kernel(x_ref, mod_ref, g_ref, wt_ref, u_ref):
    mod = mod_ref[0]
    hn = _norm_mod(x_ref[0], g_ref[...], mod[0:1], mod[1:2])
    ut = lax.dot_general(wt_ref[...], hn.astype(BF16), (((1,), (1,)), ((), ())), preferred_element_type=F32)
    u_ref[0] = ut.reshape(ut.shape[0], S5_TILE // SSM_CHUNK, SSM_CHUNK)


def _s5_in(x, mod, g, wt_bf):
    b, l, d = x.shape
    h = wt_bf.shape[0]
    cpt = S5_TILE // SSM_CHUNK
    return pl.pallas_call(
        _s5_in_kernel,
        grid=(b, l // S5_TILE),
        in_specs=[pl.BlockSpec((1, S5_TILE, d), lambda i, j: (i, j, 0)),
                  pl.BlockSpec((1, 8, d), lambda i, j: (i, 0, 0)),
                  pl.BlockSpec((1, d), lambda i, j: (0, 0)),
                  pl.BlockSpec((h, d), lambda i, j: (0, 0))],
        out_specs=pl.BlockSpec((1, h, cpt, SSM_CHUNK), lambda i, j: (i, 0, j, 0)),
        out_shape=jax.ShapeDtypeStruct((b, h, l // SSM_CHUNK, SSM_CHUNK), F32),
        compiler_params=_cparams("parallel", "parallel"),
        name="s5_in",
    )(x, mod, g, wt_bf)


def _s5_scan_kernel(u_ref, k_ref, kn_ref, pow_ref, coef_ref, a_ref, d_ref, y_ref,
                    m0_ref, m1_ref, acc_ref, s_ref, sw_ref, sp_ref):
    bsz, grp, n_chunks, tc = u_ref.shape
    g = pl.program_id(0)
    srow = lax.broadcasted_iota(jnp.int32, (tc, tc), 0)
    tcol = lax.broadcasted_iota(jnp.int32, (tc, tc), 1)
    causal = tcol >= srow

    def build_rows(src_ref, dst_ref, i):
        blks = []
        for j in range(grp):
            row = src_ref[0, pl.ds(i * grp + j, 1), :]
            blk = pltpu.roll(jnp.broadcast_to(row, (tc, tc)), 0, 1, stride=1, stride_axis=0)
            blks.append(jnp.where(causal, blk, 0.0))
        dst_ref[pl.ds(pl.multiple_of(i * tc, tc), tc), :] = jnp.concatenate(blks, axis=1).astype(BF16)

    @pl.when(g == 0)
    def _():
        def first(i, carry):
            build_rows(k_ref, m0_ref, i)
            return carry

        lax.fori_loop(0, grp, first, 0)

    x = jnp.concatenate(
        [jnp.concatenate([u_ref[b, k] for k in range(grp)], axis=1) for b in range(bsz)], axis=0)
    xb = x.astype(BF16)
    def expand(pa, pb, ca, cb):
        return jnp.concatenate(
            [(pow_ref[0, pa] * coef_ref[0, ca, i:i + 1, :] + pow_ref[0, pb] * coef_ref[0, cb, i:i + 1, :]).astype(BF16)
             for i in range(grp)], axis=0)

    f_mat = expand(0, 1, 0, 1)
    s_loc = jnp.dot(xb, f_mat, preferred_element_type=F32)
    s_ref[...] = s_loc
    half = s_loc.shape[1] // 2
    sw_ref[...] = jnp.concatenate([s_loc[:, half:], s_loc[:, :half]], axis=1)
    a1 = a_ref[0, 0:1, :]
    a2 = a_ref[0, 1:2, :]
    s = [jnp.zeros((1, s_loc.shape[1]), F32) for _ in range(bsz)]
    sw = [jnp.zeros((1, s_loc.shape[1]), F32) for _ in range(bsz)]
    for c in range(n_chunks):
        for b in range(bsz):
            r = b * n_chunks + c
            sp_ref[r:r + 1, :] = s[b]
            s_new = a1 * s[b] + a2 * sw[b] + s_ref[r:r + 1, :]
            sw[b] = a1 * sw[b] - a2 * s[b] + sw_ref[r:r + 1, :]
            s[b] = s_new
    et_mat = expand(2, 3, 2, 3)
    acc_ref[...] = lax.dot_general(sp_ref[...].astype(BF16), et_mat, (((1,), (1,)), ((), ())),
                                   preferred_element_type=F32) + d_ref[0] * x

    def run(m_cur_ref, m_next_ref):
        def step(c, carry):
            for h in range(S5_K_BLOCKS):
                build_rows(kn_ref, m_next_ref, c * S5_K_BLOCKS + h)
            xc = jnp.concatenate(
                [jnp.concatenate([u_ref[b, c * S5_K_BLOCKS + h] for h in range(S5_K_BLOCKS)], axis=1)
                 for b in range(bsz)], axis=0).astype(BF16)
            rows = pl.ds(pl.multiple_of(c * (S5_K_BLOCKS * tc), S5_K_BLOCKS * tc), S5_K_BLOCKS * tc)
            acc_ref[...] += jnp.dot(xc, m_cur_ref[rows, :], preferred_element_type=F32)
            return carry

        lax.fori_loop(0, grp // S5_K_BLOCKS, step, 0)

    @pl.when(g % 2 == 0)
    def _():
        run(m0_ref, m1_ref)

    @pl.when(g % 2 == 1)
    def _():
        run(m1_ref, m0_ref)
    for b in range(bsz):
        for j in range(grp):
            y_ref[b, j] = acc_ref[b * n_chunks:(b + 1) * n_chunks, j * tc:(j + 1) * tc]


def _s5_scan(u4, kern, pow_tab, coef_tab, a_pack, d_row):
    b, h, nc, tc = u4.shape
    g = kern.shape[0]
    grp = h // g
    k = grp * tc
    p2 = pow_tab.shape[3]
    rows = b * nc
    return pl.pallas_call(
        _s5_scan_kernel,
        grid=(g,),
        in_specs=[pl.BlockSpec((b, grp, nc, tc), lambda i: (0, i, 0, 0)),
                  pl.BlockSpec((1, grp * grp, tc), lambda i: (i, 0, 0)),
                  pl.BlockSpec((1, grp * grp, tc), lambda i: (jnp.minimum(i + 1, g - 1), 0, 0)),
                  pl.BlockSpec((1, 4, tc, p2), lambda i: (i, 0, 0, 0)),
                  pl.BlockSpec((1, 4, grp, p2), lambda i: (i, 0, 0, 0)),
                  pl.BlockSpec((1, 8, p2), lambda i: (i, 0, 0)),
                  pl.BlockSpec((1, 1, k), lambda i: (i, 0, 0))],
        out_specs=pl.BlockSpec((b, grp, nc, tc), lambda i: (0, i, 0, 0)),
        out_shape=jax.ShapeDtypeStruct(u4.shape, F32),
        scratch_shapes=[pltpu.VMEM((k, k), BF16), pltpu.VMEM((k, k), BF16),
                        pltpu.VMEM((rows, k), F32), pltpu.VMEM((rows, p2), F32),
                        pltpu.VMEM((rows, p2), F32), pltpu.VMEM((rows, p2), F32)],
        compiler_params=_cparams("arbitrary"),
        name="s5_scan",
    )(u4, kern, kern, pow_tab, coef_tab, a_pack, d_row)


def _s5_tables(lam_re, lam_im, log_dt, b_re, b_im, c_re, c_im, d_skip):
    g, p = lam_re.shape
    k = SSM_GROUP
    tc = SSM_CHUNK
    dt = jnp.exp(log_dt)[:, None]
    mag = jnp.exp(lam_re * dt)
    ab_re = mag * jnp.cos(lam_im * dt)
    ab_im = mag * jnp.sin(lam_im * dt)
    den = lam_re * lam_re + lam_im * lam_im
    cf_re = ((ab_re - 1) * lam_re + ab_im * lam_im) / den
    cf_im = (ab_im * lam_re - (ab_re - 1) * lam_im) / den
    bb_re = cf_re[..., None] * b_re - cf_im[..., None] * b_im
    bb_im = cf_re[..., None] * b_im + cf_im[..., None] * b_re
    lre = jnp.concatenate([lam_re * dt, lam_re * dt], axis=-1)
    lim = jnp.concatenate([lam_im * dt, lam_im * dt], axis=-1)
    lags = jnp.arange(tc + 1, dtype=F32)[None, :, None]
    pmag = jnp.exp(lags * lre[:, None, :])
    pw_re = pmag * jnp.cos(lags * lim[:, None, :])
    pw_im = pmag * jnp.sin(lags * lim[:, None, :])
    cb_re = jnp.einsum('gjp,gpi->gpij', c_re, bb_re) - jnp.einsum('gjp,gpi->gpij', c_im, bb_im)
    cb_im = jnp.einsum('gjp,gpi->gpij', c_re, bb_im) + jnp.einsum('gjp,gpi->gpij', c_im, bb_re)
    kern = (jnp.einsum('glp,gpij->gijl', pw_re[:, :tc, :p], cb_re, precision=HIGHEST)
            - jnp.einsum('glp,gpij->gijl', pw_im[:, :tc, :p], cb_im, precision=HIGHEST))
    kern = kern.reshape(g, k * k, tc)
    sign = jnp.concatenate([-jnp.ones((p,), F32), jnp.ones((p,), F32)])
    flip = (jnp.arange(tc)[:, None] + jnp.arange(tc)[None, :] == tc - 1).astype(F32)
    pr2 = jnp.einsum('st,gtq->gsq', flip, pw_re[:, :tc], precision=HIGHEST)
    pi2 = jnp.einsum('st,gtq->gsq', flip, pw_im[:, :tc], precision=HIGHEST) * sign
    bb1 = jnp.concatenate([bb_re, bb_im], axis=1).transpose(0, 2, 1)
    bb2 = jnp.concatenate([bb_im, bb_re], axis=1).transpose(0, 2, 1)
    qr2 = pw_re[:, 1:]
    qi2 = pw_im[:, 1:]
    ca = jnp.concatenate([c_re, -c_im], axis=-1)
    cb = jnp.concatenate([-c_im, -c_re], axis=-1)
    pow_tab = jnp.stack([pr2, pi2, qr2, qi2], axis=1)
    coef_tab = jnp.stack([bb1, bb2, ca, cb], axis=1)
    a1 = pw_re[:, tc]
    a2 = pw_im[:, tc] * sign
    a_pack = jnp.concatenate([a1[:, None], a2[:, None], jnp.zeros((g, 6, 2 * p), F32)], axis=1)
    d_row = jnp.repeat(d_skip.reshape(g, k), tc, axis=1).reshape(g, 1, k * tc)
    return kern, pow_tab, coef_tab, a_pack, d_row


def _gelu_tanh(x):
    return 0.5 * x * (1.0 + jnp.tanh(math.sqrt(2.0 / math.pi) * (x + 0.044715 * (x * x * x))))


def _s5_out_kernel(y_ref, x_ref, mod_ref, w_ref, g2_ref, wr_ref, br_ref,
                   x_out_ref, hn_ref, eid_ref, wts_ref, cnt_ref):
    mod = mod_ref[0]
    h = y_ref.shape[1]
    yt = _gelu_tanh(y_ref[0].reshape(h, S5_TILE))
    d = w_ref.shape[1] // 2
    sub_rows = MOE_BLOCK
    n_sub = S5_TILE // sub_rows
    per_block = MOE_BLOCK // sub_rows

    def project(p):
        return lax.dot_general(yt[:, p * sub_rows:(p + 1) * sub_rows].astype(BF16), w_ref[...],
                               (((0,), (0,)), ((), ())), preferred_element_type=F32)

    o_next = project(0)
    cnt = None
    for p in range(n_sub):
        o = o_next
        if p + 1 < n_sub:
            o_next = project(p + 1)
        rows = slice(p * sub_rows, (p + 1) * sub_rows)
        out = o[:, :d] * jax.nn.sigmoid(o[:, d:])
        c = _residual_and_route(x_ref[0, rows, :], out, mod, g2_ref, wr_ref, br_ref,
                                x_out_ref.at[0, rows, :], hn_ref.at[rows, :], eid_ref.at[rows, :],
                                wts_ref.at[rows, :])
        cnt = c if p % per_block == 0 else cnt + c
        if p % per_block == per_block - 1:
            cnt_ref[p // per_block] = jnp.broadcast_to(cnt, cnt_ref.shape[1:])


def _mixer_out_specs(b, l, d, tile):
    nt = l // tile
    nblk = tile // MOE_BLOCK
    specs = [pl.BlockSpec((1, tile, d), lambda i, j: (i, j, 0)),
             pl.BlockSpec((tile, d), lambda i, j: (i * nt + j, 0)),
             pl.BlockSpec((tile, TOP_K), lambda i, j: (i * nt + j, 0)),
             pl.BlockSpec((tile, TOP_K), lambda i, j: (i * nt + j, 0)),
             pl.BlockSpec((nblk, 8, LANES), lambda i, j: (i * nt + j, 0, 0))]
    shapes = [jax.ShapeDtypeStruct((b, l, d), F32),
              jax.ShapeDtypeStruct((b * l, d), BF16),
              jax.ShapeDtypeStruct((b * l, TOP_K), jnp.int32),
              jax.ShapeDtypeStruct((b * l, TOP_K), F32),
              jax.ShapeDtypeStruct((b * l // MOE_BLOCK, 8, LANES), F32)]
    return specs, shapes


def _s5_out(y4, x, mod, w_bf, g2, wr, br):
    b, l, d = x.shape
    h = y4.shape[1]
    specs, shapes = _mixer_out_specs(b, l, d, S5_TILE)
    return pl.pallas_call(
        _s5_out_kernel,
        grid=(b, l // S5_TILE),
        in_specs=[pl.BlockSpec((1, h, S5_TILE // SSM_CHUNK, SSM_CHUNK), lambda i, j: (i, 0, j, 0)),
                  pl.BlockSpec((1, S5_TILE, d), lambda i, j: (i, j, 0)),
                  pl.BlockSpec((1, 8, d), lambda i, j: (i, 0, 0)),
                  pl.BlockSpec(w_bf.shape, lambda i, j: (0, 0), pipeline_mode=pl.Buffered(1)),
                  pl.BlockSpec((1, d), lambda i, j: (0, 0)),
                  pl.BlockSpec((d, 2 * LANES), lambda i, j: (0, 0)),
                  pl.BlockSpec((1, LANES), lambda i, j: (0, 0))],
        out_specs=specs,
        out_shape=shapes,
        compiler_params=_cparams("parallel", "parallel"),
        name="s5_out",
    )(y4, x, mod, w_bf, g2, wr, br)


def _conv_mixer_body(x_in_ref, mod_ref, g1_ref, win_ref, cw_ref, wout_ref, g2_ref, wr_ref, br_ref,
                     x_out_ref, hn_ref, eid_ref, wts_ref, cnt_ref, carry_ref):
    @pl.when(pl.program_id(1) == 0)
    def _():
        carry_ref[...] = jnp.zeros_like(carry_ref)

    mod = mod_ref[0]
    d = x_in_ref.shape[1]
    cw = cw_ref[...]
    n_sub = x_in_ref.shape[0] // SUB_ROWS
    sub = [slice(s * SUB_ROWS, (s + 1) * SUB_ROWS) for s in range(n_sub)]

    def project(s):
        hn = _norm_mod(x_in_ref[sub[s], :], g1_ref[...], mod[0:1], mod[1:2])
        return jnp.dot(hn.astype(BF16), win_ref[...], preferred_element_type=F32)

    def gate_conv(p, prev_tail):
        u = p[:, d:2 * d] * p[:, 2 * d:]
        ext = jnp.concatenate([prev_tail, u], axis=0)
        z = cw[0:1] * ext[6:6 + SUB_ROWS] + cw[1:2] * ext[7:7 + SUB_ROWS] + cw[2:3] * u
        return (p[:, :d] * z).astype(BF16), u[SUB_ROWS - 8:]

    ps = [project(s) for s in range(n_sub)]
    tail = carry_ref[...]
    outs = []
    for s in range(n_sub):
        gated, tail = gate_conv(ps[s], tail)
        outs.append(jnp.dot(gated, wout_ref[...], preferred_element_type=F32))
    carry_ref[...] = tail
    cnt = None
    for s in range(n_sub):
        c = _residual_and_route(x_in_ref[sub[s], :], outs[s], mod, g2_ref, wr_ref, br_ref,
                                x_out_ref.at[0, sub[s], :], hn_ref.at[sub[s], :], eid_ref.at[sub[s], :],
                                wts_ref.at[sub[s], :])
        cnt = c if s == 0 else cnt + c
    cnt_ref[0] = jnp.broadcast_to(cnt, cnt_ref.shape[1:])


_TN = (((0,), (0,)), ((), ()))


MAX_CHUNKS = LOCAL_ROWS // CHUNK_ROWS


def _rows_copy(src, s_row, dst, d_row, rows, sem):
    return pltpu.make_async_copy(src.at[pl.ds(s_row, rows), :], dst.at[pl.ds(d_row, rows), :], sem)


def _slot_masks(lp0, lp1):
    slot = lax.broadcasted_iota(jnp.int32, (lp0.shape[0], LOCAL_ROWS), 1)
    return slot == lp0, slot == lp1


_HI16 = -65536


def _pack_bf16_pairs(x):
    h = x.shape[1] // 2
    lo = lax.shift_right_logical(pltpu.bitcast(x[:, :h], jnp.int32), 16)
    hi = pltpu.bitcast(x[:, h:], jnp.int32) & _HI16
    return hi | lo


def _unpack_bf16_pairs(w):
    lo = pltpu.bitcast(w << 16, F32).astype(BF16)
    hi = pltpu.bitcast(w & _HI16, F32).astype(BF16)
    return lo, hi


def _moe_dispatch_kernel(crow_ref, tot_ref, padn_ref, pads_ref, nv_ref,
                         eid_ref, wts_ref, hn_ref, lofff_ref, tri_ref, xs_ref, lpos_ref,
                         loc_a_ref, loc_b_ref, zero_ref, sems):
    blk = pl.program_id(0)
    d = hn_ref.shape[1]
    eid = eid_ref[...]
    t = eid.shape[0]
    lane = lax.broadcasted_iota(jnp.int32, (t, LANES), 1)
    hit0 = lane == eid[:, 0:1]
    hit1 = lane == eid[:, 1:2]
    onehot = (hit0 | hit1).astype(BF16)
    before = jnp.dot(tri_ref[...], onehot, preferred_element_type=F32) + lofff_ref[0]
    lp0 = jnp.sum(jnp.where(hit0, before, 0.0), axis=-1, keepdims=True).astype(jnp.int32)
    lp1 = jnp.sum(jnp.where(hit1, before, 0.0), axis=-1, keepdims=True).astype(jnp.int32)
    col = lax.broadcasted_iota(jnp.int32, (t, TOP_K), 1)
    lpos_ref[...] = jnp.where(col == 0, lp0, lp1)
    wts = wts_ref[...]

    del tot_ref
    n_blk = pl.num_programs(0)

    @pl.when(blk == 0)
    def _():
        loc_b_ref[...] = jnp.zeros_like(loc_b_ref)

    def starts(table_row, src, sem):
        for q in range(MAX_CHUNKS):
            _rows_copy(src, q * CHUNK_ROWS, xs_ref,
                       pl.multiple_of(crow_ref[table_row * MAX_CHUNKS + q], CHUNK_ROWS), CHUNK_ROWS, sem).start()

    def wait_all(src, sem):
        _rows_copy(src, 0, xs_ref, 0, LOCAL_ROWS, sem).wait()

    def run(cur, cur_sem, prev, prev_sem):
        @pl.when(blk >= 1)
        def _():
            wait_all(cur, cur_sem)

        starts(blk, prev, prev_sem)
        m0, m1 = _slot_masks(lp0, lp1)
        wrow = jnp.sum(jnp.where(m0, wts[:, 0:1], 0.0) + jnp.where(m1, wts[:, 1:2], 0.0), axis=0, keepdims=True)
        sorted_rows = lax.dot_general((m0 | m1).astype(BF16), hn_ref[...], _TN, preferred_element_type=F32)
        cur[:, :d // 2] = _pack_bf16_pairs(sorted_rows)
        cur[:, d // 2:] = pltpu.bitcast(jnp.broadcast_to(wrow, (LANES, LOCAL_ROWS)).T, jnp.int32)

        @pl.when(blk == n_blk - 1)
        def _():
            starts(blk + 1, cur, cur_sem)
            wait_all(prev, prev_sem)
            wait_all(cur, cur_sem)

    @pl.when(blk % 2 == 0)
    def _():
        run(loc_a_ref, sems.at[0], loc_b_ref, sems.at[1])

    @pl.when(blk % 2 == 1)
    def _():
        run(loc_b_ref, sems.at[1], loc_a_ref, sems.at[0])

    @pl.when(blk == n_blk - 1)
    def _():
        sem = sems.at[0]
        zero_ref[...] = jnp.zeros_like(zero_ref)
        pad_bits = (EXPERT_TILE // CHUNK_ROWS - 1).bit_length()

        def pad_copies(e, carry, *, wait):
            padn = padn_ref[e]
            for k in range(pad_bits):
                @pl.when(((padn >> k) & 1) == 1)
                def _():
                    done = (padn & ((1 << k) - 1)) * CHUNK_ROWS
                    cp = _rows_copy(zero_ref, 0, xs_ref, pl.multiple_of(pads_ref[e] + done, CHUNK_ROWS),
                                    CHUNK_ROWS << k, sem)
                    if wait:
                        cp.wait()
                    else:
                        cp.start()
            return carry

        lax.fori_loop(0, N_EXPERTS, functools.partial(pad_copies, wait=False), 0)
        lax.fori_loop(0, N_EXPERTS, functools.partial(pad_copies, wait=True), 0)

        def tile_copy(i):
            return pltpu.make_async_copy(zero_ref, xs_ref.at[pl.ds(pl.multiple_of(i * EXPERT_TILE, EXPERT_TILE),
                                                                   EXPERT_TILE), :], sem)

        n_tiles = xs_ref.shape[0] // EXPERT_TILE

        def start_tile(i, c):
            tile_copy(i).start()
            return c

        def wait_tile(i, c):
            tile_copy(i).wait()
            return c

        lax.fori_loop(nv_ref[0], n_tiles, start_tile, 0)
        lax.fori_loop(nv_ref[0], n_tiles, wait_tile, 0)


def _moe_dispatch(tables, eid, wts, hn, loff_f, n_rows):
    t, d = hn.shape
    pos = jnp.arange(MOE_BLOCK, dtype=jnp.int32)
    tri = (pos[None, :] < pos[:, None]).astype(BF16)
    width = d // 2 + LANES
    grid_spec = pltpu.PrefetchScalarGridSpec(
        num_scalar_prefetch=5,
        grid=(t // MOE_BLOCK,),
        in_specs=[pl.BlockSpec((MOE_BLOCK, TOP_K), lambda i, *_: (i, 0)),
                  pl.BlockSpec((MOE_BLOCK, TOP_K), lambda i, *_: (i, 0)),
                  pl.BlockSpec((MOE_BLOCK, d), lambda i, *_: (i, 0)),
                  pl.BlockSpec((1, 1, LANES), lambda i, *_: (i, 0, 0)),
                  pl.BlockSpec((MOE_BLOCK, MOE_BLOCK), lambda i, *_: (0, 0))],
        out_specs=[pl.BlockSpec(memory_space=pl.ANY),
                   pl.BlockSpec((MOE_BLOCK, TOP_K), lambda i, *_: (i, 0))],
        scratch_shapes=[pltpu.VMEM((LOCAL_ROWS, width), jnp.int32), pltpu.VMEM((LOCAL_ROWS, width), jnp.int32),
                        pltpu.VMEM((EXPERT_TILE, width), jnp.int32), pltpu.SemaphoreType.DMA((2,))],
    )
    return pl.pallas_call(
        _moe_dispatch_kernel,
        grid_spec=grid_spec,
        out_shape=[jax.ShapeDtypeStruct((n_rows, width), jnp.int32),
                   jax.ShapeDtypeStruct((t, TOP_K), jnp.int32)],
        compiler_params=_cparams("arbitrary"),
        name="moe_dispatch",
    )(*tables, eid, wts, hn, loff_f, tri)


def _moe_experts_kernel(te_ref, tv_ref, tf_ref, ts_ref, xs_ref, w1_ref, w3_ref, w2_ref, o_ref,
                        w1b_ref, w3b_ref, w2b_ref):
    del te_ref, ts_ref
    i = pl.program_id(0)

    @pl.when(tf_ref[i] != 0)
    def _():
        def cast_rows(r, carry):
            rows = pl.ds(pl.multiple_of(r * LANES, LANES), LANES)
            w1b_ref[rows, :] = w1_ref[0, 0, rows, :].astype(BF16)
            w3b_ref[rows, :] = w3_ref[0, 0, rows, :].astype(BF16)

            @pl.when(r < w2b_ref.shape[0] // LANES)
            def _():
                w2b_ref[rows, :] = w2_ref[0, 0, rows, :].astype(BF16)

            return carry

        lax.fori_loop(0, w1b_ref.shape[0] // LANES, cast_rows, 0)

    @pl.when(tv_ref[i] != 0)
    def _():
        half = o_ref.shape[1]
        x_lo, x_hi = _unpack_bf16_pairs(xs_ref[:, :half])
        w = pltpu.bitcast(xs_ref[:, half:half + 1], F32)
        de = w1b_ref.shape[1]
        o = None
        for c0 in range(0, de, MXU_WIDTH):
            cols = slice(c0, c0 + MXU_WIDTH)
            a = (jnp.dot(x_lo, w1b_ref[:half, cols], preferred_element_type=F32)
                 + jnp.dot(x_hi, w1b_ref[half:, cols], preferred_element_type=F32))
            b = (jnp.dot(x_lo, w3b_ref[:half, cols], preferred_element_type=F32)
                 + jnp.dot(x_hi, w3b_ref[half:, cols], preferred_element_type=F32))
            h = (a * jax.nn.sigmoid(a) * b).astype(BF16)
            part = jnp.dot(h, w2b_ref[cols, :], preferred_element_type=F32)
            o = part if o is None else o + part
        o = o * w
        o_ref[...] = _pack_bf16_pairs(o.astype(BF16).astype(F32))

    @pl.when(tv_ref[i] == 0)
    def _():
        o_ref[...] = jnp.zeros_like(o_ref)


def _moe_experts(tile_expert, tile_valid, tile_first, tile_src, xs, w1, w3, w2, layer):
    r, width = xs.shape
    d = (width - LANES) * 2
    de = w1.shape[3]
    grid_spec = pltpu.PrefetchScalarGridSpec(
        num_scalar_prefetch=4,
        grid=(r // EXPERT_TILE,),
        in_specs=[pl.BlockSpec((EXPERT_TILE, width), lambda i, te, tv, tf, ts: (ts[i], 0)),
                  pl.BlockSpec((1, 1, d, de), lambda i, te, tv, tf, ts: (layer, te[i], 0, 0)),
                  pl.BlockSpec((1, 1, d, de), lambda i, te, tv, tf, ts: (layer, te[i], 0, 0)),
                  pl.BlockSpec((1, 1, de, d), lambda i, te, tv, tf, ts: (layer, te[i], 0, 0))],
        out_specs=pl.BlockSpec((EXPERT_TILE, d // 2), lambda i, te, tv, tf, ts: (i, 0)),
        scratch_shapes=[pltpu.VMEM((d, de), BF16), pltpu.VMEM((d, de), BF16), pltpu.VMEM((de, d), BF16)],
    )
    return pl.pallas_call(
        _moe_experts_kernel,
        grid_spec=grid_spec,
        out_shape=jax.ShapeDtypeStruct((r, d // 2), jnp.int32),
        compiler_params=_cparams("arbitrary"),
        name="moe_experts",
    )(tile_expert, tile_valid, tile_first, tile_src, xs, w1, w3, w2)


def _combine_rows(crow_ref, lpos_ref, o_hbm_ref, loc_a_ref, loc_b_ref, sems, consume):
    blk = pl.program_id(0) * pl.num_programs(1) + pl.program_id(1)
    n_blk = pl.num_programs(0) * pl.num_programs(1)

    def fetch(b, loc, sem):
        for q in range(MAX_CHUNKS):
            _rows_copy(o_hbm_ref, pl.multiple_of(crow_ref[b * MAX_CHUNKS + q], CHUNK_ROWS), loc,
                       q * CHUNK_ROWS, CHUNK_ROWS, sem).start()

    def wait_all(loc, sem):
        _rows_copy(o_hbm_ref, 0, loc, 0, LOCAL_ROWS, sem).wait()

    @pl.when(blk == 0)
    def _():
        fetch(blk, loc_a_ref, sems.at[0])

    def run(cur, cur_sem, nxt, nxt_sem):
        wait_all(cur, cur_sem)
        fetch(jnp.minimum(blk + 1, n_blk - 1), nxt, nxt_sem)
        lp = lpos_ref[...]
        m0, m1 = _slot_masks(lp[:, 0:1], lp[:, 1:2])
        pt = (m0 | m1).astype(BF16)
        o_lo, o_hi = _unpack_bf16_pairs(cur[...])
        consume(jnp.concatenate([jnp.dot(pt, o_lo, preferred_element_type=F32),
                                 jnp.dot(pt, o_hi, preferred_element_type=F32)], axis=1))

        @pl.when(blk == n_blk - 1)
        def _():
            wait_all(nxt, nxt_sem)

    @pl.when(blk % 2 == 0)
    def _():
        run(loc_a_ref, sems.at[0], loc_b_ref, sems.at[1])

    @pl.when(blk % 2 == 1)
    def _():
        run(loc_b_ref, sems.at[1], loc_a_ref, sems.at[0])


def _combine_rows_loop(crow_ref, tot_ref, lpos_ref, o_hbm_ref, loc_ref, sems, t):
    blk = pl.program_id(0) * pl.num_programs(1) + pl.program_id(1)
    n_blk = pl.num_programs(0) * pl.num_programs(1)
    buf = blk % 2

    def fetch(b, which):
        loc = loc_ref.at[which]
        loc[TOP_K * t:, :] = jnp.zeros((LOCAL_ROWS - TOP_K * t, loc_ref.shape[2]), jnp.int32)

        def body(q, carry):
            _rows_copy(o_hbm_ref, pl.multiple_of(crow_ref[b * MAX_CHUNKS + q], CHUNK_ROWS), loc,
                       pl.multiple_of(q * CHUNK_ROWS, CHUNK_ROWS), CHUNK_ROWS, sems.at[which]).start()
            return carry

        lax.fori_loop(0, tot_ref[b], body, 0)

    @pl.when(blk == 0)
    def _():
        fetch(blk, buf)

    @pl.when(blk + 1 < n_blk)
    def _():
        fetch(blk + 1, 1 - buf)

    count = tot_ref[blk]
    for k in range(MAX_CHUNKS.bit_length()):
        @pl.when(((count >> k) & 1) == 1)
        def _():
            _rows_copy(o_hbm_ref, 0, loc_ref.at[buf], 0, CHUNK_ROWS << k, sems.at[buf]).wait()

    lp = lpos_ref[...]
    m0, m1 = _slot_masks(lp[:, 0:1], lp[:, 1:2])
    pt = (m0 | m1).astype(BF16)
    o_lo, o_hi = _unpack_bf16_pairs(loc_ref[buf])
    return jnp.concatenate([jnp.dot(pt, o_lo, preferred_element_type=F32),
                            jnp.dot(pt, o_hi, preferred_element_type=F32)], axis=1)


def _moe_combine_final_kernel(crow_ref, tot_ref, lpos_ref, x_ref, mod_ref, fg_ref, o_hbm_ref,
                              out_ref, loc_ref, sems):
    y = _combine_rows_loop(crow_ref, tot_ref, lpos_ref, o_hbm_ref, loc_ref, sems, x_ref.shape[1])
    x2 = x_ref[0] + (1.0 + mod_ref[0, 5:6]) * y
    out_ref[0] = (x2 * lax.rsqrt(jnp.mean(x2 * x2, axis=-1, keepdims=True) + RMS_EPS)) * fg_ref[...]


def _moe_combine_conv_kernel(crow_ref, tot_ref, lpos_ref, x_ref, mod_prev_ref, o_hbm_ref,
                             mod_ref, g1_ref, win_ref, cw_ref, wout_ref, g2_ref, wr_ref, br_ref,
                             x_out_ref, hn_ref, eid_ref, wts_ref, cnt_ref,
                             loc_a_ref, loc_b_ref, sems, xmid_ref, carry_ref):
    del tot_ref

    def residual(y):
        xmid_ref[...] = x_ref[0] + (1.0 + mod_prev_ref[0, 5:6]) * y

    _combine_rows(crow_ref, lpos_ref, o_hbm_ref, loc_a_ref, loc_b_ref, sems, residual)
    _conv_mixer_body(xmid_ref, mod_ref, g1_ref, win_ref, cw_ref, wout_ref, g2_ref, wr_ref, br_ref,
                     x_out_ref, hn_ref, eid_ref, wts_ref, cnt_ref, carry_ref)


def _combine_in_specs(nt, d):
    return [pl.BlockSpec((MOE_BLOCK, TOP_K), lambda i, j, *_: (i * nt + j, 0)),
            pl.BlockSpec((1, MOE_BLOCK, d), lambda i, j, *_: (i, j, 0)),
            pl.BlockSpec((1, 8, d), lambda i, j, *_: (i, 0, 0))]


def _combine_scratch(d):
    return [pltpu.VMEM((LOCAL_ROWS, d // 2), jnp.int32), pltpu.VMEM((LOCAL_ROWS, d // 2), jnp.int32),
            pltpu.SemaphoreType.DMA((2,))]


def _moe_combine_final(tables, lpos, x, mod, final_g, o_sorted):
    b, l, d = x.shape
    nt = l // MOE_BLOCK
    grid_spec = pltpu.PrefetchScalarGridSpec(
        num_scalar_prefetch=2,
        grid=(b, nt),
        in_specs=_combine_in_specs(nt, d) + [pl.BlockSpec((1, d), lambda i, j, *_: (0, 0)),
                                             pl.BlockSpec(memory_space=pl.ANY)],
        out_specs=pl.BlockSpec((1, MOE_BLOCK, d), lambda i, j, *_: (i, j, 0)),
        scratch_shapes=[pltpu.VMEM((2, LOCAL_ROWS, d // 2), jnp.int32), pltpu.SemaphoreType.DMA((2,))],
    )
    return pl.pallas_call(
        _moe_combine_final_kernel,
        grid_spec=grid_spec,
        out_shape=jax.ShapeDtypeStruct((b, l, d), F32),
        compiler_params=_cparams("arbitrary", "arbitrary"),
        name="moe_combine_final",
    )(*tables[:2], lpos, x, mod, final_g, o_sorted)


def _moe_combine_conv(tables, lpos, x, mod_prev, o_sorted, mod, g1, win_bf, cw8, wout_bf, g2, wr, br):
    b, l, d = x.shape
    nt = l // MOE_BLOCK
    specs, shapes = _mixer_out_specs(b, l, d, TOKEN_TILE)
    const = lambda shape: pl.BlockSpec(shape, lambda i, j, *_: (0,) * len(shape))
    grid_spec = pltpu.PrefetchScalarGridSpec(
        num_scalar_prefetch=2,
        grid=(b, nt),
        in_specs=_combine_in_specs(nt, d) + [
            pl.BlockSpec(memory_space=pl.ANY),
            pl.BlockSpec((1, 8, d), lambda i, j, *_: (i, 0, 0)),
            const((1, d)), const(win_bf.shape), const((8, d)), const(wout_bf.shape),
            const((1, d)), const((d, 2 * LANES)), const((1, LANES))],
        out_specs=[pl.BlockSpec(s.block_shape, lambda i, j, *_, f=s.index_map: f(i, j)) for s in specs],
        scratch_shapes=_combine_scratch(d) + [pltpu.VMEM((MOE_BLOCK, d), F32), pltpu.VMEM((8, d), F32)],
    )
    return pl.pallas_call(
        _moe_combine_conv_kernel,
        grid_spec=grid_spec,
        out_shape=shapes,
        compiler_params=_cparams("arbitrary", "arbitrary"),
        name="moe_combine_conv",
    )(*tables[:2], lpos, x, mod_prev, o_sorted, mod, g1, win_bf, cw8, wout_bf, g2, wr, br)


def _moe_sorted_experts(hn, eid, wts, cnt, w1, w3, w2, layer):
    t, d = hn.shape
    nblk = t // MOE_BLOCK
    i32 = jnp.int32
    n = cnt[:, 0, :N_EXPERTS].astype(i32)
    run = (n + CHUNK_ROWS - 1) // CHUNK_ROWS * CHUNK_ROWS
    loff = jnp.cumsum(run, axis=1) - run
    rows_e = jnp.sum(run, axis=0)
    tiles_e = (rows_e + EXPERT_TILE - 1) // EXPERT_TILE
    tile_end = jnp.cumsum(tiles_e)
    base = (tile_end - tiles_e) * EXPERT_TILE
    goff = base[None, :] + jnp.cumsum(run, axis=0) - run
    nch = run // CHUNK_ROWS
    tot = jnp.sum(nch, axis=1)
    ch_end = jnp.cumsum(nch, axis=1)
    q = jnp.arange(MAX_CHUNKS, dtype=i32)
    in_run = ((q[None, :, None] >= (ch_end - nch)[:, None, :]) & (q[None, :, None] < ch_end[:, None, :])).astype(i32)
    crow = jnp.sum(in_run * (goff[:, None, :] + (q[None, :, None] - (ch_end - nch)[:, None, :]) * CHUNK_ROWS), axis=-1)
    padn = (tiles_e * EXPERT_TILE - rows_e) // CHUNK_ROWS
    pads = base + rows_e
    loff_f = jnp.zeros((nblk, 1, LANES), F32).at[:, 0, :N_EXPERTS].set(loff.astype(F32))

    max_rows = t * TOP_K + nblk * N_EXPERTS * (CHUNK_ROWS - 1) + N_EXPERTS * (EXPERT_TILE - 1)
    real_tiles = -(-max_rows // EXPERT_TILE)
    scratch_tiles = -(-2 * LOCAL_ROWS // EXPERT_TILE)
    max_tiles = real_tiles + scratch_tiles
    scratch_row = real_tiles * EXPERT_TILE + q[None, :] * CHUNK_ROWS
    parity = (jnp.arange(nblk, dtype=i32) % 2)[:, None]
    crow_out = jnp.where(q[None, :] < tot[:, None], crow, scratch_row + parity * LOCAL_ROWS)
    crow_out = jnp.concatenate([scratch_row + LOCAL_ROWS, crow_out], axis=0)
    tables = (crow.reshape(-1), tot)
    dispatch_tables = (crow_out.reshape(-1), tot, padn, pads, tile_end[-1:])
    tile_ids = jnp.arange(max_tiles, dtype=i32)
    n_valid = tile_end[-1]
    tile_src = jnp.minimum(tile_ids, n_valid - 1)
    tile_expert = jnp.sum((tile_src[:, None] >= tile_end[None, :]).astype(i32), axis=1)
    tile_valid = (tile_ids < n_valid).astype(i32)
    tile_first = jnp.concatenate([jnp.ones((1,), i32), (tile_expert[1:] != tile_expert[:-1]).astype(i32)])

    xs, lpos = _moe_dispatch(dispatch_tables, eid, wts, hn, loff_f, max_tiles * EXPERT_TILE)
    o_sorted = _moe_experts(tile_expert, tile_valid, tile_first, tile_src, xs, w1, w3, w2, layer)
    return tables, lpos, o_sorted


def _router_pack(wg, bg, we, be):
    d = wg.shape[0]
    wr = jnp.zeros((d, LANES), F32).at[:, :N_EXPERTS].set(we).at[:, N_EXPERTS:N_EXPERTS + N_GROUPS].set(wg)
    br = jnp.zeros((1, LANES), F32).at[0, :N_EXPERTS].set(be).at[0, N_EXPERTS:N_EXPERTS + N_GROUPS].set(bg)
    wr_hi = wr.astype(BF16)
    wr_lo = (wr - wr_hi.astype(F32)).astype(BF16)
    return jnp.concatenate([wr_hi, wr_lo], axis=1), br


def kernel(x, c, ada_w, ada_b, norm1_g, norm2_g, ssm_w_in, ssm_lam_re, ssm_lam_im, ssm_log_dt, ssm_b_re, ssm_b_im, ssm_c_re, ssm_c_im, ssm_d, ssm_w_glu, conv_w_in, conv_w, conv_w_out, moe_wg, moe_bg, moe_we, moe_be, moe_w1, moe_w3, moe_w2, final_g):
    b, l, d = x.shape
    depth = ada_w.shape[0]
    c8 = jnp.zeros((8, d), F32).at[:b].set(c)
    mod_all = _adaln(c8, ada_w, ada_b)[:, :b].reshape(depth, b, 6, d)
    mod_all = jnp.concatenate([mod_all, jnp.zeros((depth, b, 2, d), F32)], axis=2)
    fg = final_g.reshape(1, d)

    mod = mod_all[0]
    u4 = _s5_in(x, mod, norm1_g[0:1], ssm_w_in[0].T.astype(BF16))
    kern, pow_tab, coef_tab, a_pack, d_row = _s5_tables(
        ssm_lam_re[0], ssm_lam_im[0], ssm_log_dt[0], ssm_b_re[0], ssm_b_im[0],
        ssm_c_re[0], ssm_c_im[0], ssm_d[0])
    y4 = _s5_scan(u4, kern, pow_tab, coef_tab, a_pack, d_row)
    wr, br = _router_pack(moe_wg[0], moe_bg[0], moe_we[0], moe_be[0])
    x1, hn, eid, wts, cnt = _s5_out(y4, x, mod, ssm_w_glu[0].astype(BF16), norm2_g[0:1], wr, br)
    tables, lpos, o_sorted = _moe_sorted_experts(hn, eid, wts, cnt, moe_w1, moe_w3, moe_w2, 0)

    mod1 = mod_all[1]
    cw8 = jnp.zeros((8, d), F32).at[:conv_w.shape[1]].set(conv_w[0])
    wr, br = _router_pack(moe_wg[1], moe_bg[1], moe_we[1], moe_be[1])
    x3, hn, eid, wts, cnt = _moe_combine_conv(
        tables, lpos, x1, mod, o_sorted, mod1, norm1_g[1:2], conv_w_in[0].astype(BF16), cw8,
        conv_w_out[0].astype(BF16), norm2_g[1:2], wr, br)
    tables, lpos, o_sorted = _moe_sorted_experts(hn, eid, wts, cnt, moe_w1, moe_w3, moe_w2, 1)
    return _moe_combine_final(tables, lpos, x3, mod1, fg, o_sorted)
```
